```python
import math
import jax, jax.numpy as jnp
from jax import lax
import numpy as np

D_MODEL = 1024
BATCH = 8
SEQ = 2048
DEPTH = 2

N_MIXERS = 2
N_MLA = (DEPTH + 1) // 2
N_SSM = DEPTH // 2
MLA_HEADS = 8
QK_NOPE = 128
QK_ROPE = 64
V_DIM = 128
Q_LORA = 256
KV_LORA = 128
ROPE_THETA = 10000.0
Q_BLOCK = 128
SSM_WIDTH = D_MODEL
GROUP_CH = 16
N_GROUPS = SSM_WIDTH // GROUP_CH
STATE = 64
DT_MIN = 0.001
DT_MAX = 0.1
D_FF = 4 * D_MODEL
ALPHA = (2 * DEPTH) ** 0.25
BETA = (8 * DEPTH) ** -0.25
LN_EPS = 1e-5
RMS_EPS = 1e-6

kernel_name = "hybrid_mla_s5_deepnorm_adaln"


def layer_norm(x, g, b):
    xf = x.astype(jnp.float32)
    mu = jnp.mean(xf, axis=-1, keepdims=True)
    var = jnp.mean(jnp.square(xf - mu), axis=-1, keepdims=True)
    y = (xf - mu) * lax.rsqrt(var + LN_EPS) * g.astype(jnp.float32) + b.astype(jnp.float32)
    return y.astype(x.dtype)


def rms_norm(x, g):
    xf = x.astype(jnp.float32)
    y = xf * lax.rsqrt(jnp.mean(jnp.square(xf), axis=-1, keepdims=True) + RMS_EPS) * g.astype(jnp.float32)
    return y.astype(x.dtype)


def rope_tables(positions):
    inv_freq = ROPE_THETA ** (-jnp.arange(0, QK_ROPE, 2, dtype=jnp.float32) / QK_ROPE)
    ang = positions.astype(jnp.float32)[..., None] * inv_freq
    return jnp.cos(ang), jnp.sin(ang)


def apply_rope(x, cos, sin):
    xf = x.astype(jnp.float32)
    x1, x2 = jnp.split(xf, 2, axis=-1)
    return jnp.concatenate([x1 * cos - x2 * sin, x1 * sin + x2 * cos], axis=-1).astype(x.dtype)


def mla_mixer(h, cos, sin, w_in, q_norm, w_qb, kv_norm, w_kvb, w_o):
    B, L, _ = h.shape
    z = h @ w_in
    cq = rms_norm(z[..., :Q_LORA], q_norm)
    ckv = rms_norm(z[..., Q_LORA:Q_LORA + KV_LORA], kv_norm)
    k_pe = apply_rope(z[..., Q_LORA + KV_LORA:], cos, sin)
    q = (cq @ w_qb).reshape(B, L, MLA_HEADS, QK_NOPE + QK_ROPE)
    q_nope = q[..., :QK_NOPE]
    q_pe = apply_rope(q[..., QK_NOPE:], cos[:, :, None, :], sin[:, :, None, :])
    kv = (ckv @ w_kvb).reshape(B, L, MLA_HEADS, QK_NOPE + V_DIM)
    k_nope = kv[..., :QK_NOPE]
    v = kv[..., QK_NOPE:]
    scale = 1.0 / math.sqrt(QK_NOPE + QK_ROPE)
    nb = L // Q_BLOCK
    qn_b = q_nope.reshape(B, nb, Q_BLOCK, MLA_HEADS, QK_NOPE).transpose(1, 0, 2, 3, 4)
    qp_b = q_pe.reshape(B, nb, Q_BLOCK, MLA_HEADS, QK_ROPE).transpose(1, 0, 2, 3, 4)
    k_idx = jnp.arange(L)

    def attend(args):
        qn, qp, start = args
        s = (jnp.einsum('bqhd,bkhd->bhqk', qn, k_nope)
             + jnp.einsum('bqhr,bkr->bhqk', qp, k_pe)).astype(jnp.float32) * scale
        q_idx = start + jnp.arange(Q_BLOCK)
        causal = k_idx[None, :] <= q_idx[:, None]
        s = jnp.where(causal, s, jnp.float32(-1e30))
        p = jax.nn.softmax(s, axis=-1).astype(v.dtype)
        return jnp.einsum('bhqk,bkhd->bqhd', p, v)

    o = lax.map(attend, (qn_b, qp_b, jnp.arange(nb) * Q_BLOCK))
    o = o.transpose(1, 0, 2, 3, 4).reshape(B, L, MLA_HEADS * V_DIM)
    return o @ w_o


def s5_mixer(h, w_in, log_dt, a_re, a_im, b_re, b_im, c_re, c_im, d_skip, w_glu, b_glu, w_out):
    B, L, _ = h.shape
    u = h @ w_in
    u32 = u.astype(jnp.float32).reshape(B, L, N_GROUPS, GROUP_CH)
    lr = a_re.astype(jnp.float32)
    li = a_im.astype(jnp.float32)
    dt = jnp.exp(log_dt.astype(jnp.float32))[:, None]
    mag = jnp.exp(lr * dt)
    ab_re = mag * jnp.cos(li * dt)
    ab_im = mag * jnp.sin(li * dt)
    den = lr * lr + li * li
    nr = ab_re - 1.0
    coef_re = ((nr * lr + ab_im * li) / den)[..., None]
    coef_im = ((ab_im * lr - nr * li) / den)[..., None]
    br = b_re.astype(jnp.float32)
    bi = b_im.astype(jnp.float32)
    bb_re = coef_re * br - coef_im * bi
    bb_im = coef_re * bi + coef_im * br
    bu_re = jnp.einsum('blgc,gpc->blgp', u32, bb_re)
    bu_im = jnp.einsum('blgc,gpc->blgp', u32, bb_im)
    a_re_t = jnp.broadcast_to(ab_re, (1, L, N_GROUPS, STATE))
    a_im_t = jnp.broadcast_to(ab_im, (1, L, N_GROUPS, STATE))

    def combine(e1, e2):
        a1r, a1i, b1r, b1i = e1
        a2r, a2i, b2r, b2i = e2
        return (a2r * a1r - a2i * a1i,
                a2r * a1i + a2i * a1r,
                a2r * b1r - a2i * b1i + b2r,
                a2r * b1i + a2i * b1r + b2i)

    _, _, xr, xi = lax.associative_scan(combine, (a_re_t, a_im_t, bu_re, bu_im), axis=1)
    y = (jnp.einsum('blgp,gcp->blgc', xr, c_re.astype(jnp.float32))
         - jnp.einsum('blgp,gcp->blgc', xi, c_im.astype(jnp.float32)))
    y = y + d_skip.astype(jnp.float32).reshape(N_GROUPS, GROUP_CH) * u32
    y = y.reshape(B, L, SSM_WIDTH).astype(h.dtype)
    g = jax.nn.gelu(y)
    z = g * jax.nn.sigmoid(g @ w_glu + b_glu)
    return z @ w_out


def sq_relu_mlp(h, w1, b1, w2, b2):
    a = jax.nn.relu(h @ w1 + b1)
    return (a * a) @ w2 + b2


def modulation(cs, w, b):
    m = cs @ w + b
    shift, scale, gate = jnp.split(m, 3, axis=-1)
    return shift[:, None, :], scale[:, None, :], gate[:, None, :]


def setup_inputs(seed: int = 0) -> dict:
    key = jax.random.key(seed)
    ks = iter(jax.random.split(key, 40))
    f32 = jnp.float32
    nrm = lambda shape, s: jax.random.normal(next(ks), shape, f32) * s
    D = D_MODEL
    x = jax.random.normal(next(ks), (BATCH, SEQ, D), f32)
    c = jax.random.normal(next(ks), (BATCH, D), f32)
    offs = jax.random.randint(next(ks), (BATCH, 1), 0, 1024, dtype=jnp.int32)
    positions = offs + jnp.arange(SEQ, dtype=jnp.int32)[None, :]
    mla_w_in = nrm((N_MLA, D, Q_LORA + KV_LORA + QK_ROPE), D ** -0.5)
    mla_q_norm = 1.0 + nrm((N_MLA, Q_LORA), 0.02)
    mla_w_qb = nrm((N_MLA, Q_LORA, MLA_HEADS * (QK_NOPE + QK_ROPE)), Q_LORA ** -0.5)
    mla_kv_norm = 1.0 + nrm((N_MLA, KV_LORA), 0.02)
    mla_w_kvb = nrm((N_MLA, KV_LORA, MLA_HEADS * (QK_NOPE + V_DIM)), KV_LORA ** -0.5)
    mla_w_o = nrm((N_MLA, MLA_HEADS * V_DIM, D), BETA * (MLA_HEADS * V_DIM) ** -0.5)
    ssm_w_in = nrm((N_SSM, D, SSM_WIDTH), D ** -0.5)
    ssm_log_dt = jax.random.uniform(next(ks), (N_SSM, N_GROUPS), f32,
                                    math.log(DT_MIN), math.log(DT_MAX))
    n_idx = jnp.arange(STATE, dtype=f32)
    ssm_a_re = -0.5 + nrm((N_SSM, N_GROUPS, STATE), 0.01)
    ssm_a_im = math.pi * n_idx + nrm((N_SSM, N_GROUPS, STATE), 0.01)
    ssm_b_re = nrm((N_SSM, N_GROUPS, STATE, GROUP_CH), (2 * GROUP_CH) ** -0.5)
    ssm_b_im = nrm((N_SSM, N_GROUPS, STATE, GROUP_CH), (2 * GROUP_CH) ** -0.5)
    ssm_c_re = nrm((N_SSM, N_GROUPS, GROUP_CH, STATE), (2 * STATE) ** -0.5)
    ssm_c_im = nrm((N_SSM, N_GROUPS, GROUP_CH, STATE), (2 * STATE) ** -0.5)
    ssm_d = nrm((N_SSM, SSM_WIDTH), 1.0)
    ssm_w_glu = nrm((N_SSM, SSM_WIDTH, SSM_WIDTH), SSM_WIDTH ** -0.5)
    ssm_b_glu = nrm((N_SSM, SSM_WIDTH), 0.01)
    ssm_w_out = nrm((N_SSM, SSM_WIDTH, D), BETA * SSM_WIDTH ** -0.5)
    mlp_w1 = nrm((DEPTH, D, D_FF), D ** -0.5)
    mlp_b1 = nrm((DEPTH, D_FF), 0.01)
    mlp_w2 = nrm((DEPTH, D_FF, D), BETA * D_FF ** -0.5)
    mlp_b2 = nrm((DEPTH, D), 0.01)
    mod_mix_w = nrm((DEPTH, D, 3 * D), 0.2 * D ** -0.5)
    mod_mix_b = nrm((DEPTH, 3 * D), 0.01)
    mod_ffn_w = nrm((DEPTH, D, 3 * D), 0.2 * D ** -0.5)
    mod_ffn_b = nrm((DEPTH, 3 * D), 0.01)
    ln_mix_g = 1.0 + nrm((DEPTH, D), 0.02)
    ln_mix_b = nrm((DEPTH, D), 0.01)
    ln_ffn_g = 1.0 + nrm((DEPTH, D), 0.02)
    ln_ffn_b = nrm((DEPTH, D), 0.01)
    return {"x": x, "c": c, "positions": positions,
            "mla_w_in": mla_w_in, "mla_q_norm": mla_q_norm, "mla_w_qb": mla_w_qb,
            "mla_kv_norm": mla_kv_norm, "mla_w_kvb": mla_w_kvb, "mla_w_o": mla_w_o,
            "ssm_w_in": ssm_w_in, "ssm_log_dt": ssm_log_dt, "ssm_a_re": ssm_a_re,
            "ssm_a_im": ssm_a_im, "ssm_b_re": ssm_b_re, "ssm_b_im": ssm_b_im,
            "ssm_c_re": ssm_c_re, "ssm_c_im": ssm_c_im, "ssm_d": ssm_d,
            "ssm_w_glu": ssm_w_glu, "ssm_b_glu": ssm_b_glu, "ssm_w_out": ssm_w_out,
            "mlp_w1": mlp_w1, "mlp_b1": mlp_b1, "mlp_w2": mlp_w2, "mlp_b2": mlp_b2,
            "mod_mix_w": mod_mix_w, "mod_mix_b": mod_mix_b,
            "mod_ffn_w": mod_ffn_w, "mod_ffn_b": mod_ffn_b,
            "ln_mix_g": ln_mix_g, "ln_mix_b": ln_mix_b,
            "ln_ffn_g": ln_ffn_g, "ln_ffn_b": ln_ffn_b}


def reference(x, c, positions,
              mla_w_in, mla_q_norm, mla_w_qb, mla_kv_norm, mla_w_kvb, mla_w_o,
              ssm_w_in, ssm_log_dt, ssm_a_re, ssm_a_im, ssm_b_re, ssm_b_im,
              ssm_c_re, ssm_c_im, ssm_d, ssm_w_glu, ssm_b_glu, ssm_w_out,
              mlp_w1, mlp_b1, mlp_w2, mlp_b2,
              mod_mix_w, mod_mix_b, mod_ffn_w, mod_ffn_b,
              ln_mix_g, ln_mix_b, ln_ffn_g, ln_ffn_b):
    cs = jax.nn.silu(c)
    cos, sin = rope_tables(positions)
    for i in range(DEPTH):
        shift, scale, gate = modulation(cs, mod_mix_w[i], mod_mix_b[i])
        h = x * (1.0 + scale) + shift
        j = i // N_MIXERS
        if i % N_MIXERS == 0:
            y = mla_mixer(h, cos, sin, mla_w_in[j], mla_q_norm[j], mla_w_qb[j],
                          mla_kv_norm[j], mla_w_kvb[j], mla_w_o[j])
        else:
            y = s5_mixer(h, ssm_w_in[j], ssm_log_dt[j], ssm_a_re[j], ssm_a_im[j],
                         ssm_b_re[j], ssm_b_im[j], ssm_c_re[j], ssm_c_im[j], ssm_d[j],
                         ssm_w_glu[j], ssm_b_glu[j], ssm_w_out[j])
        x = layer_norm(ALPHA * x + (1.0 + gate) * y, ln_mix_g[i], ln_mix_b[i])
        shift, scale, gate = modulation(cs, mod_ffn_w[i], mod_ffn_b[i])
        h = x * (1.0 + scale) + shift
        y = sq_relu_mlp(h, mlp_w1[i], mlp_b1[i], mlp_w2[i], mlp_b2[i])
        x = layer_norm(ALPHA * x + (1.0 + gate) * y, ln_ffn_g[i], ln_ffn_b[i])
    return x
```

```python
import functools
import math

import jax
import jax.numpy as jnp
from jax import lax
from jax.experimental import pallas as pl
from jax.experimental.pallas import tpu as pltpu

F32 = jnp.float32
BF16 = jnp.bfloat16

D = 1024
NB = 8
SEQ = 2048
HEADS = 8
QK_NOPE = 128
QK_ROPE = 64
V_DIM = 128
Q_LORA = 256
KV_LORA = 128
ROPE_THETA = 10000.0
GROUP_CH = 16
N_GROUPS = 64
STATE = 64
D_FF = 4 * D
DEPTH = 2
ALPHA = (2 * DEPTH) ** 0.25
LN_EPS = 1e-5
RMS_EPS = 1e-6
ATTN_SCALE = 1.0 / math.sqrt(QK_NOPE + QK_ROPE)
NEG_BIG = -1e30

HEAD_W = 256
GROUPS_PER_TILE = 16
N_TILES = N_GROUPS // GROUPS_PER_TILE
TILE_IN = GROUPS_PER_TILE * GROUP_CH
TILE_ST = GROUPS_PER_TILE * STATE

VMEM_LIMIT = 56 * 1024 * 1024
PROJ_ROWS = 512
ATTN_BLK = 256
OUT_ROWS = 512
FFN_ROWS = 512
FF_CHUNK = 1024
S5_STEPS = 64
S5_ROWS = S5_STEPS * NB
SCAN_W = 512


def _cparams(*sem):
    return pltpu.CompilerParams(dimension_semantics=sem, vmem_limit_bytes=VMEM_LIMIT)


def _const_spec(shape):
    nd = len(shape)
    return pl.BlockSpec(shape, lambda *_: (0,) * nd, pipeline_mode=pl.Buffered(1))


def _layer_norm(v, g, b):
    mu = jnp.mean(v, axis=-1, keepdims=True)
    vc = v - mu
    var = jnp.mean(vc * vc, axis=-1, keepdims=True)
    return vc * lax.rsqrt(var + LN_EPS) * g + b


def _bdot(a, b):
    return jnp.dot(a, b, preferred_element_type=F32)


def _mod_kernel(c_ref, w_ref, b_ref, o_ref):
    c = c_ref[...]
    cs = c * jax.nn.sigmoid(c)
    o_ref[0] = _bdot(cs.astype(BF16), w_ref[0].astype(BF16)) + b_ref[0]


def _modulation(c, w, b):
    n_layers = w.shape[0]
    tn = 1024
    return pl.pallas_call(
        _mod_kernel,
        out_shape=jax.ShapeDtypeStruct((n_layers, NB, 3 * D), F32),
        grid=(n_layers, 3 * D // tn),
        in_specs=[
            pl.BlockSpec((NB, D), lambda l, j: (0, 0)),
            pl.BlockSpec((1, D, tn), lambda l, j: (l, 0, j)),
            pl.BlockSpec((1, 1, tn), lambda l, j: (l, 0, j)),
        ],
        out_specs=pl.BlockSpec((1, NB, tn), lambda l, j: (l, 0, j)),
        compiler_params=_cparams("arbitrary", "arbitrary"),
        name="adaln_modulation",
    )(c, w, b.reshape(n_layers, 1, 3 * D))


def _mla_proj_kernel(x_ref, m_ref, pos_ref, freq_ref, w_in_ref, qn_ref, kvn_ref, w_qb_ref, w_kvb_ref,
                     q_ref, k_ref, v_ref):
    shift = m_ref[0, 0:1, :]
    scale = m_ref[0, 1:2, :]
    h = (x_ref[0] * (1.0 + scale) + shift).astype(BF16)
    z = _bdot(h, w_in_ref[...])

    def rms(v, g):
        return v * lax.rsqrt(jnp.mean(v * v, axis=-1, keepdims=True) + RMS_EPS) * g

    cq = rms(z[:, :Q_LORA], qn_ref[...]).astype(BF16)
    ckv = rms(z[:, Q_LORA:Q_LORA + KV_LORA], kvn_ref[...]).astype(BF16)

    ang = pos_ref[0].astype(F32) * freq_ref[...]
    lane = lax.broadcasted_iota(jnp.int32, ang.shape, 1)
    mult = jnp.where(lane < QK_ROPE, jnp.cos(ang), jnp.sin(ang))

    def rope(slab):
        s = slab * mult
        return s + pltpu.roll(s, QK_ROPE, axis=1)

    k_rope = jnp.where(lane < QK_ROPE, rope(z[:, Q_LORA + KV_LORA:]), 0.0).astype(BF16)

    q_all = _bdot(cq, w_qb_ref[...])
    kv = _bdot(ckv, w_kvb_ref[...])
    for hd in range(HEADS):
        c0 = hd * HEAD_W
        q_ref[0, hd, :, 0:QK_NOPE] = (q_all[:, c0:c0 + QK_NOPE] * ATTN_SCALE).astype(BF16)
        q_ref[0, hd, :, QK_NOPE:HEAD_W] = (rope(q_all[:, c0 + QK_NOPE:c0 + HEAD_W]) * ATTN_SCALE).astype(BF16)
        k_ref[0, hd, :, 0:QK_NOPE] = kv[:, c0:c0 + QK_NOPE].astype(BF16)
        k_ref[0, hd, :, QK_NOPE:HEAD_W] = k_rope
        v_ref[0, hd] = kv[:, c0 + QK_NOPE:c0 + HEAD_W].astype(BF16)


def _mla_proj(x, m, pos, freq, w_in, qn, kvn, w_qb, w_kvb):
    nt = SEQ // PROJ_ROWS
    return pl.pallas_call(
        _mla_proj_kernel,
        out_shape=(
            jax.ShapeDtypeStruct((NB, HEADS, SEQ, HEAD_W), BF16),
            jax.ShapeDtypeStruct((NB, HEADS, SEQ, HEAD_W), BF16),
            jax.ShapeDtypeStruct((NB, HEADS, SEQ, V_DIM), BF16),
        ),
        grid=(NB, nt),
        in_specs=[
            pl.BlockSpec((1, PROJ_ROWS, D), lambda b, i: (b, i, 0)),
            pl.BlockSpec((1, 3, D), lambda b, i: (b, 0, 0)),
            pl.BlockSpec((1, PROJ_ROWS, 1), lambda b, i: (b, i, 0)),
            _const_spec((1, 128)),
            _const_spec(w_in.shape),
            _const_spec((1, Q_LORA)),
            _const_spec((1, KV_LORA)),
            _const_spec(w_qb.shape),
            _const_spec(w_kvb.shape),
        ],
        out_specs=(
            pl.BlockSpec((1, HEADS, PROJ_ROWS, HEAD_W), lambda b, i: (b, 0, i, 0)),
            pl.BlockSpec((1, HEADS, PROJ_ROWS, HEAD_W), lambda b, i: (b, 0, i, 0)),
            pl.BlockSpec((1, HEADS, PROJ_ROWS, V_DIM), lambda b, i: (b, 0, i, 0)),
        ),
        compiler_params=_cparams("arbitrary", "arbitrary"),
        name="mla_projections",
    )(x, m, pos, freq, w_in, qn, kvn, w_qb, w_kvb)


def _attn_kernel(q_ref, k_ref, v_ref, o_ref):
    i = pl.program_id(1)
    blk = ATTN_BLK
    row = lax.broadcasted_iota(jnp.int32, (blk, blk), 0)
    col = lax.broadcasted_iota(jnp.int32, (blk, blk), 1)
    for hd in range(HEADS):
        q = q_ref[0, hd]

        def scores(j):
            kj = k_ref[0, hd, pl.ds(pl.multiple_of(j * blk, blk), blk), :]
            return lax.dot_general(q, kj, (((1,), (1,)), ((), ())), preferred_element_type=F32)

        def update(j, s, carry):
            m, l, acc = carry
            vj = v_ref[0, hd, pl.ds(pl.multiple_of(j * blk, blk), blk), :]
            m_new = jnp.maximum(m, jnp.max(s, axis=-1, keepdims=True))
            p = jnp.exp(s - m_new)
            a = jnp.exp(m - m_new)
            l = a * l + jnp.sum(p, axis=-1, keepdims=True)
            acc = a * acc + _bdot(p.astype(BF16), vj)
            return m_new, l, acc

        init = (jnp.full((blk, 1), NEG_BIG, F32), jnp.zeros((blk, 1), F32), jnp.zeros((blk, V_DIM), F32))
        carry = lax.fori_loop(0, i, lambda j, c: update(j, scores(j), c), init)
        s_diag = jnp.where(col <= row, scores(i), NEG_BIG)
        _, l, acc = update(i, s_diag, carry)
        o_ref[0, :, hd * V_DIM:(hd + 1) * V_DIM] = (acc / l).astype(BF16)


def _attention(q, k, v):
    nq = SEQ // ATTN_BLK
    return pl.pallas_call(
        _attn_kernel,
        out_shape=jax.ShapeDtypeStruct((NB, SEQ, HEADS * V_DIM), BF16),
        grid=(NB, nq),
        in_specs=[
            pl.BlockSpec((1, HEADS, ATTN_BLK, HEAD_W), lambda b, i: (b, 0, i, 0)),
            pl.BlockSpec((1, HEADS, SEQ, HEAD_W), lambda b, i: (b, 0, 0, 0)),
            pl.BlockSpec((1, HEADS, SEQ, V_DIM), lambda b, i: (b, 0, 0, 0)),
        ],
        out_specs=pl.BlockSpec((1, ATTN_BLK, HEADS * V_DIM), lambda b, i: (b, i, 0)),
        compiler_params=_cparams("arbitrary", "arbitrary"),
        name="mla_attention",
    )(q, k, v)


def _attn_out_kernel(o_ref, x_ref, m_ref, w_ref, g_ref, b_ref, y_ref):
    gate = m_ref[0, 2:3, :]
    y = _bdot(o_ref[0], w_ref[...])
    y_ref[0] = _layer_norm(ALPHA * x_ref[0] + (1.0 + gate) * y, g_ref[...], b_ref[...])


def _attn_out(o, x, m, w_o, g, b):
    nt = SEQ // OUT_ROWS
    return pl.pallas_call(
        _attn_out_kernel,
        out_shape=jax.ShapeDtypeStruct((NB, SEQ, D), F32),
        grid=(NB, nt),
        in_specs=[
            pl.BlockSpec((1, OUT_ROWS, HEADS * V_DIM), lambda b, i: (b, i, 0)),
            pl.BlockSpec((1, OUT_ROWS, D), lambda b, i: (b, i, 0)),
            pl.BlockSpec((1, 3, D), lambda b, i: (b, 0, 0)),
            _const_spec(w_o.shape),
            _const_spec((1, D)),
            _const_spec((1, D)),
        ],
        out_specs=pl.BlockSpec((1, OUT_ROWS, D), lambda b, i: (b, i, 0)),
        compiler_params=_cparams("arbitrary", "arbitrary"),
        name="mla_out_proj_norm",
    )(o, x, m, w_o, g, b)


def _ffn_kernel(x_ref, m_ref, w1_ref, b1_ref, w2_ref, b2_ref, g_ref, b_ref, y_ref):
    shift = m_ref[0, 0:1, :]
    scale = m_ref[0, 1:2, :]
    gate = m_ref[0, 2:3, :]
    x = x_ref[...]
    h = (x * (1.0 + scale) + shift).astype(BF16)
    acc = jnp.zeros(x.shape, F32)
    for c0 in range(0, D_FF, FF_CHUNK):
        a = jnp.maximum(_bdot(h, w1_ref[:, c0:c0 + FF_CHUNK]) + b1_ref[:, c0:c0 + FF_CHUNK], 0.0)
        acc = acc + _bdot((a * a).astype(BF16), w2_ref[c0:c0 + FF_CHUNK, :])
    y = acc + b2_ref[...]
    y_ref[...] = _layer_norm(ALPHA * x + (1.0 + gate) * y, g_ref[...], b_ref[...])


def _ffn(x2d, in_map, out_shape2d, out_map, m, w1, b1, w2, b2, g, b):
    nt = SEQ // FFN_ROWS
    return pl.pallas_call(
        _ffn_kernel,
        out_shape=jax.ShapeDtypeStruct(out_shape2d, F32),
        grid=(NB, nt),
        in_specs=[
            pl.BlockSpec((FFN_ROWS, D), in_map),
            pl.BlockSpec((1, 3, D), lambda b, i: (b, 0, 0)),
            _const_spec(w1.shape),
            _const_spec((1, D_FF)),
            _const_spec(w2.shape),
            _const_spec((1, D)),
            _const_spec((1, D)),
            _const_spec((1, D)),
        ],
        out_specs=pl.BlockSpec((FFN_ROWS, D), out_map),
        compiler_params=_cparams("arbitrary", "arbitrary"),
        name="ffn_norm",
    )(x2d, m, w1, b1, w2, b2, g, b)


def _rows_bt(b, i):
    return (b * (SEQ // FFN_ROWS) + i, 0)


def _rows_tb(b, i):
    return (i, b)


def _s5_disc_kernel(ldt_ref, are_ref, aim_ref, bre_ref, bim_ref, abr_ref, abi_ref, bbr_ref, bbi_ref):
    lr = are_ref[...]
    li = aim_ref[...]
    dt = jnp.exp(ldt_ref[...])
    mag = jnp.exp(lr * dt)
    ab_re = mag * jnp.cos(li * dt)
    ab_im = mag * jnp.sin(li * dt)
    den = lr * lr + li * li
    nr = ab_re - 1.0
    coef_re = (nr * lr + ab_im * li) / den
    coef_im = (ab_im * lr - nr * li) / den
    br = bre_ref[...]
    bi = bim_ref[...]
    abr_ref[...] = ab_re
    abi_ref[...] = ab_im
    bbr_ref[...] = coef_re * br - coef_im * bi
    bbi_ref[...] = coef_re * bi + coef_im * br


def _s5_discretise(log_dt, a_re, a_im, b_re, b_im):
    g3 = (N_GROUPS, 1, STATE)
    gb = (N_GROUPS, GROUP_CH, STATE)
    return pl.pallas_call(
        _s5_disc_kernel,
        out_shape=(jax.ShapeDtypeStruct(g3, F32), jax.ShapeDtypeStruct(g3, F32),
                   jax.ShapeDtypeStruct(gb, F32), jax.ShapeDtypeStruct(gb, F32)),
        name="s5_discretise",
    )(log_dt.reshape(N_GROUPS, 1, 1), a_re.reshape(g3), a_im.reshape(g3),
      jnp.swapaxes(b_re, 1, 2), jnp.swapaxes(b_im, 1, 2))


def _s5_kernel(x_ref, m_ref, w_in_ref, bmat_ref, a_re_ref, a_im_ref, cmat_ref, d_ref,
               w_glu_ref, b_glu_ref, w_out_ref, g_ref, b_ref, y_ref, st_ref, state_ref):
    @pl.when(pl.program_id(0) == 0)
    def _():
        state_ref[...] = jnp.zeros(state_ref.shape, F32)

    shift = m_ref[:, 0, :]
    scale = m_ref[:, 1, :]
    gate = m_ref[:, 2, :]
    x = x_ref[...]
    x3 = x.reshape(S5_STEPS, NB, D)
    h = (x3 * (1.0 + scale)[None] + shift[None]).reshape(S5_ROWS, D).astype(BF16)
    u = _bdot(h, w_in_ref[...])
    u_bf = u.astype(BF16)

    ys = []
    for kt in range(N_TILES):
        st_ref[...] = _bdot(u_bf[:, kt * TILE_IN:(kt + 1) * TILE_IN], bmat_ref[kt])
        for c0 in range(0, TILE_ST, SCAN_W):
            ar = jnp.broadcast_to(a_re_ref[kt, :, c0:c0 + SCAN_W], (NB, SCAN_W))
            ai = jnp.broadcast_to(a_im_ref[kt, :, c0:c0 + SCAN_W], (NB, SCAN_W))

            def step(t, carry):
                xr, xi = carry
                r0 = pl.multiple_of(t * NB, NB)
                bur = st_ref[pl.ds(r0, NB), c0:c0 + SCAN_W]
                bui = st_ref[pl.ds(r0, NB), TILE_ST + c0:TILE_ST + c0 + SCAN_W]
                nxr = ar * xr - ai * xi + bur
                nxi = ar * xi + ai * xr + bui
                st_ref[pl.ds(r0, NB), c0:c0 + SCAN_W] = nxr
                st_ref[pl.ds(r0, NB), TILE_ST + c0:TILE_ST + c0 + SCAN_W] = nxi
                return nxr, nxi

            init = (state_ref[kt, :, c0:c0 + SCAN_W], state_ref[kt, :, TILE_ST + c0:TILE_ST + c0 + SCAN_W])
            xr, xi = lax.fori_loop(0, S5_STEPS, step, init, unroll=4)
            state_ref[kt, :, c0:c0 + SCAN_W] = xr
            state_ref[kt, :, TILE_ST + c0:TILE_ST + c0 + SCAN_W] = xi
        ys.append(_bdot(st_ref[...].astype(BF16), cmat_ref[kt]))
    y = jnp.concatenate(ys, axis=1) + d_ref[...] * u

    g = 0.5 * y * (1.0 + jnp.tanh(math.sqrt(2.0 / math.pi) * (y + 0.044715 * (y * y * y))))
    zz = g * jax.nn.sigmoid(_bdot(g.astype(BF16), w_glu_ref[...]) + b_glu_ref[...])
    out = _bdot(zz.astype(BF16), w_out_ref[...])
    res = (ALPHA * x3 + (1.0 + gate)[None] * out.reshape(S5_STEPS, NB, D)).reshape(S5_ROWS, D)
    y_ref[...] = _layer_norm(res, g_ref[...], b_ref[...])


def _s5_layer(x_tb, m, w_in, bmat, a_re, a_im, cmat, d_skip, w_glu, b_glu, w_out, g, b):
    return pl.pallas_call(
        _s5_kernel,
        out_shape=jax.ShapeDtypeStruct((SEQ * NB, D), F32),
        grid=(SEQ // S5_STEPS,),
        in_specs=[
            pl.BlockSpec((S5_ROWS, D), lambda i: (i, 0)),
            _const_spec((NB, 3, D)),
            _const_spec(w_in.shape),
            _const_spec(bmat.shape),
            _const_spec(a_re.shape),
            _const_spec(a_im.shape),
            _const_spec(cmat.shape),
            _const_spec((1, D)),
            _const_spec(w_glu.shape),
            _const_spec((1, D)),
            _const_spec(w_out.shape),
            _const_spec((1, D)),
            _const_spec((1, D)),
        ],
        out_specs=pl.BlockSpec((S5_ROWS, D), lambda i: (i, 0)),
        scratch_shapes=[
            pltpu.VMEM((S5_ROWS, 2 * TILE_ST), F32),
            pltpu.VMEM((N_TILES, NB, 2 * TILE_ST), F32),
        ],
        compiler_params=_cparams("arbitrary"),
        name="s5_mixer_norm",
    )(x_tb, m, w_in, bmat, a_re, a_im, cmat, d_skip, w_glu, b_glu, w_out, g, b)


def _rotate_half_cols(w):
    half = w.shape[-1] // 2
    return jnp.concatenate([-w[..., half:], w[..., :half]], axis=-1)


def _mla_weights(w_in, w_qb):
    k_pe = w_in[:, Q_LORA + KV_LORA:]
    w_in_ext = jnp.concatenate([w_in, _rotate_half_cols(k_pe)], axis=1)
    wq = w_qb.reshape(Q_LORA, HEADS, QK_NOPE + QK_ROPE)
    wq_ext = jnp.concatenate([wq, _rotate_half_cols(wq[..., QK_NOPE:])], axis=-1)
    return w_in_ext.astype(BF16), wq_ext.reshape(Q_LORA, HEADS * HEAD_W).astype(BF16)


def _block_diag_tiles(t):
    n, gl, a, b = t.shape
    eye = jnp.eye(gl, dtype=t.dtype)
    return (t[:, :, :, None, :] * eye[None, :, None, :, None]).reshape(n, gl * a, gl * b)


def _s5_matrices(bb_re, bb_im, c_re, c_im):
    shp = (N_TILES, GROUPS_PER_TILE, GROUP_CH, STATE)
    bmat = jnp.concatenate([_block_diag_tiles(bb_re.reshape(shp)), _block_diag_tiles(bb_im.reshape(shp))], axis=2)
    ct_re = jnp.swapaxes(c_re, 1, 2).reshape(N_TILES, GROUPS_PER_TILE, STATE, GROUP_CH)
    ct_im = jnp.swapaxes(c_im, 1, 2).reshape(N_TILES, GROUPS_PER_TILE, STATE, GROUP_CH)
    cmat = jnp.concatenate([_block_diag_tiles(ct_re), -_block_diag_tiles(ct_im)], axis=1)
    return bmat.astype(BF16), cmat.astype(BF16)


def kernel(x, c, positions, mla_w_in, mla_q_norm, mla_w_qb, mla_kv_norm, mla_w_kvb, mla_w_o, ssm_w_in, ssm_log_dt, ssm_a_re, ssm_a_im, ssm_b_re, ssm_b_im, ssm_c_re, ssm_c_im, ssm_d, ssm_w_glu, ssm_b_glu, ssm_w_out, mlp_w1, mlp_b1, mlp_w2, mlp_b2, mod_mix_w, mod_mix_b, mod_ffn_w, mod_ffn_b, ln_mix_g, ln_mix_b, ln_ffn_g, ln_ffn_b):
    row = lambda v: v.reshape(1, -1)
    m_mix = _modulation(c, mod_mix_w, mod_mix_b).reshape(DEPTH, NB, 3, D)
    m_ffn = _modulation(c, mod_ffn_w, mod_ffn_b).reshape(DEPTH, NB, 3, D)

    inv_freq = ROPE_THETA ** (-jnp.arange(0, QK_ROPE, 2, dtype=F32) / QK_ROPE)
    freq = jnp.tile(inv_freq, 4).reshape(1, 128)
    w_in_ext, w_qb_ext = _mla_weights(mla_w_in[0], mla_w_qb[0])
    q, k, v = _mla_proj(x, m_mix[0], positions.reshape(NB, SEQ, 1), freq, w_in_ext,
                        row(mla_q_norm[0]), row(mla_kv_norm[0]), w_qb_ext, mla_w_kvb[0].astype(BF16))
    o = _attention(q, k, v)
    x1 = _attn_out(o, x, m_mix[0], mla_w_o[0].astype(BF16), row(ln_mix_g[0]), row(ln_mix_b[0]))

    x2 = _ffn(x1.reshape(NB * SEQ, D), _rows_bt, (SEQ, NB * D), _rows_tb, m_ffn[0],
              mlp_w1[0].astype(BF16), row(mlp_b1[0]), mlp_w2[0].astype(BF16), row(mlp_b2[0]),
              row(ln_ffn_g[0]), row(ln_ffn_b[0]))

    ab_re, ab_im, bb_re, bb_im = _s5_discretise(ssm_log_dt[0], ssm_a_re[0], ssm_a_im[0], ssm_b_re[0], ssm_b_im[0])
    bmat, cmat = _s5_matrices(bb_re, bb_im, ssm_c_re[0], ssm_c_im[0])
    x3 = _s5_layer(x2.reshape(SEQ * NB, D), m_mix[1], ssm_w_in[0].astype(BF16), bmat,
                   ab_re.reshape(N_TILES, 1, TILE_ST), ab_im.reshape(N_TILES, 1, TILE_ST), cmat,
                   row(ssm_d[0]), ssm_w_glu[0].astype(BF16), row(ssm_b_glu[0]), ssm_w_out[0].astype(BF16),
                   row(ln_mix_g[1]), row(ln_mix_b[1]))

    x4 = _ffn(x3.reshape(SEQ, NB * D), _rows_tb, (NB * SEQ, D), _rows_bt, m_ffn[1],
              mlp_w1[1].astype(BF16), row(mlp_b1[1]), mlp_w2[1].astype(BF16), row(mlp_b2[1]),
              row(ln_ffn_g[1]), row(ln_ffn_b[1]))
    return x4.reshape(NB, SEQ, D)
```

```python
import functools
import math

import jax
import jax.numpy as jnp
from jax import lax
from jax.experimental import pallas as pl
from jax.experimental.pallas import tpu as pltpu

F32 = jnp.float32
BF16 = jnp.bfloat16

D = 1024
NB = 8
SEQ = 2048
HEADS = 8
QK_NOPE = 128
QK_ROPE = 64
V_DIM = 128
Q_LORA = 256
KV_LORA = 128
ROPE_THETA = 10000.0
GROUP_CH = 16
N_GROUPS = 64
STATE = 64
D_FF = 4 * D
DEPTH = 2
ALPHA = (2 * DEPTH) ** 0.25
LN_EPS = 1e-5
RMS_EPS = 1e-6
Q_SCALE = math.log2(math.e) / math.sqrt(QK_NOPE + QK_ROPE)
NEG_BIG = -1e30

HEAD_W = 256
GROUPS_PER_TILE = 16
N_TILES = N_GROUPS // GROUPS_PER_TILE
TILE_IN = GROUPS_PER_TILE * GROUP_CH
TILE_ST = GROUPS_PER_TILE * STATE

VMEM_LIMIT = 56 * 1024 * 1024
PROJ_ROWS = 512
ATTN_BLK = 512
OUT_ROWS = 512
FFN_ROWS = 512
FF_CHUNK = 1024
S5_STEPS = 64
S5_ROWS = S5_STEPS * NB
SCAN_W = 512


def _cparams(*sem):
    return pltpu.CompilerParams(dimension_semantics=sem, vmem_limit_bytes=VMEM_LIMIT)


def _const_spec(shape):
    nd = len(shape)
    return pl.BlockSpec(shape, lambda *_: (0,) * nd, pipeline_mode=pl.Buffered(1))


def _layer_norm(v, g, b):
    mu = jnp.mean(v, axis=-1, keepdims=True)
    vc = v - mu
    var = jnp.mean(vc * vc, axis=-1, keepdims=True)
    return vc * lax.rsqrt(var + LN_EPS) * g + b


def _bdot(a, b):
    return jnp.dot(a, b, preferred_element_type=F32)


def _mod_kernel(c_ref, w_ref, b_ref, o_ref):
    c = c_ref[...]
    cs = c * jax.nn.sigmoid(c)
    o_ref[0] = _bdot(cs.astype(BF16), w_ref[0].astype(BF16)) + b_ref[0]


def _modulation(c, w, b):
    n_layers = w.shape[0]
    tn = 1024
    return pl.pallas_call(
        _mod_kernel,
        out_shape=jax.ShapeDtypeStruct((n_layers, NB, 3 * D), F32),
        grid=(n_layers, 3 * D // tn),
        in_specs=[
            pl.BlockSpec((NB, D), lambda l, j: (0, 0)),
            pl.BlockSpec((1, D, tn), lambda l, j: (l, 0, j)),
            pl.BlockSpec((1, 1, tn), lambda l, j: (l, 0, j)),
        ],
        out_specs=pl.BlockSpec((1, NB, tn), lambda l, j: (l, 0, j)),
        compiler_params=_cparams("arbitrary", "arbitrary"),
        name="adaln_modulation",
    )(c, w, b.reshape(n_layers, 1, 3 * D))


def _mla_proj_kernel(x_ref, m_ref, pos_ref, freq_ref, w_in_ref, qn_ref, kvn_ref, w_qb_ref, w_kvb_ref,
                     q_ref, k_ref, v_ref):
    shift = m_ref[0, 0:1, :]
    scale = m_ref[0, 1:2, :]
    h = (x_ref[0] * (1.0 + scale) + shift).astype(BF16)
    z = _bdot(h, w_in_ref[...])

    def rms(v, g):
        return v * lax.rsqrt(jnp.mean(v * v, axis=-1, keepdims=True) + RMS_EPS) * g

    cq = rms(z[:, :Q_LORA], qn_ref[...]).astype(BF16)
    ckv = rms(z[:, Q_LORA:Q_LORA + KV_LORA], kvn_ref[...]).astype(BF16)

    ang = pos_ref[0].astype(F32) * freq_ref[...]
    lane = lax.broadcasted_iota(jnp.int32, ang.shape, 1)
    mult = jnp.where(lane < QK_ROPE, jnp.cos(ang), jnp.sin(ang))

    def rope(slab):
        s = slab * mult
        return s + pltpu.roll(s, QK_ROPE, axis=1)

    k_rope = jnp.where(lane < QK_ROPE, rope(z[:, Q_LORA + KV_LORA:]), 0.0).astype(BF16)

    q_all = _bdot(cq, w_qb_ref[...])
    kv = _bdot(ckv, w_kvb_ref[...])
    for hd in range(HEADS):
        c0 = hd * HEAD_W
        q_ref[0, hd, :, 0:QK_NOPE] = (q_all[:, c0:c0 + QK_NOPE] * Q_SCALE).astype(BF16)
        q_ref[0, hd, :, QK_NOPE:HEAD_W] = (rope(q_all[:, c0 + QK_NOPE:c0 + HEAD_W]) * Q_SCALE).astype(BF16)
        k_ref[0, hd, :, 0:QK_NOPE] = kv[:, c0:c0 + QK_NOPE].astype(BF16)
        k_ref[0, hd, :, QK_NOPE:HEAD_W] = k_rope
        v_ref[0, hd] = kv[:, c0 + QK_NOPE:c0 + HEAD_W].astype(BF16)


def _mla_proj(x, m, pos, freq, w_in, qn, kvn, w_qb, w_kvb):
    nt = SEQ // PROJ_ROWS
    return pl.pallas_call(
        _mla_proj_kernel,
        out_shape=(
            jax.ShapeDtypeStruct((NB, HEADS, SEQ, HEAD_W), BF16),
            jax.ShapeDtypeStruct((NB, HEADS, SEQ, HEAD_W), BF16),
            jax.ShapeDtypeStruct((NB, HEADS, SEQ, V_DIM), BF16),
        ),
        grid=(NB, nt),
        in_specs=[
            pl.BlockSpec((1, PROJ_ROWS, D), lambda b, i: (b, i, 0)),
            pl.BlockSpec((1, 3, D), lambda b, i: (b, 0, 0)),
            pl.BlockSpec((1, PROJ_ROWS, 1), lambda b, i: (b, i, 0)),
            _const_spec((1, 128)),
            _const_spec(w_in.shape),
            _const_spec((1, Q_LORA)),
            _const_spec((1, KV_LORA)),
            _const_spec(w_qb.shape),
            _const_spec(w_kvb.shape),
        ],
        out_specs=(
            pl.BlockSpec((1, HEADS, PROJ_ROWS, HEAD_W), lambda b, i: (b, 0, i, 0)),
            pl.BlockSpec((1, HEADS, PROJ_ROWS, HEAD_W), lambda b, i: (b, 0, i, 0)),
            pl.BlockSpec((1, HEADS, PROJ_ROWS, V_DIM), lambda b, i: (b, 0, i, 0)),
        ),
        compiler_params=_cparams("arbitrary", "arbitrary"),
        name="mla_projections",
    )(x, m, pos, freq, w_in, qn, kvn, w_qb, w_kvb)


def _lane_groups(v):
    return [v[:, c:c + 128] for c in range(0, v.shape[1], 128)]


def _attn_kernel(q_ref, k_ref, v_ref, o_ref, s_ref):
    blk = ATTN_BLK
    nq = SEQ // blk
    row = lax.broadcasted_iota(jnp.int32, (blk, blk), 0)
    col = lax.broadcasted_iota(jnp.int32, (blk, blk), 1)

    def rows(n):
        return slice(n * blk, (n + 1) * blk)

    def slot(n, j):
        return n * (n + 1) // 2 + j

    def pass1(n):
        q = q_ref[rows(n), :]
        m = None
        for j in range(n + 1):
            s = lax.dot_general(q, k_ref[rows(j), :], (((1,), (1,)), ((), ())), preferred_element_type=F32)
            if j == n:
                s = jnp.where(col <= row, s, NEG_BIG)
            s_ref[slot(n, j)] = s
            m = functools.reduce(jnp.maximum, _lane_groups(s) + ([] if m is None else [m]))
        return jnp.broadcast_to(jnp.max(m, axis=-1, keepdims=True), (blk, 128))

    def pass2(n, mb):
        ps = [jnp.exp2(sg - mb) for j in range(n + 1) for sg in _lane_groups(s_ref[slot(n, j)])]
        l = jnp.sum(functools.reduce(jnp.add, ps), axis=-1, keepdims=True)
        acc = _bdot(jnp.concatenate(ps, axis=1).astype(BF16), v_ref[0:(n + 1) * blk, :])
        o_ref[rows(n), :] = (acc / l).astype(BF16)

    mb = pass1(0)
    for n in range(nq):
        mb_next = pass1(n + 1) if n + 1 < nq else None
        pass2(n, mb)
        mb = mb_next


def _attention(q, k, v):
    nq = SEQ // ATTN_BLK
    head_spec = lambda w: pl.BlockSpec((None, None, SEQ, w), lambda b, h: (b, h, 0, 0))
    return pl.pallas_call(
        _attn_kernel,
        out_shape=jax.ShapeDtypeStruct((NB, HEADS, SEQ, V_DIM), BF16),
        grid=(NB, HEADS),
        in_specs=[head_spec(HEAD_W), head_spec(HEAD_W), head_spec(V_DIM)],
        out_specs=head_spec(V_DIM),
        scratch_shapes=[pltpu.VMEM((nq * (nq + 1) // 2, ATTN_BLK, ATTN_BLK), F32)],
        compiler_params=_cparams("arbitrary", "arbitrary"),
        name="mla_attention",
    )(q, k, v)


def _attn_out_kernel(o_ref, x_ref, m_ref, w_ref, g_ref, b_ref, y_ref):
    gate = m_ref[0, 2:3, :]
    o = jnp.concatenate([o_ref[0, hd] for hd in range(HEADS)], axis=1)
    y = _bdot(o, w_ref[...])
    y_ref[0] = _layer_norm(ALPHA * x_ref[0] + (1.0 + gate) * y, g_ref[...], b_ref[...])


def _attn_out(o, x, m, w_o, g, b):
    nt = SEQ // OUT_ROWS
    return pl.pallas_call(
        _attn_out_kernel,
        out_shape=jax.ShapeDtypeStruct((NB, SEQ, D), F32),
        grid=(NB, nt),
        in_specs=[
            pl.BlockSpec((1, HEADS, OUT_ROWS, V_DIM), lambda b, i: (b, 0, i, 0)),
            pl.BlockSpec((1, OUT_ROWS, D), lambda b, i: (b, i, 0)),
            pl.BlockSpec((1, 3, D), lambda b, i: (b, 0, 0)),
            _const_spec(w_o.shape),
            _const_spec((1, D)),
            _const_spec((1, D)),
        ],
        out_specs=pl.BlockSpec((1, OUT_ROWS, D), lambda b, i: (b, i, 0)),
        compiler_params=_cparams("arbitrary", "arbitrary"),
        name="mla_out_proj_norm",
    )(o, x, m, w_o, g, b)


def _ffn_kernel(x_ref, m_ref, w1_ref, b1_ref, w2_ref, b2_ref, g_ref, b_ref, y_ref):
    shift = m_ref[0, 0:1, :]
    scale = m_ref[0, 1:2, :]
    gate = m_ref[0, 2:3, :]
    x = x_ref[...]
    h = (x * (1.0 + scale) + shift).astype(BF16)
    acc = jnp.zeros(x.shape, F32)
    for c0 in range(0, D_FF, FF_CHUNK):
        a = jnp.maximum(_bdot(h, w1_ref[:, c0:c0 + FF_CHUNK]) + b1_ref[:, c0:c0 + FF_CHUNK], 0.0)
        acc = acc + _bdot((a * a).astype(BF16), w2_ref[c0:c0 + FF_CHUNK, :])
    y = acc + b2_ref[...]
    y_ref[...] = _layer_norm(ALPHA * x + (1.0 + gate) * y, g_ref[...], b_ref[...])


def _ffn(x, m, w1, b1, w2, b2, g, b):
    nt = SEQ // FFN_ROWS
    x_spec = pl.BlockSpec((None, FFN_ROWS, D), lambda b, i: (b, i, 0))
    return pl.pallas_call(
        _ffn_kernel,
        out_shape=jax.ShapeDtypeStruct((NB, SEQ, D), F32),
        grid=(NB, nt),
        in_specs=[
            x_spec,
            pl.BlockSpec((1, 3, D), lambda b, i: (b, 0, 0)),
            _const_spec(w1.shape),
            _const_spec((1, D_FF)),
            _const_spec(w2.shape),
            _const_spec((1, D)),
            _const_spec((1, D)),
            _const_spec((1, D)),
        ],
        out_specs=x_spec,
        compiler_params=_cparams("arbitrary", "arbitrary"),
        name="ffn_norm",
    )(x, m, w1, b1, w2, b2, g, b)


def _s5_disc_kernel(ldt_ref, are_ref, aim_ref, bre_ref, bim_ref, abr_ref, abi_ref, bbr_ref, bbi_ref):
    lr = are_ref[...]
    li = aim_ref[...]
    dt = jnp.exp(ldt_ref[...])
    mag = jnp.exp(lr * dt)
    ab_re = mag * jnp.cos(li * dt)
    ab_im = mag * jnp.sin(li * dt)
    den = lr * lr + li * li
    nr = ab_re - 1.0
    coef_re = (nr * lr + ab_im * li) / den
    coef_im = (ab_im * lr - nr * li) / den
    br = bre_ref[...]
    bi = bim_ref[...]
    abr_ref[...] = ab_re
    abi_ref[...] = ab_im
    bbr_ref[...] = coef_re * br - coef_im * bi
    bbi_ref[...] = coef_re * bi + coef_im * br


def _s5_discretise(log_dt, a_re, a_im, b_re, b_im):
    g3 = (N_GROUPS, 1, STATE)
    gb = (N_GROUPS, GROUP_CH, STATE)
    return pl.pallas_call(
        _s5_disc_kernel,
        out_shape=(jax.ShapeDtypeStruct(g3, F32), jax.ShapeDtypeStruct(g3, F32),
                   jax.ShapeDtypeStruct(gb, F32), jax.ShapeDtypeStruct(gb, F32)),
        name="s5_discretise",
    )(log_dt.reshape(N_GROUPS, 1, 1), a_re.reshape(g3), a_im.reshape(g3),
      jnp.swapaxes(b_re, 1, 2), jnp.swapaxes(b_im, 1, 2))


def _s5_kernel(x_ref, m_ref, w_in_ref, bmat_ref, a_re_ref, a_im_ref, cmat_ref, d_ref,
               w_glu_ref, b_glu_ref, w_out_ref, g_ref, b_ref, y_ref, st_ref, state_ref):
    @pl.when(pl.program_id(0) == 0)
    def _():
        state_ref[...] = jnp.zeros(state_ref.shape, F32)

    shift = m_ref[:, 0, :]
    scale = m_ref[:, 1, :]
    gate = m_ref[:, 2, :]
    x3 = jnp.swapaxes(x_ref[...], 0, 1)
    h = (x3 * (1.0 + scale)[None] + shift[None]).reshape(S5_ROWS, D).astype(BF16)
    u = _bdot(h, w_in_ref[...])
    u_bf = u.astype(BF16)

    ys = []
    for kt in range(N_TILES):
        st_ref[...] = _bdot(u_bf[:, kt * TILE_IN:(kt + 1) * TILE_IN], bmat_ref[kt])
        for c0 in range(0, TILE_ST, SCAN_W):
            ar = jnp.broadcast_to(a_re_ref[kt, :, c0:c0 + SCAN_W], (NB, SCAN_W))
            ai = jnp.broadcast_to(a_im_ref[kt, :, c0:c0 + SCAN_W], (NB, SCAN_W))

            def step(t, carry):
                xr, xi = carry
                r0 = pl.multiple_of(t * NB, NB)
                bur = st_ref[pl.ds(r0, NB), c0:c0 + SCAN_W]
                bui = st_ref[pl.ds(r0, NB), TILE_ST + c0:TILE_ST + c0 + SCAN_W]
                nxr = ar * xr - ai * xi + bur
                nxi = ar * xi + ai * xr + bui
                st_ref[pl.ds(r0, NB), c0:c0 + SCAN_W] = nxr
                st_ref[pl.ds(r0, NB), TILE_ST + c0:TILE_ST + c0 + SCAN_W] = nxi
                return nxr, nxi

            init = (state_ref[kt, :, c0:c0 + SCAN_W], state_ref[kt, :, TILE_ST + c0:TILE_ST + c0 + SCAN_W])
            xr, xi = lax.fori_loop(0, S5_STEPS, step, init, unroll=4)
            state_ref[kt, :, c0:c0 + SCAN_W] = xr
            state_ref[kt, :, TILE_ST + c0:TILE_ST + c0 + SCAN_W] = xi
        ys.append(_bdot(st_ref[...].astype(BF16), cmat_ref[kt]))
    y = jnp.concatenate(ys, axis=1) + d_ref[...] * u

    g = 0.5 * y * (1.0 + jnp.tanh(math.sqrt(2.0 / math.pi) * (y + 0.044715 * (y * y * y))))
    zz = g * jax.nn.sigmoid(_bdot(g.astype(BF16), w_glu_ref[...]) + b_glu_ref[...])
    out = _bdot(zz.astype(BF16), w_out_ref[...])
    res = (ALPHA * x3 + (1.0 + gate)[None] * out.reshape(S5_STEPS, NB, D)).reshape(S5_ROWS, D)
    y = _layer_norm(res, g_ref[...], b_ref[...])
    y_ref[...] = jnp.swapaxes(y.reshape(S5_STEPS, NB, D), 0, 1)


def _s5_layer(x, m, w_in, bmat, a_re, a_im, cmat, d_skip, w_glu, b_glu, w_out, g, b):
    x_spec = pl.BlockSpec((NB, S5_STEPS, D), lambda i: (0, i, 0))
    return pl.pallas_call(
        _s5_kernel,
        out_shape=jax.ShapeDtypeStruct((NB, SEQ, D), F32),
        grid=(SEQ // S5_STEPS,),
        in_specs=[
            x_spec,
            _const_spec((NB, 3, D)),
            _const_spec(w_in.shape),
            _const_spec(bmat.shape),
            _const_spec(a_re.shape),
            _const_spec(a_im.shape),
            _const_spec(cmat.shape),
            _const_spec((1, D)),
            _const_spec(w_glu.shape),
            _const_spec((1, D)),
            _const_spec(w_out.shape),
            _const_spec((1, D)),
            _const_spec((1, D)),
        ],
        out_specs=x_spec,
        scratch_shapes=[
            pltpu.VMEM((S5_ROWS, 2 * TILE_ST), F32),
            pltpu.VMEM((N_TILES, NB, 2 * TILE_ST), F32),
        ],
        compiler_params=_cparams("arbitrary"),
        name="s5_mixer_norm",
    )(x, m, w_in, bmat, a_re, a_im, cmat, d_skip, w_glu, b_glu, w_out, g, b)


def _rotate_half_cols(w):
    half = w.shape[-1] // 2
    return jnp.concatenate([-w[..., half:], w[..., :half]], axis=-1)


def _mla_weights(w_in, w_qb):
    k_pe = w_in[:, Q_LORA + KV_LORA:]
    w_in_ext = jnp.concatenate([w_in, _rotate_half_cols(k_pe)], axis=1)
    wq = w_qb.reshape(Q_LORA, HEADS, QK_NOPE + QK_ROPE)
    wq_ext = jnp.concatenate([wq, _rotate_half_cols(wq[..., QK_NOPE:])], axis=-1)
    return w_in_ext.astype(BF16), wq_ext.reshape(Q_LORA, HEADS * HEAD_W).astype(BF16)


def _block_diag_tiles(t):
    n, gl, a, b = t.shape
    eye = jnp.eye(gl, dtype=t.dtype)
    return (t[:, :, :, None, :] * eye[None, :, None, :, None]).reshape(n, gl * a, gl * b)


def _s5_matrices(bb_re, bb_im, c_re, c_im):
    shp = (N_TILES, GROUPS_PER_TILE, GROUP_CH, STATE)
    bmat = jnp.concatenate([_block_diag_tiles(bb_re.reshape(shp)), _block_diag_tiles(bb_im.reshape(shp))], axis=2)
    ct_re = jnp.swapaxes(c_re, 1, 2).reshape(N_TILES, GROUPS_PER_TILE, STATE, GROUP_CH)
    ct_im = jnp.swapaxes(c_im, 1, 2).reshape(N_TILES, GROUPS_PER_TILE, STATE, GROUP_CH)
    cmat = jnp.concatenate([_block_diag_tiles(ct_re), -_block_diag_tiles(ct_im)], axis=1)
    return bmat.astype(BF16), cmat.astype(BF16)


def kernel(x, c, positions, mla_w_in, mla_q_norm, mla_w_qb, mla_kv_norm, mla_w_kvb, mla_w_o, ssm_w_in, ssm_log_dt, ssm_a_re, ssm_a_im, ssm_b_re, ssm_b_im, ssm_c_re, ssm_c_im, ssm_d, ssm_w_glu, ssm_b_glu, ssm_w_out, mlp_w1, mlp_b1, mlp_w2, mlp_b2, mod_mix_w, mod_mix_b, mod_ffn_w, mod_ffn_b, ln_mix_g, ln_mix_b, ln_ffn_g, ln_ffn_b):
    row = lambda v: v.reshape(1, -1)
    m_mix = _modulation(c, mod_mix_w, mod_mix_b).reshape(DEPTH, NB, 3, D)
    m_ffn = _modulation(c, mod_ffn_w, mod_ffn_b).reshape(DEPTH, NB, 3, D)

    inv_freq = ROPE_THETA ** (-jnp.arange(0, QK_ROPE, 2, dtype=F32) / QK_ROPE)
    freq = jnp.tile(inv_freq, 4).reshape(1, 128)
    w_in_ext, w_qb_ext = _mla_weights(mla_w_in[0], mla_w_qb[0])
    q, k, v = _mla_proj(x, m_mix[0], positions.reshape(NB, SEQ, 1), freq, w_in_ext,
                        row(mla_q_norm[0]), row(mla_kv_norm[0]), w_qb_ext, mla_w_kvb[0].astype(BF16))
    o = _attention(q, k, v)
    x1 = _attn_out(o, x, m_mix[0], mla_w_o[0].astype(BF16), row(ln_mix_g[0]), row(ln_mix_b[0]))

    x2 = _ffn(x1, m_ffn[0], mlp_w1[0].astype(BF16), row(mlp_b1[0]), mlp_w2[0].astype(BF16), row(mlp_b2[0]),
              row(ln_ffn_g[0]), row(ln_ffn_b[0]))

    ab_re, ab_im, bb_re, bb_im = _s5_discretise(ssm_log_dt[0], ssm_a_re[0], ssm_a_im[0], ssm_b_re[0], ssm_b_im[0])
    bmat, cmat = _s5_matrices(bb_re, bb_im, ssm_c_re[0], ssm_c_im[0])
    x3 = _s5_layer(x2, m_mix[1], ssm_w_in[0].astype(BF16), bmat,
                   ab_re.reshape(N_TILES, 1, TILE_ST), ab_im.reshape(N_TILES, 1, TILE_ST), cmat,
                   row(ssm_d[0]), ssm_w_glu[0].astype(BF16), row(ssm_b_glu[0]), ssm_w_out[0].astype(BF16),
                   row(ln_mix_g[1]), row(ln_mix_b[1]))

    return _ffn(x3, m_ffn[1], mlp_w1[1].astype(BF16), row(mlp_b1[1]), mlp_w2[1].astype(BF16), row(mlp_b2[1]),
                row(ln_ffn_g[1]), row(ln_ffn_b[1]))
```

```python
import functools
import math

import jax
import jax.numpy as jnp
from jax import lax
from jax.experimental import pallas as pl
from jax.experimental.pallas import tpu as pltpu

F32 = jnp.float32
BF16 = jnp.bfloat16

D = 1024
NB = 8
SEQ = 2048
HEADS = 8
QK_NOPE = 128
QK_ROPE = 64
V_DIM = 128
Q_LORA = 256
KV_LORA = 128
ROPE_THETA = 10000.0
GROUP_CH = 16
N_GROUPS = 64
STATE = 64
D_FF = 4 * D
DEPTH = 2
ALPHA = (2 * DEPTH) ** 0.25
LN_EPS = 1e-5
RMS_EPS = 1e-6
Q_SCALE = math.log2(math.e) / math.sqrt(QK_NOPE + QK_ROPE)
NEG_BIG = -1e30

HEAD_W = 256
GROUPS_PER_TILE = 16
N_TILES = N_GROUPS // GROUPS_PER_TILE
TILE_IN = GROUPS_PER_TILE * GROUP_CH
TILE_ST = GROUPS_PER_TILE * STATE

VMEM_LIMIT = 56 * 1024 * 1024
PROJ_ROWS = 512
ATTN_BLK = 512
OUT_ROWS = 512
FFN_ROWS = 512
FF_CHUNK = 1024
S5_STEPS = 64
S5_ROWS = S5_STEPS * NB
SCAN_W = 512


def _cparams(*sem):
    return pltpu.CompilerParams(dimension_semantics=sem, vmem_limit_bytes=VMEM_LIMIT)


def _const_spec(shape):
    nd = len(shape)
    return pl.BlockSpec(shape, lambda *_: (0,) * nd, pipeline_mode=pl.Buffered(1))


def _layer_norm(v, g, b):
    mu = jnp.mean(v, axis=-1, keepdims=True)
    vc = v - mu
    var = jnp.mean(vc * vc, axis=-1, keepdims=True)
    return vc * lax.rsqrt(var + LN_EPS) * g + b


def _bdot(a, b):
    return jnp.dot(a, b, preferred_element_type=F32)


def _mod_kernel(c_ref, w_ref, b_ref, o_ref):
    c = c_ref[...]
    cs = c * jax.nn.sigmoid(c)
    o_ref[0] = _bdot(cs.astype(BF16), w_ref[0].astype(BF16)) + b_ref[0]


def _modulation(c, w, b):
    n_layers = w.shape[0]
    tn = 1024
    return pl.pallas_call(
        _mod_kernel,
        out_shape=jax.ShapeDtypeStruct((n_layers, NB, 3 * D), F32),
        grid=(n_layers, 3 * D // tn),
        in_specs=[
            pl.BlockSpec((NB, D), lambda l, j: (0, 0)),
            pl.BlockSpec((1, D, tn), lambda l, j: (l, 0, j)),
            pl.BlockSpec((1, 1, tn), lambda l, j: (l, 0, j)),
        ],
        out_specs=pl.BlockSpec((1, NB, tn), lambda l, j: (l, 0, j)),
        compiler_params=_cparams("arbitrary", "arbitrary"),
        name="adaln_modulation",
    )(c, w, b.reshape(n_layers, 1, 3 * D))


def _mla_proj_kernel(x_ref, m_ref, pos_ref, freq_ref, w_in_ref, qn_ref, kvn_ref, w_qb_ref, w_kvb_ref,
                     q_ref, k_ref, v_ref):
    shift = m_ref[0, 0:1, :]
    scale = m_ref[0, 1:2, :]
    h = (x_ref[0] * (1.0 + scale) + shift).astype(BF16)
    z = _bdot(h, w_in_ref[...])

    def rms(v, g):
        return v * lax.rsqrt(jnp.mean(v * v, axis=-1, keepdims=True) + RMS_EPS) * g

    cq = rms(z[:, :Q_LORA], qn_ref[...]).astype(BF16)
    ckv = rms(z[:, Q_LORA:Q_LORA + KV_LORA], kvn_ref[...]).astype(BF16)

    ang = pos_ref[0].astype(F32) * freq_ref[...]
    lane = lax.broadcasted_iota(jnp.int32, ang.shape, 1)
    mult = jnp.where(lane < QK_ROPE, jnp.cos(ang), jnp.sin(ang))

    def rope(slab):
        s = slab * mult
        return s + pltpu.roll(s, QK_ROPE, axis=1)

    k_rope = jnp.where(lane < QK_ROPE, rope(z[:, Q_LORA + KV_LORA:]), 0.0).astype(BF16)

    q_all = _bdot(cq, w_qb_ref[...])
    kv = _bdot(ckv, w_kvb_ref[...])
    for hd in range(HEADS):
        c0 = hd * HEAD_W
        q_ref[0, hd, :, 0:QK_NOPE] = (q_all[:, c0:c0 + QK_NOPE] * Q_SCALE).astype(BF16)
        q_ref[0, hd, :, QK_NOPE:HEAD_W] = (rope(q_all[:, c0 + QK_NOPE:c0 + HEAD_W]) * Q_SCALE).astype(BF16)
        k_ref[0, hd, :, 0:QK_NOPE] = kv[:, c0:c0 + QK_NOPE].astype(BF16)
        k_ref[0, hd, :, QK_NOPE:HEAD_W] = k_rope
        v_ref[0, hd] = kv[:, c0 + QK_NOPE:c0 + HEAD_W].astype(BF16)


def _mla_proj(x, m, pos, freq, w_in, qn, kvn, w_qb, w_kvb):
    nt = SEQ // PROJ_ROWS
    return pl.pallas_call(
        _mla_proj_kernel,
        out_shape=(
            jax.ShapeDtypeStruct((NB, HEADS, SEQ, HEAD_W), BF16),
            jax.ShapeDtypeStruct((NB, HEADS, SEQ, HEAD_W), BF16),
            jax.ShapeDtypeStruct((NB, HEADS, SEQ, V_DIM), BF16),
        ),
        grid=(NB, nt),
        in_specs=[
            pl.BlockSpec((1, PROJ_ROWS, D), lambda b, i: (b, i, 0)),
            pl.BlockSpec((1, 3, D), lambda b, i: (b, 0, 0)),
            pl.BlockSpec((1, PROJ_ROWS, 1), lambda b, i: (b, i, 0)),
            _const_spec((1, 128)),
            _const_spec(w_in.shape),
            _const_spec((1, Q_LORA)),
            _const_spec((1, KV_LORA)),
            _const_spec(w_qb.shape),
            _const_spec(w_kvb.shape),
        ],
        out_specs=(
            pl.BlockSpec((1, HEADS, PROJ_ROWS, HEAD_W), lambda b, i: (b, 0, i, 0)),
            pl.BlockSpec((1, HEADS, PROJ_ROWS, HEAD_W), lambda b, i: (b, 0, i, 0)),
            pl.BlockSpec((1, HEADS, PROJ_ROWS, V_DIM), lambda b, i: (b, 0, i, 0)),
        ),
        compiler_params=_cparams("arbitrary", "arbitrary"),
        name="mla_projections",
    )(x, m, pos, freq, w_in, qn, kvn, w_qb, w_kvb)


def _lane_groups(v):
    return [v[:, c:c + 128] for c in range(0, v.shape[1], 128)]


def _attn_kernel(q_ref, k_ref, v_ref, o_ref, s_ref):
    blk = ATTN_BLK
    nq = SEQ // blk
    row = lax.broadcasted_iota(jnp.int32, (blk, blk), 0)
    col = lax.broadcasted_iota(jnp.int32, (blk, blk), 1)

    def rows(n):
        return slice(n * blk, (n + 1) * blk)

    def slot(n, j):
        return n * (n + 1) // 2 + j

    def pass1(n):
        q = q_ref[rows(n), :]
        m = None
        for j in range(n + 1):
            s = lax.dot_general(q, k_ref[rows(j), :], (((1,), (1,)), ((), ())), preferred_element_type=F32)
            if j == n:
                s = jnp.where(col <= row, s, NEG_BIG)
            s_ref[slot(n, j)] = s
            m = functools.reduce(jnp.maximum, _lane_groups(s) + ([] if m is None else [m]))
        return jnp.broadcast_to(jnp.max(m, axis=-1, keepdims=True), (blk, 128))

    def pass2(n, mb):
        ps = [jnp.exp2(sg - mb) for j in range(n + 1) for sg in _lane_groups(s_ref[slot(n, j)])]
        l = jnp.sum(functools.reduce(jnp.add, ps), axis=-1, keepdims=True)
        acc = _bdot(jnp.concatenate(ps, axis=1).astype(BF16), v_ref[0:(n + 1) * blk, :])
        o_ref[rows(n), :] = (acc / l).astype(BF16)

    mb = pass1(0)
    for n in range(nq):
        mb_next = pass1(n + 1) if n + 1 < nq else None
        pass2(n, mb)
        mb = mb_next


def _attention(q, k, v):
    nq = SEQ // ATTN_BLK
    head_spec = lambda w: pl.BlockSpec((None, None, SEQ, w), lambda b, h: (b, h, 0, 0))
    return pl.pallas_call(
        _attn_kernel,
        out_shape=jax.ShapeDtypeStruct((NB, HEADS, SEQ, V_DIM), BF16),
        grid=(NB, HEADS),
        in_specs=[head_spec(HEAD_W), head_spec(HEAD_W), head_spec(V_DIM)],
        out_specs=head_spec(V_DIM),
        scratch_shapes=[pltpu.VMEM((nq * (nq + 1) // 2, ATTN_BLK, ATTN_BLK), F32)],
        compiler_params=_cparams("arbitrary", "arbitrary"),
        name="mla_attention",
    )(q, k, v)


def _attn_out_kernel(o_ref, x_ref, m_ref, w_ref, g_ref, b_ref, y_ref):
    gate = m_ref[0, 2:3, :]
    o = jnp.concatenate([o_ref[0, hd] for hd in range(HEADS)], axis=1)
    y = _bdot(o, w_ref[...])
    y_ref[0] = _layer_norm(ALPHA * x_ref[0] + (1.0 + gate) * y, g_ref[...], b_ref[...])


def _attn_out(o, x, m, w_o, g, b):
    nt = SEQ // OUT_ROWS
    return pl.pallas_call(
        _attn_out_kernel,
        out_shape=jax.ShapeDtypeStruct((NB, SEQ, D), F32),
        grid=(NB, nt),
        in_specs=[
            pl.BlockSpec((1, HEADS, OUT_ROWS, V_DIM), lambda b, i: (b, 0, i, 0)),
            pl.BlockSpec((1, OUT_ROWS, D), lambda b, i: (b, i, 0)),
            pl.BlockSpec((1, 3, D), lambda b, i: (b, 0, 0)),
            _const_spec(w_o.shape),
            _const_spec((1, D)),
            _const_spec((1, D)),
        ],
        out_specs=pl.BlockSpec((1, OUT_ROWS, D), lambda b, i: (b, i, 0)),
        compiler_params=_cparams("arbitrary", "arbitrary"),
        name="mla_out_proj_norm",
    )(o, x, m, w_o, g, b)


def _ffn_kernel(x_ref, m_ref, w1_ref, b1_ref, w2_ref, b2_ref, g_ref, b_ref, y_ref):
    shift = m_ref[0, 0:1, :]
    scale = m_ref[0, 1:2, :]
    gate = m_ref[0, 2:3, :]
    x = x_ref[...]
    h = (x * (1.0 + scale) + shift).astype(BF16)
    acc = jnp.zeros(x.shape, F32)
    for c0 in range(0, D_FF, FF_CHUNK):
        a = jnp.maximum(_bdot(h, w1_ref[:, c0:c0 + FF_CHUNK]) + b1_ref[:, c0:c0 + FF_CHUNK], 0.0)
        acc = acc + _bdot((a * a).astype(BF16), w2_ref[c0:c0 + FF_CHUNK, :])
    y = acc + b2_ref[...]
    y_ref[...] = _layer_norm(ALPHA * x + (1.0 + gate) * y, g_ref[...], b_ref[...])


def _ffn(x, m, w1, b1, w2, b2, g, b):
    nt = SEQ // FFN_ROWS
    x_spec = pl.BlockSpec((None, FFN_ROWS, D), lambda b, i: (b, i, 0))
    return pl.pallas_call(
        _ffn_kernel,
        out_shape=jax.ShapeDtypeStruct((NB, SEQ, D), F32),
        grid=(NB, nt),
        in_specs=[
            x_spec,
            pl.BlockSpec((1, 3, D), lambda b, i: (b, 0, 0)),
            _const_spec(w1.shape),
            _const_spec((1, D_FF)),
            _const_spec(w2.shape),
            _const_spec((1, D)),
            _const_spec((1, D)),
            _const_spec((1, D)),
        ],
        out_specs=x_spec,
        compiler_params=_cparams("arbitrary", "arbitrary"),
        name="ffn_norm",
    )(x, m, w1, b1, w2, b2, g, b)


def _s5_disc_kernel(ldt_ref, are_ref, aim_ref, bre_ref, bim_ref, abr_ref, abi_ref, bbr_ref, bbi_ref):
    lr = are_ref[...]
    li = aim_ref[...]
    dt = jnp.exp(ldt_ref[...])
    mag = jnp.exp(lr * dt)
    ab_re = mag * jnp.cos(li * dt)
    ab_im = mag * jnp.sin(li * dt)
    den = lr * lr + li * li
    nr = ab_re - 1.0
    coef_re = (nr * lr + ab_im * li) / den
    coef_im = (ab_im * lr - nr * li) / den
    br = bre_ref[...]
    bi = bim_ref[...]
    abr_ref[...] = ab_re
    abi_ref[...] = ab_im
    bbr_ref[...] = coef_re * br - coef_im * bi
    bbi_ref[...] = coef_re * bi + coef_im * br


def _s5_discretise(log_dt, a_re, a_im, b_re, b_im):
    g3 = (N_GROUPS, 1, STATE)
    gb = (N_GROUPS, GROUP_CH, STATE)
    return pl.pallas_call(
        _s5_disc_kernel,
        out_shape=(jax.ShapeDtypeStruct(g3, F32), jax.ShapeDtypeStruct(g3, F32),
                   jax.ShapeDtypeStruct(gb, F32), jax.ShapeDtypeStruct(gb, F32)),
        name="s5_discretise",
    )(log_dt.reshape(N_GROUPS, 1, 1), a_re.reshape(g3), a_im.reshape(g3),
      jnp.swapaxes(b_re, 1, 2), jnp.swapaxes(b_im, 1, 2))


def _s5_kernel(x_ref, m_ref, w_in_ref, bmat_ref, a_re_ref, a_im_ref, cmat_ref, d_ref,
               w_glu_ref, b_glu_ref, w_out_ref, g_ref, b_ref, y_ref, state_ref):
    @pl.when(pl.program_id(0) == 0)
    def _():
        state_ref[...] = jnp.zeros(state_ref.shape, F32)

    shift = m_ref[:, 0, :]
    scale = m_ref[:, 1, :]
    gate = m_ref[:, 2, :]
    x3 = jnp.swapaxes(x_ref[...], 0, 1)
    h = (x3 * (1.0 + scale)[None] + shift[None]).reshape(S5_ROWS, D).astype(BF16)
    u = _bdot(h, w_in_ref[...])
    u_bf = u.astype(BF16)

    ys = []
    for kt in range(N_TILES):
        bu = _bdot(u_bf[:, kt * TILE_IN:(kt + 1) * TILE_IN], bmat_ref[kt])
        cols_r, cols_i = [], []
        for c0 in range(0, TILE_ST, SCAN_W):
            re_cols = slice(c0, c0 + SCAN_W)
            im_cols = slice(TILE_ST + c0, TILE_ST + c0 + SCAN_W)
            ar = jnp.broadcast_to(a_re_ref[kt, :, re_cols], (NB, SCAN_W))
            ai = jnp.broadcast_to(a_im_ref[kt, :, re_cols], (NB, SCAN_W))
            xr = state_ref[kt, :, re_cols]
            xi = state_ref[kt, :, im_cols]
            rows_r, rows_i = [], []
            for t in range(S5_STEPS):
                trow = slice(t * NB, (t + 1) * NB)
                xr, xi = ar * xr - ai * xi + bu[trow, re_cols], ar * xi + ai * xr + bu[trow, im_cols]
                rows_r.append(xr)
                rows_i.append(xi)
            state_ref[kt, :, re_cols] = xr
            state_ref[kt, :, im_cols] = xi
            cols_r.append(jnp.concatenate(rows_r, axis=0))
            cols_i.append(jnp.concatenate(rows_i, axis=0))
        states = jnp.concatenate(cols_r + cols_i, axis=1).astype(BF16)
        ys.append(_bdot(states, cmat_ref[kt]))
    y = jnp.concatenate(ys, axis=1) + d_ref[...] * u

    g = 0.5 * y * (1.0 + jnp.tanh(math.sqrt(2.0 / math.pi) * (y + 0.044715 * (y * y * y))))
    zz = g * jax.nn.sigmoid(_bdot(g.astype(BF16), w_glu_ref[...]) + b_glu_ref[...])
    out = _bdot(zz.astype(BF16), w_out_ref[...])
    res = (ALPHA * x3 + (1.0 + gate)[None] * out.reshape(S5_STEPS, NB, D)).reshape(S5_ROWS, D)
    y = _layer_norm(res, g_ref[...], b_ref[...])
    y_ref[...] = jnp.swapaxes(y.reshape(S5_STEPS, NB, D), 0, 1)


def _s5_layer(x, m, w_in, bmat, a_re, a_im, cmat, d_skip, w_glu, b_glu, w_out, g, b):
    x_spec = pl.BlockSpec((NB, S5_STEPS, D), lambda i: (0, i, 0))
    return pl.pallas_call(
        _s5_kernel,
        out_shape=jax.ShapeDtypeStruct((NB, SEQ, D), F32),
        grid=(SEQ // S5_STEPS,),
        in_specs=[
            x_spec,
            _const_spec((NB, 3, D)),
            _const_spec(w_in.shape),
            _const_spec(bmat.shape),
            _const_spec(a_re.shape),
            _const_spec(a_im.shape),
            _const_spec(cmat.shape),
            _const_spec((1, D)),
            _const_spec(w_glu.shape),
            _const_spec((1, D)),
            _const_spec(w_out.shape),
            _const_spec((1, D)),
            _const_spec((1, D)),
        ],
        out_specs=x_spec,
        scratch_shapes=[pltpu.VMEM((N_TILES, NB, 2 * TILE_ST), F32)],
        compiler_params=_cparams("arbitrary"),
        name="s5_mixer_norm",
    )(x, m, w_in, bmat, a_re, a_im, cmat, d_skip, w_glu, b_glu, w_out, g, b)


def _rotate_half_cols(w):
    half = w.shape[-1] // 2
    return jnp.concatenate([-w[..., half:], w[..., :half]], axis=-1)


def _mla_weights(w_in, w_qb):
    k_pe = w_in[:, Q_LORA + KV_LORA:]
    w_in_ext = jnp.concatenate([w_in, _rotate_half_cols(k_pe)], axis=1)
    wq = w_qb.reshape(Q_LORA, HEADS, QK_NOPE + QK_ROPE)
    wq_ext = jnp.concatenate([wq, _rotate_half_cols(wq[..., QK_NOPE:])], axis=-1)
    return w_in_ext.astype(BF16), wq_ext.reshape(Q_LORA, HEADS * HEAD_W).astype(BF16)


def _block_diag_tiles(t):
    n, gl, a, b = t.shape
    eye = jnp.eye(gl, dtype=t.dtype)
    return (t[:, :, :, None, :] * eye[None, :, None, :, None]).reshape(n, gl * a, gl * b)


def _s5_matrices(bb_re, bb_im, c_re, c_im):
    shp = (N_TILES, GROUPS_PER_TILE, GROUP_CH, STATE)
    bmat = jnp.concatenate([_block_diag_tiles(bb_re.reshape(shp)), _block_diag_tiles(bb_im.reshape(shp))], axis=2)
    ct_re = jnp.swapaxes(c_re, 1, 2).reshape(N_TILES, GROUPS_PER_TILE, STATE, GROUP_CH)
    ct_im = jnp.swapaxes(c_im, 1, 2).reshape(N_TILES, GROUPS_PER_TILE, STATE, GROUP_CH)
    cmat = jnp.concatenate([_block_diag_tiles(ct_re), -_block_diag_tiles(ct_im)], axis=1)
    return bmat.astype(BF16), cmat.astype(BF16)


def kernel(x, c, positions, mla_w_in, mla_q_norm, mla_w_qb, mla_kv_norm, mla_w_kvb, mla_w_o, ssm_w_in, ssm_log_dt, ssm_a_re, ssm_a_im, ssm_b_re, ssm_b_im, ssm_c_re, ssm_c_im, ssm_d, ssm_w_glu, ssm_b_glu, ssm_w_out, mlp_w1, mlp_b1, mlp_w2, mlp_b2, mod_mix_w, mod_mix_b, mod_ffn_w, mod_ffn_b, ln_mix_g, ln_mix_b, ln_ffn_g, ln_ffn_b):
    row = lambda v: v.reshape(1, -1)
    m_mix = _modulation(c, mod_mix_w, mod_mix_b).reshape(DEPTH, NB, 3, D)
    m_ffn = _modulation(c, mod_ffn_w, mod_ffn_b).reshape(DEPTH, NB, 3, D)

    inv_freq = ROPE_THETA ** (-jnp.arange(0, QK_ROPE, 2, dtype=F32) / QK_ROPE)
    freq = jnp.tile(inv_freq, 4).reshape(1, 128)
    w_in_ext, w_qb_ext = _mla_weights(mla_w_in[0], mla_w_qb[0])
    q, k, v = _mla_proj(x, m_mix[0], positions.reshape(NB, SEQ, 1), freq, w_in_ext,
                        row(mla_q_norm[0]), row(mla_kv_norm[0]), w_qb_ext, mla_w_kvb[0].astype(BF16))
    o = _attention(q, k, v)
    x1 = _attn_out(o, x, m_mix[0], mla_w_o[0].astype(BF16), row(ln_mix_g[0]), row(ln_mix_b[0]))

    x2 = _ffn(x1, m_ffn[0], mlp_w1[0].astype(BF16), row(mlp_b1[0]), mlp_w2[0].astype(BF16), row(mlp_b2[0]),
              row(ln_ffn_g[0]), row(ln_ffn_b[0]))

    ab_re, ab_im, bb_re, bb_im = _s5_discretise(ssm_log_dt[0], ssm_a_re[0], ssm_a_im[0], ssm_b_re[0], ssm_b_im[0])
    bmat, cmat = _s5_matrices(bb_re, bb_im, ssm_c_re[0], ssm_c_im[0])
    x3 = _s5_layer(x2, m_mix[1], ssm_w_in[0].astype(BF16), bmat,
                   ab_re.reshape(N_TILES, 1, TILE_ST), ab_im.reshape(N_TILES, 1, TILE_ST), cmat,
                   row(ssm_d[0]), ssm_w_glu[0].astype(BF16), row(ssm_b_glu[0]), ssm_w_out[0].astype(BF16),
                   row(ln_mix_g[1]), row(ln_mix_b[1]))

    return _ffn(x3, m_ffn[1], mlp_w1[1].astype(BF16), row(mlp_b1[1]), mlp_w2[1].astype(BF16), row(mlp_b2[1]),
                row(ln_ffn_g[1]), row(ln_ffn_b[1]))
```

```python
import functools
import math

import jax
import jax.numpy as jnp
from jax import lax
from jax.experimental import pallas as pl
from jax.experimental.pallas import tpu as pltpu

F32 = jnp.float32
BF16 = jnp.bfloat16

D = 1024
NB = 8
SEQ = 2048
HEADS = 8
QK_NOPE = 128
QK_ROPE = 64
V_DIM = 128
Q_LORA = 256
KV_LORA = 128
ROPE_THETA = 10000.0
GROUP_CH = 16
N_GROUPS = 64
STATE = 64
D_FF = 4 * D
DEPTH = 2
ALPHA = (2 * DEPTH) ** 0.25
LN_EPS = 1e-5
RMS_EPS = 1e-6
Q_SCALE = math.log2(math.e) / math.sqrt(QK_NOPE + QK_ROPE)
NEG_BIG = -1e30

HEAD_W = 256
CHUNK = 8
N_PAIRS = N_GROUPS // 2
PAIR_CH = 2 * GROUP_CH
PAIR_W = CHUNK * PAIR_CH
assert PAIR_W == 4 * STATE

VMEM_LIMIT = 56 * 1024 * 1024
PROJ_ROWS = 512
ATTN_BLK = 512
OUT_ROWS = 512
FFN_ROWS = 512
FF_CHUNK = 1024
S5_STEPS = 64
S5_ROWS = S5_STEPS * NB
PAIRS_PER_STEP = 2


def _cparams(*sem):
    return pltpu.CompilerParams(dimension_semantics=sem, vmem_limit_bytes=VMEM_LIMIT)


def _const_spec(shape):
    nd = len(shape)
    return pl.BlockSpec(shape, lambda *_: (0,) * nd, pipeline_mode=pl.Buffered(1))


def _layer_norm(v, g, b):
    mu = jnp.mean(v, axis=-1, keepdims=True)
    vc = v - mu
    var = jnp.mean(vc * vc, axis=-1, keepdims=True)
    return vc * lax.rsqrt(var + LN_EPS) * g + b


def _bdot(a, b):
    return jnp.dot(a, b, preferred_element_type=F32)


def _mod_kernel(c_ref, w_ref, b_ref, o_ref):
    c = c_ref[...]
    cs = c * jax.nn.sigmoid(c)
    o_ref[0] = _bdot(cs.astype(BF16), w_ref[0].astype(BF16)) + b_ref[0]


def _modulation(c, w, b):
    n_layers = w.shape[0]
    tn = 1024
    return pl.pallas_call(
        _mod_kernel,
        out_shape=jax.ShapeDtypeStruct((n_layers, NB, 3 * D), F32),
        grid=(n_layers, 3 * D // tn),
        in_specs=[
            pl.BlockSpec((NB, D), lambda l, j: (0, 0)),
            pl.BlockSpec((1, D, tn), lambda l, j: (l, 0, j)),
            pl.BlockSpec((1, 1, tn), lambda l, j: (l, 0, j)),
        ],
        out_specs=pl.BlockSpec((1, NB, tn), lambda l, j: (l, 0, j)),
        compiler_params=_cparams("arbitrary", "arbitrary"),
        name="adaln_modulation",
    )(c, w, b.reshape(n_layers, 1, 3 * D))


def _mla_proj_kernel(x_ref, m_ref, pos_ref, freq_ref, w_in_ref, qn_ref, kvn_ref, w_qb_ref, w_kvb_ref,
                     q_ref, k_ref, v_ref):
    shift = m_ref[0, 0:1, :]
    scale = m_ref[0, 1:2, :]
    h = (x_ref[0] * (1.0 + scale) + shift).astype(BF16)
    z = _bdot(h, w_in_ref[...])

    def rms(v, g):
        return v * lax.rsqrt(jnp.mean(v * v, axis=-1, keepdims=True) + RMS_EPS) * g

    cq = rms(z[:, :Q_LORA], qn_ref[...]).astype(BF16)
    ckv = rms(z[:, Q_LORA:Q_LORA + KV_LORA], kvn_ref[...]).astype(BF16)

    ang = pos_ref[0].astype(F32) * freq_ref[...]
    lane = lax.broadcasted_iota(jnp.int32, ang.shape, 1)
    mult = jnp.where(lane < QK_ROPE, jnp.cos(ang), jnp.sin(ang))

    def rope(slab):
        s = slab * mult
        return s + pltpu.roll(s, QK_ROPE, axis=1)

    k_rope = jnp.where(lane < QK_ROPE, rope(z[:, Q_LORA + KV_LORA:]), 0.0).astype(BF16)

    q_all = _bdot(cq, w_qb_ref[...])
    kv = _bdot(ckv, w_kvb_ref[...])
    for hd in range(HEADS):
        c0 = hd * HEAD_W
        q_ref[0, hd, :, 0:QK_NOPE] = (q_all[:, c0:c0 + QK_NOPE] * Q_SCALE).astype(BF16)
        q_ref[0, hd, :, QK_NOPE:HEAD_W] = (rope(q_all[:, c0 + QK_NOPE:c0 + HEAD_W]) * Q_SCALE).astype(BF16)
        k_ref[0, hd, :, 0:QK_NOPE] = kv[:, c0:c0 + QK_NOPE].astype(BF16)
        k_ref[0, hd, :, QK_NOPE:HEAD_W] = k_rope
        v_ref[0, hd] = kv[:, c0 + QK_NOPE:c0 + HEAD_W].astype(BF16)


def _mla_proj(x, m, pos, freq, w_in, qn, kvn, w_qb, w_kvb):
    nt = SEQ // PROJ_ROWS
    return pl.pallas_call(
        _mla_proj_kernel,
        out_shape=(
            jax.ShapeDtypeStruct((NB, HEADS, SEQ, HEAD_W), BF16),
            jax.ShapeDtypeStruct((NB, HEADS, SEQ, HEAD_W), BF16),
            jax.ShapeDtypeStruct((NB, HEADS, SEQ, V_DIM), BF16),
        ),
        grid=(NB, nt),
        in_specs=[
            pl.BlockSpec((1, PROJ_ROWS, D), lambda b, i: (b, i, 0)),
            pl.BlockSpec((1, 3, D), lambda b, i: (b, 0, 0)),
            pl.BlockSpec((1, PROJ_ROWS, 1), lambda b, i: (b, i, 0)),
            _const_spec((1, 128)),
            _const_spec(w_in.shape),
            _const_spec((1, Q_LORA)),
            _const_spec((1, KV_LORA)),
            _const_spec(w_qb.shape),
            _const_spec(w_kvb.shape),
        ],
        out_specs=(
            pl.BlockSpec((1, HEADS, PROJ_ROWS, HEAD_W), lambda b, i: (b, 0, i, 0)),
            pl.BlockSpec((1, HEADS, PROJ_ROWS, HEAD_W), lambda b, i: (b, 0, i, 0)),
            pl.BlockSpec((1, HEADS, PROJ_ROWS, V_DIM), lambda b, i: (b, 0, i, 0)),
        ),
        compiler_params=_cparams("arbitrary", "arbitrary"),
        name="mla_projections",
    )(x, m, pos, freq, w_in, qn, kvn, w_qb, w_kvb)


def _lane_groups(v):
    return [v[:, c:c + 128] for c in range(0, v.shape[1], 128)]


def _attn_kernel(q_ref, k_ref, v_ref, o_ref, s_ref):
    blk = ATTN_BLK
    nq = SEQ // blk
    row = lax.broadcasted_iota(jnp.int32, (blk, blk), 0)
    col = lax.broadcasted_iota(jnp.int32, (blk, blk), 1)

    def rows(n):
        return slice(n * blk, (n + 1) * blk)

    def slot(n, j):
        return n * (n + 1) // 2 + j

    def pass1(n):
        q = q_ref[rows(n), :]
        m = None
        for j in range(n + 1):
            s = lax.dot_general(q, k_ref[rows(j), :], (((1,), (1,)), ((), ())), preferred_element_type=F32)
            if j == n:
                s = jnp.where(col <= row, s, NEG_BIG)
            s_ref[slot(n, j)] = s
            m = functools.reduce(jnp.maximum, _lane_groups(s) + ([] if m is None else [m]))
        return jnp.broadcast_to(jnp.max(m, axis=-1, keepdims=True), (blk, 128))

    def pass2(n, mb):
        ps = [jnp.exp2(sg - mb) for j in range(n + 1) for sg in _lane_groups(s_ref[slot(n, j)])]
        l = jnp.sum(functools.reduce(jnp.add, ps), axis=-1, keepdims=True)
        acc = _bdot(jnp.concatenate(ps, axis=1).astype(BF16), v_ref[0:(n + 1) * blk, :])
        o_ref[rows(n), :] = (acc / l).astype(BF16)

    mb = pass1(0)
    for n in range(nq):
        mb_next = pass1(n + 1) if n + 1 < nq else None
        pass2(n, mb)
        mb = mb_next


def _attention(q, k, v):
    nq = SEQ // ATTN_BLK
    head_spec = lambda w: pl.BlockSpec((None, None, SEQ, w), lambda b, h: (b, h, 0, 0))
    return pl.pallas_call(
        _attn_kernel,
        out_shape=jax.ShapeDtypeStruct((NB, HEADS, SEQ, V_DIM), BF16),
        grid=(NB, HEADS),
        in_specs=[head_spec(HEAD_W), head_spec(HEAD_W), head_spec(V_DIM)],
        out_specs=head_spec(V_DIM),
        scratch_shapes=[pltpu.VMEM((nq * (nq + 1) // 2, ATTN_BLK, ATTN_BLK), F32)],
        compiler_params=_cparams("arbitrary", "arbitrary"),
        name="mla_attention",
    )(q, k, v)


def _attn_out_kernel(o_ref, x_ref, m_ref, w_ref, g_ref, b_ref, y_ref):
    gate = m_ref[0, 2:3, :]
    o = jnp.concatenate([o_ref[0, hd] for hd in range(HEADS)], axis=1)
    y = _bdot(o, w_ref[...])
    y_ref[0] = _layer_norm(ALPHA * x_ref[0] + (1.0 + gate) * y, g_ref[...], b_ref[...])


def _attn_out(o, x, m, w_o, g, b):
    nt = SEQ // OUT_ROWS
    return pl.pallas_call(
        _attn_out_kernel,
        out_shape=jax.ShapeDtypeStruct((NB, SEQ, D), F32),
        grid=(NB, nt),
        in_specs=[
            pl.BlockSpec((1, HEADS, OUT_ROWS, V_DIM), lambda b, i: (b, 0, i, 0)),
            pl.BlockSpec((1, OUT_ROWS, D), lambda b, i: (b, i, 0)),
            pl.BlockSpec((1, 3, D), lambda b, i: (b, 0, 0)),
            _const_spec(w_o.shape),
            _const_spec((1, D)),
            _const_spec((1, D)),
        ],
        out_specs=pl.BlockSpec((1, OUT_ROWS, D), lambda b, i: (b, i, 0)),
        compiler_params=_cparams("arbitrary", "arbitrary"),
        name="mla_out_proj_norm",
    )(o, x, m, w_o, g, b)


def _ffn_kernel(x_ref, m_ref, w1_ref, b1_ref, w2_ref, b2_ref, g_ref, b_ref, y_ref):
    shift = m_ref[0, 0:1, :]
    scale = m_ref[0, 1:2, :]
    gate = m_ref[0, 2:3, :]
    x = x_ref[...]
    h = (x * (1.0 + scale) + shift).astype(BF16)
    acc = jnp.zeros(x.shape, F32)
    for c0 in range(0, D_FF, FF_CHUNK):
        a = jnp.maximum(_bdot(h, w1_ref[:, c0:c0 + FF_CHUNK]) + b1_ref[:, c0:c0 + FF_CHUNK], 0.0)
        acc = acc + _bdot((a * a).astype(BF16), w2_ref[c0:c0 + FF_CHUNK, :])
    y = acc + b2_ref[...]
    y_ref[...] = _layer_norm(ALPHA * x + (1.0 + gate) * y, g_ref[...], b_ref[...])


def _ffn(x, m, w1, b1, w2, b2, g, b):
    nt = SEQ // FFN_ROWS
    x_spec = pl.BlockSpec((None, FFN_ROWS, D), lambda b, i: (b, i, 0))
    return pl.pallas_call(
        _ffn_kernel,
        out_shape=jax.ShapeDtypeStruct((NB, SEQ, D), F32),
        grid=(NB, nt),
        in_specs=[
            x_spec,
            pl.BlockSpec((1, 3, D), lambda b, i: (b, 0, 0)),
            _const_spec(w1.shape),
            _const_spec((1, D_FF)),
            _const_spec(w2.shape),
            _const_spec((1, D)),
            _const_spec((1, D)),
            _const_spec((1, D)),
        ],
        out_specs=x_spec,
        compiler_params=_cparams("arbitrary", "arbitrary"),
        name="ffn_norm",
    )(x, m, w1, b1, w2, b2, g, b)


def _cmul(ar, ai, br, bi):
    return ar * br - ai * bi, ar * bi + ai * br


def _pair_slot(piece, zero, slot):
    return [piece, zero] if slot == 0 else [zero, piece]


def _s5_disc_kernel(ldt_ref, are_ref, aim_ref, bre_ref, bim_ref, cre_ref, cim_ref, d_ref,
                    a8r_ref, a8i_ref, wb_ref, wck_ref):
    lr = are_ref[...]
    li = aim_ref[...]
    dt = jnp.exp(ldt_ref[...])
    mag = jnp.exp(lr * dt)
    ab_re = mag * jnp.cos(li * dt)
    ab_im = mag * jnp.sin(li * dt)
    den = lr * lr + li * li
    nr = ab_re - 1.0
    coef_re = (nr * lr + ab_im * li) / den
    coef_im = (ab_im * lr - nr * li) / den
    bb_re, bb_im = _cmul(coef_re, coef_im, bre_ref[...], bim_ref[...])
    c_re = cre_ref[...]
    c_im = cim_ref[...]

    pw = [(jnp.ones_like(ab_re), jnp.zeros_like(ab_im))]
    for _ in range(CHUNK):
        pw.append(_cmul(pw[-1][0], pw[-1][1], ab_re, ab_im))
    a8r_ref[...] = pw[CHUNK][0]
    a8i_ref[...] = pw[CHUNK][1]

    def pairs(v):
        v4 = v.reshape(N_PAIRS, 2, v.shape[1], v.shape[2])
        return v4[:, 0], v4[:, 1]

    z64 = jnp.zeros((N_PAIRS, GROUP_CH, STATE), F32)
    z16 = jnp.zeros((N_PAIRS, GROUP_CH, GROUP_CH), F32)
    eye = (lax.broadcasted_iota(jnp.int32, (GROUP_CH, GROUP_CH), 0)
           == lax.broadcasted_iota(jnp.int32, (GROUP_CH, GROUP_CH), 1)).astype(F32)

    ab_l = [_cmul(pw[l][0], pw[l][1], bb_re, bb_im) for l in range(CHUNK)]
    wb_rows = []
    for k in range(CHUNK):
        m_re, m_im = ab_l[CHUNK - 1 - k]
        for slot in range(2):
            wb_rows.append(jnp.concatenate(_pair_slot(pairs(m_re)[slot], z64, slot)
                                           + _pair_slot(pairs(m_im)[slot], z64, slot), axis=2))
    wb_ref[...] = jnp.concatenate(wb_rows, axis=1).astype(BF16)

    def nt(a, b):
        return lax.dot_general(a, b, (((2,), (2,)), ((0,), (0,))), precision=lax.Precision.HIGHEST,
                               preferred_element_type=F32)

    k_l = [nt(c_re, ab_l[l][0]) - nt(c_im, ab_l[l][1]) for l in range(CHUNK)]
    k_l[0] = k_l[0] + d_ref[...] * eye[None]

    wck_rows = []
    for kp in range(CHUNK):
        ca_re, ca_im = _cmul(c_re, c_im, pw[kp + 1][0], pw[kp + 1][1])
        for slot in range(2):
            state_cols = (_pair_slot(pairs(ca_re)[slot], z64, slot)
                          + _pair_slot(-pairs(ca_im)[slot], z64, slot))
            direct_cols = []
            for k in range(CHUNK):
                blk = pairs(k_l[kp - k])[slot] if k <= kp else z16
                direct_cols += _pair_slot(blk, z16, slot)
            wck_rows.append(jnp.concatenate(state_cols + direct_cols, axis=2))
    wck_ref[...] = jnp.concatenate(wck_rows, axis=1).astype(BF16)


def _s5_discretise(log_dt, a_re, a_im, b_re, b_im, c_re, c_im, d_skip):
    g3 = (N_GROUPS, 1, STATE)
    return pl.pallas_call(
        _s5_disc_kernel,
        out_shape=(jax.ShapeDtypeStruct(g3, F32), jax.ShapeDtypeStruct(g3, F32),
                   jax.ShapeDtypeStruct((N_PAIRS, PAIR_W, PAIR_W), BF16),
                   jax.ShapeDtypeStruct((N_PAIRS, PAIR_W, 2 * PAIR_W), BF16)),
        compiler_params=pltpu.CompilerParams(vmem_limit_bytes=VMEM_LIMIT),
        name="s5_discretise",
    )(log_dt.reshape(N_GROUPS, 1, 1), a_re.reshape(g3), a_im.reshape(g3),
      jnp.swapaxes(b_re, 1, 2), jnp.swapaxes(b_im, 1, 2), c_re, c_im, d_skip.reshape(N_GROUPS, GROUP_CH, 1))


S5_CHUNKS = S5_STEPS // CHUNK
U8_ROWS = S5_CHUNKS * NB


def _s5_in_kernel(x_ref, m_ref, w_in_ref, u8_ref):
    shift = m_ref[:, 0, :]
    scale = m_ref[:, 1, :]
    x3 = jnp.swapaxes(x_ref[...], 0, 1)
    h = (x3 * (1.0 + scale)[None] + shift[None]).reshape(S5_ROWS, D).astype(BF16)
    u = _bdot(h, w_in_ref[...])
    u4 = u.reshape(S5_CHUNKS, CHUNK, NB, D)
    per_k = [u4[:, k].reshape(U8_ROWS, D) for k in range(CHUNK)]
    cols = [per_k[k][:, j * PAIR_CH:(j + 1) * PAIR_CH] for j in range(N_PAIRS) for k in range(CHUNK)]
    u8_ref[...] = jnp.concatenate(cols, axis=1).astype(BF16)


def _s5_in(x, m, w_in):
    return pl.pallas_call(
        _s5_in_kernel,
        out_shape=jax.ShapeDtypeStruct((SEQ // CHUNK * NB, N_PAIRS * PAIR_W), BF16),
        grid=(SEQ // S5_STEPS,),
        in_specs=[
            pl.BlockSpec((NB, S5_STEPS, D), lambda i: (0, i, 0)),
            _const_spec((NB, 3, D)),
            _const_spec(w_in.shape),
        ],
        out_specs=pl.BlockSpec((U8_ROWS, N_PAIRS * PAIR_W), lambda i: (i, 0)),
        compiler_params=_cparams("arbitrary"),
        name="s5_in_proj",
    )(x, m, w_in)


def _s5_core_kernel(u8_ref, wb_ref, wck_ref, a8r_ref, a8i_ref, y8_ref):
    half = PAIR_W // 2
    for jj in range(PAIRS_PER_STEP):
        lanes = slice(jj * PAIR_W, (jj + 1) * PAIR_W)
        u8 = u8_ref[:, lanes]
        v = _bdot(u8, wb_ref[jj])
        ar = jnp.broadcast_to(a8r_ref[:, jj * half:(jj + 1) * half], (NB, half))
        ai = jnp.broadcast_to(a8i_ref[:, jj * half:(jj + 1) * half], (NB, half))
        xr = jnp.zeros((NB, half), F32)
        xi = jnp.zeros((NB, half), F32)
        prev_r, prev_i = [], []
        for s in range(SEQ // CHUNK):
            prev_r.append(xr)
            prev_i.append(xi)
            rows = slice(s * NB, (s + 1) * NB)
            xr, xi = ar * xr - ai * xi + v[rows, :half], ar * xi + ai * xr + v[rows, half:]
        x_prev = jnp.concatenate([jnp.concatenate(prev_r, axis=0), jnp.concatenate(prev_i, axis=0)], axis=1)
        lhs = jnp.concatenate([x_prev.astype(BF16), u8], axis=1)
        y8_ref[:, lanes] = lax.dot_general(lhs, wck_ref[jj], (((1,), (1,)), ((), ())), preferred_element_type=F32)


def _s5_core(u8, wb, wck, a8r, a8i):
    n_rows = u8.shape[0]
    w = PAIRS_PER_STEP * PAIR_W
    return pl.pallas_call(
        _s5_core_kernel,
        out_shape=jax.ShapeDtypeStruct(u8.shape, F32),
        grid=(N_PAIRS // PAIRS_PER_STEP,),
        in_specs=[
            pl.BlockSpec((n_rows, w), lambda i: (0, i)),
            pl.BlockSpec((PAIRS_PER_STEP, PAIR_W, PAIR_W), lambda i: (i, 0, 0)),
            pl.BlockSpec((PAIRS_PER_STEP, PAIR_W, 2 * PAIR_W), lambda i: (i, 0, 0)),
            pl.BlockSpec((1, w // 2), lambda i: (0, i)),
            pl.BlockSpec((1, w // 2), lambda i: (0, i)),
        ],
        out_specs=pl.BlockSpec((n_rows, w), lambda i: (0, i)),
        compiler_params=_cparams("arbitrary"),
        name="s5_recurrence",
    )(u8, wb, wck, a8r, a8i)


def _s5_out_kernel(y8_ref, x_ref, m_ref, w_glu_ref, b_glu_ref, w_out_ref, g_ref, b_ref, o_ref):
    gate = m_ref[:, 2, :]
    y8 = y8_ref[...]
    per_k = [jnp.concatenate([y8[:, j * PAIR_W + k * PAIR_CH:j * PAIR_W + (k + 1) * PAIR_CH]
                              for j in range(N_PAIRS)], axis=1) for k in range(CHUNK)]
    y = jnp.stack([p.reshape(S5_CHUNKS, NB, D) for p in per_k], axis=1).reshape(S5_ROWS, D)

    g = 0.5 * y * (1.0 + jnp.tanh(math.sqrt(2.0 / math.pi) * (y + 0.044715 * (y * y * y))))
    zz = g * jax.nn.sigmoid(_bdot(g.astype(BF16), w_glu_ref[...]) + b_glu_ref[...])
    out = jnp.swapaxes(_bdot(zz.astype(BF16), w_out_ref[...]).reshape(S5_STEPS, NB, D), 0, 1)
    res = ALPHA * x_ref[...] + (1.0 + gate)[:, None, :] * out
    o_ref[...] = _layer_norm(res, g_ref[...], b_ref[...])


def _s5_out(y8, x, m, w_glu, b_glu, w_out, g, b):
    x_spec = pl.BlockSpec((NB, S5_STEPS, D), lambda i: (0, i, 0))
    return pl.pallas_call(
        _s5_out_kernel,
        out_shape=jax.ShapeDtypeStruct((NB, SEQ, D), F32),
        grid=(SEQ // S5_STEPS,),
        in_specs=[
            pl.BlockSpec((U8_ROWS, N_PAIRS * PAIR_W), lambda i: (i, 0)),
            x_spec,
            _const_spec((NB, 3, D)),
            _const_spec(w_glu.shape),
            _const_spec((1, D)),
            _const_spec(w_out.shape),
            _const_spec((1, D)),
            _const_spec((1, D)),
        ],
        out_specs=x_spec,
        compiler_params=_cparams("arbitrary"),
        name="s5_out_norm",
    )(y8, x, m, w_glu, b_glu, w_out, g, b)


def _rotate_half_cols(w):
    half = w.shape[-1] // 2
    return jnp.concatenate([-w[..., half:], w[..., :half]], axis=-1)


def _mla_weights(w_in, w_qb):
    k_pe = w_in[:, Q_LORA + KV_LORA:]
    w_in_ext = jnp.concatenate([w_in, _rotate_half_cols(k_pe)], axis=1)
    wq = w_qb.reshape(Q_LORA, HEADS, QK_NOPE + QK_ROPE)
    wq_ext = jnp.concatenate([wq, _rotate_half_cols(wq[..., QK_NOPE:])], axis=-1)
    return w_in_ext.astype(BF16), wq_ext.reshape(Q_LORA, HEADS * HEAD_W).astype(BF16)


def kernel(x, c, positions, mla_w_in, mla_q_norm, mla_w_qb, mla_kv_norm, mla_w_kvb, mla_w_o, ssm_w_in, ssm_log_dt, ssm_a_re, ssm_a_im, ssm_b_re, ssm_b_im, ssm_c_re, ssm_c_im, ssm_d, ssm_w_glu, ssm_b_glu, ssm_w_out, mlp_w1, mlp_b1, mlp_w2, mlp_b2, mod_mix_w, mod_mix_b, mod_ffn_w, mod_ffn_b, ln_mix_g, ln_mix_b, ln_ffn_g, ln_ffn_b):
    row = lambda v: v.reshape(1, -1)
    m_mix = _modulation(c, mod_mix_w, mod_mix_b).reshape(DEPTH, NB, 3, D)
    m_ffn = _modulation(c, mod_ffn_w, mod_ffn_b).reshape(DEPTH, NB, 3, D)

    inv_freq = ROPE_THETA ** (-jnp.arange(0, QK_ROPE, 2, dtype=F32) / QK_ROPE)
    freq = jnp.tile(inv_freq, 4).reshape(1, 128)
    w_in_ext, w_qb_ext = _mla_weights(mla_w_in[0], mla_w_qb[0])
    q, k, v = _mla_proj(x, m_mix[0], positions.reshape(NB, SEQ, 1), freq, w_in_ext,
                        row(mla_q_norm[0]), row(mla_kv_norm[0]), w_qb_ext, mla_w_kvb[0].astype(BF16))
    o = _attention(q, k, v)
    x1 = _attn_out(o, x, m_mix[0], mla_w_o[0].astype(BF16), row(ln_mix_g[0]), row(ln_mix_b[0]))

    x2 = _ffn(x1, m_ffn[0], mlp_w1[0].astype(BF16), row(mlp_b1[0]), mlp_w2[0].astype(BF16), row(mlp_b2[0]),
              row(ln_ffn_g[0]), row(ln_ffn_b[0]))

    a8r, a8i, wb, wck = _s5_discretise(ssm_log_dt[0], ssm_a_re[0], ssm_a_im[0], ssm_b_re[0], ssm_b_im[0],
                                       ssm_c_re[0], ssm_c_im[0], ssm_d[0])
    u8 = _s5_in(x2, m_mix[1], ssm_w_in[0].astype(BF16))
    y8 = _s5_core(u8, wb, wck, row(a8r), row(a8i))
    x3 = _s5_out(y8, x2, m_mix[1], ssm_w_glu[0].astype(BF16), row(ssm_b_glu[0]), ssm_w_out[0].astype(BF16),
                 row(ln_mix_g[1]), row(ln_mix_b[1]))

    return _ffn(x3, m_ffn[1], mlp_w1[1].astype(BF16), row(mlp_b1[1]), mlp_w2[1].astype(BF16), row(mlp_b2[1]),
                row(ln_ffn_g[1]), row(ln_ffn_b[1]))
```

```python
import functools
import math

import jax
import jax.numpy as jnp
from jax import lax
from jax.experimental import pallas as pl
from jax.experimental.pallas import tpu as pltpu

F32 = jnp.float32
BF16 = jnp.bfloat16

D = 1024
NB = 8
SEQ = 2048
HEADS = 8
QK_NOPE = 128
QK_ROPE = 64
V_DIM = 128
Q_LORA = 256
KV_LORA = 128
ROPE_THETA = 10000.0
GROUP_CH = 16
N_GROUPS = 64
STATE = 64
D_FF = 4 * D
DEPTH = 2
ALPHA = (2 * DEPTH) ** 0.25
LN_EPS = 1e-5
RMS_EPS = 1e-6
Q_SCALE = math.log2(math.e) / math.sqrt(QK_NOPE + QK_ROPE)
NEG_BIG = -1e30

HEAD_W = 256
CHUNK = 8
N_PAIRS = N_GROUPS // 2
PAIR_CH = 2 * GROUP_CH
PAIR_W = CHUNK * PAIR_CH
assert PAIR_W == 4 * STATE

VMEM_LIMIT = 56 * 1024 * 1024
PROJ_ROWS = 512
ATTN_BLK = 512
FFN_ROWS = 512
FFN_SUB = 2
FF_CHUNK = 1024
S5_STEPS = 64
S5_ROWS = S5_STEPS * NB
S5_SUB = 2
PAIRS_PER_STEP = 2


def _cparams(*sem):
    return pltpu.CompilerParams(dimension_semantics=sem, vmem_limit_bytes=VMEM_LIMIT)


def _const_spec(shape):
    nd = len(shape)
    return pl.BlockSpec(shape, lambda *_: (0,) * nd, pipeline_mode=pl.Buffered(1))


def _layer_norm(v, g, b):
    mu = jnp.mean(v, axis=-1, keepdims=True)
    vc = v - mu
    var = jnp.mean(vc * vc, axis=-1, keepdims=True)
    return vc * lax.rsqrt(var + LN_EPS) * g + b


def _bdot(a, b):
    return jnp.dot(a, b, preferred_element_type=F32)


def _mod_kernel(c_ref, w_ref, b_ref, o_ref):
    c = c_ref[...]
    cs = c * jax.nn.sigmoid(c)
    o_ref[...] = _bdot(cs.astype(BF16), w_ref[...].astype(BF16)) + b_ref[...]


def _modulation(c, w, b):
    n_layers = w.shape[0]
    return pl.pallas_call(
        _mod_kernel,
        out_shape=jax.ShapeDtypeStruct((n_layers, 3, NB, D), F32),
        grid=(n_layers, 3),
        in_specs=[
            pl.BlockSpec((NB, D), lambda l, j: (0, 0)),
            pl.BlockSpec((None, D, D), lambda l, j: (l, 0, j)),
            pl.BlockSpec((None, 1, D), lambda l, j: (l, 0, j)),
        ],
        out_specs=pl.BlockSpec((None, None, NB, D), lambda l, j: (l, j, 0, 0)),
        compiler_params=_cparams("arbitrary", "arbitrary"),
        name="adaln_modulation",
    )(c, w, b.reshape(n_layers, 1, 3 * D))


def _mod_spec(layer):
    return pl.BlockSpec((None, 3, NB, D), lambda *_: (layer, 0, 0, 0), pipeline_mode=pl.Buffered(1))


SHIFT, SCALE, GATE = 0, 1, 2


def _mla_proj_kernel(x_ref, m_ref, pos_ref, freq_ref, w_in_ref, qn_ref, kvn_ref, w_qb_ref, w_kvb_ref,
                     q_ref, kn_ref, kr_ref, v_ref):
    b = pl.program_id(0)
    shift = m_ref[SHIFT, pl.ds(b, 1), :]
    scale = m_ref[SCALE, pl.ds(b, 1), :]
    h = (x_ref[0] * (1.0 + scale) + shift).astype(BF16)
    z = _bdot(h, w_in_ref[...])

    def rms(v, g):
        return v * lax.rsqrt(jnp.mean(v * v, axis=-1, keepdims=True) + RMS_EPS) * g

    cq = rms(z[:, :Q_LORA], qn_ref[...]).astype(BF16)
    ckv = rms(z[:, Q_LORA:Q_LORA + KV_LORA], kvn_ref[...]).astype(BF16)

    ang = pos_ref[0].astype(F32) * freq_ref[...]
    lane = lax.broadcasted_iota(jnp.int32, ang.shape, 1)
    mult = jnp.where(lane < QK_ROPE, jnp.cos(ang), jnp.sin(ang))

    def rope(slab):
        s = slab * mult
        return s + pltpu.roll(s, QK_ROPE, axis=1)

    k_rope = jnp.where(lane < QK_ROPE, rope(z[:, Q_LORA + KV_LORA:]), 0.0).astype(BF16)

    q_all = _bdot(cq, w_qb_ref[...])
    kv = _bdot(ckv, w_kvb_ref[...])
    for hd in range(HEADS):
        c0 = hd * HEAD_W
        q_ref[0, hd, :, 0:QK_NOPE] = (q_all[:, c0:c0 + QK_NOPE] * Q_SCALE).astype(BF16)
        q_ref[0, hd, :, QK_NOPE:HEAD_W] = (rope(q_all[:, c0 + QK_NOPE:c0 + HEAD_W]) * Q_SCALE).astype(BF16)
        kn_ref[0, hd] = kv[:, c0:c0 + QK_NOPE].astype(BF16)
        v_ref[0, hd] = kv[:, c0 + QK_NOPE:c0 + HEAD_W].astype(BF16)
    kr_ref[0] = k_rope


def _mla_proj(x, m, pos, freq, w_in, qn, kvn, w_qb, w_kvb):
    nt = SEQ // PROJ_ROWS
    return pl.pallas_call(
        _mla_proj_kernel,
        out_shape=(
            jax.ShapeDtypeStruct((NB, HEADS, SEQ, HEAD_W), BF16),
            jax.ShapeDtypeStruct((NB, HEADS, SEQ, QK_NOPE), BF16),
            jax.ShapeDtypeStruct((NB, SEQ, HEAD_W - QK_NOPE), BF16),
            jax.ShapeDtypeStruct((NB, HEADS, SEQ, V_DIM), BF16),
        ),
        grid=(NB, nt),
        in_specs=[
            pl.BlockSpec((1, PROJ_ROWS, D), lambda b, i: (b, i, 0)),
            _mod_spec(0),
            pl.BlockSpec((1, PROJ_ROWS, 1), lambda b, i: (b, i, 0)),
            _const_spec((1, 128)),
            _const_spec(w_in.shape),
            _const_spec((1, Q_LORA)),
            _const_spec((1, KV_LORA)),
            _const_spec(w_qb.shape),
            _const_spec(w_kvb.shape),
        ],
        out_specs=(
            pl.BlockSpec((1, HEADS, PROJ_ROWS, HEAD_W), lambda b, i: (b, 0, i, 0)),
            pl.BlockSpec((1, HEADS, PROJ_ROWS, QK_NOPE), lambda b, i: (b, 0, i, 0)),
            pl.BlockSpec((1, PROJ_ROWS, HEAD_W - QK_NOPE), lambda b, i: (b, i, 0)),
            pl.BlockSpec((1, HEADS, PROJ_ROWS, V_DIM), lambda b, i: (b, 0, i, 0)),
        ),
        compiler_params=_cparams("arbitrary", "arbitrary"),
        name="mla_projections",
    )(x, m, pos, freq, w_in, qn, kvn, w_qb, w_kvb)


def _lane_groups(v):
    return [v[:, c:c + 128] for c in range(0, v.shape[1], 128)]


def _attn_kernel(q_ref, kn_ref, kr_ref, v_ref, o_ref, s_ref):
    blk = ATTN_BLK
    nq = SEQ // blk
    row = lax.broadcasted_iota(jnp.int32, (blk, blk), 0)
    col = lax.broadcasted_iota(jnp.int32, (blk, blk), 1)

    def rows(n):
        return slice(n * blk, (n + 1) * blk)

    def slot(n, j):
        return n * (n + 1) // 2 + j

    def pass1(n):
        q = q_ref[rows(n), :]
        m = None
        for j in range(n + 1):
            k = jnp.concatenate([kn_ref[rows(j), :], kr_ref[rows(j), :]], axis=1)
            s = lax.dot_general(q, k, (((1,), (1,)), ((), ())), preferred_element_type=F32)
            if j == n:
                s = jnp.where(col <= row, s, NEG_BIG)
            s_ref[slot(n, j)] = s
            m = functools.reduce(jnp.maximum, _lane_groups(s) + ([] if m is None else [m]))
        return jnp.broadcast_to(jnp.max(m, axis=-1, keepdims=True), (blk, 128))

    def pass2(n, mb):
        ps = [jnp.exp2(sg - mb) for j in range(n + 1) for sg in _lane_groups(s_ref[slot(n, j)])]
        l = jnp.sum(functools.reduce(jnp.add, ps), axis=-1, keepdims=True)
        acc = _bdot(jnp.concatenate(ps, axis=1).astype(BF16), v_ref[0:(n + 1) * blk, :])
        o_ref[rows(n), :] = (acc / l).astype(BF16)

    mb = pass1(0)
    for n in range(nq):
        mb_next = pass1(n + 1) if n + 1 < nq else None
        pass2(n, mb)
        mb = mb_next


def _attention(q, kn, kr, v):
    nq = SEQ // ATTN_BLK
    head_spec = lambda w: pl.BlockSpec((None, None, SEQ, w), lambda b, h: (b, h, 0, 0))
    kr_spec = pl.BlockSpec((None, SEQ, HEAD_W - QK_NOPE), lambda b, h: (b, 0, 0))
    return pl.pallas_call(
        _attn_kernel,
        out_shape=jax.ShapeDtypeStruct((NB, HEADS, SEQ, V_DIM), BF16),
        grid=(NB, HEADS),
        in_specs=[head_spec(HEAD_W), head_spec(QK_NOPE), kr_spec, head_spec(V_DIM)],
        out_specs=head_spec(V_DIM),
        scratch_shapes=[pltpu.VMEM((nq * (nq + 1) // 2, ATTN_BLK, ATTN_BLK), F32)],
        compiler_params=_cparams("arbitrary", "arbitrary"),
        name="mla_attention",
    )(q, kn, kr, v)


def _ffn_rows(x, shift, scale, gate, w1_ref, b1_ref, w2_ref, b2_ref, g_ref, b_ref):
    h = (x * (1.0 + scale) + shift).astype(BF16)
    acc = jnp.zeros(x.shape, F32)
    for c0 in range(0, D_FF, FF_CHUNK):
        a = jnp.maximum(_bdot(h, w1_ref[:, c0:c0 + FF_CHUNK]) + b1_ref[:, c0:c0 + FF_CHUNK], 0.0)
        acc = acc + _bdot((a * a).astype(BF16), w2_ref[c0:c0 + FF_CHUNK, :])
    y = acc + b2_ref[...]
    return _layer_norm(ALPHA * x + (1.0 + gate) * y, g_ref[...], b_ref[...])


def _batch_mod(m_ref):
    b = pl.program_id(0)
    return [m_ref[i, pl.ds(b, 1), :] for i in (SHIFT, SCALE, GATE)]


def _ffn_kernel(x_ref, m_ref, w1_ref, b1_ref, w2_ref, b2_ref, g_ref, b_ref, y_ref):
    mod = _batch_mod(m_ref)
    for sb in range(FFN_SUB):
        rows = slice(sb * FFN_ROWS, (sb + 1) * FFN_ROWS)
        y_ref[rows, :] = _ffn_rows(x_ref[rows, :], *mod, w1_ref, b1_ref, w2_ref, b2_ref, g_ref, b_ref)


def _attn_out_ffn_kernel(o_ref, x_ref, mm_ref, mf_ref, wo_ref, gm_ref, bm_ref,
                         w1_ref, b1_ref, w2_ref, b2_ref, g_ref, b_ref, y_ref):
    gate_mix = _batch_mod(mm_ref)[GATE]
    mod = _batch_mod(mf_ref)
    for sb in range(FFN_SUB):
        rows = slice(sb * FFN_ROWS, (sb + 1) * FFN_ROWS)
        o = jnp.concatenate([o_ref[hd, rows, :] for hd in range(HEADS)], axis=1)
        x1 = _layer_norm(ALPHA * x_ref[rows, :] + (1.0 + gate_mix) * _bdot(o, wo_ref[...]), gm_ref[...], bm_ref[...])
        y_ref[rows, :] = _ffn_rows(x1, *mod, w1_ref, b1_ref, w2_ref, b2_ref, g_ref, b_ref)


_FFN_X_SPEC = pl.BlockSpec((None, FFN_SUB * FFN_ROWS, D), lambda b, i: (b, i, 0))
_FFN_GRID = (NB, SEQ // (FFN_SUB * FFN_ROWS))


def _ffn_weight_specs():
    return [_const_spec((D, D_FF)), _const_spec((1, D_FF)), _const_spec((D_FF, D)), _const_spec((1, D)),
            _const_spec((1, D)), _const_spec((1, D))]


def _ffn(x, m, layer, w1, b1, w2, b2, g, b):
    return pl.pallas_call(
        _ffn_kernel,
        out_shape=jax.ShapeDtypeStruct((NB, SEQ, D), F32),
        grid=_FFN_GRID,
        in_specs=[_FFN_X_SPEC, _mod_spec(layer)] + _ffn_weight_specs(),
        out_specs=_FFN_X_SPEC,
        compiler_params=_cparams("arbitrary", "arbitrary"),
        name="ffn_norm",
    )(x, m, w1, b1, w2, b2, g, b)


def _attn_out_ffn(o, x, m_mix, m_ffn, layer, w_o, g_mix, b_mix, w1, b1, w2, b2, g, b):
    o_spec = pl.BlockSpec((None, HEADS, FFN_SUB * FFN_ROWS, V_DIM), lambda b, i: (b, 0, i, 0))
    return pl.pallas_call(
        _attn_out_ffn_kernel,
        out_shape=jax.ShapeDtypeStruct((NB, SEQ, D), F32),
        grid=_FFN_GRID,
        in_specs=[o_spec, _FFN_X_SPEC, _mod_spec(layer), _mod_spec(layer),
                  _const_spec(w_o.shape), _const_spec((1, D)), _const_spec((1, D))] + _ffn_weight_specs(),
        out_specs=_FFN_X_SPEC,
        compiler_params=_cparams("arbitrary", "arbitrary"),
        name="attn_out_ffn_norm",
    )(o, x, m_mix, m_ffn, w_o, g_mix, b_mix, w1, b1, w2, b2, g, b)


def _cmul(ar, ai, br, bi):
    return ar * br - ai * bi, ar * bi + ai * br


def _pair_slot(piece, zero, slot):
    return [piece, zero] if slot == 0 else [zero, piece]


def _s5_disc_kernel(ldt_ref, are_ref, aim_ref, bre_ref, bim_ref, cre_ref, cim_ref, d_ref,
                    a8r_ref, a8i_ref, wb_ref, wck_ref):
    lr = are_ref[...]
    li = aim_ref[...]
    dt = jnp.exp(ldt_ref[...])
    mag = jnp.exp(lr * dt)
    ab_re = mag * jnp.cos(li * dt)
    ab_im = mag * jnp.sin(li * dt)
    den = lr * lr + li * li
    nr = ab_re - 1.0
    coef_re = (nr * lr + ab_im * li) / den
    coef_im = (ab_im * lr - nr * li) / den
    bb_re, bb_im = _cmul(coef_re, coef_im, bre_ref[...], bim_ref[...])
    c_re = cre_ref[...]
    c_im = cim_ref[...]

    pw = [(jnp.ones_like(ab_re), jnp.zeros_like(ab_im))]
    for _ in range(CHUNK):
        pw.append(_cmul(pw[-1][0], pw[-1][1], ab_re, ab_im))
    a8r_ref[...] = pw[CHUNK][0]
    a8i_ref[...] = pw[CHUNK][1]

    def pairs(v):
        v4 = v.reshape(N_PAIRS, 2, v.shape[1], v.shape[2])
        return v4[:, 0], v4[:, 1]

    z64 = jnp.zeros((N_PAIRS, GROUP_CH, STATE), F32)
    z16 = jnp.zeros((N_PAIRS, GROUP_CH, GROUP_CH), F32)
    eye = (lax.broadcasted_iota(jnp.int32, (GROUP_CH, GROUP_CH), 0)
           == lax.broadcasted_iota(jnp.int32, (GROUP_CH, GROUP_CH), 1)).astype(F32)

    ab_l = [_cmul(pw[l][0], pw[l][1], bb_re, bb_im) for l in range(CHUNK)]
    wb_rows = []
    for k in range(CHUNK):
        m_re, m_im = ab_l[CHUNK - 1 - k]
        for slot in range(2):
            wb_rows.append(jnp.concatenate(_pair_slot(pairs(m_re)[slot], z64, slot)
                                           + _pair_slot(pairs(m_im)[slot], z64, slot), axis=2))
    wb_ref[...] = jnp.concatenate(wb_rows, axis=1).astype(BF16)

    def nt(a, b):
        return lax.dot_general(a, b, (((2,), (2,)), ((0,), (0,))), precision=lax.Precision.HIGHEST,
                               preferred_element_type=F32)

    k_l = [nt(c_re, ab_l[l][0]) - nt(c_im, ab_l[l][1]) for l in range(CHUNK)]
    k_l[0] = k_l[0] + d_ref[...] * eye[None]

    wck_rows = []
    for kp in range(CHUNK):
        ca_re, ca_im = _cmul(c_re, c_im, pw[kp + 1][0], pw[kp + 1][1])
        for slot in range(2):
            state_cols = (_pair_slot(pairs(ca_re)[slot], z64, slot)
                          + _pair_slot(-pairs(ca_im)[slot], z64, slot))
            direct_cols = []
            for k in range(CHUNK):
                blk = pairs(k_l[kp - k])[slot] if k <= kp else z16
                direct_cols += _pair_slot(blk, z16, slot)
            wck_rows.append(jnp.concatenate(state_cols + direct_cols, axis=2))
    wck_ref[...] = jnp.concatenate(wck_rows, axis=1).astype(BF16)


def _s5_discretise(log_dt, a_re, a_im, b_re, b_im, c_re, c_im, d_skip):
    g3 = (N_GROUPS, 1, STATE)
    return pl.pallas_call(
        _s5_disc_kernel,
        out_shape=(jax.ShapeDtypeStruct(g3, F32), jax.ShapeDtypeStruct(g3, F32),
                   jax.ShapeDtypeStruct((N_PAIRS, PAIR_W, PAIR_W), BF16),
                   jax.ShapeDtypeStruct((N_PAIRS, PAIR_W, 2 * PAIR_W), BF16)),
        compiler_params=pltpu.CompilerParams(vmem_limit_bytes=VMEM_LIMIT),
        name="s5_discretise",
    )(log_dt.reshape(N_GROUPS, 1, 1), a_re.reshape(g3), a_im.reshape(g3),
      jnp.swapaxes(b_re, 1, 2), jnp.swapaxes(b_im, 1, 2), c_re, c_im, d_skip.reshape(N_GROUPS, GROUP_CH, 1))


S5_CHUNKS = S5_STEPS // CHUNK
U8_ROWS = S5_CHUNKS * NB
S5_X_SPEC = pl.BlockSpec((NB, S5_SUB * S5_STEPS, D), lambda i: (0, i, 0))
S5_U8_SPEC = pl.BlockSpec((S5_SUB * U8_ROWS, N_PAIRS * PAIR_W), lambda i: (i, 0))


def _s5_in_kernel(x_ref, m_ref, w_in_ref, u8_ref):
    shift = m_ref[SHIFT]
    scale = m_ref[SCALE]
    for sb in range(S5_SUB):
        x3 = jnp.swapaxes(x_ref[:, sb * S5_STEPS:(sb + 1) * S5_STEPS, :], 0, 1)
        h = (x3 * (1.0 + scale)[None] + shift[None]).reshape(S5_ROWS, D).astype(BF16)
        u = _bdot(h, w_in_ref[...])
        u4 = u.reshape(S5_CHUNKS, CHUNK, NB, D)
        per_k = [u4[:, k].reshape(U8_ROWS, D) for k in range(CHUNK)]
        cols = [per_k[k][:, j * PAIR_CH:(j + 1) * PAIR_CH] for j in range(N_PAIRS) for k in range(CHUNK)]
        u8_ref[sb * U8_ROWS:(sb + 1) * U8_ROWS, :] = jnp.concatenate(cols, axis=1).astype(BF16)


def _s5_in(x, m, w_in):
    return pl.pallas_call(
        _s5_in_kernel,
        out_shape=jax.ShapeDtypeStruct((SEQ // CHUNK * NB, N_PAIRS * PAIR_W), BF16),
        grid=(SEQ // (S5_SUB * S5_STEPS),),
        in_specs=[
            S5_X_SPEC,
            _mod_spec(1),
            _const_spec(w_in.shape),
        ],
        out_specs=S5_U8_SPEC,
        compiler_params=_cparams("arbitrary"),
        name="s5_in_proj",
    )(x, m, w_in)


def _s5_core_kernel(u8_ref, wb_ref, wck_ref, a8r_ref, a8i_ref, y8_ref):
    half = PAIR_W // 2
    for jj in range(PAIRS_PER_STEP):
        lanes = slice(jj * PAIR_W, (jj + 1) * PAIR_W)
        u8 = u8_ref[:, lanes]
        v = _bdot(u8, wb_ref[jj])
        ar = jnp.broadcast_to(a8r_ref[:, jj * half:(jj + 1) * half], (NB, half))
        ai = jnp.broadcast_to(a8i_ref[:, jj * half:(jj + 1) * half], (NB, half))
        xr = jnp.zeros((NB, half), F32)
        xi = jnp.zeros((NB, half), F32)
        prev_r, prev_i = [], []
        for s in range(SEQ // CHUNK):
            prev_r.append(xr)
            prev_i.append(xi)
            rows = slice(s * NB, (s + 1) * NB)
            xr, xi = ar * xr - ai * xi + v[rows, :half], ar * xi + ai * xr + v[rows, half:]
        x_prev = jnp.concatenate([jnp.concatenate(prev_r, axis=0), jnp.concatenate(prev_i, axis=0)], axis=1)
        lhs = jnp.concatenate([x_prev.astype(BF16), u8], axis=1)
        y8_ref[:, lanes] = lax.dot_general(lhs, wck_ref[jj], (((1,), (1,)), ((), ())), preferred_element_type=F32)


def _s5_core(u8, wb, wck, a8r, a8i):
    n_rows = u8.shape[0]
    w = PAIRS_PER_STEP * PAIR_W
    return pl.pallas_call(
        _s5_core_kernel,
        out_shape=jax.ShapeDtypeStruct(u8.shape, F32),
        grid=(N_PAIRS // PAIRS_PER_STEP,),
        in_specs=[
            pl.BlockSpec((n_rows, w), lambda i: (0, i)),
            pl.BlockSpec((PAIRS_PER_STEP, PAIR_W, PAIR_W), lambda i: (i, 0, 0)),
            pl.BlockSpec((PAIRS_PER_STEP, PAIR_W, 2 * PAIR_W), lambda i: (i, 0, 0)),
            pl.BlockSpec((1, w // 2), lambda i: (0, i)),
            pl.BlockSpec((1, w // 2), lambda i: (0, i)),
        ],
        out_specs=pl.BlockSpec((n_rows, w), lambda i: (0, i)),
        compiler_params=_cparams("arbitrary"),
        name="s5_recurrence",
    )(u8, wb, wck, a8r, a8i)


def _s5_out_kernel(y8_ref, x_ref, m_ref, w_glu_ref, b_glu_ref, w_out_ref, g_ref, b_ref, o_ref):
    gate = m_ref[GATE]
    for sb in range(S5_SUB):
        steps = slice(sb * S5_STEPS, (sb + 1) * S5_STEPS)
        y8 = y8_ref[sb * U8_ROWS:(sb + 1) * U8_ROWS, :]
        per_k = [jnp.concatenate([y8[:, j * PAIR_W + k * PAIR_CH:j * PAIR_W + (k + 1) * PAIR_CH]
                                  for j in range(N_PAIRS)], axis=1) for k in range(CHUNK)]
        y = jnp.stack([p.reshape(S5_CHUNKS, NB, D) for p in per_k], axis=1).reshape(S5_ROWS, D)

        g = 0.5 * y * (1.0 + jnp.tanh(math.sqrt(2.0 / math.pi) * (y + 0.044715 * (y * y * y))))
        zz = g * jax.nn.sigmoid(_bdot(g.astype(BF16), w_glu_ref[...]) + b_glu_ref[...])
        out = jnp.swapaxes(_bdot(zz.astype(BF16), w_out_ref[...]).reshape(S5_STEPS, NB, D), 0, 1)
        res = ALPHA * x_ref[:, steps, :] + (1.0 + gate)[:, None, :] * out
        o_ref[:, steps, :] = _layer_norm(res, g_ref[...], b_ref[...])


def _s5_out(y8, x, m, w_glu, b_glu, w_out, g, b):
    x_spec = S5_X_SPEC
    return pl.pallas_call(
        _s5_out_kernel,
        out_shape=jax.ShapeDtypeStruct((NB, SEQ, D), F32),
        grid=(SEQ // (S5_SUB * S5_STEPS),),
        in_specs=[
            S5_U8_SPEC,
            x_spec,
            _mod_spec(1),
            _const_spec(w_glu.shape),
            _const_spec((1, D)),
            _const_spec(w_out.shape),
            _const_spec((1, D)),
            _const_spec((1, D)),
        ],
        out_specs=x_spec,
        compiler_params=_cparams("arbitrary"),
        name="s5_out_norm",
    )(y8, x, m, w_glu, b_glu, w_out, g, b)


def _rotate_half_cols(w):
    half = w.shape[-1] // 2
    return jnp.concatenate([-w[..., half:], w[..., :half]], axis=-1)


def _mla_weights(w_in, w_qb):
    k_pe = w_in[:, Q_LORA + KV_LORA:]
    w_in_ext = jnp.concatenate([w_in, _rotate_half_cols(k_pe)], axis=1)
    wq = w_qb.reshape(Q_LORA, HEADS, QK_NOPE + QK_ROPE)
    wq_ext = jnp.concatenate([wq, _rotate_half_cols(wq[..., QK_NOPE:])], axis=-1)
    return w_in_ext.astype(BF16), wq_ext.reshape(Q_LORA, HEADS * HEAD_W).astype(BF16)


def kernel(x, c, positions, mla_w_in, mla_q_norm, mla_w_qb, mla_kv_norm, mla_w_kvb, mla_w_o, ssm_w_in, ssm_log_dt, ssm_a_re, ssm_a_im, ssm_b_re, ssm_b_im, ssm_c_re, ssm_c_im, ssm_d, ssm_w_glu, ssm_b_glu, ssm_w_out, mlp_w1, mlp_b1, mlp_w2, mlp_b2, mod_mix_w, mod_mix_b, mod_ffn_w, mod_ffn_b, ln_mix_g, ln_mix_b, ln_ffn_g, ln_ffn_b):
    row = lambda v: v.reshape(1, -1)
    m_mix = _modulation(c, mod_mix_w, mod_mix_b)
    m_ffn = _modulation(c, mod_ffn_w, mod_ffn_b)

    inv_freq = ROPE_THETA ** (-jnp.arange(0, QK_ROPE, 2, dtype=F32) / QK_ROPE)
    freq = jnp.tile(inv_freq, 4).reshape(1, 128)
    w_in_ext, w_qb_ext = _mla_weights(mla_w_in[0], mla_w_qb[0])
    q, kn, kr, v = _mla_proj(x, m_mix, positions.reshape(NB, SEQ, 1), freq, w_in_ext,
                             row(mla_q_norm[0]), row(mla_kv_norm[0]), w_qb_ext, mla_w_kvb[0].astype(BF16))
    o = _attention(q, kn, kr, v)
    x2 = _attn_out_ffn(o, x, m_mix, m_ffn, 0, mla_w_o[0].astype(BF16), row(ln_mix_g[0]), row(ln_mix_b[0]),
                       mlp_w1[0].astype(BF16), row(mlp_b1[0]), mlp_w2[0].astype(BF16), row(mlp_b2[0]),
                       row(ln_ffn_g[0]), row(ln_ffn_b[0]))

    a8r, a8i, wb, wck = _s5_discretise(ssm_log_dt[0], ssm_a_re[0], ssm_a_im[0], ssm_b_re[0], ssm_b_im[0],
                                       ssm_c_re[0], ssm_c_im[0], ssm_d[0])
    u8 = _s5_in(x2, m_mix, ssm_w_in[0].astype(BF16))
    y8 = _s5_core(u8, wb, wck, row(a8r), row(a8i))
    x3 = _s5_out(y8, x2, m_mix, ssm_w_glu[0].astype(BF16), row(ssm_b_glu[0]), ssm_w_out[0].astype(BF16),
                 row(ln_mix_g[1]), row(ln_mix_b[1]))
    return _ffn(x3, m_ffn, 1, mlp_w1[1].astype(BF16), row(mlp_b1[1]), mlp_w2[1].astype(BF16), row(mlp_b2[1]),
                row(ln_ffn_g[1]), row(ln_ffn_b[1]))
```

```python
import functools
import math

import jax
import jax.numpy as jnp
from jax import lax
from jax.experimental import pallas as pl
from jax.experimental.pallas import tpu as pltpu

F32 = jnp.float32
BF16 = jnp.bfloat16

D = 1024
NB = 8
SEQ = 2048
HEADS = 8
QK_NOPE = 128
QK_ROPE = 64
V_DIM = 128
Q_LORA = 256
KV_LORA = 128
ROPE_THETA = 10000.0
GROUP_CH = 16
N_GROUPS = 64
STATE = 64
D_FF = 4 * D
DEPTH = 2
ALPHA = (2 * DEPTH) ** 0.25
LN_EPS = 1e-5
RMS_EPS = 1e-6
Q_SCALE = math.log2(math.e) / math.sqrt(QK_NOPE + QK_ROPE)
NEG_BIG = -1e30

HEAD_W = 256
CHUNK = 8
N_PAIRS = N_GROUPS // 2
PAIR_CH = 2 * GROUP_CH
PAIR_W = CHUNK * PAIR_CH
assert PAIR_W == 4 * STATE

VMEM_LIMIT = 56 * 1024 * 1024
PROJ_ROWS = 512
ATTN_BLK = 512
ATTN_HEADS = 2
FFN_ROWS = 512
FFN_SUB = 2
FF_CHUNK = 1024
S5_STEPS = 64
S5_ROWS = S5_STEPS * NB
S5_SUB = 2
PAIRS_PER_STEP = 2


def _cparams(*sem):
    return pltpu.CompilerParams(dimension_semantics=sem, vmem_limit_bytes=VMEM_LIMIT)


def _const_spec(shape):
    nd = len(shape)
    return pl.BlockSpec(shape, lambda *_: (0,) * nd, pipeline_mode=pl.Buffered(1))


def _layer_norm(v, g, b):
    mu = jnp.mean(v, axis=-1, keepdims=True)
    vc = v - mu
    var = jnp.mean(vc * vc, axis=-1, keepdims=True)
    return vc * lax.rsqrt(var + LN_EPS) * g + b


def _bdot(a, b):
    return jnp.dot(a, b, preferred_element_type=F32)


def _mod_kernel(c_ref, w_ref, b_ref, o_ref):
    c = c_ref[...]
    cs = c * jax.nn.sigmoid(c)
    o_ref[...] = _bdot(cs.astype(BF16), w_ref[...].astype(BF16)) + b_ref[...]


def _modulation(c, w, b):
    n_layers = w.shape[0]
    return pl.pallas_call(
        _mod_kernel,
        out_shape=jax.ShapeDtypeStruct((n_layers, 3, NB, D), F32),
        grid=(n_layers, 3),
        in_specs=[
            pl.BlockSpec((NB, D), lambda l, j: (0, 0)),
            pl.BlockSpec((None, D, D), lambda l, j: (l, 0, j)),
            pl.BlockSpec((None, 1, D), lambda l, j: (l, 0, j)),
        ],
        out_specs=pl.BlockSpec((None, None, NB, D), lambda l, j: (l, j, 0, 0)),
        compiler_params=_cparams("arbitrary", "arbitrary"),
        name="adaln_modulation",
    )(c, w, b.reshape(n_layers, 1, 3 * D))


def _mod_spec(layer):
    return pl.BlockSpec((None, 3, NB, D), lambda *_: (layer, 0, 0, 0), pipeline_mode=pl.Buffered(1))


SHIFT, SCALE, GATE = 0, 1, 2


def _mla_proj_kernel(x_ref, m_ref, pos_ref, freq_ref, w_in_ref, qn_ref, kvn_ref, w_qb_ref, w_kvb_ref,
                     q_ref, kn_ref, kr_ref, v_ref):
    b = pl.program_id(0)
    shift = m_ref[SHIFT, pl.ds(b, 1), :]
    scale = m_ref[SCALE, pl.ds(b, 1), :]
    h = (x_ref[0] * (1.0 + scale) + shift).astype(BF16)
    z = _bdot(h, w_in_ref[...])

    def rms(v, g):
        return v * lax.rsqrt(jnp.mean(v * v, axis=-1, keepdims=True) + RMS_EPS) * g

    cq = rms(z[:, :Q_LORA], qn_ref[...]).astype(BF16)
    ckv = rms(z[:, Q_LORA:Q_LORA + KV_LORA], kvn_ref[...]).astype(BF16)

    ang = pos_ref[0].astype(F32) * freq_ref[...]
    lane = lax.broadcasted_iota(jnp.int32, ang.shape, 1)
    mult = jnp.where(lane < QK_ROPE, jnp.cos(ang), jnp.sin(ang))

    def rope(slab):
        s = slab * mult
        return s + pltpu.roll(s, QK_ROPE, axis=1)

    k_rope = jnp.where(lane < QK_ROPE, rope(z[:, Q_LORA + KV_LORA:]), 0.0).astype(BF16)

    q_all = _bdot(cq, w_qb_ref[...])
    kv = _bdot(ckv, w_kvb_ref[...])
    for hd in range(HEADS):
        c0 = hd * HEAD_W
        q_ref[0, hd, :, 0:QK_NOPE] = (q_all[:, c0:c0 + QK_NOPE] * Q_SCALE).astype(BF16)
        q_ref[0, hd, :, QK_NOPE:HEAD_W] = (rope(q_all[:, c0 + QK_NOPE:c0 + HEAD_W]) * Q_SCALE).astype(BF16)
        kn_ref[0, hd] = kv[:, c0:c0 + QK_NOPE].astype(BF16)
        v_ref[0, hd] = kv[:, c0 + QK_NOPE:c0 + HEAD_W].astype(BF16)
    kr_ref[0] = k_rope


def _mla_proj(x, m, pos, freq, w_in, qn, kvn, w_qb, w_kvb):
    nt = SEQ // PROJ_ROWS
    return pl.pallas_call(
        _mla_proj_kernel,
        out_shape=(
            jax.ShapeDtypeStruct((NB, HEADS, SEQ, HEAD_W), BF16),
            jax.ShapeDtypeStruct((NB, HEADS, SEQ, QK_NOPE), BF16),
            jax.ShapeDtypeStruct((NB, SEQ, HEAD_W - QK_NOPE), BF16),
            jax.ShapeDtypeStruct((NB, HEADS, SEQ, V_DIM), BF16),
        ),
        grid=(NB, nt),
        in_specs=[
            pl.BlockSpec((1, PROJ_ROWS, D), lambda b, i: (b, i, 0)),
            _mod_spec(0),
            pl.BlockSpec((1, PROJ_ROWS, 1), lambda b, i: (b, i, 0)),
            _const_spec((1, 128)),
            _const_spec(w_in.shape),
            _const_spec((1, Q_LORA)),
            _const_spec((1, KV_LORA)),
            _const_spec(w_qb.shape),
            _const_spec(w_kvb.shape),
        ],
        out_specs=(
            pl.BlockSpec((1, HEADS, PROJ_ROWS, HEAD_W), lambda b, i: (b, 0, i, 0)),
            pl.BlockSpec((1, HEADS, PROJ_ROWS, QK_NOPE), lambda b, i: (b, 0, i, 0)),
            pl.BlockSpec((1, PROJ_ROWS, HEAD_W - QK_NOPE), lambda b, i: (b, i, 0)),
            pl.BlockSpec((1, HEADS, PROJ_ROWS, V_DIM), lambda b, i: (b, 0, i, 0)),
        ),
        compiler_params=_cparams("arbitrary", "arbitrary"),
        name="mla_projections",
    )(x, m, pos, freq, w_in, qn, kvn, w_qb, w_kvb)


def _lane_groups(v):
    return [v[:, c:c + 128] for c in range(0, v.shape[1], 128)]


def _attn_kernel(q_ref, kn_ref, kr_ref, v_ref, o_ref, s_ref):
    blk = ATTN_BLK
    nq = SEQ // blk
    row = lax.broadcasted_iota(jnp.int32, (blk, blk), 0)
    col = lax.broadcasted_iota(jnp.int32, (blk, blk), 1)

    def rows(n):
        return slice(n * blk, (n + 1) * blk)

    def slot(hd, n, j):
        return hd * (nq * (nq + 1) // 2) + n * (n + 1) // 2 + j

    def pass1(hd, n):
        q = q_ref[hd, rows(n), :]
        m = None
        for j in range(n + 1):
            k = jnp.concatenate([kn_ref[hd, rows(j), :], kr_ref[rows(j), :]], axis=1)
            s = lax.dot_general(q, k, (((1,), (1,)), ((), ())), preferred_element_type=F32)
            if j == n:
                s = jnp.where(col <= row, s, NEG_BIG)
            s_ref[slot(hd, n, j)] = s
            m = functools.reduce(jnp.maximum, _lane_groups(s) + ([] if m is None else [m]))
        return jnp.broadcast_to(jnp.max(m, axis=-1, keepdims=True), (blk, 128))

    def pass2(hd, n, mb):
        ps = [jnp.exp2(sg - mb) for j in range(n + 1) for sg in _lane_groups(s_ref[slot(hd, n, j)])]
        l = jnp.sum(functools.reduce(jnp.add, ps), axis=-1, keepdims=True)
        acc = _bdot(jnp.concatenate(ps, axis=1).astype(BF16), v_ref[hd, 0:(n + 1) * blk, :])
        o_ref[hd, rows(n), :] = (acc / l).astype(BF16)

    mb = [pass1(hd, 0) for hd in range(ATTN_HEADS)]
    for n in range(nq):
        for hd in range(ATTN_HEADS):
            mb_next = pass1(hd, n + 1) if n + 1 < nq else None
            pass2(hd, n, mb[hd])
            mb[hd] = mb_next


def _attention(q, kn, kr, v):
    nq = SEQ // ATTN_BLK
    head_spec = lambda w: pl.BlockSpec((None, ATTN_HEADS, SEQ, w), lambda b, h: (b, h, 0, 0))
    kr_spec = pl.BlockSpec((None, SEQ, HEAD_W - QK_NOPE), lambda b, h: (b, 0, 0))
    return pl.pallas_call(
        _attn_kernel,
        out_shape=jax.ShapeDtypeStruct((NB, HEADS, SEQ, V_DIM), BF16),
        grid=(NB, HEADS // ATTN_HEADS),
        in_specs=[head_spec(HEAD_W), head_spec(QK_NOPE), kr_spec, head_spec(V_DIM)],
        out_specs=head_spec(V_DIM),
        scratch_shapes=[pltpu.VMEM((ATTN_HEADS * nq * (nq + 1) // 2, ATTN_BLK, ATTN_BLK), F32)],
        compiler_params=_cparams("arbitrary", "arbitrary"),
        name="mla_attention",
    )(q, kn, kr, v)


def _ffn_rows(x, shift, scale, gate, w1_ref, b1_ref, w2_ref, b2_ref, g_ref, b_ref):
    h = (x * (1.0 + scale) + shift).astype(BF16)
    acc = jnp.zeros(x.shape, F32)
    for c0 in range(0, D_FF, FF_CHUNK):
        a = jnp.maximum(_bdot(h, w1_ref[:, c0:c0 + FF_CHUNK]) + b1_ref[:, c0:c0 + FF_CHUNK], 0.0)
        acc = acc + _bdot((a * a).astype(BF16), w2_ref[c0:c0 + FF_CHUNK, :])
    y = acc + b2_ref[...]
    return _layer_norm(ALPHA * x + (1.0 + gate) * y, g_ref[...], b_ref[...])


def _batch_mod(m_ref):
    b = pl.program_id(0)
    return [m_ref[i, pl.ds(b, 1), :] for i in (SHIFT, SCALE, GATE)]


def _ffn_kernel(x_ref, m_ref, w1_ref, b1_ref, w2_ref, b2_ref, g_ref, b_ref, y_ref):
    mod = _batch_mod(m_ref)
    for sb in range(FFN_SUB):
        rows = slice(sb * FFN_ROWS, (sb + 1) * FFN_ROWS)
        y_ref[rows, :] = _ffn_rows(x_ref[rows, :], *mod, w1_ref, b1_ref, w2_ref, b2_ref, g_ref, b_ref)


def _attn_out_ffn_kernel(o_ref, x_ref, mm_ref, mf_ref, wo_ref, gm_ref, bm_ref,
                         w1_ref, b1_ref, w2_ref, b2_ref, g_ref, b_ref, y_ref):
    gate_mix = _batch_mod(mm_ref)[GATE]
    mod = _batch_mod(mf_ref)
    for sb in range(FFN_SUB):
        rows = slice(sb * FFN_ROWS, (sb + 1) * FFN_ROWS)
        o = jnp.concatenate([o_ref[hd, rows, :] for hd in range(HEADS)], axis=1)
        x1 = _layer_norm(ALPHA * x_ref[rows, :] + (1.0 + gate_mix) * _bdot(o, wo_ref[...]), gm_ref[...], bm_ref[...])
        y_ref[rows, :] = _ffn_rows(x1, *mod, w1_ref, b1_ref, w2_ref, b2_ref, g_ref, b_ref)


_FFN_X_SPEC = pl.BlockSpec((None, FFN_SUB * FFN_ROWS, D), lambda b, i: (b, i, 0))
_FFN_GRID = (NB, SEQ // (FFN_SUB * FFN_ROWS))


def _ffn_weight_specs():
    return [_const_spec((D, D_FF)), _const_spec((1, D_FF)), _const_spec((D_FF, D)), _const_spec((1, D)),
            _const_spec((1, D)), _const_spec((1, D))]


def _ffn(x, m, layer, w1, b1, w2, b2, g, b):
    return pl.pallas_call(
        _ffn_kernel,
        out_shape=jax.ShapeDtypeStruct((NB, SEQ, D), F32),
        grid=_FFN_GRID,
        in_specs=[_FFN_X_SPEC, _mod_spec(layer)] + _ffn_weight_specs(),
        out_specs=_FFN_X_SPEC,
        compiler_params=_cparams("arbitrary", "arbitrary"),
        name="ffn_norm",
    )(x, m, w1, b1, w2, b2, g, b)


def _attn_out_ffn(o, x, m_mix, m_ffn, layer, w_o, g_mix, b_mix, w1, b1, w2, b2, g, b):
    o_spec = pl.BlockSpec((None, HEADS, FFN_SUB * FFN_ROWS, V_DIM), lambda b, i: (b, 0, i, 0))
    return pl.pallas_call(
        _attn_out_ffn_kernel,
        out_shape=jax.ShapeDtypeStruct((NB, SEQ, D), F32),
        grid=_FFN_GRID,
        in_specs=[o_spec, _FFN_X_SPEC, _mod_spec(layer), _mod_spec(layer),
                  _const_spec(w_o.shape), _const_spec((1, D)), _const_spec((1, D))] + _ffn_weight_specs(),
        out_specs=_FFN_X_SPEC,
        compiler_params=_cparams("arbitrary", "arbitrary"),
        name="attn_out_ffn_norm",
    )(o, x, m_mix, m_ffn, w_o, g_mix, b_mix, w1, b1, w2, b2, g, b)


def _cmul(ar, ai, br, bi):
    return ar * br - ai * bi, ar * bi + ai * br


def _pair_slot(piece, zero, slot):
    return [piece, zero] if slot == 0 else [zero, piece]


def _s5_disc_kernel(ldt_ref, are_ref, aim_ref, bre_ref, bim_ref, cre_ref, cim_ref, d_ref,
                    a8r_ref, a8i_ref, wb_ref, wck_ref):
    lr = are_ref[...]
    li = aim_ref[...]
    dt = jnp.exp(ldt_ref[...])
    mag = jnp.exp(lr * dt)
    ab_re = mag * jnp.cos(li * dt)
    ab_im = mag * jnp.sin(li * dt)
    den = lr * lr + li * li
    nr = ab_re - 1.0
    coef_re = (nr * lr + ab_im * li) / den
    coef_im = (ab_im * lr - nr * li) / den
    bb_re, bb_im = _cmul(coef_re, coef_im, bre_ref[...], bim_ref[...])
    c_re = cre_ref[...]
    c_im = cim_ref[...]

    pw = [(jnp.ones_like(ab_re), jnp.zeros_like(ab_im))]
    for _ in range(CHUNK):
        pw.append(_cmul(pw[-1][0], pw[-1][1], ab_re, ab_im))
    a8r_ref[...] = pw[CHUNK][0]
    a8i_ref[...] = pw[CHUNK][1]

    def pairs(v):
        v4 = v.reshape(N_PAIRS, 2, v.shape[1], v.shape[2])
        return v4[:, 0], v4[:, 1]

    z64 = jnp.zeros((N_PAIRS, GROUP_CH, STATE), F32)
    z16 = jnp.zeros((N_PAIRS, GROUP_CH, GROUP_CH), F32)
    eye = (lax.broadcasted_iota(jnp.int32, (GROUP_CH, GROUP_CH), 0)
           == lax.broadcasted_iota(jnp.int32, (GROUP_CH, GROUP_CH), 1)).astype(F32)

    ab_l = [_cmul(pw[l][0], pw[l][1], bb_re, bb_im) for l in range(CHUNK)]
    wb_rows = []
    for k in range(CHUNK):
        m_re, m_im = ab_l[CHUNK - 1 - k]
        for slot in range(2):
            wb_rows.append(jnp.concatenate(_pair_slot(pairs(m_re)[slot], z64, slot)
                                           + _pair_slot(pairs(m_im)[slot], z64, slot), axis=2))
    wb_ref[...] = jnp.concatenate(wb_rows, axis=1).astype(BF16)

    def nt(a, b):
        return lax.dot_general(a.astype(BF16), b.astype(BF16), (((2,), (2,)), ((0,), (0,))),
                               preferred_element_type=F32)

    k_l = [nt(c_re, ab_l[l][0]) - nt(c_im, ab_l[l][1]) for l in range(CHUNK)]
    k_l[0] = k_l[0] + d_ref[...] * eye[None]

    wck_rows = []
    for kp in range(CHUNK):
        ca_re, ca_im = _cmul(c_re, c_im, pw[kp + 1][0], pw[kp + 1][1])
        for slot in range(2):
            state_cols = (_pair_slot(pairs(ca_re)[slot], z64, slot)
                          + _pair_slot(-pairs(ca_im)[slot], z64, slot))
            direct_cols = []
            for k in range(CHUNK):
                blk = pairs(k_l[kp - k])[slot] if k <= kp else z16
                direct_cols += _pair_slot(blk, z16, slot)
            wck_rows.append(jnp.concatenate(state_cols + direct_cols, axis=2))
    wck_ref[...] = jnp.concatenate(wck_rows, axis=1).astype(BF16)


def _s5_discretise(log_dt, a_re, a_im, b_re, b_im, c_re, c_im, d_skip):
    g3 = (N_GROUPS, 1, STATE)
    return pl.pallas_call(
        _s5_disc_kernel,
        out_shape=(jax.ShapeDtypeStruct(g3, F32), jax.ShapeDtypeStruct(g3, F32),
                   jax.ShapeDtypeStruct((N_PAIRS, PAIR_W, PAIR_W), BF16),
                   jax.ShapeDtypeStruct((N_PAIRS, PAIR_W, 2 * PAIR_W), BF16)),
        compiler_params=pltpu.CompilerParams(vmem_limit_bytes=VMEM_LIMIT),
        name="s5_discretise",
    )(log_dt.reshape(N_GROUPS, 1, 1), a_re.reshape(g3), a_im.reshape(g3),
      jnp.swapaxes(b_re, 1, 2), jnp.swapaxes(b_im, 1, 2), c_re, c_im, d_skip.reshape(N_GROUPS, GROUP_CH, 1))


S5_CHUNKS = S5_STEPS // CHUNK
U8_ROWS = S5_CHUNKS * NB
S5_X_SPEC = pl.BlockSpec((NB, S5_SUB * S5_STEPS, D), lambda i: (0, i, 0))
S5_U8_SPEC = pl.BlockSpec((S5_SUB * U8_ROWS, N_PAIRS * PAIR_W), lambda i: (i, 0))


def _s5_in_kernel(x_ref, m_ref, w_in_ref, u8_ref):
    shift = m_ref[SHIFT]
    scale = m_ref[SCALE]
    for sb in range(S5_SUB):
        x3 = jnp.swapaxes(x_ref[:, sb * S5_STEPS:(sb + 1) * S5_STEPS, :], 0, 1)
        h = (x3 * (1.0 + scale)[None] + shift[None]).reshape(S5_ROWS, D).astype(BF16)
        u = _bdot(h, w_in_ref[...])
        u4 = u.reshape(S5_CHUNKS, CHUNK, NB, D)
        per_k = [u4[:, k].reshape(U8_ROWS, D) for k in range(CHUNK)]
        cols = [per_k[k][:, j * PAIR_CH:(j + 1) * PAIR_CH] for j in range(N_PAIRS) for k in range(CHUNK)]
        u8_ref[sb * U8_ROWS:(sb + 1) * U8_ROWS, :] = jnp.concatenate(cols, axis=1).astype(BF16)


def _s5_in(x, m, w_in):
    return pl.pallas_call(
        _s5_in_kernel,
        out_shape=jax.ShapeDtypeStruct((SEQ // CHUNK * NB, N_PAIRS * PAIR_W), BF16),
        grid=(SEQ // (S5_SUB * S5_STEPS),),
        in_specs=[
            S5_X_SPEC,
            _mod_spec(1),
            _const_spec(w_in.shape),
        ],
        out_specs=S5_U8_SPEC,
        compiler_params=_cparams("arbitrary"),
        name="s5_in_proj",
    )(x, m, w_in)


def _s5_core_kernel(u8_ref, wb_ref, wck_ref, a8r_ref, a8i_ref, y8_ref):
    half = PAIR_W // 2
    for jj in range(PAIRS_PER_STEP):
        lanes = slice(jj * PAIR_W, (jj + 1) * PAIR_W)
        u8 = u8_ref[:, lanes]
        v = _bdot(u8, wb_ref[jj])
        ar = jnp.broadcast_to(a8r_ref[:, jj * half:(jj + 1) * half], (NB, half))
        ai = jnp.broadcast_to(a8i_ref[:, jj * half:(jj + 1) * half], (NB, half))
        xr = jnp.zeros((NB, half), F32)
        xi = jnp.zeros((NB, half), F32)
        prev_r, prev_i = [], []
        for s in range(SEQ // CHUNK):
            prev_r.append(xr)
            prev_i.append(xi)
            rows = slice(s * NB, (s + 1) * NB)
            xr, xi = ar * xr - ai * xi + v[rows, :half], ar * xi + ai * xr + v[rows, half:]
        x_prev = jnp.concatenate([jnp.concatenate(prev_r, axis=0), jnp.concatenate(prev_i, axis=0)], axis=1)
        lhs = jnp.concatenate([x_prev.astype(BF16), u8], axis=1)
        y8_ref[:, lanes] = lax.dot_general(lhs, wck_ref[jj], (((1,), (1,)), ((), ())), preferred_element_type=F32)


def _s5_core(u8, wb, wck, a8r, a8i):
    n_rows = u8.shape[0]
    w = PAIRS_PER_STEP * PAIR_W
    return pl.pallas_call(
        _s5_core_kernel,
        out_shape=jax.ShapeDtypeStruct(u8.shape, F32),
        grid=(N_PAIRS // PAIRS_PER_STEP,),
        in_specs=[
            pl.BlockSpec((n_rows, w), lambda i: (0, i)),
            pl.BlockSpec((PAIRS_PER_STEP, PAIR_W, PAIR_W), lambda i: (i, 0, 0)),
            pl.BlockSpec((PAIRS_PER_STEP, PAIR_W, 2 * PAIR_W), lambda i: (i, 0, 0)),
            pl.BlockSpec((1, w // 2), lambda i: (0, i)),
            pl.BlockSpec((1, w // 2), lambda i: (0, i)),
        ],
        out_specs=pl.BlockSpec((n_rows, w), lambda i: (0, i)),
        compiler_params=_cparams("arbitrary"),
        name="s5_recurrence",
    )(u8, wb, wck, a8r, a8i)


def _s5_out_kernel(y8_ref, x_ref, m_ref, w_glu_ref, b_glu_ref, w_out_ref, g_ref, b_ref, o_ref):
    gate = m_ref[GATE]
    for sb in range(S5_SUB):
        steps = slice(sb * S5_STEPS, (sb + 1) * S5_STEPS)
        y8 = y8_ref[sb * U8_ROWS:(sb + 1) * U8_ROWS, :]
        per_k = [jnp.concatenate([y8[:, j * PAIR_W + k * PAIR_CH:j * PAIR_W + (k + 1) * PAIR_CH]
                                  for j in range(N_PAIRS)], axis=1) for k in range(CHUNK)]
        y = jnp.stack([p.reshape(S5_CHUNKS, NB, D) for p in per_k], axis=1).reshape(S5_ROWS, D)

        g = 0.5 * y * (1.0 + jnp.tanh(math.sqrt(2.0 / math.pi) * (y + 0.044715 * (y * y * y))))
        zz = g * jax.nn.sigmoid(_bdot(g.astype(BF16), w_glu_ref[...]) + b_glu_ref[...])
        out = jnp.swapaxes(_bdot(zz.astype(BF16), w_out_ref[...]).reshape(S5_STEPS, NB, D), 0, 1)
        res = ALPHA * x_ref[:, steps, :] + (1.0 + gate)[:, None, :] * out
        o_ref[:, steps, :] = _layer_norm(res, g_ref[...], b_ref[...])


def _s5_out(y8, x, m, w_glu, b_glu, w_out, g, b):
    x_spec = S5_X_SPEC
    return pl.pallas_call(
        _s5_out_kernel,
        out_shape=jax.ShapeDtypeStruct((NB, SEQ, D), F32),
        grid=(SEQ // (S5_SUB * S5_STEPS),),
        in_specs=[
            S5_U8_SPEC,
            x_spec,
            _mod_spec(1),
            _const_spec(w_glu.shape),
            _const_spec((1, D)),
            _const_spec(w_out.shape),
            _const_spec((1, D)),
            _const_spec((1, D)),
        ],
        out_specs=x_spec,
        compiler_params=_cparams("arbitrary"),
        name="s5_out_norm",
    )(y8, x, m, w_glu, b_glu, w_out, g, b)


def _rotate_half_cols(w):
    half = w.shape[-1] // 2
    return jnp.concatenate([-w[..., half:], w[..., :half]], axis=-1)


def _mla_weights(w_in, w_qb):
    k_pe = w_in[:, Q_LORA + KV_LORA:]
    w_in_ext = jnp.concatenate([w_in, _rotate_half_cols(k_pe)], axis=1)
    wq = w_qb.reshape(Q_LORA, HEADS, QK_NOPE + QK_ROPE)
    wq_ext = jnp.concatenate([wq, _rotate_half_cols(wq[..., QK_NOPE:])], axis=-1)
    return w_in_ext.astype(BF16), wq_ext.reshape(Q_LORA, HEADS * HEAD_W).astype(BF16)


def kernel(x, c, positions, mla_w_in, mla_q_norm, mla_w_qb, mla_kv_norm, mla_w_kvb, mla_w_o, ssm_w_in, ssm_log_dt, ssm_a_re, ssm_a_im, ssm_b_re, ssm_b_im, ssm_c_re, ssm_c_im, ssm_d, ssm_w_glu, ssm_b_glu, ssm_w_out, mlp_w1, mlp_b1, mlp_w2, mlp_b2, mod_mix_w, mod_mix_b, mod_ffn_w, mod_ffn_b, ln_mix_g, ln_mix_b, ln_ffn_g, ln_ffn_b):
    row = lambda v: v.reshape(1, -1)
    m_mix = _modulation(c, mod_mix_w, mod_mix_b)
    m_ffn = _modulation(c, mod_ffn_w, mod_ffn_b)

    inv_freq = ROPE_THETA ** (-jnp.arange(0, QK_ROPE, 2, dtype=F32) / QK_ROPE)
    freq = jnp.tile(inv_freq, 4).reshape(1, 128)
    w_in_ext, w_qb_ext = _mla_weights(mla_w_in[0], mla_w_qb[0])
    q, kn, kr, v = _mla_proj(x, m_mix, positions.reshape(NB, SEQ, 1), freq, w_in_ext,
                             row(mla_q_norm[0]), row(mla_kv_norm[0]), w_qb_ext, mla_w_kvb[0].astype(BF16))
    o = _attention(q, kn, kr, v)
    x2 = _attn_out_ffn(o, x, m_mix, m_ffn, 0, mla_w_o[0].astype(BF16), row(ln_mix_g[0]), row(ln_mix_b[0]),
                       mlp_w1[0].astype(BF16), row(mlp_b1[0]), mlp_w2[0].astype(BF16), row(mlp_b2[0]),
                       row(ln_ffn_g[0]), row(ln_ffn_b[0]))

    a8r, a8i, wb, wck = _s5_discretise(ssm_log_dt[0], ssm_a_re[0], ssm_a_im[0], ssm_b_re[0], ssm_b_im[0],
                                       ssm_c_re[0], ssm_c_im[0], ssm_d[0])
    u8 = _s5_in(x2, m_mix, ssm_w_in[0].astype(BF16))
    y8 = _s5_core(u8, wb, wck, row(a8r), row(a8i))
    x3 = _s5_out(y8, x2, m_mix, ssm_w_glu[0].astype(BF16), row(ssm_b_glu[0]), ssm_w_out[0].astype(BF16),
                 row(ln_mix_g[1]), row(ln_mix_b[1]))
    return _ffn(x3, m_ffn, 1, mlp_w1[1].astype(BF16), row(mlp_b1[1]), mlp_w2[1].astype(BF16), row(mlp_b2[1]),
                row(ln_ffn_g[1]), row(ln_ffn_b[1]))
```

```python
import functools
import math

import jax
import jax.numpy as jnp
from jax import lax
from jax.experimental import pallas as pl
from jax.experimental.pallas import tpu as pltpu

F32 = jnp.float32
BF16 = jnp.bfloat16

D = 1024
NB = 8
SEQ = 2048
HEADS = 8
QK_NOPE = 128
QK_ROPE = 64
V_DIM = 128
Q_LORA = 256
KV_LORA = 128
ROPE_THETA = 10000.0
GROUP_CH = 16
N_GROUPS = 64
STATE = 64
D_FF = 4 * D
DEPTH = 2
ALPHA = (2 * DEPTH) ** 0.25
LN_EPS = 1e-5
RMS_EPS = 1e-6
Q_SCALE = math.log2(math.e) / math.sqrt(QK_NOPE + QK_ROPE)
NEG_BIG = -1e30

HEAD_W = 256
CHUNK = 8
N_PAIRS = N_GROUPS // 2
PAIR_CH = 2 * GROUP_CH
PAIR_W = CHUNK * PAIR_CH
assert PAIR_W == 4 * STATE

VMEM_LIMIT = 56 * 1024 * 1024
PROJ_ROWS = 512
PROJ_SUB = 2
ATTN_BLK = 512
ATTN_HEADS = 2
FFN_ROWS = 512
FFN_SUB = 2
FF_CHUNK = 1024
S5_STEPS = 64
S5_ROWS = S5_STEPS * NB
S5_SUB = 2
PAIRS_PER_STEP = 2


def _cparams(*sem):
    return pltpu.CompilerParams(dimension_semantics=sem, vmem_limit_bytes=VMEM_LIMIT)


def _const_spec(shape):
    nd = len(shape)
    return pl.BlockSpec(shape, lambda *_: (0,) * nd, pipeline_mode=pl.Buffered(1))


def _layer_norm(v, g, b):
    mu = jnp.mean(v, axis=-1, keepdims=True)
    vc = v - mu
    var = jnp.mean(vc * vc, axis=-1, keepdims=True)
    return vc * lax.rsqrt(var + LN_EPS) * g + b


def _bdot(a, b):
    return jnp.dot(a, b, preferred_element_type=F32)


def _mod_kernel(c_ref, w_ref, b_ref, o_ref):
    c = c_ref[...]
    cs = c * jax.nn.sigmoid(c)
    o_ref[...] = _bdot(cs.astype(BF16), w_ref[...].astype(BF16)) + b_ref[...]


def _modulation(c, w, b):
    n_layers = w.shape[0]
    return pl.pallas_call(
        _mod_kernel,
        out_shape=jax.ShapeDtypeStruct((n_layers, 3, NB, D), F32),
        grid=(n_layers, 3),
        in_specs=[
            pl.BlockSpec((NB, D), lambda l, j: (0, 0)),
            pl.BlockSpec((None, D, D), lambda l, j: (l, 0, j)),
            pl.BlockSpec((None, 1, D), lambda l, j: (l, 0, j)),
        ],
        out_specs=pl.BlockSpec((None, None, NB, D), lambda l, j: (l, j, 0, 0)),
        compiler_params=_cparams("arbitrary", "arbitrary"),
        name="adaln_modulation",
    )(c, w, b.reshape(n_layers, 1, 3 * D))


def _mod_spec(layer):
    return pl.BlockSpec((None, 3, NB, D), lambda *_: (layer, 0, 0, 0), pipeline_mode=pl.Buffered(1))


SHIFT, SCALE, GATE = 0, 1, 2


def _mla_proj_kernel(x_ref, m_ref, pos_ref, freq_ref, w_in_ref, qn_ref, kvn_ref, w_qb_ref, w_kvb_ref,
                     q_ref, kn_ref, kr_ref, v_ref):
    b = pl.program_id(0)
    shift = m_ref[SHIFT, pl.ds(b, 1), :]
    scale = m_ref[SCALE, pl.ds(b, 1), :]

    def rms(v, g):
        return v * lax.rsqrt(jnp.mean(v * v, axis=-1, keepdims=True) + RMS_EPS) * g

    for sb in range(PROJ_SUB):
        rows = slice(sb * PROJ_ROWS, (sb + 1) * PROJ_ROWS)
        h = (x_ref[0, rows, :] * (1.0 + scale) + shift).astype(BF16)
        z = _bdot(h, w_in_ref[...])
        cq = rms(z[:, :Q_LORA], qn_ref[...]).astype(BF16)
        ckv = rms(z[:, Q_LORA:Q_LORA + KV_LORA], kvn_ref[...]).astype(BF16)

        ang = pos_ref[0, rows, :].astype(F32) * freq_ref[...]
        lane = lax.broadcasted_iota(jnp.int32, ang.shape, 1)
        mult = jnp.where(lane < QK_ROPE, jnp.cos(ang), jnp.sin(ang))

        def rope(slab):
            s = slab * mult
            return s + pltpu.roll(s, QK_ROPE, axis=1)

        kr_ref[0, rows, :] = jnp.where(lane < QK_ROPE, rope(z[:, Q_LORA + KV_LORA:]), 0.0).astype(BF16)

        q_all = _bdot(cq, w_qb_ref[...])
        kv = _bdot(ckv, w_kvb_ref[...])
        for hd in range(HEADS):
            c0 = hd * HEAD_W
            q_ref[0, hd, rows, 0:QK_NOPE] = q_all[:, c0:c0 + QK_NOPE].astype(BF16)
            q_ref[0, hd, rows, QK_NOPE:HEAD_W] = rope(q_all[:, c0 + QK_NOPE:c0 + HEAD_W]).astype(BF16)
            kn_ref[0, hd, rows, :] = kv[:, c0:c0 + QK_NOPE].astype(BF16)
            v_ref[0, hd, rows, :] = kv[:, c0 + QK_NOPE:c0 + HEAD_W].astype(BF16)


def _mla_proj(x, m, pos, freq, w_in, qn, kvn, w_qb, w_kvb):
    nt = SEQ // (PROJ_SUB * PROJ_ROWS)
    blk = PROJ_SUB * PROJ_ROWS
    return pl.pallas_call(
        _mla_proj_kernel,
        out_shape=(
            jax.ShapeDtypeStruct((NB, HEADS, SEQ, HEAD_W), BF16),
            jax.ShapeDtypeStruct((NB, HEADS, SEQ, QK_NOPE), BF16),
            jax.ShapeDtypeStruct((NB, SEQ, HEAD_W - QK_NOPE), BF16),
            jax.ShapeDtypeStruct((NB, HEADS, SEQ, V_DIM), BF16),
        ),
        grid=(NB, nt),
        in_specs=[
            pl.BlockSpec((1, blk, D), lambda b, i: (b, i, 0)),
            _mod_spec(0),
            pl.BlockSpec((1, blk, 1), lambda b, i: (b, i, 0)),
            _const_spec((1, 128)),
            _const_spec(w_in.shape),
            _const_spec((1, Q_LORA)),
            _const_spec((1, KV_LORA)),
            _const_spec(w_qb.shape),
            _const_spec(w_kvb.shape),
        ],
        out_specs=(
            pl.BlockSpec((1, HEADS, blk, HEAD_W), lambda b, i: (b, 0, i, 0)),
            pl.BlockSpec((1, HEADS, blk, QK_NOPE), lambda b, i: (b, 0, i, 0)),
            pl.BlockSpec((1, blk, HEAD_W - QK_NOPE), lambda b, i: (b, i, 0)),
            pl.BlockSpec((1, HEADS, blk, V_DIM), lambda b, i: (b, 0, i, 0)),
        ),
        compiler_params=_cparams("arbitrary", "arbitrary"),
        name="mla_projections",
    )(x, m, pos, freq, w_in, qn, kvn, w_qb, w_kvb)


def _lane_groups(v):
    return [v[:, c:c + 128] for c in range(0, v.shape[1], 128)]


def _attn_kernel(q_ref, kn_ref, kr_ref, v_ref, o_ref, s_ref):
    blk = ATTN_BLK
    nq = SEQ // blk
    row = lax.broadcasted_iota(jnp.int32, (blk, blk), 0)
    col = lax.broadcasted_iota(jnp.int32, (blk, blk), 1)

    def rows(n):
        return slice(n * blk, (n + 1) * blk)

    def slot(hd, n, j):
        return hd * (nq * (nq + 1) // 2) + n * (n + 1) // 2 + j

    def pass1(hd, n):
        q = q_ref[hd, rows(n), :]
        m = None
        for j in range(n + 1):
            k = jnp.concatenate([kn_ref[hd, rows(j), :], kr_ref[rows(j), :]], axis=1)
            s = lax.dot_general(q, k, (((1,), (1,)), ((), ())), preferred_element_type=F32)
            if j == n:
                s = jnp.where(col <= row, s, NEG_BIG)
            s_ref[slot(hd, n, j)] = s
            m = functools.reduce(jnp.maximum, _lane_groups(s) + ([] if m is None else [m]))
        return jnp.broadcast_to(jnp.max(m, axis=-1, keepdims=True), (blk, 128))

    def pass2(hd, n, mb):
        ps = [jnp.exp2(sg - mb) for j in range(n + 1) for sg in _lane_groups(s_ref[slot(hd, n, j)])]
        l = jnp.sum(functools.reduce(jnp.add, ps), axis=-1, keepdims=True)
        acc = _bdot(jnp.concatenate(ps, axis=1).astype(BF16), v_ref[hd, 0:(n + 1) * blk, :])
        o_ref[hd, rows(n), :] = (acc / l).astype(BF16)

    mb = [pass1(hd, 0) for hd in range(ATTN_HEADS)]
    for n in range(nq):
        for hd in range(ATTN_HEADS):
            mb_next = pass1(hd, n + 1) if n + 1 < nq else None
            pass2(hd, n, mb[hd])
            mb[hd] = mb_next


def _attention(q, kn, kr, v):
    nq = SEQ // ATTN_BLK
    head_spec = lambda w: pl.BlockSpec((None, ATTN_HEADS, SEQ, w), lambda b, h: (b, h, 0, 0))
    kr_spec = pl.BlockSpec((None, SEQ, HEAD_W - QK_NOPE), lambda b, h: (b, 0, 0))
    return pl.pallas_call(
        _attn_kernel,
        out_shape=jax.ShapeDtypeStruct((NB, HEADS, SEQ, V_DIM), BF16),
        grid=(NB, HEADS // ATTN_HEADS),
        in_specs=[head_spec(HEAD_W), head_spec(QK_NOPE), kr_spec, head_spec(V_DIM)],
        out_specs=head_spec(V_DIM),
        scratch_shapes=[pltpu.VMEM((ATTN_HEADS * nq * (nq + 1) // 2, ATTN_BLK, ATTN_BLK), F32)],
        compiler_params=_cparams("arbitrary", "arbitrary"),
        name="mla_attention",
    )(q, kn, kr, v)


def _ffn_rows(x, shift, scale, gate, w1_ref, b1_ref, w2_ref, b2_ref, g_ref, b_ref):
    h = (x * (1.0 + scale) + shift).astype(BF16)
    acc = jnp.zeros(x.shape, F32)
    for c0 in range(0, D_FF, FF_CHUNK):
        a = jnp.maximum(_bdot(h, w1_ref[:, c0:c0 + FF_CHUNK]) + b1_ref[:, c0:c0 + FF_CHUNK], 0.0)
        acc = acc + _bdot((a * a).astype(BF16), w2_ref[c0:c0 + FF_CHUNK, :])
    y = acc + b2_ref[...]
    return _layer_norm(ALPHA * x + (1.0 + gate) * y, g_ref[...], b_ref[...])


def _batch_mod(m_ref):
    b = pl.program_id(0)
    return [m_ref[i, pl.ds(b, 1), :] for i in (SHIFT, SCALE, GATE)]


def _ffn_kernel(x_ref, m_ref, w1_ref, b1_ref, w2_ref, b2_ref, g_ref, b_ref, y_ref):
    mod = _batch_mod(m_ref)
    for sb in range(FFN_SUB):
        rows = slice(sb * FFN_ROWS, (sb + 1) * FFN_ROWS)
        y_ref[rows, :] = _ffn_rows(x_ref[rows, :], *mod, w1_ref, b1_ref, w2_ref, b2_ref, g_ref, b_ref)


def _attn_out_ffn_kernel(o_ref, x_ref, mm_ref, mf_ref, wo_ref, gm_ref, bm_ref,
                         w1_ref, b1_ref, w2_ref, b2_ref, g_ref, b_ref, y_ref):
    gate_mix = _batch_mod(mm_ref)[GATE]
    mod = _batch_mod(mf_ref)
    for sb in range(FFN_SUB):
        rows = slice(sb * FFN_ROWS, (sb + 1) * FFN_ROWS)
        o = jnp.concatenate([o_ref[hd, rows, :] for hd in range(HEADS)], axis=1)
        x1 = _layer_norm(ALPHA * x_ref[rows, :] + (1.0 + gate_mix) * _bdot(o, wo_ref[...]), gm_ref[...], bm_ref[...])
        y_ref[rows, :] = _ffn_rows(x1, *mod, w1_ref, b1_ref, w2_ref, b2_ref, g_ref, b_ref)


_FFN_X_SPEC = pl.BlockSpec((None, FFN_SUB * FFN_ROWS, D), lambda b, i: (b, i, 0))
_FFN_GRID = (NB, SEQ // (FFN_SUB * FFN_ROWS))


def _ffn_weight_specs():
    return [_const_spec((D, D_FF)), _const_spec((1, D_FF)), _const_spec((D_FF, D)), _const_spec((1, D)),
            _const_spec((1, D)), _const_spec((1, D))]


def _ffn(x, m, layer, w1, b1, w2, b2, g, b):
    return pl.pallas_call(
        _ffn_kernel,
        out_shape=jax.ShapeDtypeStruct((NB, SEQ, D), F32),
        grid=_FFN_GRID,
        in_specs=[_FFN_X_SPEC, _mod_spec(layer)] + _ffn_weight_specs(),
        out_specs=_FFN_X_SPEC,
        compiler_params=_cparams("arbitrary", "arbitrary"),
        name="ffn_norm",
    )(x, m, w1, b1, w2, b2, g, b)


def _attn_out_ffn(o, x, m_mix, m_ffn, layer, w_o, g_mix, b_mix, w1, b1, w2, b2, g, b):
    o_spec = pl.BlockSpec((None, HEADS, FFN_SUB * FFN_ROWS, V_DIM), lambda b, i: (b, 0, i, 0))
    return pl.pallas_call(
        _attn_out_ffn_kernel,
        out_shape=jax.ShapeDtypeStruct((NB, SEQ, D), F32),
        grid=_FFN_GRID,
        in_specs=[o_spec, _FFN_X_SPEC, _mod_spec(layer), _mod_spec(layer),
                  _const_spec(w_o.shape), _const_spec((1, D)), _const_spec((1, D))] + _ffn_weight_specs(),
        out_specs=_FFN_X_SPEC,
        compiler_params=_cparams("arbitrary", "arbitrary"),
        name="attn_out_ffn_norm",
    )(o, x, m_mix, m_ffn, w_o, g_mix, b_mix, w1, b1, w2, b2, g, b)


def _cmul(ar, ai, br, bi):
    return ar * br - ai * bi, ar * bi + ai * br


def _pair_slot(piece, zero, slot):
    return [piece, zero] if slot == 0 else [zero, piece]


def _s5_disc_kernel(ldt_ref, are_ref, aim_ref, bre_ref, bim_ref, cre_ref, cim_ref, d_ref,
                    a8r_ref, a8i_ref, wb_ref, wck_ref):
    lr = are_ref[...]
    li = aim_ref[...]
    dt = jnp.exp(ldt_ref[...])
    mag = jnp.exp(lr * dt)
    ab_re = mag * jnp.cos(li * dt)
    ab_im = mag * jnp.sin(li * dt)
    den = lr * lr + li * li
    nr = ab_re - 1.0
    coef_re = (nr * lr + ab_im * li) / den
    coef_im = (ab_im * lr - nr * li) / den
    bb_re, bb_im = _cmul(coef_re, coef_im, bre_ref[...], bim_ref[...])
    c_re = cre_ref[...]
    c_im = cim_ref[...]

    pw = [(jnp.ones_like(ab_re), jnp.zeros_like(ab_im))]
    for _ in range(CHUNK):
        pw.append(_cmul(pw[-1][0], pw[-1][1], ab_re, ab_im))
    a8r_ref[...] = pw[CHUNK][0]
    a8i_ref[...] = pw[CHUNK][1]

    def pairs(v):
        v4 = v.reshape(N_PAIRS, 2, v.shape[1], v.shape[2])
        return v4[:, 0], v4[:, 1]

    z64 = jnp.zeros((N_PAIRS, GROUP_CH, STATE), F32)
    z16 = jnp.zeros((N_PAIRS, GROUP_CH, GROUP_CH), F32)
    eye = (lax.broadcasted_iota(jnp.int32, (GROUP_CH, GROUP_CH), 0)
           == lax.broadcasted_iota(jnp.int32, (GROUP_CH, GROUP_CH), 1)).astype(F32)

    ab_l = [_cmul(pw[l][0], pw[l][1], bb_re, bb_im) for l in range(CHUNK)]
    wb_rows = []
    for k in range(CHUNK):
        m_re, m_im = ab_l[CHUNK - 1 - k]
        for slot in range(2):
            wb_rows.append(jnp.concatenate(_pair_slot(pairs(m_re)[slot], z64, slot)
                                           + _pair_slot(pairs(m_im)[slot], z64, slot), axis=2))
    wb_ref[...] = jnp.concatenate(wb_rows, axis=1).astype(BF16)

    def nt(a, b):
        return lax.dot_general(a.astype(BF16), b.astype(BF16), (((2,), (2,)), ((0,), (0,))),
                               preferred_element_type=F32)

    k_l = [nt(c_re, ab_l[l][0]) - nt(c_im, ab_l[l][1]) for l in range(CHUNK)]
    k_l[0] = k_l[0] + d_ref[...] * eye[None]

    wck_rows = []
    for kp in range(CHUNK):
        ca_re, ca_im = _cmul(c_re, c_im, pw[kp + 1][0], pw[kp + 1][1])
        for slot in range(2):
            state_cols = (_pair_slot(pairs(ca_re)[slot], z64, slot)
                          + _pair_slot(-pairs(ca_im)[slot], z64, slot))
            direct_cols = []
            for k in range(CHUNK):
                blk = pairs(k_l[kp - k])[slot] if k <= kp else z16
                direct_cols += _pair_slot(blk, z16, slot)
            wck_rows.append(jnp.concatenate(state_cols + direct_cols, axis=2))
    wck_ref[...] = jnp.concatenate(wck_rows, axis=1).astype(BF16)


def _s5_discretise(log_dt, a_re, a_im, b_re, b_im, c_re, c_im, d_skip):
    g3 = (N_GROUPS, 1, STATE)
    return pl.pallas_call(
        _s5_disc_kernel,
        out_shape=(jax.ShapeDtypeStruct(g3, F32), jax.ShapeDtypeStruct(g3, F32),
                   jax.ShapeDtypeStruct((N_PAIRS, PAIR_W, PAIR_W), BF16),
                   jax.ShapeDtypeStruct((N_PAIRS, PAIR_W, 2 * PAIR_W), BF16)),
        compiler_params=pltpu.CompilerParams(vmem_limit_bytes=VMEM_LIMIT),
        name="s5_discretise",
    )(log_dt.reshape(N_GROUPS, 1, 1), a_re.reshape(g3), a_im.reshape(g3),
      jnp.swapaxes(b_re, 1, 2), jnp.swapaxes(b_im, 1, 2), c_re, c_im, d_skip.reshape(N_GROUPS, GROUP_CH, 1))


S5_CHUNKS = S5_STEPS // CHUNK
U8_ROWS = S5_CHUNKS * NB
S5_X_SPEC = pl.BlockSpec((NB, S5_SUB * S5_STEPS, D), lambda i: (0, i, 0))
S5_U8_SPEC = pl.BlockSpec((S5_SUB * U8_ROWS, N_PAIRS * PAIR_W), lambda i: (i, 0))


def _s5_in_kernel(x_ref, m_ref, w_in_ref, u8_ref):
    shift = m_ref[SHIFT]
    scale = m_ref[SCALE]
    for sb in range(S5_SUB):
        x3 = jnp.swapaxes(x_ref[:, sb * S5_STEPS:(sb + 1) * S5_STEPS, :], 0, 1)
        h = (x3 * (1.0 + scale)[None] + shift[None]).reshape(S5_ROWS, D).astype(BF16)
        u = _bdot(h, w_in_ref[...])
        u4 = u.reshape(S5_CHUNKS, CHUNK, NB, D)
        per_k = [u4[:, k].reshape(U8_ROWS, D) for k in range(CHUNK)]
        cols = [per_k[k][:, j * PAIR_CH:(j + 1) * PAIR_CH] for j in range(N_PAIRS) for k in range(CHUNK)]
        u8_ref[sb * U8_ROWS:(sb + 1) * U8_ROWS, :] = jnp.concatenate(cols, axis=1).astype(BF16)


def _s5_in(x, m, w_in):
    return pl.pallas_call(
        _s5_in_kernel,
        out_shape=jax.ShapeDtypeStruct((SEQ // CHUNK * NB, N_PAIRS * PAIR_W), BF16),
        grid=(SEQ // (S5_SUB * S5_STEPS),),
        in_specs=[
            S5_X_SPEC,
            _mod_spec(1),
            _const_spec(w_in.shape),
        ],
        out_specs=S5_U8_SPEC,
        compiler_params=_cparams("arbitrary"),
        name="s5_in_proj",
    )(x, m, w_in)


def _s5_core_kernel(u8_ref, wb_ref, wck_ref, a8r_ref, a8i_ref, y8_ref):
    half = PAIR_W // 2
    for jj in range(PAIRS_PER_STEP):
        lanes = slice(jj * PAIR_W, (jj + 1) * PAIR_W)
        u8 = u8_ref[:, lanes]
        v = _bdot(u8, wb_ref[jj])
        ar = jnp.broadcast_to(a8r_ref[:, jj * half:(jj + 1) * half], (NB, half))
        ai = jnp.broadcast_to(a8i_ref[:, jj * half:(jj + 1) * half], (NB, half))
        xr = jnp.zeros((NB, half), F32)
        xi = jnp.zeros((NB, half), F32)
        prev_r, prev_i = [], []
        for s in range(SEQ // CHUNK):
            prev_r.append(xr)
            prev_i.append(xi)
            rows = slice(s * NB, (s + 1) * NB)
            xr, xi = ar * xr - ai * xi + v[rows, :half], ar * xi + ai * xr + v[rows, half:]
        x_prev = jnp.concatenate([jnp.concatenate(prev_r, axis=0), jnp.concatenate(prev_i, axis=0)], axis=1)
        lhs = jnp.concatenate([x_prev.astype(BF16), u8], axis=1)
        y8_ref[:, lanes] = lax.dot_general(lhs, wck_ref[jj], (((1,), (1,)), ((), ())),
                                           preferred_element_type=F32).astype(y8_ref.dtype)


def _s5_core(u8, wb, wck, a8r, a8i):
    n_rows = u8.shape[0]
    w = PAIRS_PER_STEP * PAIR_W
    return pl.pallas_call(
        _s5_core_kernel,
        out_shape=jax.ShapeDtypeStruct(u8.shape, BF16),
        grid=(N_PAIRS // PAIRS_PER_STEP,),
        in_specs=[
            pl.BlockSpec((n_rows, w), lambda i: (0, i)),
            pl.BlockSpec((PAIRS_PER_STEP, PAIR_W, PAIR_W), lambda i: (i, 0, 0)),
            pl.BlockSpec((PAIRS_PER_STEP, PAIR_W, 2 * PAIR_W), lambda i: (i, 0, 0)),
            pl.BlockSpec((1, w // 2), lambda i: (0, i)),
            pl.BlockSpec((1, w // 2), lambda i: (0, i)),
        ],
        out_specs=pl.BlockSpec((n_rows, w), lambda i: (0, i)),
        compiler_params=_cparams("arbitrary"),
        name="s5_recurrence",
    )(u8, wb, wck, a8r, a8i)


def _s5_out_kernel(y8_ref, x_ref, m_ref, w_glu_ref, b_glu_ref, w_out_ref, g_ref, b_ref, o_ref):
    gate = m_ref[GATE]
    for sb in range(S5_SUB):
        steps = slice(sb * S5_STEPS, (sb + 1) * S5_STEPS)
        y8 = y8_ref[sb * U8_ROWS:(sb + 1) * U8_ROWS, :].astype(F32)
        per_k = [jnp.concatenate([y8[:, j * PAIR_W + k * PAIR_CH:j * PAIR_W + (k + 1) * PAIR_CH]
                                  for j in range(N_PAIRS)], axis=1) for k in range(CHUNK)]
        y = jnp.stack([p.reshape(S5_CHUNKS, NB, D) for p in per_k], axis=1).reshape(S5_ROWS, D)

        g = 0.5 * y * (1.0 + jnp.tanh(math.sqrt(2.0 / math.pi) * (y + 0.044715 * (y * y * y))))
        zz = g * jax.nn.sigmoid(_bdot(g.astype(BF16), w_glu_ref[...]) + b_glu_ref[...])
        out = jnp.swapaxes(_bdot(zz.astype(BF16), w_out_ref[...]).reshape(S5_STEPS, NB, D), 0, 1)
        res = ALPHA * x_ref[:, steps, :] + (1.0 + gate)[:, None, :] * out
        o_ref[:, steps, :] = _layer_norm(res, g_ref[...], b_ref[...])


def _s5_out(y8, x, m, w_glu, b_glu, w_out, g, b):
    x_spec = S5_X_SPEC
    return pl.pallas_call(
        _s5_out_kernel,
        out_shape=jax.ShapeDtypeStruct((NB, SEQ, D), F32),
        grid=(SEQ // (S5_SUB * S5_STEPS),),
        in_specs=[
            S5_U8_SPEC,
            x_spec,
            _mod_spec(1),
            _const_spec(w_glu.shape),
            _const_spec((1, D)),
            _const_spec(w_out.shape),
            _const_spec((1, D)),
            _const_spec((1, D)),
        ],
        out_specs=x_spec,
        compiler_params=_cparams("arbitrary"),
        name="s5_out_norm",
    )(y8, x, m, w_glu, b_glu, w_out, g, b)


def _rotate_half_cols(w):
    half = w.shape[-1] // 2
    return jnp.concatenate([-w[..., half:], w[..., :half]], axis=-1)


def _mla_weights(w_in, w_qb):
    k_pe = w_in[:, Q_LORA + KV_LORA:]
    w_in_ext = jnp.concatenate([w_in, _rotate_half_cols(k_pe)], axis=1)
    wq = w_qb.reshape(Q_LORA, HEADS, QK_NOPE + QK_ROPE) * Q_SCALE
    wq_ext = jnp.concatenate([wq, _rotate_half_cols(wq[..., QK_NOPE:])], axis=-1)
    return w_in_ext.astype(BF16), wq_ext.reshape(Q_LORA, HEADS * HEAD_W).astype(BF16)


def kernel(x, c, positions, mla_w_in, mla_q_norm, mla_w_qb, mla_kv_norm, mla_w_kvb, mla_w_o, ssm_w_in, ssm_log_dt, ssm_a_re, ssm_a_im, ssm_b_re, ssm_b_im, ssm_c_re, ssm_c_im, ssm_d, ssm_w_glu, ssm_b_glu, ssm_w_out, mlp_w1, mlp_b1, mlp_w2, mlp_b2, mod_mix_w, mod_mix_b, mod_ffn_w, mod_ffn_b, ln_mix_g, ln_mix_b, ln_ffn_g, ln_ffn_b):
    row = lambda v: v.reshape(1, -1)
    m_mix = _modulation(c, mod_mix_w, mod_mix_b)
    m_ffn = _modulation(c, mod_ffn_w, mod_ffn_b)

    inv_freq = ROPE_THETA ** (-jnp.arange(0, QK_ROPE, 2, dtype=F32) / QK_ROPE)
    freq = jnp.tile(inv_freq, 4).reshape(1, 128)
    w_in_ext, w_qb_ext = _mla_weights(mla_w_in[0], mla_w_qb[0])
    q, kn, kr, v = _mla_proj(x, m_mix, positions.reshape(NB, SEQ, 1), freq, w_in_ext,
                             row(mla_q_norm[0]), row(mla_kv_norm[0]), w_qb_ext, mla_w_kvb[0].astype(BF16))
    o = _attention(q, kn, kr, v)
    x2 = _attn_out_ffn(o, x, m_mix, m_ffn, 0, mla_w_o[0].astype(BF16), row(ln_mix_g[0]), row(ln_mix_b[0]),
                       mlp_w1[0].astype(BF16), row(mlp_b1[0]), mlp_w2[0].astype(BF16), row(mlp_b2[0]),
                       row(ln_ffn_g[0]), row(ln_ffn_b[0]))

    a8r, a8i, wb, wck = _s5_discretise(ssm_log_dt[0], ssm_a_re[0], ssm_a_im[0], ssm_b_re[0], ssm_b_im[0],
                                       ssm_c_re[0], ssm_c_im[0], ssm_d[0])
    u8 = _s5_in(x2, m_mix, ssm_w_in[0].astype(BF16))
    y8 = _s5_core(u8, wb, wck, row(a8r), row(a8i))
    x3 = _s5_out(y8, x2, m_mix, ssm_w_glu[0].astype(BF16), row(ssm_b_glu[0]), ssm_w_out[0].astype(BF16),
                 row(ln_mix_g[1]), row(ln_mix_b[1]))
    return _ffn(x3, m_ffn, 1, mlp_w1[1].astype(BF16), row(mlp_b1[1]), mlp_w2[1].astype(BF16), row(mlp_b2[1]),
                row(ln_ffn_g[1]), row(ln_ffn_b[1]))
```

```python
import functools
import math

import jax
import jax.numpy as jnp
from jax import lax
from jax.experimental import pallas as pl
from jax.experimental.pallas import tpu as pltpu

F32 = jnp.float32
BF16 = jnp.bfloat16

D = 1024
NB = 8
SEQ = 2048
HEADS = 8
QK_NOPE = 128
QK_ROPE = 64
V_DIM = 128
Q_LORA = 256
KV_LORA = 128
ROPE_THETA = 10000.0
GROUP_CH = 16
N_GROUPS = 64
STATE = 64
D_FF = 4 * D
DEPTH = 2
ALPHA = (2 * DEPTH) ** 0.25
LN_EPS = 1e-5
RMS_EPS = 1e-6
Q_SCALE = math.log2(math.e) / math.sqrt(QK_NOPE + QK_ROPE)
NEG_BIG = -1e30
GELU_C = math.sqrt(2.0 / math.pi)

HEAD_W = 256
CHUNK = 8
N_PAIRS = N_GROUPS // 2
PAIR_CH = 2 * GROUP_CH
PAIR_W = CHUNK * PAIR_CH
assert PAIR_W == 4 * STATE

VMEM_LIMIT = 56 * 1024 * 1024
PROJ_ROWS = 512
PROJ_SUB = 2
ATTN_BLK = 512
ATTN_HEADS = 4
_ATTN_SLOTS = 2 * (SEQ // ATTN_BLK) - 1
FFN_ROWS = 512
FFN_SUB = 2
FF_CHUNK = 1024
S5_BLOCK = 128
S5_IN_STEPS = 64
S5_OUT_STEPS = 32
PAIRS_PER_STEP = 2


def _cparams(*sem):
    return pltpu.CompilerParams(dimension_semantics=sem, vmem_limit_bytes=VMEM_LIMIT)


def _const_spec(shape):
    nd = len(shape)
    return pl.BlockSpec(shape, lambda *_: (0,) * nd, pipeline_mode=pl.Buffered(1))


def _layer_norm(v, g, b):
    mu = jnp.mean(v, axis=-1, keepdims=True)
    vc = v - mu
    var = jnp.mean(vc * vc, axis=-1, keepdims=True)
    return vc * lax.rsqrt(var + LN_EPS) * g + b


def _bdot(a, b):
    return jnp.dot(a, b, preferred_element_type=F32)


def _mod_kernel(c_ref, w_ref, b_ref, o_ref):
    c = c_ref[...]
    cs = c * jax.nn.sigmoid(c)
    o_ref[...] = _bdot(cs.astype(BF16), w_ref[...].astype(BF16)) + b_ref[...]


def _modulation(c, w, b):
    n_layers = w.shape[0]
    return pl.pallas_call(
        _mod_kernel,
        out_shape=jax.ShapeDtypeStruct((n_layers, 3, NB, D), F32),
        grid=(n_layers, 3),
        in_specs=[
            pl.BlockSpec((NB, D), lambda l, j: (0, 0)),
            pl.BlockSpec((None, D, D), lambda l, j: (l, 0, j)),
            pl.BlockSpec((None, 1, D), lambda l, j: (l, 0, j)),
        ],
        out_specs=pl.BlockSpec((None, None, NB, D), lambda l, j: (l, j, 0, 0)),
        compiler_params=_cparams("arbitrary", "arbitrary"),
        name="adaln_modulation",
    )(c, w, b.reshape(n_layers, 1, 3 * D))


def _mod_spec(layer):
    return pl.BlockSpec((None, 3, NB, D), lambda *_: (layer, 0, 0, 0), pipeline_mode=pl.Buffered(1))


SHIFT, SCALE, GATE = 0, 1, 2


def _mla_proj_kernel(x_ref, m_ref, pos_ref, freq_ref, w_in_ref, qn_ref, kvn_ref, w_qb_ref, w_kvb_ref,
                     q_ref, kn_ref, kr_ref, v_ref):
    b = pl.program_id(0)
    shift = m_ref[SHIFT, pl.ds(b, 1), :]
    scale = m_ref[SCALE, pl.ds(b, 1), :]

    def rms(v, g):
        return v * lax.rsqrt(jnp.mean(v * v, axis=-1, keepdims=True) + RMS_EPS) * g

    for sb in range(PROJ_SUB):
        rows = slice(sb * PROJ_ROWS, (sb + 1) * PROJ_ROWS)
        h = (x_ref[0, rows, :] * (1.0 + scale) + shift).astype(BF16)
        z = _bdot(h, w_in_ref[...])
        cq = rms(z[:, :Q_LORA], qn_ref[...]).astype(BF16)
        ckv = rms(z[:, Q_LORA:Q_LORA + KV_LORA], kvn_ref[...]).astype(BF16)

        ang = pos_ref[0, rows, :].astype(F32) * freq_ref[...]
        lane = lax.broadcasted_iota(jnp.int32, ang.shape, 1)
        mult = jnp.where(lane < QK_ROPE, jnp.cos(ang), jnp.sin(ang))

        def rope(slab):
            s = slab * mult
            return s + pltpu.roll(s, QK_ROPE, axis=1)

        kr_ref[0, rows, :] = jnp.where(lane < QK_ROPE, rope(z[:, Q_LORA + KV_LORA:]), 0.0).astype(BF16)

        q_all = _bdot(cq, w_qb_ref[...])
        kv = _bdot(ckv, w_kvb_ref[...])
        for hd in range(HEADS):
            c0 = hd * HEAD_W
            q_ref[0, hd, rows, 0:QK_NOPE] = q_all[:, c0:c0 + QK_NOPE].astype(BF16)
            q_ref[0, hd, rows, QK_NOPE:HEAD_W] = rope(q_all[:, c0 + QK_NOPE:c0 + HEAD_W]).astype(BF16)
            kn_ref[0, hd, rows, :] = kv[:, c0:c0 + QK_NOPE].astype(BF16)
            v_ref[0, hd, rows, :] = kv[:, c0 + QK_NOPE:c0 + HEAD_W].astype(BF16)


def _mla_proj(x, m, pos, freq, w_in, qn, kvn, w_qb, w_kvb):
    nt = SEQ // (PROJ_SUB * PROJ_ROWS)
    blk = PROJ_SUB * PROJ_ROWS
    return pl.pallas_call(
        _mla_proj_kernel,
        out_shape=(
            jax.ShapeDtypeStruct((NB, HEADS, SEQ, HEAD_W), BF16),
            jax.ShapeDtypeStruct((NB, HEADS, SEQ, QK_NOPE), BF16),
            jax.ShapeDtypeStruct((NB, SEQ, HEAD_W - QK_NOPE), BF16),
            jax.ShapeDtypeStruct((NB, HEADS, SEQ, V_DIM), BF16),
        ),
        grid=(NB, nt),
        in_specs=[
            pl.BlockSpec((1, blk, D), lambda b, i: (b, i, 0)),
            _mod_spec(0),
            pl.BlockSpec((1, blk, 1), lambda b, i: (b, i, 0)),
            _const_spec((1, 128)),
            _const_spec(w_in.shape),
            _const_spec((1, Q_LORA)),
            _const_spec((1, KV_LORA)),
            _const_spec(w_qb.shape),
            _const_spec(w_kvb.shape),
        ],
        out_specs=(
            pl.BlockSpec((1, HEADS, blk, HEAD_W), lambda b, i: (b, 0, i, 0)),
            pl.BlockSpec((1, HEADS, blk, QK_NOPE), lambda b, i: (b, 0, i, 0)),
            pl.BlockSpec((1, blk, HEAD_W - QK_NOPE), lambda b, i: (b, i, 0)),
            pl.BlockSpec((1, HEADS, blk, V_DIM), lambda b, i: (b, 0, i, 0)),
        ),
        compiler_params=_cparams("arbitrary", "arbitrary"),
        name="mla_projections",
    )(x, m, pos, freq, w_in, qn, kvn, w_qb, w_kvb)


def _lane_groups(v):
    return [v[:, c:c + 128] for c in range(0, v.shape[1], 128)]


def _attn_kernel(q_ref, kn_ref, kr_ref, v_ref, o_ref, s_ref):
    blk = ATTN_BLK
    nq = SEQ // blk
    row = lax.broadcasted_iota(jnp.int32, (blk, blk), 0)
    col = lax.broadcasted_iota(jnp.int32, (blk, blk), 1)

    def rows(n):
        return slice(n * blk, (n + 1) * blk)

    def slot(hd, n, j):
        return hd * _ATTN_SLOTS + (0 if n % 2 == 0 else nq - 1) + j

    def pass1(hd, n):
        q = q_ref[hd, rows(n), :]
        m = None
        for j in range(n + 1):
            k = jnp.concatenate([kn_ref[hd, rows(j), :], kr_ref[rows(j), :]], axis=1)
            s = lax.dot_general(q, k, (((1,), (1,)), ((), ())), preferred_element_type=F32)
            if j == n:
                s = jnp.where(col <= row, s, NEG_BIG)
            s_ref[slot(hd, n, j)] = s
            m = functools.reduce(jnp.maximum, _lane_groups(s) + ([] if m is None else [m]))
        return jnp.broadcast_to(jnp.max(m, axis=-1, keepdims=True), (blk, 128))

    def pass2(hd, n, mb):
        ps = [jnp.exp2(sg - mb) for j in range(n + 1) for sg in _lane_groups(s_ref[slot(hd, n, j)])]
        l = jnp.sum(functools.reduce(jnp.add, ps), axis=-1, keepdims=True)
        acc = _bdot(jnp.concatenate(ps, axis=1).astype(BF16), v_ref[hd, 0:(n + 1) * blk, :])
        o_ref[hd, rows(n), :] = (acc / l).astype(BF16)

    mb = [pass1(hd, 0) for hd in range(ATTN_HEADS)]
    for n in range(nq):
        for hd in range(ATTN_HEADS):
            mb_next = pass1(hd, n + 1) if n + 1 < nq else None
            pass2(hd, n, mb[hd])
            mb[hd] = mb_next


def _attention(q, kn, kr, v):
    nq = SEQ // ATTN_BLK
    head_spec = lambda w: pl.BlockSpec((None, ATTN_HEADS, SEQ, w), lambda b, h: (b, h, 0, 0))
    kr_spec = pl.BlockSpec((None, SEQ, HEAD_W - QK_NOPE), lambda b, h: (b, 0, 0))
    return pl.pallas_call(
        _attn_kernel,
        out_shape=jax.ShapeDtypeStruct((NB, HEADS, SEQ, V_DIM), BF16),
        grid=(NB, HEADS // ATTN_HEADS),
        in_specs=[head_spec(HEAD_W), head_spec(QK_NOPE), kr_spec, head_spec(V_DIM)],
        out_specs=head_spec(V_DIM),
        scratch_shapes=[pltpu.VMEM((ATTN_HEADS * _ATTN_SLOTS, ATTN_BLK, ATTN_BLK), F32)],
        compiler_params=_cparams("arbitrary", "arbitrary"),
        name="mla_attention",
    )(q, kn, kr, v)


def _ffn_rows(x, shift, scale, gate, w1_ref, b1_ref, w2_ref, b2_ref, g_ref, b_ref):
    h = (x * (1.0 + scale) + shift).astype(BF16)
    acc = jnp.zeros(x.shape, F32)
    for c0 in range(0, D_FF, FF_CHUNK):
        a = jnp.maximum(_bdot(h, w1_ref[:, c0:c0 + FF_CHUNK]) + b1_ref[:, c0:c0 + FF_CHUNK], 0.0)
        acc = acc + _bdot((a * a).astype(BF16), w2_ref[c0:c0 + FF_CHUNK, :])
    y = acc + b2_ref[...]
    return _layer_norm(ALPHA * x + (1.0 + gate) * y, g_ref[...], b_ref[...])


def _batch_mod(m_ref):
    b = pl.program_id(0)
    return [m_ref[i, pl.ds(b, 1), :] for i in (SHIFT, SCALE, GATE)]


def _ffn_kernel(x_ref, m_ref, w1_ref, b1_ref, w2_ref, b2_ref, g_ref, b_ref, y_ref):
    mod = _batch_mod(m_ref)
    for sb in range(FFN_SUB):
        rows = slice(sb * FFN_ROWS, (sb + 1) * FFN_ROWS)
        y_ref[rows, :] = _ffn_rows(x_ref[rows, :], *mod, w1_ref, b1_ref, w2_ref, b2_ref, g_ref, b_ref)


def _attn_out_ffn_kernel(o_ref, x_ref, mm_ref, mf_ref, wo_ref, gm_ref, bm_ref,
                         w1_ref, b1_ref, w2_ref, b2_ref, g_ref, b_ref, y_ref):
    gate_mix = _batch_mod(mm_ref)[GATE]
    mod = _batch_mod(mf_ref)
    for sb in range(FFN_SUB):
        rows = slice(sb * FFN_ROWS, (sb + 1) * FFN_ROWS)
        o = jnp.concatenate([o_ref[hd, rows, :] for hd in range(HEADS)], axis=1)
        x1 = _layer_norm(ALPHA * x_ref[rows, :] + (1.0 + gate_mix) * _bdot(o, wo_ref[...]), gm_ref[...], bm_ref[...])
        y_ref[rows, :] = _ffn_rows(x1, *mod, w1_ref, b1_ref, w2_ref, b2_ref, g_ref, b_ref)


_FFN_X_SPEC = pl.BlockSpec((None, FFN_SUB * FFN_ROWS, D), lambda b, i: (b, i, 0))
_FFN_GRID = (NB, SEQ // (FFN_SUB * FFN_ROWS))


def _ffn_weight_specs():
    return [_const_spec((D, D_FF)), _const_spec((1, D_FF)), _const_spec((D_FF, D)), _const_spec((1, D)),
            _const_spec((1, D)), _const_spec((1, D))]


def _ffn(x, m, layer, w1, b1, w2, b2, g, b):
    return pl.pallas_call(
        _ffn_kernel,
        out_shape=jax.ShapeDtypeStruct((NB, SEQ, D), F32),
        grid=_FFN_GRID,
        in_specs=[_FFN_X_SPEC, _mod_spec(layer)] + _ffn_weight_specs(),
        out_specs=_FFN_X_SPEC,
        compiler_params=_cparams("arbitrary", "arbitrary"),
        name="ffn_norm",
    )(x, m, w1, b1, w2, b2, g, b)


def _attn_out_ffn(o, x, m_mix, m_ffn, layer, w_o, g_mix, b_mix, w1, b1, w2, b2, g, b):
    o_spec = pl.BlockSpec((None, HEADS, FFN_SUB * FFN_ROWS, V_DIM), lambda b, i: (b, 0, i, 0))
    return pl.pallas_call(
        _attn_out_ffn_kernel,
        out_shape=jax.ShapeDtypeStruct((NB, SEQ, D), F32),
        grid=_FFN_GRID,
        in_specs=[o_spec, _FFN_X_SPEC, _mod_spec(layer), _mod_spec(layer),
                  _const_spec(w_o.shape), _const_spec((1, D)), _const_spec((1, D))] + _ffn_weight_specs(),
        out_specs=_FFN_X_SPEC,
        compiler_params=_cparams("arbitrary", "arbitrary"),
        name="attn_out_ffn_norm",
    )(o, x, m_mix, m_ffn, w_o, g_mix, b_mix, w1, b1, w2, b2, g, b)


def _cmul(ar, ai, br, bi):
    return ar * br - ai * bi, ar * bi + ai * br


def _pair_slot(piece, zero, slot):
    return [piece, zero] if slot == 0 else [zero, piece]


def _s5_disc_kernel(ldt_ref, are_ref, aim_ref, bre_ref, bim_ref, cre_ref, cim_ref, d_ref,
                    a8r_ref, a8i_ref, wb_ref, wck_ref):
    lr = are_ref[...]
    li = aim_ref[...]
    dt = jnp.exp(ldt_ref[...])
    mag = jnp.exp(lr * dt)
    ab_re = mag * jnp.cos(li * dt)
    ab_im = mag * jnp.sin(li * dt)
    den = lr * lr + li * li
    nr = ab_re - 1.0
    coef_re = (nr * lr + ab_im * li) / den
    coef_im = (ab_im * lr - nr * li) / den
    bb_re, bb_im = _cmul(coef_re, coef_im, bre_ref[...], bim_ref[...])
    c_re = cre_ref[...]
    c_im = cim_ref[...]

    pw = [(jnp.ones_like(ab_re), jnp.zeros_like(ab_im))]
    for _ in range(CHUNK):
        pw.append(_cmul(pw[-1][0], pw[-1][1], ab_re, ab_im))
    a8r_ref[...] = pw[CHUNK][0]
    a8i_ref[...] = pw[CHUNK][1]

    def pairs(v):
        v4 = v.reshape(N_PAIRS, 2, v.shape[1], v.shape[2])
        return v4[:, 0], v4[:, 1]

    z64 = jnp.zeros((N_PAIRS, GROUP_CH, STATE), F32)
    z16 = jnp.zeros((N_PAIRS, GROUP_CH, GROUP_CH), F32)
    eye = (lax.broadcasted_iota(jnp.int32, (GROUP_CH, GROUP_CH), 0)
           == lax.broadcasted_iota(jnp.int32, (GROUP_CH, GROUP_CH), 1)).astype(F32)

    ab_l = [_cmul(pw[l][0], pw[l][1], bb_re, bb_im) for l in range(CHUNK)]
    wb_rows = []
    for k in range(CHUNK):
        m_re, m_im = ab_l[CHUNK - 1 - k]
        for slot in range(2):
            wb_rows.append(jnp.concatenate(_pair_slot(pairs(m_re)[slot], z64, slot)
                                           + _pair_slot(pairs(m_im)[slot], z64, slot), axis=2))
    wb_ref[...] = jnp.concatenate(wb_rows, axis=1).astype(BF16)

    def nt(a, b):
        return lax.dot_general(a.astype(BF16), b.astype(BF16), (((2,), (2,)), ((0,), (0,))),
                               preferred_element_type=F32)

    k_l = [nt(c_re, ab_l[l][0]) - nt(c_im, ab_l[l][1]) for l in range(CHUNK)]
    k_l[0] = k_l[0] + d_ref[...] * eye[None]

    wck_rows = []
    for kp in range(CHUNK):
        ca_re, ca_im = _cmul(c_re, c_im, pw[kp + 1][0], pw[kp + 1][1])
        for slot in range(2):
            state_cols = (_pair_slot(pairs(ca_re)[slot], z64, slot)
                          + _pair_slot(-pairs(ca_im)[slot], z64, slot))
            direct_cols = []
            for k in range(CHUNK):
                blk = pairs(k_l[kp - k])[slot] if k <= kp else z16
                direct_cols += _pair_slot(blk, z16, slot)
            wck_rows.append(jnp.concatenate(state_cols + direct_cols, axis=2))
    wck_ref[...] = jnp.concatenate(wck_rows, axis=1).astype(BF16)


def _s5_discretise(log_dt, a_re, a_im, b_re, b_im, c_re, c_im, d_skip):
    g3 = (N_GROUPS, 1, STATE)
    return pl.pallas_call(
        _s5_disc_kernel,
        out_shape=(jax.ShapeDtypeStruct(g3, F32), jax.ShapeDtypeStruct(g3, F32),
                   jax.ShapeDtypeStruct((N_PAIRS, PAIR_W, PAIR_W), BF16),
                   jax.ShapeDtypeStruct((N_PAIRS, PAIR_W, 2 * PAIR_W), BF16)),
        compiler_params=pltpu.CompilerParams(vmem_limit_bytes=VMEM_LIMIT),
        name="s5_discretise",
    )(log_dt.reshape(N_GROUPS, 1, 1), a_re.reshape(g3), a_im.reshape(g3),
      jnp.swapaxes(b_re, 1, 2), jnp.swapaxes(b_im, 1, 2), c_re, c_im, d_skip.reshape(N_GROUPS, GROUP_CH, 1))


S5_X_SPEC = pl.BlockSpec((NB, S5_BLOCK, D), lambda i: (0, i, 0))
S5_U8_SPEC = pl.BlockSpec((S5_BLOCK // CHUNK * NB, N_PAIRS * PAIR_W), lambda i: (i, 0))


def _s5_in_kernel(x_ref, m_ref, w_in_ref, u8_ref):
    shift = m_ref[SHIFT]
    scale = m_ref[SCALE]
    steps = S5_IN_STEPS
    chunks = steps // CHUNK
    for sb in range(S5_BLOCK // steps):
        x3 = jnp.swapaxes(x_ref[:, sb * steps:(sb + 1) * steps, :], 0, 1)
        h = (x3 * (1.0 + scale)[None] + shift[None]).reshape(steps * NB, D).astype(BF16)
        u = _bdot(h, w_in_ref[...])
        u4 = u.reshape(chunks, CHUNK, NB, D)
        per_k = [u4[:, k].reshape(chunks * NB, D) for k in range(CHUNK)]
        cols = [per_k[k][:, j * PAIR_CH:(j + 1) * PAIR_CH] for j in range(N_PAIRS) for k in range(CHUNK)]
        u8_ref[sb * chunks * NB:(sb + 1) * chunks * NB, :] = jnp.concatenate(cols, axis=1).astype(BF16)


def _s5_in(x, m, w_in):
    return pl.pallas_call(
        _s5_in_kernel,
        out_shape=jax.ShapeDtypeStruct((SEQ // CHUNK * NB, N_PAIRS * PAIR_W), BF16),
        grid=(SEQ // S5_BLOCK,),
        in_specs=[
            S5_X_SPEC,
            _mod_spec(1),
            _const_spec(w_in.shape),
        ],
        out_specs=S5_U8_SPEC,
        compiler_params=_cparams("arbitrary"),
        name="s5_in_proj",
    )(x, m, w_in)


def _s5_core_kernel(u8_ref, wb_ref, wck_ref, a8r_ref, a8i_ref, y8_ref):
    half = PAIR_W // 2
    for jj in range(PAIRS_PER_STEP):
        lanes = slice(jj * PAIR_W, (jj + 1) * PAIR_W)
        u8 = u8_ref[:, lanes]
        v = _bdot(u8, wb_ref[jj])
        ar = jnp.broadcast_to(a8r_ref[:, jj * half:(jj + 1) * half], (NB, half))
        ai = jnp.broadcast_to(a8i_ref[:, jj * half:(jj + 1) * half], (NB, half))
        xr = jnp.zeros((NB, half), F32)
        xi = jnp.zeros((NB, half), F32)
        prev_r, prev_i = [], []
        for s in range(SEQ // CHUNK):
            prev_r.append(xr)
            prev_i.append(xi)
            rows = slice(s * NB, (s + 1) * NB)
            xr, xi = ar * xr - ai * xi + v[rows, :half], ar * xi + ai * xr + v[rows, half:]
        x_prev = jnp.concatenate([jnp.concatenate(prev_r, axis=0), jnp.concatenate(prev_i, axis=0)], axis=1)
        lhs = jnp.concatenate([x_prev.astype(BF16), u8], axis=1)
        y8_ref[:, lanes] = lax.dot_general(lhs, wck_ref[jj], (((1,), (1,)), ((), ())), preferred_element_type=F32)


def _s5_core(u8, wb, wck, a8r, a8i):
    n_rows = u8.shape[0]
    w = PAIRS_PER_STEP * PAIR_W
    return pl.pallas_call(
        _s5_core_kernel,
        out_shape=jax.ShapeDtypeStruct(u8.shape, F32),
        grid=(N_PAIRS // PAIRS_PER_STEP,),
        in_specs=[
            pl.BlockSpec((n_rows, w), lambda i: (0, i)),
            pl.BlockSpec((PAIRS_PER_STEP, PAIR_W, PAIR_W), lambda i: (i, 0, 0)),
            pl.BlockSpec((PAIRS_PER_STEP, PAIR_W, 2 * PAIR_W), lambda i: (i, 0, 0)),
            pl.BlockSpec((1, w // 2), lambda i: (0, i)),
            pl.BlockSpec((1, w // 2), lambda i: (0, i)),
        ],
        out_specs=pl.BlockSpec((n_rows, w), lambda i: (0, i)),
        compiler_params=_cparams("arbitrary"),
        name="s5_recurrence",
    )(u8, wb, wck, a8r, a8i)


def _s5_out_kernel(y8_ref, x_ref, m_ref, w_glu_ref, b_glu_ref, w_out_ref, g_ref, b_ref, o_ref):
    gate = m_ref[GATE]
    steps = S5_OUT_STEPS
    chunks = steps // CHUNK
    for sb in range(S5_BLOCK // steps):
        y8 = y8_ref[sb * chunks * NB:(sb + 1) * chunks * NB, :]
        per_k = [jnp.concatenate([y8[:, j * PAIR_W + k * PAIR_CH:j * PAIR_W + (k + 1) * PAIR_CH]
                                  for j in range(N_PAIRS)], axis=1) for k in range(CHUNK)]
        y = jnp.stack([p.reshape(chunks, NB, D) for p in per_k], axis=1).reshape(steps * NB, D)

        half_y = 0.5 * y
        g = half_y + half_y * jnp.tanh(y * (GELU_C + (GELU_C * 0.044715) * (y * y)))
        zz = g * jax.nn.sigmoid(_bdot(g.astype(BF16), w_glu_ref[...]) + b_glu_ref[...])
        out = jnp.swapaxes(_bdot(zz.astype(BF16), w_out_ref[...]).reshape(steps, NB, D), 0, 1)
        t_rows = slice(sb * steps, (sb + 1) * steps)
        res = ALPHA * x_ref[:, t_rows, :] + (1.0 + gate)[:, None, :] * out
        o_ref[:, t_rows, :] = _layer_norm(res, g_ref[...], b_ref[...])


def _s5_out(y8, x, m, w_glu, b_glu, w_out, g, b):
    x_spec = S5_X_SPEC
    return pl.pallas_call(
        _s5_out_kernel,
        out_shape=jax.ShapeDtypeStruct((NB, SEQ, D), F32),
        grid=(SEQ // S5_BLOCK,),
        in_specs=[
            S5_U8_SPEC,
            x_spec,
            _mod_spec(1),
            _const_spec(w_glu.shape),
            _const_spec((1, D)),
            _const_spec(w_out.shape),
            _const_spec((1, D)),
            _const_spec((1, D)),
        ],
        out_specs=x_spec,
        compiler_params=_cparams("arbitrary"),
        name="s5_out_norm",
    )(y8, x, m, w_glu, b_glu, w_out, g, b)


def _rotate_half_cols(w):
    half = w.shape[-1] // 2
    return jnp.concatenate([-w[..., half:], w[..., :half]], axis=-1)


def _mla_weights(w_in, w_qb):
    k_pe = w_in[:, Q_LORA + KV_LORA:]
    w_in_ext = jnp.concatenate([w_in, _rotate_half_cols(k_pe)], axis=1)
    wq = w_qb.reshape(Q_LORA, HEADS, QK_NOPE + QK_ROPE) * Q_SCALE
    wq_ext = jnp.concatenate([wq, _rotate_half_cols(wq[..., QK_NOPE:])], axis=-1)
    return w_in_ext.astype(BF16), wq_ext.reshape(Q_LORA, HEADS * HEAD_W).astype(BF16)


def kernel(x, c, positions, mla_w_in, mla_q_norm, mla_w_qb, mla_kv_norm, mla_w_kvb, mla_w_o, ssm_w_in, ssm_log_dt, ssm_a_re, ssm_a_im, ssm_b_re, ssm_b_im, ssm_c_re, ssm_c_im, ssm_d, ssm_w_glu, ssm_b_glu, ssm_w_out, mlp_w1, mlp_b1, mlp_w2, mlp_b2, mod_mix_w, mod_mix_b, mod_ffn_w, mod_ffn_b, ln_mix_g, ln_mix_b, ln_ffn_g, ln_ffn_b):
    row = lambda v: v.reshape(1, -1)
    m_mix = _modulation(c, mod_mix_w, mod_mix_b)
    m_ffn = _modulation(c, mod_ffn_w, mod_ffn_b)

    inv_freq = ROPE_THETA ** (-jnp.arange(0, QK_ROPE, 2, dtype=F32) / QK_ROPE)
    freq = jnp.tile(inv_freq, 4).reshape(1, 128)
    w_in_ext, w_qb_ext = _mla_weights(mla_w_in[0], mla_w_qb[0])
    q, kn, kr, v = _mla_proj(x, m_mix, positions.reshape(NB, SEQ, 1), freq, w_in_ext,
                             row(mla_q_norm[0]), row(mla_kv_norm[0]), w_qb_ext, mla_w_kvb[0].astype(BF16))
    o = _attention(q, kn, kr, v)
    x2 = _attn_out_ffn(o, x, m_mix, m_ffn, 0, mla_w_o[0].astype(BF16), row(ln_mix_g[0]), row(ln_mix_b[0]),
                       mlp_w1[0].astype(BF16), row(mlp_b1[0]), mlp_w2[0].astype(BF16), row(mlp_b2[0]),
                       row(ln_ffn_g[0]), row(ln_ffn_b[0]))

    a8r, a8i, wb, wck = _s5_discretise(ssm_log_dt[0], ssm_a_re[0], ssm_a_im[0], ssm_b_re[0], ssm_b_im[0],
                                       ssm_c_re[0], ssm_c_im[0], ssm_d[0])
    u8 = _s5_in(x2, m_mix, ssm_w_in[0].astype(BF16))
    y8 = _s5_core(u8, wb, wck, row(a8r), row(a8i))
    x3 = _s5_out(y8, x2, m_mix, ssm_w_glu[0].astype(BF16), row(ssm_b_glu[0]), ssm_w_out[0].astype(BF16),
                 row(ln_mix_g[1]), row(ln_mix_b[1]))
    return _ffn(x3, m_ffn, 1, mlp_w1[1].astype(BF16), row(mlp_b1[1]), mlp_w2[1].astype(BF16), row(mlp_b2[1]),
                row(ln_ffn_g[1]), row(ln_ffn_b[1]))
```

```python
import functools
import math

import jax
import jax.numpy as jnp
from jax import lax
from jax.experimental import pallas as pl
from jax.experimental.pallas import tpu as pltpu

F32 = jnp.float32
BF16 = jnp.bfloat16

D = 1024
NB = 8
SEQ = 2048
HEADS = 8
QK_NOPE = 128
QK_ROPE = 64
V_DIM = 128
Q_LORA = 256
KV_LORA = 128
ROPE_THETA = 10000.0
GROUP_CH = 16
N_GROUPS = 64
STATE = 64
D_FF = 4 * D
DEPTH = 2
ALPHA = (2 * DEPTH) ** 0.25
LN_EPS = 1e-5
RMS_EPS = 1e-6
Q_SCALE = math.log2(math.e) / math.sqrt(QK_NOPE + QK_ROPE)
NEG_BIG = -1e30
GELU_C = math.sqrt(2.0 / math.pi)

HEAD_W = 256
CHUNK = 8
N_PAIRS = N_GROUPS // 2
PAIR_CH = 2 * GROUP_CH
PAIR_W = CHUNK * PAIR_CH
assert PAIR_W == 4 * STATE

VMEM_LIMIT = 56 * 1024 * 1024
PROJ_ROWS = 512
PROJ_SUB = 2
ATTN_BLK = 512
ATTN_HEADS = 4
_ATTN_SLOTS = 2 * (SEQ // ATTN_BLK) - 1
FFN_ROWS = 512
FFN_SUB = 2
FF_CHUNK = 1024
S5_BLOCK = 128
S5_IN_STEPS = 64
S5_OUT_STEPS = 32
PAIRS_PER_STEP = 2


def _cparams(*sem):
    return pltpu.CompilerParams(dimension_semantics=sem, vmem_limit_bytes=VMEM_LIMIT)


def _const_spec(shape):
    nd = len(shape)
    return pl.BlockSpec(shape, lambda *_: (0,) * nd, pipeline_mode=pl.Buffered(1))


def _cast_specs(shape, grid, layer):
    _, rows, cols = shape
    n_steps = math.prod(grid)

    def step(*ids):
        s = 0
        for i, n in zip(ids, grid):
            s = s * n + i
        return s

    block = (1, rows // n_steps, cols)
    return (pl.BlockSpec(block, lambda *ids: (layer, step(*ids), 0)),
            pl.BlockSpec(block, lambda *ids: (0, step(*ids), 0)),
            jax.ShapeDtypeStruct((1, rows, cols), BF16))


def _cast_slices(src_refs, dst_refs):
    for src, dst in zip(src_refs, dst_refs):
        dst[...] = src[...].astype(dst.dtype)


def _layer_norm(v, g, b):
    mu = jnp.mean(v, axis=-1, keepdims=True)
    vc = v - mu
    var = jnp.mean(vc * vc, axis=-1, keepdims=True)
    return vc * lax.rsqrt(var + LN_EPS) * g + b


def _bdot(a, b):
    return jnp.dot(a, b, preferred_element_type=F32)


def _mod_kernel(c_ref, w_ref, b_ref, o_ref):
    c = c_ref[...]
    cs = c * jax.nn.sigmoid(c)
    o_ref[...] = _bdot(cs.astype(BF16), w_ref[...].astype(BF16)) + b_ref[...]


def _modulation(c, w, b):
    n_layers = w.shape[0]
    return pl.pallas_call(
        _mod_kernel,
        out_shape=jax.ShapeDtypeStruct((n_layers, 3, NB, D), F32),
        grid=(n_layers, 3),
        in_specs=[
            pl.BlockSpec((NB, D), lambda l, j: (0, 0)),
            pl.BlockSpec((None, D, D), lambda l, j: (l, 0, j)),
            pl.BlockSpec((None, 1, D), lambda l, j: (l, 0, j)),
        ],
        out_specs=pl.BlockSpec((None, None, NB, D), lambda l, j: (l, j, 0, 0)),
        compiler_params=_cparams("arbitrary", "arbitrary"),
        name="adaln_modulation",
    )(c, w, b.reshape(n_layers, 1, 3 * D))


def _mod_spec(layer):
    return pl.BlockSpec((None, 3, NB, D), lambda *_: (layer, 0, 0, 0), pipeline_mode=pl.Buffered(1))


SHIFT, SCALE, GATE = 0, 1, 2


def _mla_proj_kernel(x_ref, m_ref, pos_ref, freq_ref, w_in_ref, qn_ref, kvn_ref, w_qb_ref, w_kvb_ref,
                     ca_ref, cb_ref, cc_ref, q_ref, kn_ref, kr_ref, v_ref, cab_ref, cbb_ref, ccb_ref):
    _cast_slices((ca_ref, cb_ref, cc_ref), (cab_ref, cbb_ref, ccb_ref))
    b = pl.program_id(0)
    shift = m_ref[SHIFT, pl.ds(b, 1), :]
    scale = m_ref[SCALE, pl.ds(b, 1), :]

    def rms(v, g):
        return v * lax.rsqrt(jnp.mean(v * v, axis=-1, keepdims=True) + RMS_EPS) * g

    for sb in range(PROJ_SUB):
        rows = slice(sb * PROJ_ROWS, (sb + 1) * PROJ_ROWS)
        h = (x_ref[0, rows, :] * (1.0 + scale) + shift).astype(BF16)
        z = _bdot(h, w_in_ref[...])
        cq = rms(z[:, :Q_LORA], qn_ref[...]).astype(BF16)
        ckv = rms(z[:, Q_LORA:Q_LORA + KV_LORA], kvn_ref[...]).astype(BF16)

        ang = pos_ref[0, rows, :].astype(F32) * freq_ref[...]
        lane = lax.broadcasted_iota(jnp.int32, ang.shape, 1)
        mult = jnp.where(lane < QK_ROPE, jnp.cos(ang), jnp.sin(ang))

        def rope(slab):
            s = slab * mult
            return s + pltpu.roll(s, QK_ROPE, axis=1)

        kr_ref[0, rows, :] = jnp.where(lane < QK_ROPE, rope(z[:, Q_LORA + KV_LORA:]), 0.0).astype(BF16)

        q_all = _bdot(cq, w_qb_ref[...])
        kv = _bdot(ckv, w_kvb_ref[...])
        for hd in range(HEADS):
            c0 = hd * HEAD_W
            q_ref[0, hd, rows, 0:QK_NOPE] = q_all[:, c0:c0 + QK_NOPE].astype(BF16)
            q_ref[0, hd, rows, QK_NOPE:HEAD_W] = rope(q_all[:, c0 + QK_NOPE:c0 + HEAD_W]).astype(BF16)
            kn_ref[0, hd, rows, :] = kv[:, c0:c0 + QK_NOPE].astype(BF16)
            v_ref[0, hd, rows, :] = kv[:, c0 + QK_NOPE:c0 + HEAD_W].astype(BF16)


def _mla_proj(x, m, pos, freq, w_in, qn, kvn, w_qb, w_kvb, casts):
    nt = SEQ // (PROJ_SUB * PROJ_ROWS)
    blk = PROJ_SUB * PROJ_ROWS
    grid = (NB, nt)
    cast = [_cast_specs(w.shape, grid, layer) for w, layer in casts]
    return pl.pallas_call(
        _mla_proj_kernel,
        out_shape=[
            jax.ShapeDtypeStruct((NB, HEADS, SEQ, HEAD_W), BF16),
            jax.ShapeDtypeStruct((NB, HEADS, SEQ, QK_NOPE), BF16),
            jax.ShapeDtypeStruct((NB, SEQ, HEAD_W - QK_NOPE), BF16),
            jax.ShapeDtypeStruct((NB, HEADS, SEQ, V_DIM), BF16),
        ] + [c[2] for c in cast],
        grid=grid,
        in_specs=[
            pl.BlockSpec((1, blk, D), lambda b, i: (b, i, 0)),
            _mod_spec(0),
            pl.BlockSpec((1, blk, 1), lambda b, i: (b, i, 0)),
            _const_spec((1, 128)),
            _const_spec(w_in.shape),
            _const_spec((1, Q_LORA)),
            _const_spec((1, KV_LORA)),
            _const_spec(w_qb.shape),
            _const_spec(w_kvb.shape),
        ] + [c[0] for c in cast],
        out_specs=[
            pl.BlockSpec((1, HEADS, blk, HEAD_W), lambda b, i: (b, 0, i, 0)),
            pl.BlockSpec((1, HEADS, blk, QK_NOPE), lambda b, i: (b, 0, i, 0)),
            pl.BlockSpec((1, blk, HEAD_W - QK_NOPE), lambda b, i: (b, i, 0)),
            pl.BlockSpec((1, HEADS, blk, V_DIM), lambda b, i: (b, 0, i, 0)),
        ] + [c[1] for c in cast],
        compiler_params=_cparams("arbitrary", "arbitrary"),
        name="mla_projections",
    )(x, m, pos, freq, w_in, qn, kvn, w_qb, w_kvb, *[w for w, _ in casts])


def _lane_groups(v):
    return [v[:, c:c + 128] for c in range(0, v.shape[1], 128)]


def _attn_kernel(q_ref, kn_ref, kr_ref, v_ref, o_ref, s_ref):
    blk = ATTN_BLK
    nq = SEQ // blk
    row = lax.broadcasted_iota(jnp.int32, (blk, blk), 0)
    col = lax.broadcasted_iota(jnp.int32, (blk, blk), 1)

    def rows(n):
        return slice(n * blk, (n + 1) * blk)

    def slot(hd, n, j):
        return hd * _ATTN_SLOTS + (0 if n % 2 == 0 else nq - 1) + j

    def pass1(hd, n):
        q = q_ref[hd, rows(n), :]
        m = None
        for j in range(n + 1):
            k = jnp.concatenate([kn_ref[hd, rows(j), :], kr_ref[rows(j), :]], axis=1)
            s = lax.dot_general(q, k, (((1,), (1,)), ((), ())), preferred_element_type=F32)
            if j == n:
                s = jnp.where(col <= row, s, NEG_BIG)
            s_ref[slot(hd, n, j)] = s
            m = functools.reduce(jnp.maximum, _lane_groups(s) + ([] if m is None else [m]))
        return jnp.broadcast_to(jnp.max(m, axis=-1, keepdims=True), (blk, 128))

    def pass2(hd, n, mb):
        ps = [jnp.exp2(sg - mb) for j in range(n + 1) for sg in _lane_groups(s_ref[slot(hd, n, j)])]
        l = jnp.sum(functools.reduce(jnp.add, ps), axis=-1, keepdims=True)
        acc = _bdot(jnp.concatenate(ps, axis=1).astype(BF16), v_ref[hd, 0:(n + 1) * blk, :])
        o_ref[hd, rows(n), :] = (acc / l).astype(BF16)

    mb = [pass1(hd, 0) for hd in range(ATTN_HEADS)]
    for n in range(nq):
        for hd in range(ATTN_HEADS):
            mb_next = pass1(hd, n + 1) if n + 1 < nq else None
            pass2(hd, n, mb[hd])
            mb[hd] = mb_next


def _attention(q, kn, kr, v):
    head_spec = lambda w: pl.BlockSpec((None, ATTN_HEADS, SEQ, w), lambda b, h: (b, h, 0, 0))
    kr_spec = pl.BlockSpec((None, SEQ, HEAD_W - QK_NOPE), lambda b, h: (b, 0, 0))
    return pl.pallas_call(
        _attn_kernel,
        out_shape=jax.ShapeDtypeStruct((NB, HEADS, SEQ, V_DIM), BF16),
        grid=(NB, HEADS // ATTN_HEADS),
        in_specs=[head_spec(HEAD_W), head_spec(QK_NOPE), kr_spec, head_spec(V_DIM)],
        out_specs=head_spec(V_DIM),
        scratch_shapes=[pltpu.VMEM((ATTN_HEADS * _ATTN_SLOTS, ATTN_BLK, ATTN_BLK), F32)],
        compiler_params=_cparams("arbitrary", "arbitrary"),
        name="mla_attention",
    )(q, kn, kr, v)


def _ffn_rows(x, shift, scale, gate, w1_ref, b1_ref, w2_ref, b2_ref, g_ref, b_ref):
    h = (x * (1.0 + scale) + shift).astype(BF16)
    acc = jnp.zeros(x.shape, F32)
    for c0 in range(0, D_FF, FF_CHUNK):
        a = jnp.maximum(_bdot(h, w1_ref[:, c0:c0 + FF_CHUNK]) + b1_ref[:, c0:c0 + FF_CHUNK], 0.0)
        acc = acc + _bdot((a * a).astype(BF16), w2_ref[c0:c0 + FF_CHUNK, :])
    y = acc + b2_ref[...]
    return _layer_norm(ALPHA * x + (1.0 + gate) * y, g_ref[...], b_ref[...])


def _batch_mod(m_ref):
    b = pl.program_id(0)
    return [m_ref[i, pl.ds(b, 1), :] for i in (SHIFT, SCALE, GATE)]


def _ffn_kernel(x_ref, m_ref, w1_ref, b1_ref, w2_ref, b2_ref, g_ref, b_ref, y_ref):
    mod = _batch_mod(m_ref)
    for sb in range(FFN_SUB):
        rows = slice(sb * FFN_ROWS, (sb + 1) * FFN_ROWS)
        y_ref[rows, :] = _ffn_rows(x_ref[rows, :], *mod, w1_ref, b1_ref, w2_ref, b2_ref, g_ref, b_ref)


def _attn_out_ffn_kernel(o_ref, x_ref, mm_ref, mf_ref, wo_ref, gm_ref, bm_ref,
                         w1_ref, b1_ref, w2_ref, b2_ref, g_ref, b_ref, sa_ref, sb_ref, sc_ref,
                         y_ref, sab_ref, sbb_ref, scb_ref):
    _cast_slices((sa_ref, sb_ref, sc_ref), (sab_ref, sbb_ref, scb_ref))
    gate_mix = _batch_mod(mm_ref)[GATE]
    mod = _batch_mod(mf_ref)
    for sb in range(FFN_SUB):
        rows = slice(sb * FFN_ROWS, (sb + 1) * FFN_ROWS)
        o = jnp.concatenate([o_ref[hd, rows, :] for hd in range(HEADS)], axis=1)
        y_ref[rows, :] = _layer_norm(ALPHA * x_ref[rows, :] + (1.0 + gate_mix) * _bdot(o, wo_ref[...]),
                                     gm_ref[...], bm_ref[...])
    for sb in range(FFN_SUB):
        rows = slice(sb * FFN_ROWS, (sb + 1) * FFN_ROWS)
        y_ref[rows, :] = _ffn_rows(y_ref[rows, :], *mod, w1_ref, b1_ref, w2_ref, b2_ref, g_ref, b_ref)


_FFN_X_SPEC = pl.BlockSpec((None, FFN_SUB * FFN_ROWS, D), lambda b, i: (b, i, 0))
_FFN_GRID = (NB, SEQ // (FFN_SUB * FFN_ROWS))


def _layer_spec(shape, layer):
    return pl.BlockSpec((None,) + tuple(shape[1:]), lambda *_: (layer, 0, 0), pipeline_mode=pl.Buffered(1))


def _ffn_weight_specs(w1, w2):
    return [_layer_spec(w1.shape, 0), _const_spec((1, D_FF)), _layer_spec(w2.shape, 0), _const_spec((1, D)),
            _const_spec((1, D)), _const_spec((1, D))]


def _ffn(x, m, layer, w1, b1, w2, b2, g, b):
    return pl.pallas_call(
        _ffn_kernel,
        out_shape=jax.ShapeDtypeStruct((NB, SEQ, D), F32),
        grid=_FFN_GRID,
        in_specs=[_FFN_X_SPEC, _mod_spec(layer)] + _ffn_weight_specs(w1, w2),
        out_specs=_FFN_X_SPEC,
        compiler_params=_cparams("arbitrary", "arbitrary"),
        name="ffn_norm",
    )(x, m, w1, b1, w2, b2, g, b)


def _attn_out_ffn(o, x, m_mix, m_ffn, layer, w_o, g_mix, b_mix, w1, b1, w2, b2, g, b, casts):
    o_spec = pl.BlockSpec((None, HEADS, FFN_SUB * FFN_ROWS, V_DIM), lambda b, i: (b, 0, i, 0))
    cast = [_cast_specs(w.shape, _FFN_GRID, l) for w, l in casts]
    return pl.pallas_call(
        _attn_out_ffn_kernel,
        out_shape=[jax.ShapeDtypeStruct((NB, SEQ, D), F32)] + [c[2] for c in cast],
        grid=_FFN_GRID,
        in_specs=[o_spec, _FFN_X_SPEC, _mod_spec(layer), _mod_spec(layer),
                  _layer_spec(w_o.shape, 0), _const_spec((1, D)), _const_spec((1, D))]
                 + _ffn_weight_specs(w1, w2) + [c[0] for c in cast],
        out_specs=[_FFN_X_SPEC] + [c[1] for c in cast],
        compiler_params=_cparams("arbitrary", "arbitrary"),
        name="attn_out_ffn_norm",
    )(o, x, m_mix, m_ffn, w_o, g_mix, b_mix, w1, b1, w2, b2, g, b, *[w for w, _ in casts])


def _cmul(ar, ai, br, bi):
    return ar * br - ai * bi, ar * bi + ai * br


def _pair_slot(piece, zero, slot):
    return [piece, zero] if slot == 0 else [zero, piece]


def _s5_disc_kernel(ldt_ref, are_ref, aim_ref, bre_ref, bim_ref, cre_ref, cim_ref, d_ref,
                    a8r_ref, a8i_ref, wb_ref, wck_ref):
    lr = are_ref[...]
    li = aim_ref[...]
    dt = jnp.exp(ldt_ref[...])
    mag = jnp.exp(lr * dt)
    ab_re = mag * jnp.cos(li * dt)
    ab_im = mag * jnp.sin(li * dt)
    den = lr * lr + li * li
    nr = ab_re - 1.0
    coef_re = (nr * lr + ab_im * li) / den
    coef_im = (ab_im * lr - nr * li) / den
    bb_re, bb_im = _cmul(coef_re, coef_im, bre_ref[...], bim_ref[...])
    c_re = cre_ref[...]
    c_im = cim_ref[...]

    pw = [(jnp.ones_like(ab_re), jnp.zeros_like(ab_im))]
    for _ in range(CHUNK):
        pw.append(_cmul(pw[-1][0], pw[-1][1], ab_re, ab_im))
    a8r_ref[...] = pw[CHUNK][0]
    a8i_ref[...] = pw[CHUNK][1]

    def pairs(v):
        v4 = v.reshape(N_PAIRS, 2, v.shape[1], v.shape[2])
        return v4[:, 0], v4[:, 1]

    z64 = jnp.zeros((N_PAIRS, GROUP_CH, STATE), F32)
    z16 = jnp.zeros((N_PAIRS, GROUP_CH, GROUP_CH), F32)
    eye = (lax.broadcasted_iota(jnp.int32, (GROUP_CH, GROUP_CH), 0)
           == lax.broadcasted_iota(jnp.int32, (GROUP_CH, GROUP_CH), 1)).astype(F32)

    ab_l = [_cmul(pw[l][0], pw[l][1], bb_re, bb_im) for l in range(CHUNK)]
    wb_rows = []
    for k in range(CHUNK):
        m_re, m_im = ab_l[CHUNK - 1 - k]
        for slot in range(2):
            wb_rows.append(jnp.concatenate(_pair_slot(pairs(m_re)[slot], z64, slot)
                                           + _pair_slot(pairs(m_im)[slot], z64, slot), axis=2))
    wb_ref[...] = jnp.concatenate(wb_rows, axis=1).astype(BF16)

    def nt(a, b):
        return lax.dot_general(a.astype(BF16), b.astype(BF16), (((2,), (2,)), ((0,), (0,))),
                               preferred_element_type=F32)

    k_l = [nt(c_re, ab_l[l][0]) - nt(c_im, ab_l[l][1]) for l in range(CHUNK)]
    k_l[0] = k_l[0] + d_ref[...] * eye[None]

    wck_rows = []
    for kp in range(CHUNK):
        ca_re, ca_im = _cmul(c_re, c_im, pw[kp + 1][0], pw[kp + 1][1])
        for slot in range(2):
            state_cols = (_pair_slot(pairs(ca_re)[slot], z64, slot)
                          + _pair_slot(-pairs(ca_im)[slot], z64, slot))
            direct_cols = []
            for k in range(CHUNK):
                blk = pairs(k_l[kp - k])[slot] if k <= kp else z16
                direct_cols += _pair_slot(blk, z16, slot)
            wck_rows.append(jnp.concatenate(state_cols + direct_cols, axis=2))
    wck_ref[...] = jnp.concatenate(wck_rows, axis=1).astype(BF16)


def _s5_discretise(log_dt, a_re, a_im, b_re, b_im, c_re, c_im, d_skip):
    g3 = (N_GROUPS, 1, STATE)
    return pl.pallas_call(
        _s5_disc_kernel,
        out_shape=(jax.ShapeDtypeStruct(g3, F32), jax.ShapeDtypeStruct(g3, F32),
                   jax.ShapeDtypeStruct((N_PAIRS, PAIR_W, PAIR_W), BF16),
                   jax.ShapeDtypeStruct((N_PAIRS, PAIR_W, 2 * PAIR_W), BF16)),
        compiler_params=pltpu.CompilerParams(vmem_limit_bytes=VMEM_LIMIT),
        name="s5_discretise",
    )(log_dt.reshape(N_GROUPS, 1, 1), a_re.reshape(g3), a_im.reshape(g3),
      jnp.swapaxes(b_re, 1, 2), jnp.swapaxes(b_im, 1, 2), c_re, c_im, d_skip.reshape(N_GROUPS, GROUP_CH, 1))


S5_X_SPEC = pl.BlockSpec((NB, S5_BLOCK, D), lambda i: (0, i, 0))
S5_U8_SPEC = pl.BlockSpec((S5_BLOCK // CHUNK * NB, N_PAIRS * PAIR_W), lambda i: (i, 0))


def _s5_in_kernel(x_ref, m_ref, w_in_ref, u8_ref):
    shift = m_ref[SHIFT]
    scale = m_ref[SCALE]
    steps = S5_IN_STEPS
    chunks = steps // CHUNK
    for sb in range(S5_BLOCK // steps):
        x3 = jnp.swapaxes(x_ref[:, sb * steps:(sb + 1) * steps, :], 0, 1)
        h = (x3 * (1.0 + scale)[None] + shift[None]).reshape(steps * NB, D).astype(BF16)
        u = _bdot(h, w_in_ref[...])
        u4 = u.reshape(chunks, CHUNK, NB, D)
        per_k = [u4[:, k].reshape(chunks * NB, D) for k in range(CHUNK)]
        cols = [per_k[k][:, j * PAIR_CH:(j + 1) * PAIR_CH] for j in range(N_PAIRS) for k in range(CHUNK)]
        u8_ref[sb * chunks * NB:(sb + 1) * chunks * NB, :] = jnp.concatenate(cols, axis=1).astype(BF16)


def _s5_in(x, m, w_in):
    return pl.pallas_call(
        _s5_in_kernel,
        out_shape=jax.ShapeDtypeStruct((SEQ // CHUNK * NB, N_PAIRS * PAIR_W), BF16),
        grid=(SEQ // S5_BLOCK,),
        in_specs=[
            S5_X_SPEC,
            _mod_spec(1),
            _layer_spec(w_in.shape, 0),
        ],
        out_specs=S5_U8_SPEC,
        compiler_params=_cparams("arbitrary"),
        name="s5_in_proj",
    )(x, m, w_in)


def _s5_core_kernel(u8_ref, wb_ref, wck_ref, a8r_ref, a8i_ref, y8_ref):
    half = PAIR_W // 2
    for jj in range(PAIRS_PER_STEP):
        lanes = slice(jj * PAIR_W, (jj + 1) * PAIR_W)
        u8 = u8_ref[:, lanes]
        v = _bdot(u8, wb_ref[jj])
        ar = jnp.broadcast_to(a8r_ref[:, jj * half:(jj + 1) * half], (NB, half))
        ai = jnp.broadcast_to(a8i_ref[:, jj * half:(jj + 1) * half], (NB, half))
        xr = jnp.zeros((NB, half), F32)
        xi = jnp.zeros((NB, half), F32)
        prev_r, prev_i = [], []
        for s in range(SEQ // CHUNK):
            prev_r.append(xr)
            prev_i.append(xi)
            rows = slice(s * NB, (s + 1) * NB)
            xr, xi = ar * xr - ai * xi + v[rows, :half], ar * xi + ai * xr + v[rows, half:]
        x_prev = jnp.concatenate([jnp.concatenate(prev_r, axis=0), jnp.concatenate(prev_i, axis=0)], axis=1)
        lhs = jnp.concatenate([x_prev.astype(BF16), u8], axis=1)
        y8_ref[:, lanes] = lax.dot_general(lhs, wck_ref[jj], (((1,), (1,)), ((), ())), preferred_element_type=F32)


def _s5_core(u8, wb, wck, a8r, a8i):
    n_rows = u8.shape[0]
    w = PAIRS_PER_STEP * PAIR_W
    return pl.pallas_call(
        _s5_core_kernel,
        out_shape=jax.ShapeDtypeStruct(u8.shape, F32),
        grid=(N_PAIRS // PAIRS_PER_STEP,),
        in_specs=[
            pl.BlockSpec((n_rows, w), lambda i: (0, i)),
            pl.BlockSpec((PAIRS_PER_STEP, PAIR_W, PAIR_W), lambda i: (i, 0, 0)),
            pl.BlockSpec((PAIRS_PER_STEP, PAIR_W, 2 * PAIR_W), lambda i: (i, 0, 0)),
            pl.BlockSpec((1, w // 2), lambda i: (0, i)),
            pl.BlockSpec((1, w // 2), lambda i: (0, i)),
        ],
        out_specs=pl.BlockSpec((n_rows, w), lambda i: (0, i)),
        compiler_params=_cparams("arbitrary"),
        name="s5_recurrence",
    )(u8, wb, wck, a8r, a8i)


def _s5_out_kernel(y8_ref, x_ref, m_ref, w_glu_ref, b_glu_ref, w_out_ref, g_ref, b_ref, ca_ref, cb_ref,
                   o_ref, cab_ref, cbb_ref):
    _cast_slices((ca_ref, cb_ref), (cab_ref, cbb_ref))
    gate = m_ref[GATE]
    steps = S5_OUT_STEPS
    chunks = steps // CHUNK
    for sb in range(S5_BLOCK // steps):
        y8 = y8_ref[sb * chunks * NB:(sb + 1) * chunks * NB, :]
        per_k = [jnp.concatenate([y8[:, j * PAIR_W + k * PAIR_CH:j * PAIR_W + (k + 1) * PAIR_CH]
                                  for j in range(N_PAIRS)], axis=1) for k in range(CHUNK)]
        y = jnp.stack([p.reshape(chunks, NB, D) for p in per_k], axis=1).reshape(steps * NB, D)

        half_y = 0.5 * y
        g = half_y + half_y * jnp.tanh(y * (GELU_C + (GELU_C * 0.044715) * (y * y)))
        zz = g * jax.nn.sigmoid(_bdot(g.astype(BF16), w_glu_ref[...]) + b_glu_ref[...])
        out = jnp.swapaxes(_bdot(zz.astype(BF16), w_out_ref[...]).reshape(steps, NB, D), 0, 1)
        t_rows = slice(sb * steps, (sb + 1) * steps)
        res = ALPHA * x_ref[:, t_rows, :] + (1.0 + gate)[:, None, :] * out
        o_ref[:, t_rows, :] = _layer_norm(res, g_ref[...], b_ref[...])


def _s5_out(y8, x, m, w_glu, b_glu, w_out, g, b, casts):
    x_spec = S5_X_SPEC
    grid = (SEQ // S5_BLOCK,)
    cast = [_cast_specs(w.shape, grid, l) for w, l in casts]
    return pl.pallas_call(
        _s5_out_kernel,
        out_shape=[jax.ShapeDtypeStruct((NB, SEQ, D), F32)] + [c[2] for c in cast],
        grid=grid,
        in_specs=[
            S5_U8_SPEC,
            x_spec,
            _mod_spec(1),
            _layer_spec(w_glu.shape, 0),
            _const_spec((1, D)),
            _layer_spec(w_out.shape, 0),
            _const_spec((1, D)),
            _const_spec((1, D)),
        ] + [c[0] for c in cast],
        out_specs=[x_spec] + [c[1] for c in cast],
        compiler_params=_cparams("arbitrary"),
        name="s5_out_norm",
    )(y8, x, m, w_glu, b_glu, w_out, g, b, *[w for w, _ in casts])


def _rotate_half_cols(w):
    half = w.shape[-1] // 2
    return jnp.concatenate([-w[..., half:], w[..., :half]], axis=-1)


def _mla_weights(w_in, w_qb):
    k_pe = w_in[:, Q_LORA + KV_LORA:]
    w_in_ext = jnp.concatenate([w_in, _rotate_half_cols(k_pe)], axis=1)
    wq = w_qb.reshape(Q_LORA, HEADS, QK_NOPE + QK_ROPE) * Q_SCALE
    wq_ext = jnp.concatenate([wq, _rotate_half_cols(wq[..., QK_NOPE:])], axis=-1)
    return w_in_ext.astype(BF16), wq_ext.reshape(Q_LORA, HEADS * HEAD_W).astype(BF16)


def kernel(x, c, positions, mla_w_in, mla_q_norm, mla_w_qb, mla_kv_norm, mla_w_kvb, mla_w_o, ssm_w_in, ssm_log_dt, ssm_a_re, ssm_a_im, ssm_b_re, ssm_b_im, ssm_c_re, ssm_c_im, ssm_d, ssm_w_glu, ssm_b_glu, ssm_w_out, mlp_w1, mlp_b1, mlp_w2, mlp_b2, mod_mix_w, mod_mix_b, mod_ffn_w, mod_ffn_b, ln_mix_g, ln_mix_b, ln_ffn_g, ln_ffn_b):
    row = lambda v: v.reshape(1, -1)
    m_mix = _modulation(c, mod_mix_w, mod_mix_b)
    m_ffn = _modulation(c, mod_ffn_w, mod_ffn_b)

    inv_freq = ROPE_THETA ** (-jnp.arange(0, QK_ROPE, 2, dtype=F32) / QK_ROPE)
    freq = jnp.tile(inv_freq, 4).reshape(1, 128)
    w_in_ext, w_qb_ext = _mla_weights(mla_w_in[0], mla_w_qb[0])
    q, kn, kr, v, w1_0, w2_0, w_o = _mla_proj(
        x, m_mix, positions.reshape(NB, SEQ, 1), freq, w_in_ext, row(mla_q_norm[0]), row(mla_kv_norm[0]),
        w_qb_ext, mla_w_kvb[0].astype(BF16), ((mlp_w1, 0), (mlp_w2, 0), (mla_w_o, 0)))
    o = _attention(q, kn, kr, v)
    x2, s_w_in, s_w_glu, s_w_out = _attn_out_ffn(
        o, x, m_mix, m_ffn, 0, w_o, row(ln_mix_g[0]), row(ln_mix_b[0]),
        w1_0, row(mlp_b1[0]), w2_0, row(mlp_b2[0]), row(ln_ffn_g[0]), row(ln_ffn_b[0]),
        ((ssm_w_in, 0), (ssm_w_glu, 0), (ssm_w_out, 0)))

    a8r, a8i, wb, wck = _s5_discretise(ssm_log_dt[0], ssm_a_re[0], ssm_a_im[0], ssm_b_re[0], ssm_b_im[0],
                                       ssm_c_re[0], ssm_c_im[0], ssm_d[0])
    u8 = _s5_in(x2, m_mix, s_w_in)
    y8 = _s5_core(u8, wb, wck, row(a8r), row(a8i))
    x3, w1_1, w2_1 = _s5_out(y8, x2, m_mix, s_w_glu, row(ssm_b_glu[0]), s_w_out, row(ln_mix_g[1]), row(ln_mix_b[1]),
                             ((mlp_w1, 1), (mlp_w2, 1)))
    return _ffn(x3, m_ffn, 1, w1_1, row(mlp_b1[1]), w2_1, row(mlp_b2[1]), row(ln_ffn_g[1]), row(ln_ffn_b[1]))
```

```python
import functools
import math

import jax
import jax.numpy as jnp
from jax import lax
from jax.experimental import pallas as pl
from jax.experimental.pallas import tpu as pltpu

F32 = jnp.float32
BF16 = jnp.bfloat16

D = 1024
NB = 8
SEQ = 2048
HEADS = 8
QK_NOPE = 128
QK_ROPE = 64
V_DIM = 128
Q_LORA = 256
KV_LORA = 128
ROPE_THETA = 10000.0
GROUP_CH = 16
N_GROUPS = 64
STATE = 64
D_FF = 4 * D
DEPTH = 2
ALPHA = (2 * DEPTH) ** 0.25
LN_EPS = 1e-5
RMS_EPS = 1e-6
Q_SCALE = math.log2(math.e) / math.sqrt(QK_NOPE + QK_ROPE)
NEG_BIG = -1e30
GELU_C = math.sqrt(2.0 / math.pi)

HEAD_W = 256
CHUNK = 8
N_PAIRS = N_GROUPS // 2
PAIR_CH = 2 * GROUP_CH
PAIR_W = CHUNK * PAIR_CH
assert PAIR_W == 4 * STATE

VMEM_LIMIT = 56 * 1024 * 1024
PROJ_ROWS = 512
PROJ_SUB = 2
ATTN_BLK = 512
ATTN_HEADS = 4
_ATTN_SLOTS = 2 * (SEQ // ATTN_BLK) - 1
FFN_ROWS = 512
FFN_SUB = 2
FF_CHUNK = 1024
S5_BLOCK = 128
S5_IN_STEPS = 64
S5_OUT_STEPS = 32
PAIRS_PER_STEP = 2


VEC_LN_MIX_G, VEC_LN_MIX_B, VEC_LN_FFN_G, VEC_LN_FFN_B, VEC_MLP_B2 = 0, DEPTH, 2 * DEPTH, 3 * DEPTH, 4 * DEPTH
VEC_B_GLU = 5 * DEPTH
VEC_MLP_B1 = VEC_B_GLU + 1
VEC_MOD_MIX_B = VEC_MLP_B1 + DEPTH * (D_FF // D)
VEC_MOD_FFN_B = VEC_MOD_MIX_B + 3 * DEPTH
VEC_Q_NORM = VEC_MOD_FFN_B + 3 * DEPTH
VEC_KV_NORM = VEC_Q_NORM + 1
VEC_ROWS = -(-(VEC_KV_NORM + 1) // 8) * 8
assert FF_CHUNK == D


def _pack_vectors(ln_mix_g, ln_mix_b, ln_ffn_g, ln_ffn_b, mlp_b2, b_glu, mlp_b1, mod_mix_b, mod_ffn_b, q_norm, kv_norm):
    pad = lambda v: jnp.pad(v.reshape(1, -1), ((0, 0), (0, D - v.size)))
    rows = [ln_mix_g, ln_mix_b, ln_ffn_g, ln_ffn_b, mlp_b2, b_glu.reshape(1, D), mlp_b1.reshape(-1, D),
            mod_mix_b.reshape(-1, D), mod_ffn_b.reshape(-1, D), pad(q_norm), pad(kv_norm)]
    table = jnp.concatenate(rows, axis=0)
    return jnp.pad(table, ((0, VEC_ROWS - table.shape[0]), (0, 0)))


def _cparams(*sem):
    return pltpu.CompilerParams(dimension_semantics=sem, vmem_limit_bytes=VMEM_LIMIT)


def _const_spec(shape):
    nd = len(shape)
    return pl.BlockSpec(shape, lambda *_: (0,) * nd, pipeline_mode=pl.Buffered(1))


def _cast_specs(shape, grid, layer):
    _, rows, cols = shape
    n_steps = math.prod(grid)

    def step(*ids):
        s = 0
        for i, n in zip(ids, grid):
            s = s * n + i
        return s

    block = (1, rows // n_steps, cols)
    return (pl.BlockSpec(block, lambda *ids: (layer, step(*ids), 0)),
            pl.BlockSpec(block, lambda *ids: (0, step(*ids), 0)),
            jax.ShapeDtypeStruct((1, rows, cols), BF16))


def _cast_slices(src_refs, dst_refs):
    for src, dst in zip(src_refs, dst_refs):
        dst[...] = src[...].astype(dst.dtype)


def _layer_norm(v, g, b):
    mu = jnp.mean(v, axis=-1, keepdims=True)
    vc = v - mu
    var = jnp.mean(vc * vc, axis=-1, keepdims=True)
    return vc * lax.rsqrt(var + LN_EPS) * g + b


def _bdot(a, b):
    return jnp.dot(a, b, preferred_element_type=F32)


def _mod_kernel(c_ref, w_ref, vec_ref, o_ref, *, bias_row):
    c = c_ref[...]
    cs = c * jax.nn.sigmoid(c)
    bias = vec_ref[pl.ds(bias_row + 3 * pl.program_id(0) + pl.program_id(1), 1), :]
    o_ref[...] = _bdot(cs.astype(BF16), w_ref[...].astype(BF16)) + bias


def _modulation(c, w, vec, bias_row):
    n_layers = w.shape[0]
    return pl.pallas_call(
        functools.partial(_mod_kernel, bias_row=bias_row),
        out_shape=jax.ShapeDtypeStruct((n_layers, 3, NB, D), F32),
        grid=(n_layers, 3),
        in_specs=[
            pl.BlockSpec((NB, D), lambda l, j: (0, 0)),
            pl.BlockSpec((None, D, D), lambda l, j: (l, 0, j)),
            _const_spec(vec.shape),
        ],
        out_specs=pl.BlockSpec((None, None, NB, D), lambda l, j: (l, j, 0, 0)),
        compiler_params=_cparams("arbitrary", "arbitrary"),
        name="adaln_modulation",
    )(c, w, vec)


def _mod_spec(layer):
    return pl.BlockSpec((None, 3, NB, D), lambda *_: (layer, 0, 0, 0), pipeline_mode=pl.Buffered(1))


SHIFT, SCALE, GATE = 0, 1, 2


def _mla_proj_kernel(x_ref, m_ref, pos_ref, freq_ref, w_in_ref, vec_ref, w_qb_ref, w_kvb_ref,
                     ca_ref, cb_ref, cc_ref, q_ref, kn_ref, kr_ref, v_ref, cab_ref, cbb_ref, ccb_ref):
    _cast_slices((ca_ref, cb_ref, cc_ref), (cab_ref, cbb_ref, ccb_ref))
    qn_ref = vec_ref.at[VEC_Q_NORM:VEC_Q_NORM + 1, 0:Q_LORA]
    kvn_ref = vec_ref.at[VEC_KV_NORM:VEC_KV_NORM + 1, 0:KV_LORA]
    b = pl.program_id(0)
    shift = m_ref[SHIFT, pl.ds(b, 1), :]
    scale = m_ref[SCALE, pl.ds(b, 1), :]

    def rms(v, g):
        return v * lax.rsqrt(jnp.mean(v * v, axis=-1, keepdims=True) + RMS_EPS) * g

    for sb in range(PROJ_SUB):
        rows = slice(sb * PROJ_ROWS, (sb + 1) * PROJ_ROWS)
        h = (x_ref[0, rows, :] * (1.0 + scale) + shift).astype(BF16)
        z = _bdot(h, w_in_ref[...])
        cq = rms(z[:, :Q_LORA], qn_ref[...]).astype(BF16)
        ckv = rms(z[:, Q_LORA:Q_LORA + KV_LORA], kvn_ref[...]).astype(BF16)

        ang = pos_ref[0, rows, :].astype(F32) * freq_ref[...]
        lane = lax.broadcasted_iota(jnp.int32, ang.shape, 1)
        mult = jnp.where(lane < QK_ROPE, jnp.cos(ang), jnp.sin(ang))

        def rope(slab):
            s = slab * mult
            return s + pltpu.roll(s, QK_ROPE, axis=1)

        kr_ref[0, rows, :] = jnp.where(lane < QK_ROPE, rope(z[:, Q_LORA + KV_LORA:]), 0.0).astype(BF16)

        q_all = _bdot(cq, w_qb_ref[...])
        kv = _bdot(ckv, w_kvb_ref[...])
        for hd in range(HEADS):
            c0 = hd * HEAD_W
            q_ref[0, hd, rows, 0:QK_NOPE] = q_all[:, c0:c0 + QK_NOPE].astype(BF16)
            q_ref[0, hd, rows, QK_NOPE:HEAD_W] = rope(q_all[:, c0 + QK_NOPE:c0 + HEAD_W]).astype(BF16)
            kn_ref[0, hd, rows, :] = kv[:, c0:c0 + QK_NOPE].astype(BF16)
            v_ref[0, hd, rows, :] = kv[:, c0 + QK_NOPE:c0 + HEAD_W].astype(BF16)


def _mla_proj(x, m, pos, freq, w_in, vec, w_qb, w_kvb, casts):
    nt = SEQ // (PROJ_SUB * PROJ_ROWS)
    blk = PROJ_SUB * PROJ_ROWS
    grid = (NB, nt)
    cast = [_cast_specs(w.shape, grid, layer) for w, layer in casts]
    return pl.pallas_call(
        _mla_proj_kernel,
        out_shape=[
            jax.ShapeDtypeStruct((NB, HEADS, SEQ, HEAD_W), BF16),
            jax.ShapeDtypeStruct((NB, HEADS, SEQ, QK_NOPE), BF16),
            jax.ShapeDtypeStruct((NB, SEQ, HEAD_W - QK_NOPE), BF16),
            jax.ShapeDtypeStruct((NB, HEADS, SEQ, V_DIM), BF16),
        ] + [c[2] for c in cast],
        grid=grid,
        in_specs=[
            pl.BlockSpec((1, blk, D), lambda b, i: (b, i, 0)),
            _mod_spec(0),
            pl.BlockSpec((1, blk, 1), lambda b, i: (b, i, 0)),
            _const_spec((1, 128)),
            _const_spec(w_in.shape),
            _const_spec(vec.shape),
            _const_spec(w_qb.shape),
            _const_spec(w_kvb.shape),
        ] + [c[0] for c in cast],
        out_specs=[
            pl.BlockSpec((1, HEADS, blk, HEAD_W), lambda b, i: (b, 0, i, 0)),
            pl.BlockSpec((1, HEADS, blk, QK_NOPE), lambda b, i: (b, 0, i, 0)),
            pl.BlockSpec((1, blk, HEAD_W - QK_NOPE), lambda b, i: (b, i, 0)),
            pl.BlockSpec((1, HEADS, blk, V_DIM), lambda b, i: (b, 0, i, 0)),
        ] + [c[1] for c in cast],
        compiler_params=_cparams("arbitrary", "arbitrary"),
        name="mla_projections",
    )(x, m, pos, freq, w_in, vec, w_qb, w_kvb, *[w for w, _ in casts])


def _lane_groups(v):
    return [v[:, c:c + 128] for c in range(0, v.shape[1], 128)]


def _attn_kernel(q_ref, kn_ref, kr_ref, v_ref, o_ref, s_ref):
    blk = ATTN_BLK
    nq = SEQ // blk
    row = lax.broadcasted_iota(jnp.int32, (blk, blk), 0)
    col = lax.broadcasted_iota(jnp.int32, (blk, blk), 1)

    def rows(n):
        return slice(n * blk, (n + 1) * blk)

    def slot(hd, n, j):
        return hd * _ATTN_SLOTS + (0 if n % 2 == 0 else nq - 1) + j

    def pass1(hd, n):
        q = q_ref[hd, rows(n), :]
        m = None
        for j in range(n + 1):
            k = jnp.concatenate([kn_ref[hd, rows(j), :], kr_ref[rows(j), :]], axis=1)
            s = lax.dot_general(q, k, (((1,), (1,)), ((), ())), preferred_element_type=F32)
            if j == n:
                s = jnp.where(col <= row, s, NEG_BIG)
            s_ref[slot(hd, n, j)] = s
            m = functools.reduce(jnp.maximum, _lane_groups(s) + ([] if m is None else [m]))
        return jnp.broadcast_to(jnp.max(m, axis=-1, keepdims=True), (blk, 128))

    def pass2(hd, n, mb):
        ps = [jnp.exp2(sg - mb) for j in range(n + 1) for sg in _lane_groups(s_ref[slot(hd, n, j)])]
        l = jnp.sum(functools.reduce(jnp.add, ps), axis=-1, keepdims=True)
        acc = _bdot(jnp.concatenate(ps, axis=1).astype(BF16), v_ref[hd, 0:(n + 1) * blk, :])
        o_ref[hd, rows(n), :] = (acc / l).astype(BF16)

    mb = [pass1(hd, 0) for hd in range(ATTN_HEADS)]
    for n in range(nq):
        for hd in range(ATTN_HEADS):
            mb_next = pass1(hd, n + 1) if n + 1 < nq else None
            pass2(hd, n, mb[hd])
            mb[hd] = mb_next


def _attention(q, kn, kr, v):
    head_spec = lambda w: pl.BlockSpec((None, ATTN_HEADS, SEQ, w), lambda b, h: (b, h, 0, 0))
    kr_spec = pl.BlockSpec((None, SEQ, HEAD_W - QK_NOPE), lambda b, h: (b, 0, 0))
    return pl.pallas_call(
        _attn_kernel,
        out_shape=jax.ShapeDtypeStruct((NB, HEADS, SEQ, V_DIM), BF16),
        grid=(NB, HEADS // ATTN_HEADS),
        in_specs=[head_spec(HEAD_W), head_spec(QK_NOPE), kr_spec, head_spec(V_DIM)],
        out_specs=head_spec(V_DIM),
        scratch_shapes=[pltpu.VMEM((ATTN_HEADS * _ATTN_SLOTS, ATTN_BLK, ATTN_BLK), F32)],
        compiler_params=_cparams("arbitrary", "arbitrary"),
        name="mla_attention",
    )(q, kn, kr, v)


def _vec_row(vec_ref, row):
    return vec_ref[row:row + 1, :]


def _ffn_rows(x, shift, scale, gate, w1_ref, w2_ref, vec_ref, layer):
    h = (x * (1.0 + scale) + shift).astype(BF16)
    acc = jnp.zeros(x.shape, F32)
    for c in range(D_FF // FF_CHUNK):
        cols = slice(c * FF_CHUNK, (c + 1) * FF_CHUNK)
        b1 = _vec_row(vec_ref, VEC_MLP_B1 + layer * (D_FF // FF_CHUNK) + c)
        a = jnp.maximum(_bdot(h, w1_ref[:, cols]) + b1, 0.0)
        acc = acc + _bdot((a * a).astype(BF16), w2_ref[cols, :])
    y = acc + _vec_row(vec_ref, VEC_MLP_B2 + layer)
    return _layer_norm(ALPHA * x + (1.0 + gate) * y,
                       _vec_row(vec_ref, VEC_LN_FFN_G + layer), _vec_row(vec_ref, VEC_LN_FFN_B + layer))


def _batch_mod(m_ref):
    b = pl.program_id(0)
    return [m_ref[i, pl.ds(b, 1), :] for i in (SHIFT, SCALE, GATE)]


def _ffn_kernel(x_ref, m_ref, w1_ref, w2_ref, vec_ref, y_ref, *, layer):
    mod = _batch_mod(m_ref)
    for sb in range(FFN_SUB):
        rows = slice(sb * FFN_ROWS, (sb + 1) * FFN_ROWS)
        y_ref[rows, :] = _ffn_rows(x_ref[rows, :], *mod, w1_ref, w2_ref, vec_ref, layer)


def _attn_out_ffn_kernel(o_ref, x_ref, mm_ref, mf_ref, wo_ref, w1_ref, w2_ref, vec_ref, sa_ref, sb_ref, sc_ref,
                         y_ref, sab_ref, sbb_ref, scb_ref, *, layer):
    _cast_slices((sa_ref, sb_ref, sc_ref), (sab_ref, sbb_ref, scb_ref))
    gm_ref = vec_ref.at[VEC_LN_MIX_G + layer:VEC_LN_MIX_G + layer + 1]
    bm_ref = vec_ref.at[VEC_LN_MIX_B + layer:VEC_LN_MIX_B + layer + 1]
    gate_mix = _batch_mod(mm_ref)[GATE]
    mod = _batch_mod(mf_ref)
    for sb in range(FFN_SUB):
        rows = slice(sb * FFN_ROWS, (sb + 1) * FFN_ROWS)
        o = jnp.concatenate([o_ref[hd, rows, :] for hd in range(HEADS)], axis=1)
        y_ref[rows, :] = _layer_norm(ALPHA * x_ref[rows, :] + (1.0 + gate_mix) * _bdot(o, wo_ref[...]),
                                     gm_ref[...], bm_ref[...])
    for sb in range(FFN_SUB):
        rows = slice(sb * FFN_ROWS, (sb + 1) * FFN_ROWS)
        y_ref[rows, :] = _ffn_rows(y_ref[rows, :], *mod, w1_ref, w2_ref, vec_ref, layer)


_FFN_X_SPEC = pl.BlockSpec((None, FFN_SUB * FFN_ROWS, D), lambda b, i: (b, i, 0))
_FFN_GRID = (NB, SEQ // (FFN_SUB * FFN_ROWS))


def _layer_spec(shape, layer):
    return pl.BlockSpec((None,) + tuple(shape[1:]), lambda *_: (layer, 0, 0), pipeline_mode=pl.Buffered(1))


def _ffn(x, m, layer, w1, w2, vec):
    return pl.pallas_call(
        functools.partial(_ffn_kernel, layer=layer),
        out_shape=jax.ShapeDtypeStruct((NB, SEQ, D), F32),
        grid=_FFN_GRID,
        in_specs=[_FFN_X_SPEC, _mod_spec(layer), _layer_spec(w1.shape, 0), _layer_spec(w2.shape, 0),
                  _const_spec(vec.shape)],
        out_specs=_FFN_X_SPEC,
        compiler_params=_cparams("arbitrary", "arbitrary"),
        name="ffn_norm",
    )(x, m, w1, w2, vec)


def _attn_out_ffn(o, x, m_mix, m_ffn, layer, w_o, w1, w2, vec, casts):
    o_spec = pl.BlockSpec((None, HEADS, FFN_SUB * FFN_ROWS, V_DIM), lambda b, i: (b, 0, i, 0))
    cast = [_cast_specs(w.shape, _FFN_GRID, l) for w, l in casts]
    return pl.pallas_call(
        functools.partial(_attn_out_ffn_kernel, layer=layer),
        out_shape=[jax.ShapeDtypeStruct((NB, SEQ, D), F32)] + [c[2] for c in cast],
        grid=_FFN_GRID,
        in_specs=[o_spec, _FFN_X_SPEC, _mod_spec(layer), _mod_spec(layer), _layer_spec(w_o.shape, 0),
                  _layer_spec(w1.shape, 0), _layer_spec(w2.shape, 0), _const_spec(vec.shape)]
                 + [c[0] for c in cast],
        out_specs=[_FFN_X_SPEC] + [c[1] for c in cast],
        compiler_params=_cparams("arbitrary", "arbitrary"),
        name="attn_out_ffn_norm",
    )(o, x, m_mix, m_ffn, w_o, w1, w2, vec, *[w for w, _ in casts])


def _cmul(ar, ai, br, bi):
    return ar * br - ai * bi, ar * bi + ai * br


def _pair_slot(piece, zero, slot):
    return [piece, zero] if slot == 0 else [zero, piece]


def _s5_disc_kernel(ldt_ref, are_ref, aim_ref, bre_ref, bim_ref, cre_ref, cim_ref, d_ref,
                    a8r_ref, a8i_ref, wb_ref, wck_ref):
    lr = are_ref[...]
    li = aim_ref[...]
    dt = jnp.exp(ldt_ref[...])
    mag = jnp.exp(lr * dt)
    ab_re = mag * jnp.cos(li * dt)
    ab_im = mag * jnp.sin(li * dt)
    den = lr * lr + li * li
    nr = ab_re - 1.0
    coef_re = (nr * lr + ab_im * li) / den
    coef_im = (ab_im * lr - nr * li) / den
    bb_re, bb_im = _cmul(coef_re, coef_im, bre_ref[...], bim_ref[...])
    c_re = cre_ref[...]
    c_im = cim_ref[...]

    pw = [(jnp.ones_like(ab_re), jnp.zeros_like(ab_im))]
    for _ in range(CHUNK):
        pw.append(_cmul(pw[-1][0], pw[-1][1], ab_re, ab_im))
    a8r_ref[...] = pw[CHUNK][0]
    a8i_ref[...] = pw[CHUNK][1]

    def pairs(v):
        v4 = v.reshape(N_PAIRS, 2, v.shape[1], v.shape[2])
        return v4[:, 0], v4[:, 1]

    z64 = jnp.zeros((N_PAIRS, GROUP_CH, STATE), F32)
    z16 = jnp.zeros((N_PAIRS, GROUP_CH, GROUP_CH), F32)
    eye = (lax.broadcasted_iota(jnp.int32, (GROUP_CH, GROUP_CH), 0)
           == lax.broadcasted_iota(jnp.int32, (GROUP_CH, GROUP_CH), 1)).astype(F32)

    ab_l = [_cmul(pw[l][0], pw[l][1], bb_re, bb_im) for l in range(CHUNK)]
    wb_rows = []
    for k in range(CHUNK):
        m_re, m_im = ab_l[CHUNK - 1 - k]
        for slot in range(2):
            wb_rows.append(jnp.concatenate(_pair_slot(pairs(m_re)[slot], z64, slot)
                                           + _pair_slot(pairs(m_im)[slot], z64, slot), axis=2))
    wb_ref[...] = jnp.concatenate(wb_rows, axis=1).astype(BF16)

    def nt(a, b):
        return lax.dot_general(a.astype(BF16), b.astype(BF16), (((2,), (2,)), ((0,), (0,))),
                               preferred_element_type=F32)

    k_l = [nt(c_re, ab_l[l][0]) - nt(c_im, ab_l[l][1]) for l in range(CHUNK)]
    k_l[0] = k_l[0] + d_ref[...] * eye[None]

    wck_rows = []
    for kp in range(CHUNK):
        ca_re, ca_im = _cmul(c_re, c_im, pw[kp + 1][0], pw[kp + 1][1])
        for slot in range(2):
            state_cols = (_pair_slot(pairs(ca_re)[slot], z64, slot)
                          + _pair_slot(-pairs(ca_im)[slot], z64, slot))
            direct_cols = []
            for k in range(CHUNK):
                blk = pairs(k_l[kp - k])[slot] if k <= kp else z16
                direct_cols += _pair_slot(blk, z16, slot)
            wck_rows.append(jnp.concatenate(state_cols + direct_cols, axis=2))
    wck_ref[...] = jnp.concatenate(wck_rows, axis=1).astype(BF16)


def _s5_discretise(log_dt, a_re, a_im, b_re, b_im, c_re, c_im, d_skip):
    g3 = (N_GROUPS, 1, STATE)
    return pl.pallas_call(
        _s5_disc_kernel,
        out_shape=(jax.ShapeDtypeStruct(g3, F32), jax.ShapeDtypeStruct(g3, F32),
                   jax.ShapeDtypeStruct((N_PAIRS, PAIR_W, PAIR_W), BF16),
                   jax.ShapeDtypeStruct((N_PAIRS, PAIR_W, 2 * PAIR_W), BF16)),
        compiler_params=pltpu.CompilerParams(vmem_limit_bytes=VMEM_LIMIT),
        name="s5_discretise",
    )(log_dt.reshape(N_GROUPS, 1, 1), a_re.reshape(g3), a_im.reshape(g3),
      jnp.swapaxes(b_re, 1, 2), jnp.swapaxes(b_im, 1, 2), c_re, c_im, d_skip.reshape(N_GROUPS, GROUP_CH, 1))


S5_X_SPEC = pl.BlockSpec((NB, S5_BLOCK, D), lambda i: (0, i, 0))
S5_U8_SPEC = pl.BlockSpec((S5_BLOCK // CHUNK * NB, N_PAIRS * PAIR_W), lambda i: (i, 0))


def _s5_in_kernel(x_ref, m_ref, w_in_ref, u8_ref):
    shift = m_ref[SHIFT]
    scale = m_ref[SCALE]
    steps = S5_IN_STEPS
    chunks = steps // CHUNK
    for sb in range(S5_BLOCK // steps):
        x3 = jnp.swapaxes(x_ref[:, sb * steps:(sb + 1) * steps, :], 0, 1)
        h = (x3 * (1.0 + scale)[None] + shift[None]).reshape(steps * NB, D).astype(BF16)
        u = _bdot(h, w_in_ref[...])
        u4 = u.reshape(chunks, CHUNK, NB, D)
        per_k = [u4[:, k].reshape(chunks * NB, D) for k in range(CHUNK)]
        cols = [per_k[k][:, j * PAIR_CH:(j + 1) * PAIR_CH] for j in range(N_PAIRS) for k in range(CHUNK)]
        u8_ref[sb * chunks * NB:(sb + 1) * chunks * NB, :] = jnp.concatenate(cols, axis=1).astype(BF16)


def _s5_in(x, m, w_in):
    return pl.pallas_call(
        _s5_in_kernel,
        out_shape=jax.ShapeDtypeStruct((SEQ // CHUNK * NB, N_PAIRS * PAIR_W), BF16),
        grid=(SEQ // S5_BLOCK,),
        in_specs=[
            S5_X_SPEC,
            _mod_spec(1),
            _layer_spec(w_in.shape, 0),
        ],
        out_specs=S5_U8_SPEC,
        compiler_params=_cparams("arbitrary"),
        name="s5_in_proj",
    )(x, m, w_in)


def _s5_core_kernel(u8_ref, wb_ref, wck_ref, a8r_ref, a8i_ref, y8_ref):
    half = PAIR_W // 2
    for jj in range(PAIRS_PER_STEP):
        lanes = slice(jj * PAIR_W, (jj + 1) * PAIR_W)
        u8 = u8_ref[:, lanes]
        v = _bdot(u8, wb_ref[jj])
        ar = jnp.broadcast_to(a8r_ref[:, jj * half:(jj + 1) * half], (NB, half))
        ai = jnp.broadcast_to(a8i_ref[:, jj * half:(jj + 1) * half], (NB, half))
        xr = jnp.zeros((NB, half), F32)
        xi = jnp.zeros((NB, half), F32)
        prev_r, prev_i = [], []
        for s in range(SEQ // CHUNK):
            prev_r.append(xr)
            prev_i.append(xi)
            rows = slice(s * NB, (s + 1) * NB)
            xr, xi = ar * xr - ai * xi + v[rows, :half], ar * xi + ai * xr + v[rows, half:]
        x_prev = jnp.concatenate([jnp.concatenate(prev_r, axis=0), jnp.concatenate(prev_i, axis=0)], axis=1)
        lhs = jnp.concatenate([x_prev.astype(BF16), u8], axis=1)
        y8_ref[:, lanes] = lax.dot_general(lhs, wck_ref[jj], (((1,), (1,)), ((), ())), preferred_element_type=F32)


def _s5_core(u8, wb, wck, a8r, a8i):
    n_rows = u8.shape[0]
    w = PAIRS_PER_STEP * PAIR_W
    return pl.pallas_call(
        _s5_core_kernel,
        out_shape=jax.ShapeDtypeStruct(u8.shape, F32),
        grid=(N_PAIRS // PAIRS_PER_STEP,),
        in_specs=[
            pl.BlockSpec((n_rows, w), lambda i: (0, i)),
            pl.BlockSpec((PAIRS_PER_STEP, PAIR_W, PAIR_W), lambda i: (i, 0, 0)),
            pl.BlockSpec((PAIRS_PER_STEP, PAIR_W, 2 * PAIR_W), lambda i: (i, 0, 0)),
            pl.BlockSpec((1, w // 2), lambda i: (0, i)),
            pl.BlockSpec((1, w // 2), lambda i: (0, i)),
        ],
        out_specs=pl.BlockSpec((n_rows, w), lambda i: (0, i)),
        compiler_params=_cparams("arbitrary"),
        name="s5_recurrence",
    )(u8, wb, wck, a8r, a8i)


def _s5_out_kernel(y8_ref, x_ref, m_ref, w_glu_ref, w_out_ref, vec_ref, ca_ref, cb_ref, o_ref, cab_ref, cbb_ref):
    _cast_slices((ca_ref, cb_ref), (cab_ref, cbb_ref))
    b_glu_ref = vec_ref.at[VEC_B_GLU:VEC_B_GLU + 1]
    g_ref = vec_ref.at[VEC_LN_MIX_G + 1:VEC_LN_MIX_G + 2]
    b_ref = vec_ref.at[VEC_LN_MIX_B + 1:VEC_LN_MIX_B + 2]
    gate = m_ref[GATE]
    steps = S5_OUT_STEPS
    chunks = steps // CHUNK
    for sb in range(S5_BLOCK // steps):
        y8 = y8_ref[sb * chunks * NB:(sb + 1) * chunks * NB, :]
        per_k = [jnp.concatenate([y8[:, j * PAIR_W + k * PAIR_CH:j * PAIR_W + (k + 1) * PAIR_CH]
                                  for j in range(N_PAIRS)], axis=1) for k in range(CHUNK)]
        y = jnp.stack([p.reshape(chunks, NB, D) for p in per_k], axis=1).reshape(steps * NB, D)

        half_y = 0.5 * y
        g = half_y + half_y * jnp.tanh(y * (GELU_C + (GELU_C * 0.044715) * (y * y)))
        zz = g * jax.nn.sigmoid(_bdot(g.astype(BF16), w_glu_ref[...]) + b_glu_ref[...])
        out = jnp.swapaxes(_bdot(zz.astype(BF16), w_out_ref[...]).reshape(steps, NB, D), 0, 1)
        t_rows = slice(sb * steps, (sb + 1) * steps)
        res = ALPHA * x_ref[:, t_rows, :] + (1.0 + gate)[:, None, :] * out
        o_ref[:, t_rows, :] = _layer_norm(res, g_ref[...], b_ref[...])


def _s5_out(y8, x, m, w_glu, w_out, vec, casts):
    x_spec = S5_X_SPEC
    grid = (SEQ // S5_BLOCK,)
    cast = [_cast_specs(w.shape, grid, l) for w, l in casts]
    return pl.pallas_call(
        _s5_out_kernel,
        out_shape=[jax.ShapeDtypeStruct((NB, SEQ, D), F32)] + [c[2] for c in cast],
        grid=grid,
        in_specs=[
            S5_U8_SPEC,
            x_spec,
            _mod_spec(1),
            _layer_spec(w_glu.shape, 0),
            _layer_spec(w_out.shape, 0),
            _const_spec(vec.shape),
        ] + [c[0] for c in cast],
        out_specs=[x_spec] + [c[1] for c in cast],
        compiler_params=_cparams("arbitrary"),
        name="s5_out_norm",
    )(y8, x, m, w_glu, w_out, vec, *[w for w, _ in casts])


def _rotate_half_cols(w):
    half = w.shape[-1] // 2
    return jnp.concatenate([-w[..., half:], w[..., :half]], axis=-1)


def _mla_weights(w_in, w_qb):
    k_pe = w_in[:, Q_LORA + KV_LORA:]
    w_in_ext = jnp.concatenate([w_in, _rotate_half_cols(k_pe)], axis=1)
    wq = w_qb.reshape(Q_LORA, HEADS, QK_NOPE + QK_ROPE) * Q_SCALE
    wq_ext = jnp.concatenate([wq, _rotate_half_cols(wq[..., QK_NOPE:])], axis=-1)
    return w_in_ext.astype(BF16), wq_ext.reshape(Q_LORA, HEADS * HEAD_W).astype(BF16)


def kernel(x, c, positions, mla_w_in, mla_q_norm, mla_w_qb, mla_kv_norm, mla_w_kvb, mla_w_o, ssm_w_in, ssm_log_dt, ssm_a_re, ssm_a_im, ssm_b_re, ssm_b_im, ssm_c_re, ssm_c_im, ssm_d, ssm_w_glu, ssm_b_glu, ssm_w_out, mlp_w1, mlp_b1, mlp_w2, mlp_b2, mod_mix_w, mod_mix_b, mod_ffn_w, mod_ffn_b, ln_mix_g, ln_mix_b, ln_ffn_g, ln_ffn_b):
    row = lambda v: v.reshape(1, -1)
    vec = _pack_vectors(ln_mix_g, ln_mix_b, ln_ffn_g, ln_ffn_b, mlp_b2, ssm_b_glu, mlp_b1, mod_mix_b, mod_ffn_b,
                        mla_q_norm, mla_kv_norm)
    m_mix = _modulation(c, mod_mix_w, vec, VEC_MOD_MIX_B)
    m_ffn = _modulation(c, mod_ffn_w, vec, VEC_MOD_FFN_B)

    inv_freq = ROPE_THETA ** (-jnp.arange(0, QK_ROPE, 2, dtype=F32) / QK_ROPE)
    freq = jnp.tile(inv_freq, 4).reshape(1, 128)
    w_in_ext, w_qb_ext = _mla_weights(mla_w_in[0], mla_w_qb[0])
    q, kn, kr, v, w1_0, w2_0, w_o = _mla_proj(
        x, m_mix, positions.reshape(NB, SEQ, 1), freq, w_in_ext, vec, w_qb_ext, mla_w_kvb[0].astype(BF16),
        ((mlp_w1, 0), (mlp_w2, 0), (mla_w_o, 0)))
    o = _attention(q, kn, kr, v)
    x2, s_w_in, s_w_glu, s_w_out = _attn_out_ffn(
        o, x, m_mix, m_ffn, 0, w_o, w1_0, w2_0, vec, ((ssm_w_in, 0), (ssm_w_glu, 0), (ssm_w_out, 0)))

    a8r, a8i, wb, wck = _s5_discretise(ssm_log_dt[0], ssm_a_re[0], ssm_a_im[0], ssm_b_re[0], ssm_b_im[0],
                                       ssm_c_re[0], ssm_c_im[0], ssm_d[0])
    u8 = _s5_in(x2, m_mix, s_w_in)
    y8 = _s5_core(u8, wb, wck, row(a8r), row(a8i))
    x3, w1_1, w2_1 = _s5_out(y8, x2, m_mix, s_w_glu, s_w_out, vec, ((mlp_w1, 1), (mlp_w2, 1)))
    return _ffn(x3, m_ffn, 1, w1_1, w2_1, vec)
```

```python
import functools
import math

import jax
import jax.numpy as jnp
from jax import lax
from jax.experimental import pallas as pl
from jax.experimental.pallas import tpu as pltpu

F32 = jnp.float32
BF16 = jnp.bfloat16

D = 1024
NB = 8
SEQ = 2048
HEADS = 8
QK_NOPE = 128
QK_ROPE = 64
V_DIM = 128
Q_LORA = 256
KV_LORA = 128
ROPE_THETA = 10000.0
GROUP_CH = 16
N_GROUPS = 64
STATE = 64
D_FF = 4 * D
DEPTH = 2
ALPHA = (2 * DEPTH) ** 0.25
LN_EPS = 1e-5
RMS_EPS = 1e-6
Q_SCALE = math.log2(math.e) / math.sqrt(QK_NOPE + QK_ROPE)
NEG_BIG = -1e30
GELU_C = math.sqrt(2.0 / math.pi)

HEAD_W = 256
CHUNK = 8
N_PAIRS = N_GROUPS // 2
PAIR_CH = 2 * GROUP_CH
PAIR_W = CHUNK * PAIR_CH
assert PAIR_W == 4 * STATE

VMEM_LIMIT = 56 * 1024 * 1024
PROJ_ROWS = 512
PROJ_SUB = 2
ATTN_BLK = 512
ATTN_HEADS = 4
_ATTN_SLOTS = 2 * (SEQ // ATTN_BLK) - 1
FFN_ROWS = 512
FFN_SUB = 2
FF_CHUNK = 1024
S5_BLOCK = 128
S5_IN_STEPS = 64
S5_OUT_STEPS = 32
PAIRS_PER_STEP = 2


VEC_LN_MIX_G, VEC_LN_MIX_B, VEC_LN_FFN_G, VEC_LN_FFN_B, VEC_MLP_B2 = 0, DEPTH, 2 * DEPTH, 3 * DEPTH, 4 * DEPTH
VEC_B_GLU = 5 * DEPTH
VEC_MLP_B1 = VEC_B_GLU + 1
VEC_MOD_MIX_B = VEC_MLP_B1 + DEPTH * (D_FF // D)
VEC_MOD_FFN_B = VEC_MOD_MIX_B + 3 * DEPTH
VEC_Q_NORM = VEC_MOD_FFN_B + 3 * DEPTH
VEC_KV_NORM = VEC_Q_NORM + 1
VEC_ROWS = -(-(VEC_KV_NORM + 1) // 8) * 8
assert FF_CHUNK == D


def _pack_vectors(ln_mix_g, ln_mix_b, ln_ffn_g, ln_ffn_b, mlp_b2, b_glu, mlp_b1, mod_mix_b, mod_ffn_b, q_norm, kv_norm):
    pad = lambda v: jnp.pad(v.reshape(1, -1), ((0, 0), (0, D - v.size)))
    rows = [ln_mix_g, ln_mix_b, ln_ffn_g, ln_ffn_b, mlp_b2, b_glu.reshape(1, D), mlp_b1.reshape(-1, D),
            mod_mix_b.reshape(-1, D), mod_ffn_b.reshape(-1, D), pad(q_norm), pad(kv_norm)]
    table = jnp.concatenate(rows, axis=0)
    return jnp.pad(table, ((0, VEC_ROWS - table.shape[0]), (0, 0)))


def _cparams(*sem):
    return pltpu.CompilerParams(dimension_semantics=sem, vmem_limit_bytes=VMEM_LIMIT)


def _const_spec(shape):
    nd = len(shape)
    return pl.BlockSpec(shape, lambda *_: (0,) * nd, pipeline_mode=pl.Buffered(1))


def _cast_specs(shape, grid, layer):
    _, rows, cols = shape
    n_steps = math.prod(grid)

    def step(*ids):
        s = 0
        for i, n in zip(ids, grid):
            s = s * n + i
        return s

    block = (1, rows // n_steps, cols)
    return (pl.BlockSpec(block, lambda *ids: (layer, step(*ids), 0)),
            pl.BlockSpec(block, lambda *ids: (0, step(*ids), 0)),
            jax.ShapeDtypeStruct((1, rows, cols), BF16))


def _cast_slices(src_refs, dst_refs):
    for src, dst in zip(src_refs, dst_refs):
        dst[...] = src[...].astype(dst.dtype)


def _layer_norm(v, g, b):
    mu = jnp.mean(v, axis=-1, keepdims=True)
    vc = v - mu
    var = jnp.mean(vc * vc, axis=-1, keepdims=True)
    return vc * lax.rsqrt(var + LN_EPS) * g + b


def _bdot(a, b):
    return jnp.dot(a, b, preferred_element_type=F32)


def _mod_kernel(c_ref, w_ref, vec_ref, o_ref, *, bias_row):
    c = c_ref[...]
    cs = c * jax.nn.sigmoid(c)
    bias = vec_ref[pl.ds(bias_row + 3 * pl.program_id(0) + pl.program_id(1), 1), :]
    o_ref[...] = _bdot(cs.astype(BF16), w_ref[...].astype(BF16)) + bias


def _modulation(c, w, vec, bias_row):
    n_layers = w.shape[0]
    return pl.pallas_call(
        functools.partial(_mod_kernel, bias_row=bias_row),
        out_shape=jax.ShapeDtypeStruct((n_layers, 3, NB, D), F32),
        grid=(n_layers, 3),
        in_specs=[
            pl.BlockSpec((NB, D), lambda l, j: (0, 0)),
            pl.BlockSpec((None, D, D), lambda l, j: (l, 0, j)),
            _const_spec(vec.shape),
        ],
        out_specs=pl.BlockSpec((None, None, NB, D), lambda l, j: (l, j, 0, 0)),
        compiler_params=_cparams("arbitrary", "arbitrary"),
        name="adaln_modulation",
    )(c, w, vec)


def _mod_spec(layer):
    return pl.BlockSpec((None, 3, NB, D), lambda *_: (layer, 0, 0, 0), pipeline_mode=pl.Buffered(1))


SHIFT, SCALE, GATE = 0, 1, 2


def _mla_proj_kernel(x_ref, m_ref, pos_ref, freq_ref, w_in_ref, vec_ref, w_qb_ref, w_kvb_ref,
                     ca_ref, cb_ref, cc_ref, q_ref, kn_ref, kr_ref, v_ref, cab_ref, cbb_ref, ccb_ref):
    _cast_slices((ca_ref, cb_ref, cc_ref), (cab_ref, cbb_ref, ccb_ref))
    qn_ref = vec_ref.at[VEC_Q_NORM:VEC_Q_NORM + 1, 0:Q_LORA]
    kvn_ref = vec_ref.at[VEC_KV_NORM:VEC_KV_NORM + 1, 0:KV_LORA]
    b = pl.program_id(0)
    shift = m_ref[SHIFT, pl.ds(b, 1), :]
    scale = m_ref[SCALE, pl.ds(b, 1), :]

    def rms(v, g):
        return v * lax.rsqrt(jnp.mean(v * v, axis=-1, keepdims=True) + RMS_EPS) * g

    for sb in range(PROJ_SUB):
        rows = slice(sb * PROJ_ROWS, (sb + 1) * PROJ_ROWS)
        h = (x_ref[0, rows, :] * (1.0 + scale) + shift).astype(BF16)
        z = _bdot(h, w_in_ref[...])
        cq = rms(z[:, :Q_LORA], qn_ref[...]).astype(BF16)
        ckv = rms(z[:, Q_LORA:Q_LORA + KV_LORA], kvn_ref[...]).astype(BF16)

        ang = pos_ref[0, rows, :].astype(F32) * freq_ref[...]
        lane = lax.broadcasted_iota(jnp.int32, ang.shape, 1)
        mult = jnp.where(lane < QK_ROPE, jnp.cos(ang), jnp.sin(ang))

        def rope(slab):
            s = slab * mult
            return s + pltpu.roll(s, QK_ROPE, axis=1)

        kr_ref[0, rows, :] = jnp.where(lane < QK_ROPE, rope(z[:, Q_LORA + KV_LORA:]), 0.0).astype(BF16)

        q_all = _bdot(cq, w_qb_ref[...])
        kv = _bdot(ckv, w_kvb_ref[...])
        for hd in range(HEADS):
            c0 = hd * HEAD_W
            q_ref[0, hd, rows, 0:QK_NOPE] = q_all[:, c0:c0 + QK_NOPE].astype(BF16)
            q_ref[0, hd, rows, QK_NOPE:HEAD_W] = rope(q_all[:, c0 + QK_NOPE:c0 + HEAD_W]).astype(BF16)
            kn_ref[0, hd, rows, :] = kv[:, c0:c0 + QK_NOPE].astype(BF16)
            v_ref[0, hd, rows, :] = kv[:, c0 + QK_NOPE:c0 + HEAD_W].astype(BF16)


def _mla_proj(x, m, pos, freq, w_in, vec, w_qb, w_kvb, casts):
    nt = SEQ // (PROJ_SUB * PROJ_ROWS)
    blk = PROJ_SUB * PROJ_ROWS
    grid = (NB, nt)
    cast = [_cast_specs(w.shape, grid, layer) for w, layer in casts]
    return pl.pallas_call(
        _mla_proj_kernel,
        out_shape=[
            jax.ShapeDtypeStruct((NB, HEADS, SEQ, HEAD_W), BF16),
            jax.ShapeDtypeStruct((NB, HEADS, SEQ, QK_NOPE), BF16),
            jax.ShapeDtypeStruct((NB, SEQ, HEAD_W - QK_NOPE), BF16),
            jax.ShapeDtypeStruct((NB, HEADS, SEQ, V_DIM), BF16),
        ] + [c[2] for c in cast],
        grid=grid,
        in_specs=[
            pl.BlockSpec((1, blk, D), lambda b, i: (b, i, 0)),
            _mod_spec(0),
            pl.BlockSpec((1, blk, 1), lambda b, i: (b, i, 0)),
            _const_spec((1, 128)),
            _const_spec(w_in.shape),
            _const_spec(vec.shape),
            _const_spec(w_qb.shape),
            _const_spec(w_kvb.shape),
        ] + [c[0] for c in cast],
        out_specs=[
            pl.BlockSpec((1, HEADS, blk, HEAD_W), lambda b, i: (b, 0, i, 0)),
            pl.BlockSpec((1, HEADS, blk, QK_NOPE), lambda b, i: (b, 0, i, 0)),
            pl.BlockSpec((1, blk, HEAD_W - QK_NOPE), lambda b, i: (b, i, 0)),
            pl.BlockSpec((1, HEADS, blk, V_DIM), lambda b, i: (b, 0, i, 0)),
        ] + [c[1] for c in cast],
        compiler_params=_cparams("arbitrary", "arbitrary"),
        name="mla_projections",
    )(x, m, pos, freq, w_in, vec, w_qb, w_kvb, *[w for w, _ in casts])


def _lane_groups(v):
    return [v[:, c:c + 128] for c in range(0, v.shape[1], 128)]


def _attn_kernel(q_ref, kn_ref, kr_ref, v_ref, o_ref, s_ref):
    blk = ATTN_BLK
    half = blk // 2
    nq = SEQ // blk
    tri = lax.broadcasted_iota(jnp.int32, (half, half), 1) <= lax.broadcasted_iota(jnp.int32, (half, half), 0)

    def rows(n):
        return slice(n * blk, (n + 1) * blk)

    def slot(hd, n, j):
        return hd * _ATTN_SLOTS + (0 if n % 2 == 0 else nq - 1) + j

    def nt_dot(a, b):
        return lax.dot_general(a, b, (((1,), (1,)), ((), ())), preferred_element_type=F32)

    def keys(hd, j):
        return jnp.concatenate([kn_ref[hd, rows(j), :], kr_ref[rows(j), :]], axis=1)

    def lane_max(vals):
        return functools.reduce(jnp.maximum, vals)

    def pass1(hd, n):
        q = q_ref[hd, rows(n), :]
        m = []
        for j in range(n):
            s = nt_dot(q, keys(hd, j))
            s_ref[slot(hd, n, j)] = s
            m = [lane_max(_lane_groups(s) + m)]
        k = keys(hd, n)
        s_top = jnp.where(tri, nt_dot(q[:half], k[:half]), NEG_BIG)
        s_bot = nt_dot(q[half:], k)
        s_bot = jnp.concatenate([s_bot[:, :half], jnp.where(tri, s_bot[:, half:], NEG_BIG)], axis=1)
        s_ref[slot(hd, n, n), 0:half, 0:half] = s_top
        s_ref[slot(hd, n, n), half:blk, :] = s_bot
        m_top = lane_max(_lane_groups(s_top) + [v[:half] for v in m])
        m_bot = lane_max(_lane_groups(s_bot) + [v[half:] for v in m])
        m = jnp.concatenate([m_top, m_bot], axis=0)
        return jnp.broadcast_to(jnp.max(m, axis=-1, keepdims=True), (blk, 128))

    def pass2(hd, n, mb):
        diag = slot(hd, n, n)
        ps = [jnp.exp2(sg - mb) for j in range(n) for sg in _lane_groups(s_ref[slot(hd, n, j)])]
        p_top = [jnp.exp2(sg - mb[:half]) for sg in _lane_groups(s_ref[diag, 0:half, 0:half])]
        p_bot = [jnp.exp2(sg - mb[half:]) for sg in _lane_groups(s_ref[diag, half:blk, :])]
        l_top = functools.reduce(jnp.add, p_top + [p[:half] for p in ps])
        l_bot = functools.reduce(jnp.add, p_bot + [p[half:] for p in ps])
        l = jnp.sum(jnp.concatenate([l_top, l_bot], axis=0), axis=-1, keepdims=True)
        v0 = n * blk
        acc = jnp.concatenate([
            _bdot(jnp.concatenate(p_top, axis=1).astype(BF16), v_ref[hd, v0:v0 + half, :]),
            _bdot(jnp.concatenate(p_bot, axis=1).astype(BF16), v_ref[hd, v0:v0 + blk, :])], axis=0)
        if n > 0:
            acc = acc + _bdot(jnp.concatenate(ps, axis=1).astype(BF16), v_ref[hd, 0:v0, :])
        o_ref[hd, rows(n), :] = (acc / l).astype(BF16)

    mb = [pass1(hd, 0) for hd in range(ATTN_HEADS)]
    for n in range(nq):
        for hd in range(ATTN_HEADS):
            mb_next = pass1(hd, n + 1) if n + 1 < nq else None
            pass2(hd, n, mb[hd])
            mb[hd] = mb_next


def _attention(q, kn, kr, v):
    head_spec = lambda w: pl.BlockSpec((None, ATTN_HEADS, SEQ, w), lambda b, h: (b, h, 0, 0))
    kr_spec = pl.BlockSpec((None, SEQ, HEAD_W - QK_NOPE), lambda b, h: (b, 0, 0))
    return pl.pallas_call(
        _attn_kernel,
        out_shape=jax.ShapeDtypeStruct((NB, HEADS, SEQ, V_DIM), BF16),
        grid=(NB, HEADS // ATTN_HEADS),
        in_specs=[head_spec(HEAD_W), head_spec(QK_NOPE), kr_spec, head_spec(V_DIM)],
        out_specs=head_spec(V_DIM),
        scratch_shapes=[pltpu.VMEM((ATTN_HEADS * _ATTN_SLOTS, ATTN_BLK, ATTN_BLK), F32)],
        compiler_params=_cparams("arbitrary", "arbitrary"),
        name="mla_attention",
    )(q, kn, kr, v)


def _vec_row(vec_ref, row):
    return vec_ref[row:row + 1, :]


def _ffn_rows(x, shift, scale, gate, w1_ref, w2_ref, vec_ref, layer):
    h = (x * (1.0 + scale) + shift).astype(BF16)
    acc = jnp.zeros(x.shape, F32)
    for c in range(D_FF // FF_CHUNK):
        cols = slice(c * FF_CHUNK, (c + 1) * FF_CHUNK)
        b1 = _vec_row(vec_ref, VEC_MLP_B1 + layer * (D_FF // FF_CHUNK) + c)
        a = jnp.maximum(_bdot(h, w1_ref[:, cols]) + b1, 0.0)
        acc = acc + _bdot((a * a).astype(BF16), w2_ref[cols, :])
    y = acc + _vec_row(vec_ref, VEC_MLP_B2 + layer)
    return _layer_norm(ALPHA * x + (1.0 + gate) * y,
                       _vec_row(vec_ref, VEC_LN_FFN_G + layer), _vec_row(vec_ref, VEC_LN_FFN_B + layer))


def _batch_mod(m_ref):
    b = pl.program_id(0)
    return [m_ref[i, pl.ds(b, 1), :] for i in (SHIFT, SCALE, GATE)]


def _ffn_kernel(x_ref, m_ref, w1_ref, w2_ref, vec_ref, y_ref, *, layer):
    mod = _batch_mod(m_ref)
    for sb in range(FFN_SUB):
        rows = slice(sb * FFN_ROWS, (sb + 1) * FFN_ROWS)
        y_ref[rows, :] = _ffn_rows(x_ref[rows, :], *mod, w1_ref, w2_ref, vec_ref, layer)


def _attn_out_ffn_kernel(o_ref, x_ref, mm_ref, mf_ref, wo_ref, w1_ref, w2_ref, vec_ref, sa_ref, sb_ref, sc_ref,
                         y_ref, sab_ref, sbb_ref, scb_ref, *, layer):
    _cast_slices((sa_ref, sb_ref, sc_ref), (sab_ref, sbb_ref, scb_ref))
    gm_ref = vec_ref.at[VEC_LN_MIX_G + layer:VEC_LN_MIX_G + layer + 1]
    bm_ref = vec_ref.at[VEC_LN_MIX_B + layer:VEC_LN_MIX_B + layer + 1]
    gate_mix = _batch_mod(mm_ref)[GATE]
    mod = _batch_mod(mf_ref)
    for sb in range(FFN_SUB):
        rows = slice(sb * FFN_ROWS, (sb + 1) * FFN_ROWS)
        o = jnp.concatenate([o_ref[hd, rows, :] for hd in range(HEADS)], axis=1)
        y_ref[rows, :] = _layer_norm(ALPHA * x_ref[rows, :] + (1.0 + gate_mix) * _bdot(o, wo_ref[...]),
                                     gm_ref[...], bm_ref[...])
    for sb in range(FFN_SUB):
        rows = slice(sb * FFN_ROWS, (sb + 1) * FFN_ROWS)
        y_ref[rows, :] = _ffn_rows(y_ref[rows, :], *mod, w1_ref, w2_ref, vec_ref, layer)


_FFN_X_SPEC = pl.BlockSpec((None, FFN_SUB * FFN_ROWS, D), lambda b, i: (b, i, 0))
_FFN_GRID = (NB, SEQ // (FFN_SUB * FFN_ROWS))


def _layer_spec(shape, layer):
    return pl.BlockSpec((None,) + tuple(shape[1:]), lambda *_: (layer, 0, 0), pipeline_mode=pl.Buffered(1))


def _ffn(x, m, layer, w1, w2, vec):
    return pl.pallas_call(
        functools.partial(_ffn_kernel, layer=layer),
        out_shape=jax.ShapeDtypeStruct((NB, SEQ, D), F32),
        grid=_FFN_GRID,
        in_specs=[_FFN_X_SPEC, _mod_spec(layer), _layer_spec(w1.shape, 0), _layer_spec(w2.shape, 0),
                  _const_spec(vec.shape)],
        out_specs=_FFN_X_SPEC,
        compiler_params=_cparams("arbitrary", "arbitrary"),
        name="ffn_norm",
    )(x, m, w1, w2, vec)


def _attn_out_ffn(o, x, m_mix, m_ffn, layer, w_o, w1, w2, vec, casts):
    o_spec = pl.BlockSpec((None, HEADS, FFN_SUB * FFN_ROWS, V_DIM), lambda b, i: (b, 0, i, 0))
    cast = [_cast_specs(w.shape, _FFN_GRID, l) for w, l in casts]
    return pl.pallas_call(
        functools.partial(_attn_out_ffn_kernel, layer=layer),
        out_shape=[jax.ShapeDtypeStruct((NB, SEQ, D), F32)] + [c[2] for c in cast],
        grid=_FFN_GRID,
        in_specs=[o_spec, _FFN_X_SPEC, _mod_spec(layer), _mod_spec(layer), _layer_spec(w_o.shape, 0),
                  _layer_spec(w1.shape, 0), _layer_spec(w2.shape, 0), _const_spec(vec.shape)]
                 + [c[0] for c in cast],
        out_specs=[_FFN_X_SPEC] + [c[1] for c in cast],
        compiler_params=_cparams("arbitrary", "arbitrary"),
        name="attn_out_ffn_norm",
    )(o, x, m_mix, m_ffn, w_o, w1, w2, vec, *[w for w, _ in casts])


def _cmul(ar, ai, br, bi):
    return ar * br - ai * bi, ar * bi + ai * br


def _pair_slot(piece, zero, slot):
    return [piece, zero] if slot == 0 else [zero, piece]


def _s5_disc_kernel(ldt_ref, are_ref, aim_ref, bre_ref, bim_ref, cre_ref, cim_ref, d_ref,
                    a8r_ref, a8i_ref, wb_ref, wck_ref):
    lr = are_ref[...]
    li = aim_ref[...]
    dt = jnp.exp(ldt_ref[...])
    mag = jnp.exp(lr * dt)
    ab_re = mag * jnp.cos(li * dt)
    ab_im = mag * jnp.sin(li * dt)
    den = lr * lr + li * li
    nr = ab_re - 1.0
    coef_re = (nr * lr + ab_im * li) / den
    coef_im = (ab_im * lr - nr * li) / den
    bb_re, bb_im = _cmul(coef_re, coef_im, bre_ref[...], bim_ref[...])
    c_re = cre_ref[...]
    c_im = cim_ref[...]

    pw = [(jnp.ones_like(ab_re), jnp.zeros_like(ab_im))]
    for _ in range(CHUNK):
        pw.append(_cmul(pw[-1][0], pw[-1][1], ab_re, ab_im))
    a8r_ref[...] = pw[CHUNK][0]
    a8i_ref[...] = pw[CHUNK][1]

    def pairs(v):
        v4 = v.reshape(N_PAIRS, 2, v.shape[1], v.shape[2])
        return v4[:, 0], v4[:, 1]

    z64 = jnp.zeros((N_PAIRS, GROUP_CH, STATE), F32)
    z16 = jnp.zeros((N_PAIRS, GROUP_CH, GROUP_CH), F32)
    eye = (lax.broadcasted_iota(jnp.int32, (GROUP_CH, GROUP_CH), 0)
           == lax.broadcasted_iota(jnp.int32, (GROUP_CH, GROUP_CH), 1)).astype(F32)

    ab_l = [_cmul(pw[l][0], pw[l][1], bb_re, bb_im) for l in range(CHUNK)]
    wb_rows = []
    for k in range(CHUNK):
        m_re, m_im = ab_l[CHUNK - 1 - k]
        for slot in range(2):
            wb_rows.append(jnp.concatenate(_pair_slot(pairs(m_re)[slot], z64, slot)
                                           + _pair_slot(pairs(m_im)[slot], z64, slot), axis=2))
    wb_ref[...] = jnp.concatenate(wb_rows, axis=1).astype(BF16)

    def nt(a, b):
        return lax.dot_general(a.astype(BF16), b.astype(BF16), (((2,), (2,)), ((0,), (0,))),
                               preferred_element_type=F32)

    k_l = [nt(c_re, ab_l[l][0]) - nt(c_im, ab_l[l][1]) for l in range(CHUNK)]
    k_l[0] = k_l[0] + d_ref[...] * eye[None]

    wck_rows = []
    for kp in range(CHUNK):
        ca_re, ca_im = _cmul(c_re, c_im, pw[kp + 1][0], pw[kp + 1][1])
        for slot in range(2):
            state_cols = (_pair_slot(pairs(ca_re)[slot], z64, slot)
                          + _pair_slot(-pairs(ca_im)[slot], z64, slot))
            direct_cols = []
            for k in range(CHUNK):
                blk = pairs(k_l[kp - k])[slot] if k <= kp else z16
                direct_cols += _pair_slot(blk, z16, slot)
            wck_rows.append(jnp.concatenate(state_cols + direct_cols, axis=2))
    wck_ref[...] = jnp.concatenate(wck_rows, axis=1).astype(BF16)


def _s5_discretise(log_dt, a_re, a_im, b_re, b_im, c_re, c_im, d_skip):
    g3 = (N_GROUPS, 1, STATE)
    return pl.pallas_call(
        _s5_disc_kernel,
        out_shape=(jax.ShapeDtypeStruct(g3, F32), jax.ShapeDtypeStruct(g3, F32),
                   jax.ShapeDtypeStruct((N_PAIRS, PAIR_W, PAIR_W), BF16),
                   jax.ShapeDtypeStruct((N_PAIRS, PAIR_W, 2 * PAIR_W), BF16)),
        compiler_params=pltpu.CompilerParams(vmem_limit_bytes=VMEM_LIMIT),
        name="s5_discretise",
    )(log_dt.reshape(N_GROUPS, 1, 1), a_re.reshape(g3), a_im.reshape(g3),
      jnp.swapaxes(b_re, 1, 2), jnp.swapaxes(b_im, 1, 2), c_re, c_im, d_skip.reshape(N_GROUPS, GROUP_CH, 1))


S5_X_SPEC = pl.BlockSpec((NB, S5_BLOCK, D), lambda i: (0, i, 0))
S5_U8_SPEC = pl.BlockSpec((S5_BLOCK // CHUNK * NB, N_PAIRS * PAIR_W), lambda i: (i, 0))


def _s5_in_kernel(x_ref, m_ref, w_in_ref, u8_ref):
    shift = m_ref[SHIFT]
    scale = m_ref[SCALE]
    steps = S5_IN_STEPS
    chunks = steps // CHUNK
    for sb in range(S5_BLOCK // steps):
        x3 = jnp.swapaxes(x_ref[:, sb * steps:(sb + 1) * steps, :], 0, 1)
        h = (x3 * (1.0 + scale)[None] + shift[None]).reshape(steps * NB, D).astype(BF16)
        u = _bdot(h, w_in_ref[...])
        u4 = u.reshape(chunks, CHUNK, NB, D)
        per_k = [u4[:, k].reshape(chunks * NB, D) for k in range(CHUNK)]
        cols = [per_k[k][:, j * PAIR_CH:(j + 1) * PAIR_CH] for j in range(N_PAIRS) for k in range(CHUNK)]
        u8_ref[sb * chunks * NB:(sb + 1) * chunks * NB, :] = jnp.concatenate(cols, axis=1).astype(BF16)


def _s5_in(x, m, w_in):
    return pl.pallas_call(
        _s5_in_kernel,
        out_shape=jax.ShapeDtypeStruct((SEQ // CHUNK * NB, N_PAIRS * PAIR_W), BF16),
        grid=(SEQ // S5_BLOCK,),
        in_specs=[
            S5_X_SPEC,
            _mod_spec(1),
            _layer_spec(w_in.shape, 0),
        ],
        out_specs=S5_U8_SPEC,
        compiler_params=_cparams("arbitrary"),
        name="s5_in_proj",
    )(x, m, w_in)


def _s5_core_kernel(u8_ref, wb_ref, wck_ref, a8r_ref, a8i_ref, y8_ref):
    half = PAIR_W // 2
    for jj in range(PAIRS_PER_STEP):
        lanes = slice(jj * PAIR_W, (jj + 1) * PAIR_W)
        u8 = u8_ref[:, lanes]
        v = _bdot(u8, wb_ref[jj])
        pair = lambda ref: jnp.concatenate([ref[2 * jj], ref[2 * jj + 1]], axis=1)
        ar = jnp.broadcast_to(pair(a8r_ref), (NB, half))
        ai = jnp.broadcast_to(pair(a8i_ref), (NB, half))
        xr = jnp.zeros((NB, half), F32)
        xi = jnp.zeros((NB, half), F32)
        prev_r, prev_i = [], []
        for s in range(SEQ // CHUNK):
            prev_r.append(xr)
            prev_i.append(xi)
            rows = slice(s * NB, (s + 1) * NB)
            xr, xi = ar * xr - ai * xi + v[rows, :half], ar * xi + ai * xr + v[rows, half:]
        x_prev = jnp.concatenate([jnp.concatenate(prev_r, axis=0), jnp.concatenate(prev_i, axis=0)], axis=1)
        lhs = jnp.concatenate([x_prev.astype(BF16), u8], axis=1)
        y8_ref[:, lanes] = lax.dot_general(lhs, wck_ref[jj], (((1,), (1,)), ((), ())), preferred_element_type=F32)


def _s5_core(u8, wb, wck, a8r, a8i):
    n_rows = u8.shape[0]
    w = PAIRS_PER_STEP * PAIR_W
    return pl.pallas_call(
        _s5_core_kernel,
        out_shape=jax.ShapeDtypeStruct(u8.shape, F32),
        grid=(N_PAIRS // PAIRS_PER_STEP,),
        in_specs=[
            pl.BlockSpec((n_rows, w), lambda i: (0, i)),
            pl.BlockSpec((PAIRS_PER_STEP, PAIR_W, PAIR_W), lambda i: (i, 0, 0)),
            pl.BlockSpec((PAIRS_PER_STEP, PAIR_W, 2 * PAIR_W), lambda i: (i, 0, 0)),
            pl.BlockSpec((2 * PAIRS_PER_STEP, 1, STATE), lambda i: (i, 0, 0)),
            pl.BlockSpec((2 * PAIRS_PER_STEP, 1, STATE), lambda i: (i, 0, 0)),
        ],
        out_specs=pl.BlockSpec((n_rows, w), lambda i: (0, i)),
        compiler_params=_cparams("arbitrary"),
        name="s5_recurrence",
    )(u8, wb, wck, a8r, a8i)


def _s5_out_kernel(y8_ref, x_ref, m_ref, w_glu_ref, w_out_ref, vec_ref, ca_ref, cb_ref, o_ref, cab_ref, cbb_ref):
    _cast_slices((ca_ref, cb_ref), (cab_ref, cbb_ref))
    b_glu_ref = vec_ref.at[VEC_B_GLU:VEC_B_GLU + 1]
    g_ref = vec_ref.at[VEC_LN_MIX_G + 1:VEC_LN_MIX_G + 2]
    b_ref = vec_ref.at[VEC_LN_MIX_B + 1:VEC_LN_MIX_B + 2]
    gate = m_ref[GATE]
    steps = S5_OUT_STEPS
    chunks = steps // CHUNK
    for sb in range(S5_BLOCK // steps):
        y8 = y8_ref[sb * chunks * NB:(sb + 1) * chunks * NB, :]
        per_k = [jnp.concatenate([y8[:, j * PAIR_W + k * PAIR_CH:j * PAIR_W + (k + 1) * PAIR_CH]
                                  for j in range(N_PAIRS)], axis=1) for k in range(CHUNK)]
        y = jnp.stack([p.reshape(chunks, NB, D) for p in per_k], axis=1).reshape(steps * NB, D)

        half_y = 0.5 * y
        g = half_y + half_y * jnp.tanh(y * (GELU_C + (GELU_C * 0.044715) * (y * y)))
        zz = g * jax.nn.sigmoid(_bdot(g.astype(BF16), w_glu_ref[...]) + b_glu_ref[...])
        out = jnp.swapaxes(_bdot(zz.astype(BF16), w_out_ref[...]).reshape(steps, NB, D), 0, 1)
        t_rows = slice(sb * steps, (sb + 1) * steps)
        res = ALPHA * x_ref[:, t_rows, :] + (1.0 + gate)[:, None, :] * out
        o_ref[:, t_rows, :] = _layer_norm(res, g_ref[...], b_ref[...])


def _s5_out(y8, x, m, w_glu, w_out, vec, casts):
    x_spec = S5_X_SPEC
    grid = (SEQ // S5_BLOCK,)
    cast = [_cast_specs(w.shape, grid, l) for w, l in casts]
    return pl.pallas_call(
        _s5_out_kernel,
        out_shape=[jax.ShapeDtypeStruct((NB, SEQ, D), F32)] + [c[2] for c in cast],
        grid=grid,
        in_specs=[
            S5_U8_SPEC,
            x_spec,
            _mod_spec(1),
            _layer_spec(w_glu.shape, 0),
            _layer_spec(w_out.shape, 0),
            _const_spec(vec.shape),
        ] + [c[0] for c in cast],
        out_specs=[x_spec] + [c[1] for c in cast],
        compiler_params=_cparams("arbitrary"),
        name="s5_out_norm",
    )(y8, x, m, w_glu, w_out, vec, *[w for w, _ in casts])


def _rotate_half_cols(w):
    half = w.shape[-1] // 2
    return jnp.concatenate([-w[..., half:], w[..., :half]], axis=-1)


def _mla_weights(w_in, w_qb):
    k_pe = w_in[:, Q_LORA + KV_LORA:]
    w_in_ext = jnp.concatenate([w_in, _rotate_half_cols(k_pe)], axis=1)
    wq = w_qb.reshape(Q_LORA, HEADS, QK_NOPE + QK_ROPE) * Q_SCALE
    wq_ext = jnp.concatenate([wq, _rotate_half_cols(wq[..., QK_NOPE:])], axis=-1)
    return w_in_ext.astype(BF16), wq_ext.reshape(Q_LORA, HEADS * HEAD_W).astype(BF16)


def kernel(x, c, positions, mla_w_in, mla_q_norm, mla_w_qb, mla_kv_norm, mla_w_kvb, mla_w_o, ssm_w_in, ssm_log_dt, ssm_a_re, ssm_a_im, ssm_b_re, ssm_b_im, ssm_c_re, ssm_c_im, ssm_d, ssm_w_glu, ssm_b_glu, ssm_w_out, mlp_w1, mlp_b1, mlp_w2, mlp_b2, mod_mix_w, mod_mix_b, mod_ffn_w, mod_ffn_b, ln_mix_g, ln_mix_b, ln_ffn_g, ln_ffn_b):
    vec = _pack_vectors(ln_mix_g, ln_mix_b, ln_ffn_g, ln_ffn_b, mlp_b2, ssm_b_glu, mlp_b1, mod_mix_b, mod_ffn_b,
                        mla_q_norm, mla_kv_norm)
    m_mix = _modulation(c, mod_mix_w, vec, VEC_MOD_MIX_B)
    m_ffn = _modulation(c, mod_ffn_w, vec, VEC_MOD_FFN_B)

    inv_freq = ROPE_THETA ** (-jnp.arange(0, QK_ROPE, 2, dtype=F32) / QK_ROPE)
    freq = jnp.tile(inv_freq, 4).reshape(1, 128)
    w_in_ext, w_qb_ext = _mla_weights(mla_w_in[0], mla_w_qb[0])
    q, kn, kr, v, w1_0, w2_0, w_o = _mla_proj(
        x, m_mix, positions.reshape(NB, SEQ, 1), freq, w_in_ext, vec, w_qb_ext, mla_w_kvb[0].astype(BF16),
        ((mlp_w1, 0), (mlp_w2, 0), (mla_w_o, 0)))
    o = _attention(q, kn, kr, v)
    x2, s_w_in, s_w_glu, s_w_out = _attn_out_ffn(
        o, x, m_mix, m_ffn, 0, w_o, w1_0, w2_0, vec, ((ssm_w_in, 0), (ssm_w_glu, 0), (ssm_w_out, 0)))

    a8r, a8i, wb, wck = _s5_discretise(ssm_log_dt[0], ssm_a_re[0], ssm_a_im[0], ssm_b_re[0], ssm_b_im[0],
                                       ssm_c_re[0], ssm_c_im[0], ssm_d[0])
    u8 = _s5_in(x2, m_mix, s_w_in)
    y8 = _s5_core(u8, wb, wck, a8r, a8i)
    x3, w1_1, w2_1 = _s5_out(y8, x2, m_mix, s_w_glu, s_w_out, vec, ((mlp_w1, 1), (mlp_w2, 1)))
    return _ffn(x3, m_ffn, 1, w1_1, w2_1, vec)
```

```python
import functools
import math

import jax
import jax.numpy as jnp
from jax import lax
from jax.experimental import pallas as pl
from jax.experimental.pallas import tpu as pltpu

F32 = jnp.float32
BF16 = jnp.bfloat16

D = 1024
NB = 8
SEQ = 2048
HEADS = 8
QK_NOPE = 128
QK_ROPE = 64
V_DIM = 128
Q_LORA = 256
KV_LORA = 128
ROPE_THETA = 10000.0
GROUP_CH = 16
N_GROUPS = 64
STATE = 64
D_FF = 4 * D
DEPTH = 2
ALPHA = (2 * DEPTH) ** 0.25
LN_EPS = 1e-5
RMS_EPS = 1e-6
Q_SCALE = math.log2(math.e) / math.sqrt(QK_NOPE + QK_ROPE)
NEG_BIG = -1e30
GELU_C = math.sqrt(2.0 / math.pi)

HEAD_W = 256
CHUNK = 8
N_PAIRS = N_GROUPS // 2
PAIR_CH = 2 * GROUP_CH
PAIR_W = CHUNK * PAIR_CH
assert PAIR_W == 4 * STATE

VMEM_LIMIT = 56 * 1024 * 1024
PROJ_ROWS = 512
PROJ_SUB = 2
ATTN_BLK = 512
ATTN_HEADS = 4
_ATTN_SLOTS = 2 * (SEQ // ATTN_BLK) - 1
FFN_ROWS = 512
FFN_SUB = 2
FF_CHUNK = 1024
S5_BLOCK = 128
S5_IN_STEPS = 64
S5_OUT_STEPS = 32
PAIRS_PER_STEP = 4
DISC_GROUPS = 16


VEC_LN_MIX_G, VEC_LN_MIX_B, VEC_LN_FFN_G, VEC_LN_FFN_B, VEC_MLP_B2 = 0, DEPTH, 2 * DEPTH, 3 * DEPTH, 4 * DEPTH
VEC_B_GLU = 5 * DEPTH
VEC_MLP_B1 = VEC_B_GLU + 1
VEC_MOD_MIX_B = VEC_MLP_B1 + DEPTH * (D_FF // D)
VEC_MOD_FFN_B = VEC_MOD_MIX_B + 3 * DEPTH
VEC_Q_NORM = VEC_MOD_FFN_B + 3 * DEPTH
VEC_KV_NORM = VEC_Q_NORM + 1
VEC_ROWS = -(-(VEC_KV_NORM + 1) // 8) * 8
assert FF_CHUNK == D


def _pack_vectors(ln_mix_g, ln_mix_b, ln_ffn_g, ln_ffn_b, mlp_b2, b_glu, mlp_b1, mod_mix_b, mod_ffn_b, q_norm, kv_norm):
    pad = lambda v: jnp.pad(v.reshape(1, -1), ((0, 0), (0, D - v.size)))
    rows = [ln_mix_g, ln_mix_b, ln_ffn_g, ln_ffn_b, mlp_b2, b_glu.reshape(1, D), mlp_b1.reshape(-1, D),
            mod_mix_b.reshape(-1, D), mod_ffn_b.reshape(-1, D), pad(q_norm), pad(kv_norm)]
    table = jnp.concatenate(rows, axis=0)
    return jnp.pad(table, ((0, VEC_ROWS - table.shape[0]), (0, 0)))


def _cparams(*sem):
    return pltpu.CompilerParams(dimension_semantics=sem, vmem_limit_bytes=VMEM_LIMIT)


def _const_spec(shape):
    nd = len(shape)
    return pl.BlockSpec(shape, lambda *_: (0,) * nd, pipeline_mode=pl.Buffered(1))


def _cast_specs(shape, grid, layer):
    _, rows, cols = shape
    n_steps = math.prod(grid)

    def step(*ids):
        s = 0
        for i, n in zip(ids, grid):
            s = s * n + i
        return s

    block = (1, rows // n_steps, cols)
    return (pl.BlockSpec(block, lambda *ids: (layer, step(*ids), 0)),
            pl.BlockSpec(block, lambda *ids: (0, step(*ids), 0)),
            jax.ShapeDtypeStruct((1, rows, cols), BF16))


def _cast_slices(src_refs, dst_refs):
    for src, dst in zip(src_refs, dst_refs):
        dst[...] = src[...].astype(dst.dtype)


def _layer_norm(v, g, b):
    mu = jnp.mean(v, axis=-1, keepdims=True)
    vc = v - mu
    var = jnp.mean(vc * vc, axis=-1, keepdims=True)
    return vc * lax.rsqrt(var + LN_EPS) * g + b


def _bdot(a, b):
    return jnp.dot(a, b, preferred_element_type=F32)


def _mod_kernel(c_ref, w_ref, vec_ref, o_ref, *, bias_row):
    c = c_ref[...]
    cs = c * jax.nn.sigmoid(c)
    bias = vec_ref[pl.ds(bias_row + 3 * pl.program_id(0) + pl.program_id(1), 1), :]
    o_ref[...] = _bdot(cs.astype(BF16), w_ref[...].astype(BF16)) + bias


def _modulation(c, w, vec, bias_row):
    n_layers = w.shape[0]
    return pl.pallas_call(
        functools.partial(_mod_kernel, bias_row=bias_row),
        out_shape=jax.ShapeDtypeStruct((n_layers, 3, NB, D), F32),
        grid=(n_layers, 3),
        in_specs=[
            pl.BlockSpec((NB, D), lambda l, j: (0, 0)),
            pl.BlockSpec((None, D, D), lambda l, j: (l, 0, j)),
            _const_spec(vec.shape),
        ],
        out_specs=pl.BlockSpec((None, None, NB, D), lambda l, j: (l, j, 0, 0)),
        compiler_params=_cparams("arbitrary", "arbitrary"),
        name="adaln_modulation",
    )(c, w, vec)


def _mod_spec(layer):
    return pl.BlockSpec((None, 3, NB, D), lambda *_: (layer, 0, 0, 0), pipeline_mode=pl.Buffered(1))


SHIFT, SCALE, GATE = 0, 1, 2


def _mla_proj_kernel(x_ref, m_ref, pos_ref, freq_ref, w_in_ref, vec_ref, w_qb_ref, w_kvb_ref,
                     ca_ref, cb_ref, cc_ref, q_ref, kn_ref, kr_ref, v_ref, cab_ref, cbb_ref, ccb_ref):
    _cast_slices((ca_ref, cb_ref, cc_ref), (cab_ref, cbb_ref, ccb_ref))
    qn_ref = vec_ref.at[VEC_Q_NORM:VEC_Q_NORM + 1, 0:Q_LORA]
    kvn_ref = vec_ref.at[VEC_KV_NORM:VEC_KV_NORM + 1, 0:KV_LORA]
    b = pl.program_id(0)
    shift = m_ref[SHIFT, pl.ds(b, 1), :]
    scale = m_ref[SCALE, pl.ds(b, 1), :]

    def rms(v, g):
        return v * lax.rsqrt(jnp.mean(v * v, axis=-1, keepdims=True) + RMS_EPS) * g

    eye = lax.broadcasted_iota(jnp.int32, (128, 128), 0) == lax.broadcasted_iota(jnp.int32, (128, 128), 1)
    for sb in range(PROJ_SUB):
        rows = slice(sb * PROJ_ROWS, (sb + 1) * PROJ_ROWS)
        h = (x_ref[0, rows, :] * (1.0 + scale) + shift).astype(BF16)
        z = _bdot(h, w_in_ref[...])
        cq = rms(z[:, :Q_LORA], qn_ref[...]).astype(BF16)
        ckv = rms(z[:, Q_LORA:Q_LORA + KV_LORA], kvn_ref[...]).astype(BF16)

        pos = pos_ref[0, sb * (PROJ_ROWS // 128):(sb + 1) * (PROJ_ROWS // 128), :].astype(F32)
        pos_col = jnp.concatenate(
            [jnp.sum(jnp.where(eye, jnp.broadcast_to(pos[i:i + 1, :], (128, 128)), 0.0), axis=1, keepdims=True)
             for i in range(PROJ_ROWS // 128)], axis=0)
        ang = pos_col * freq_ref[...]
        lane = lax.broadcasted_iota(jnp.int32, ang.shape, 1)
        mult = jnp.where(lane < QK_ROPE, jnp.cos(ang), jnp.sin(ang))

        def rope(slab):
            s = slab * mult
            return s + pltpu.roll(s, QK_ROPE, axis=1)

        kr_ref[0, rows, :] = jnp.where(lane < QK_ROPE, rope(z[:, Q_LORA + KV_LORA:]), 0.0).astype(BF16)

        q_all = _bdot(cq, w_qb_ref[...])
        kv = _bdot(ckv, w_kvb_ref[...])
        for hd in range(HEADS):
            c0 = hd * HEAD_W
            q_ref[0, hd, rows, 0:QK_NOPE] = q_all[:, c0:c0 + QK_NOPE].astype(BF16)
            q_ref[0, hd, rows, QK_NOPE:HEAD_W] = rope(q_all[:, c0 + QK_NOPE:c0 + HEAD_W]).astype(BF16)
            kn_ref[0, hd, rows, :] = kv[:, c0:c0 + QK_NOPE].astype(BF16)
            v_ref[0, hd, rows, :] = kv[:, c0 + QK_NOPE:c0 + HEAD_W].astype(BF16)


def _mla_proj(x, m, pos, freq, w_in, vec, w_qb, w_kvb, casts):
    nt = SEQ // (PROJ_SUB * PROJ_ROWS)
    blk = PROJ_SUB * PROJ_ROWS
    grid = (NB, nt)
    cast = [_cast_specs(w.shape, grid, layer) for w, layer in casts]
    return pl.pallas_call(
        _mla_proj_kernel,
        out_shape=[
            jax.ShapeDtypeStruct((NB, HEADS, SEQ, HEAD_W), BF16),
            jax.ShapeDtypeStruct((NB, HEADS, SEQ, QK_NOPE), BF16),
            jax.ShapeDtypeStruct((NB, SEQ, HEAD_W - QK_NOPE), BF16),
            jax.ShapeDtypeStruct((NB, HEADS, SEQ, V_DIM), BF16),
        ] + [c[2] for c in cast],
        grid=grid,
        in_specs=[
            pl.BlockSpec((1, blk, D), lambda b, i: (b, i, 0)),
            _mod_spec(0),
            pl.BlockSpec((1, blk // 128, 128), lambda b, i: (b, i, 0)),
            _const_spec((1, 128)),
            _const_spec(w_in.shape),
            _const_spec(vec.shape),
            _const_spec(w_qb.shape),
            _const_spec(w_kvb.shape),
        ] + [c[0] for c in cast],
        out_specs=[
            pl.BlockSpec((1, HEADS, blk, HEAD_W), lambda b, i: (b, 0, i, 0)),
            pl.BlockSpec((1, HEADS, blk, QK_NOPE), lambda b, i: (b, 0, i, 0)),
            pl.BlockSpec((1, blk, HEAD_W - QK_NOPE), lambda b, i: (b, i, 0)),
            pl.BlockSpec((1, HEADS, blk, V_DIM), lambda b, i: (b, 0, i, 0)),
        ] + [c[1] for c in cast],
        compiler_params=_cparams("arbitrary", "arbitrary"),
        name="mla_projections",
    )(x, m, pos, freq, w_in, vec, w_qb, w_kvb, *[w for w, _ in casts])


def _lane_groups(v):
    return [v[:, c:c + 128] for c in range(0, v.shape[1], 128)]


def _attn_kernel(q_ref, kn_ref, kr_ref, v_ref, o_ref, s_ref):
    blk = ATTN_BLK
    half = blk // 2
    nq = SEQ // blk
    tri = lax.broadcasted_iota(jnp.int32, (half, half), 1) <= lax.broadcasted_iota(jnp.int32, (half, half), 0)

    def rows(n):
        return slice(n * blk, (n + 1) * blk)

    def slot(hd, n, j):
        return hd * _ATTN_SLOTS + (0 if n % 2 == 0 else nq - 1) + j

    def nt_dot(a, b):
        return lax.dot_general(a, b, (((1,), (1,)), ((), ())), preferred_element_type=F32)

    def keys(hd, j):
        return jnp.concatenate([kn_ref[hd, rows(j), :], kr_ref[rows(j), :]], axis=1)

    def lane_max(vals):
        return functools.reduce(jnp.maximum, vals)

    def pass1(hd, n):
        q = q_ref[hd, rows(n), :]
        m = []
        for j in range(n):
            s = nt_dot(q, keys(hd, j))
            s_ref[slot(hd, n, j)] = s
            m = [lane_max(_lane_groups(s) + m)]
        k = keys(hd, n)
        s_top = jnp.where(tri, nt_dot(q[:half], k[:half]), NEG_BIG)
        s_bot = nt_dot(q[half:], k)
        s_bot = jnp.concatenate([s_bot[:, :half], jnp.where(tri, s_bot[:, half:], NEG_BIG)], axis=1)
        s_ref[slot(hd, n, n), 0:half, 0:half] = s_top
        s_ref[slot(hd, n, n), half:blk, :] = s_bot
        m_top = lane_max(_lane_groups(s_top) + [v[:half] for v in m])
        m_bot = lane_max(_lane_groups(s_bot) + [v[half:] for v in m])
        m = jnp.concatenate([m_top, m_bot], axis=0)
        return jnp.broadcast_to(jnp.max(m, axis=-1, keepdims=True), (blk, 128))

    def pass2(hd, n, mb):
        diag = slot(hd, n, n)
        ps = [jnp.exp2(sg - mb) for j in range(n) for sg in _lane_groups(s_ref[slot(hd, n, j)])]
        p_top = [jnp.exp2(sg - mb[:half]) for sg in _lane_groups(s_ref[diag, 0:half, 0:half])]
        p_bot = [jnp.exp2(sg - mb[half:]) for sg in _lane_groups(s_ref[diag, half:blk, :])]
        l_top = functools.reduce(jnp.add, p_top + [p[:half] for p in ps])
        l_bot = functools.reduce(jnp.add, p_bot + [p[half:] for p in ps])
        l = jnp.sum(jnp.concatenate([l_top, l_bot], axis=0), axis=-1, keepdims=True)
        v0 = n * blk
        acc = jnp.concatenate([
            _bdot(jnp.concatenate(p_top, axis=1).astype(BF16), v_ref[hd, v0:v0 + half, :]),
            _bdot(jnp.concatenate(p_bot, axis=1).astype(BF16), v_ref[hd, v0:v0 + blk, :])], axis=0)
        if n > 0:
            acc = acc + _bdot(jnp.concatenate(ps, axis=1).astype(BF16), v_ref[hd, 0:v0, :])
        o_ref[hd, rows(n), :] = (acc / l).astype(BF16)

    mb = [pass1(hd, 0) for hd in range(ATTN_HEADS)]
    for n in range(nq):
        for hd in range(ATTN_HEADS):
            mb_next = pass1(hd, n + 1) if n + 1 < nq else None
            pass2(hd, n, mb[hd])
            mb[hd] = mb_next


def _attention(q, kn, kr, v):
    head_spec = lambda w: pl.BlockSpec((None, ATTN_HEADS, SEQ, w), lambda b, h: (b, h, 0, 0))
    kr_spec = pl.BlockSpec((None, SEQ, HEAD_W - QK_NOPE), lambda b, h: (b, 0, 0))
    return pl.pallas_call(
        _attn_kernel,
        out_shape=jax.ShapeDtypeStruct((NB, HEADS, SEQ, V_DIM), BF16),
        grid=(NB, HEADS // ATTN_HEADS),
        in_specs=[head_spec(HEAD_W), head_spec(QK_NOPE), kr_spec, head_spec(V_DIM)],
        out_specs=head_spec(V_DIM),
        scratch_shapes=[pltpu.VMEM((ATTN_HEADS * _ATTN_SLOTS, ATTN_BLK, ATTN_BLK), F32)],
        compiler_params=_cparams("arbitrary", "arbitrary"),
        name="mla_attention",
    )(q, kn, kr, v)


def _vec_row(vec_ref, row):
    return vec_ref[row:row + 1, :]


def _ffn_rows(x, shift, scale, gate, w1_ref, w2_ref, vec_ref, layer):
    h = (x * (1.0 + scale) + shift).astype(BF16)
    acc = jnp.zeros(x.shape, F32)
    for c in range(D_FF // FF_CHUNK):
        cols = slice(c * FF_CHUNK, (c + 1) * FF_CHUNK)
        b1 = _vec_row(vec_ref, VEC_MLP_B1 + layer * (D_FF // FF_CHUNK) + c)
        a = jnp.maximum(_bdot(h, w1_ref[:, cols]) + b1, 0.0)
        acc = acc + _bdot((a * a).astype(BF16), w2_ref[cols, :])
    y = acc + _vec_row(vec_ref, VEC_MLP_B2 + layer)
    return _layer_norm(ALPHA * x + (1.0 + gate) * y,
                       _vec_row(vec_ref, VEC_LN_FFN_G + layer), _vec_row(vec_ref, VEC_LN_FFN_B + layer))


def _batch_mod(m_ref):
    b = pl.program_id(0)
    return [m_ref[i, pl.ds(b, 1), :] for i in (SHIFT, SCALE, GATE)]


def _ffn_kernel(x_ref, m_ref, w1_ref, w2_ref, vec_ref, y_ref, *, layer):
    mod = _batch_mod(m_ref)
    for sb in range(FFN_SUB):
        rows = slice(sb * FFN_ROWS, (sb + 1) * FFN_ROWS)
        y_ref[rows, :] = _ffn_rows(x_ref[rows, :], *mod, w1_ref, w2_ref, vec_ref, layer)


def _attn_out_ffn_kernel(o_ref, x_ref, mm_ref, mf_ref, wo_ref, w1_ref, w2_ref, vec_ref, sa_ref, sb_ref, sc_ref,
                         y_ref, sab_ref, sbb_ref, scb_ref, *, layer):
    _cast_slices((sa_ref, sb_ref, sc_ref), (sab_ref, sbb_ref, scb_ref))
    gm_ref = vec_ref.at[VEC_LN_MIX_G + layer:VEC_LN_MIX_G + layer + 1]
    bm_ref = vec_ref.at[VEC_LN_MIX_B + layer:VEC_LN_MIX_B + layer + 1]
    gate_mix = _batch_mod(mm_ref)[GATE]
    mod = _batch_mod(mf_ref)
    for sb in range(FFN_SUB):
        rows = slice(sb * FFN_ROWS, (sb + 1) * FFN_ROWS)
        o = jnp.concatenate([o_ref[hd, rows, :] for hd in range(HEADS)], axis=1)
        y_ref[rows, :] = _layer_norm(ALPHA * x_ref[rows, :] + (1.0 + gate_mix) * _bdot(o, wo_ref[...]),
                                     gm_ref[...], bm_ref[...])
    for sb in range(FFN_SUB):
        rows = slice(sb * FFN_ROWS, (sb + 1) * FFN_ROWS)
        y_ref[rows, :] = _ffn_rows(y_ref[rows, :], *mod, w1_ref, w2_ref, vec_ref, layer)


_FFN_X_SPEC = pl.BlockSpec((None, FFN_SUB * FFN_ROWS, D), lambda b, i: (b, i, 0))
_FFN_GRID = (NB, SEQ // (FFN_SUB * FFN_ROWS))


def _layer_spec(shape, layer):
    return pl.BlockSpec((None,) + tuple(shape[1:]), lambda *_: (layer, 0, 0), pipeline_mode=pl.Buffered(1))


def _ffn(x, m, layer, w1, w2, vec):
    return pl.pallas_call(
        functools.partial(_ffn_kernel, layer=layer),
        out_shape=jax.ShapeDtypeStruct((NB, SEQ, D), F32),
        grid=_FFN_GRID,
        in_specs=[_FFN_X_SPEC, _mod_spec(layer), _layer_spec(w1.shape, 0), _layer_spec(w2.shape, 0),
                  _const_spec(vec.shape)],
        out_specs=_FFN_X_SPEC,
        compiler_params=_cparams("arbitrary", "arbitrary"),
        name="ffn_norm",
    )(x, m, w1, w2, vec)


def _attn_out_ffn(o, x, m_mix, m_ffn, layer, w_o, w1, w2, vec, casts):
    o_spec = pl.BlockSpec((None, HEADS, FFN_SUB * FFN_ROWS, V_DIM), lambda b, i: (b, 0, i, 0))
    cast = [_cast_specs(w.shape, _FFN_GRID, l) for w, l in casts]
    return pl.pallas_call(
        functools.partial(_attn_out_ffn_kernel, layer=layer),
        out_shape=[jax.ShapeDtypeStruct((NB, SEQ, D), F32)] + [c[2] for c in cast],
        grid=_FFN_GRID,
        in_specs=[o_spec, _FFN_X_SPEC, _mod_spec(layer), _mod_spec(layer), _layer_spec(w_o.shape, 0),
                  _layer_spec(w1.shape, 0), _layer_spec(w2.shape, 0), _const_spec(vec.shape)]
                 + [c[0] for c in cast],
        out_specs=[_FFN_X_SPEC] + [c[1] for c in cast],
        compiler_params=_cparams("arbitrary", "arbitrary"),
        name="attn_out_ffn_norm",
    )(o, x, m_mix, m_ffn, w_o, w1, w2, vec, *[w for w, _ in casts])


def _cmul(ar, ai, br, bi):
    return ar * br - ai * bi, ar * bi + ai * br


def _pair_slot(piece, zero, slot):
    return [piece, zero] if slot == 0 else [zero, piece]


def _s5_disc_kernel(ldt_ref, are_ref, aim_ref, bre_ref, bim_ref, cre_ref, cim_ref, d_ref,
                    a8r_ref, a8i_ref, wb_ref, wck_ref):
    lr = are_ref[...]
    li = aim_ref[...]
    dt = jnp.exp(ldt_ref[...])
    mag = jnp.exp(lr * dt)
    ab_re = mag * jnp.cos(li * dt)
    ab_im = mag * jnp.sin(li * dt)
    den = lr * lr + li * li
    nr = ab_re - 1.0
    coef_re = (nr * lr + ab_im * li) / den
    coef_im = (ab_im * lr - nr * li) / den
    bb_re, bb_im = _cmul(coef_re, coef_im, bre_ref[...], bim_ref[...])
    c_re = cre_ref[...]
    c_im = cim_ref[...]

    pw = [(jnp.ones_like(ab_re), jnp.zeros_like(ab_im))]
    for _ in range(CHUNK):
        pw.append(_cmul(pw[-1][0], pw[-1][1], ab_re, ab_im))
    a8r_ref[...] = pw[CHUNK][0]
    a8i_ref[...] = pw[CHUNK][1]

    n_pairs = wb_ref.shape[0]

    def pairs(v):
        v4 = v.reshape(n_pairs, 2, v.shape[1], v.shape[2])
        return v4[:, 0], v4[:, 1]

    z64 = jnp.zeros((n_pairs, GROUP_CH, STATE), F32)
    z16 = jnp.zeros((n_pairs, GROUP_CH, GROUP_CH), F32)
    eye = (lax.broadcasted_iota(jnp.int32, (GROUP_CH, GROUP_CH), 0)
           == lax.broadcasted_iota(jnp.int32, (GROUP_CH, GROUP_CH), 1)).astype(F32)

    ab_l = [_cmul(pw[l][0], pw[l][1], bb_re, bb_im) for l in range(CHUNK)]
    wb_rows = []
    for k in range(CHUNK):
        m_re, m_im = ab_l[CHUNK - 1 - k]
        for slot in range(2):
            wb_rows.append(jnp.concatenate(_pair_slot(pairs(m_re)[slot], z64, slot)
                                           + _pair_slot(pairs(m_im)[slot], z64, slot), axis=2))
    wb_ref[...] = jnp.concatenate(wb_rows, axis=1).astype(BF16)

    def nt(a, b):
        return lax.dot_general(a.astype(BF16), b.astype(BF16), (((2,), (2,)), ((0,), (0,))),
                               preferred_element_type=F32)

    k_l = [nt(c_re, ab_l[l][0]) - nt(c_im, ab_l[l][1]) for l in range(CHUNK)]
    k_l[0] = k_l[0] + d_ref[...] * eye[None]

    wck_rows = []
    for kp in range(CHUNK):
        ca_re, ca_im = _cmul(c_re, c_im, pw[kp + 1][0], pw[kp + 1][1])
        for slot in range(2):
            state_cols = (_pair_slot(pairs(ca_re)[slot], z64, slot)
                          + _pair_slot(-pairs(ca_im)[slot], z64, slot))
            direct_cols = []
            for k in range(CHUNK):
                blk = pairs(k_l[kp - k])[slot] if k <= kp else z16
                direct_cols += _pair_slot(blk, z16, slot)
            wck_rows.append(jnp.concatenate(state_cols + direct_cols, axis=2))
    wck_ref[...] = jnp.concatenate(wck_rows, axis=1).astype(BF16)


def _s5_discretise(log_dt, a_re, a_im, b_re, b_im, c_re, c_im, d_skip):
    g3 = (N_GROUPS, 1, STATE)
    gb = DISC_GROUPS
    spec = lambda *tail: pl.BlockSpec((gb,) + tail, lambda i: (i,) + (0,) * len(tail))
    pair_spec = lambda cols: pl.BlockSpec((gb // 2, PAIR_W, cols), lambda i: (i, 0, 0))
    big = spec(GROUP_CH, STATE)
    return pl.pallas_call(
        _s5_disc_kernel,
        out_shape=(jax.ShapeDtypeStruct(g3, F32), jax.ShapeDtypeStruct(g3, F32),
                   jax.ShapeDtypeStruct((N_PAIRS, PAIR_W, PAIR_W), BF16),
                   jax.ShapeDtypeStruct((N_PAIRS, PAIR_W, 2 * PAIR_W), BF16)),
        grid=(N_GROUPS // gb,),
        in_specs=[spec(1, 1), spec(1, STATE), spec(1, STATE), big, big, big, big, spec(GROUP_CH, 1)],
        out_specs=(spec(1, STATE), spec(1, STATE), pair_spec(PAIR_W), pair_spec(2 * PAIR_W)),
        compiler_params=_cparams("arbitrary"),
        name="s5_discretise",
    )(log_dt.reshape(N_GROUPS, 1, 1), a_re.reshape(g3), a_im.reshape(g3),
      jnp.swapaxes(b_re, 1, 2), jnp.swapaxes(b_im, 1, 2), c_re, c_im, d_skip.reshape(N_GROUPS, GROUP_CH, 1))


S5_X_SPEC = pl.BlockSpec((NB, S5_BLOCK, D), lambda i: (0, i, 0))
S5_U8_SPEC = pl.BlockSpec((S5_BLOCK // CHUNK * NB, N_PAIRS * PAIR_W), lambda i: (i, 0))


def _s5_in_kernel(x_ref, m_ref, w_in_ref, u8_ref):
    shift = m_ref[SHIFT]
    scale = m_ref[SCALE]
    steps = S5_IN_STEPS
    chunks = steps // CHUNK
    for sb in range(S5_BLOCK // steps):
        x3 = jnp.swapaxes(x_ref[:, sb * steps:(sb + 1) * steps, :], 0, 1)
        h = (x3 * (1.0 + scale)[None] + shift[None]).reshape(steps * NB, D).astype(BF16)
        u = _bdot(h, w_in_ref[...])
        u4 = u.reshape(chunks, CHUNK, NB, D)
        per_k = [u4[:, k].reshape(chunks * NB, D) for k in range(CHUNK)]
        cols = [per_k[k][:, j * PAIR_CH:(j + 1) * PAIR_CH] for j in range(N_PAIRS) for k in range(CHUNK)]
        u8_ref[sb * chunks * NB:(sb + 1) * chunks * NB, :] = jnp.concatenate(cols, axis=1).astype(BF16)


def _s5_in(x, m, w_in):
    return pl.pallas_call(
        _s5_in_kernel,
        out_shape=jax.ShapeDtypeStruct((SEQ // CHUNK * NB, N_PAIRS * PAIR_W), BF16),
        grid=(SEQ // S5_BLOCK,),
        in_specs=[
            S5_X_SPEC,
            _mod_spec(1),
            _layer_spec(w_in.shape, 0),
        ],
        out_specs=S5_U8_SPEC,
        compiler_params=_cparams("arbitrary"),
        name="s5_in_proj",
    )(x, m, w_in)


def _s5_core_kernel(u8_ref, wb_ref, wck_ref, a8r_ref, a8i_ref, y8_ref):
    half = PAIR_W // 2
    for jj in range(PAIRS_PER_STEP):
        lanes = slice(jj * PAIR_W, (jj + 1) * PAIR_W)
        u8 = u8_ref[:, lanes]
        v = _bdot(u8, wb_ref[jj])
        pair = lambda ref: jnp.concatenate([ref[2 * jj], ref[2 * jj + 1]], axis=1)
        ar = jnp.broadcast_to(pair(a8r_ref), (NB, half))
        ai = jnp.broadcast_to(pair(a8i_ref), (NB, half))
        xr = jnp.zeros((NB, half), F32)
        xi = jnp.zeros((NB, half), F32)
        prev_r, prev_i = [], []
        for s in range(SEQ // CHUNK):
            prev_r.append(xr)
            prev_i.append(xi)
            rows = slice(s * NB, (s + 1) * NB)
            xr, xi = ar * xr - ai * xi + v[rows, :half], ar * xi + ai * xr + v[rows, half:]
        x_prev = jnp.concatenate([jnp.concatenate(prev_r, axis=0), jnp.concatenate(prev_i, axis=0)], axis=1)
        lhs = jnp.concatenate([x_prev.astype(BF16), u8], axis=1)
        y8_ref[:, lanes] = lax.dot_general(lhs, wck_ref[jj], (((1,), (1,)), ((), ())), preferred_element_type=F32)


def _s5_core(u8, wb, wck, a8r, a8i):
    n_rows = u8.shape[0]
    w = PAIRS_PER_STEP * PAIR_W
    return pl.pallas_call(
        _s5_core_kernel,
        out_shape=jax.ShapeDtypeStruct(u8.shape, F32),
        grid=(N_PAIRS // PAIRS_PER_STEP,),
        in_specs=[
            pl.BlockSpec((n_rows, w), lambda i: (0, i)),
            pl.BlockSpec((PAIRS_PER_STEP, PAIR_W, PAIR_W), lambda i: (i, 0, 0)),
            pl.BlockSpec((PAIRS_PER_STEP, PAIR_W, 2 * PAIR_W), lambda i: (i, 0, 0)),
            pl.BlockSpec((2 * PAIRS_PER_STEP, 1, STATE), lambda i: (i, 0, 0)),
            pl.BlockSpec((2 * PAIRS_PER_STEP, 1, STATE), lambda i: (i, 0, 0)),
        ],
        out_specs=pl.BlockSpec((n_rows, w), lambda i: (0, i)),
        compiler_params=_cparams("arbitrary"),
        name="s5_recurrence",
    )(u8, wb, wck, a8r, a8i)


def _s5_out_kernel(y8_ref, x_ref, m_ref, w_glu_ref, w_out_ref, vec_ref, ca_ref, cb_ref, o_ref, cab_ref, cbb_ref):
    _cast_slices((ca_ref, cb_ref), (cab_ref, cbb_ref))
    b_glu_ref = vec_ref.at[VEC_B_GLU:VEC_B_GLU + 1]
    g_ref = vec_ref.at[VEC_LN_MIX_G + 1:VEC_LN_MIX_G + 2]
    b_ref = vec_ref.at[VEC_LN_MIX_B + 1:VEC_LN_MIX_B + 2]
    gate = m_ref[GATE]
    steps = S5_OUT_STEPS
    chunks = steps // CHUNK
    for sb in range(S5_BLOCK // steps):
        y8 = y8_ref[sb * chunks * NB:(sb + 1) * chunks * NB, :]
        per_k = [jnp.concatenate([y8[:, j * PAIR_W + k * PAIR_CH:j * PAIR_W + (k + 1) * PAIR_CH]
                                  for j in range(N_PAIRS)], axis=1) for k in range(CHUNK)]
        y = jnp.stack([p.reshape(chunks, NB, D) for p in per_k], axis=1).reshape(steps * NB, D)

        half_y = 0.5 * y
        g = half_y + half_y * jnp.tanh(y * (GELU_C + (GELU_C * 0.044715) * (y * y)))
        zz = g * jax.nn.sigmoid(_bdot(g.astype(BF16), w_glu_ref[...]) + b_glu_ref[...])
        out = jnp.swapaxes(_bdot(zz.astype(BF16), w_out_ref[...]).reshape(steps, NB, D), 0, 1)
        t_rows = slice(sb * steps, (sb + 1) * steps)
        res = ALPHA * x_ref[:, t_rows, :] + (1.0 + gate)[:, None, :] * out
        o_ref[:, t_rows, :] = _layer_norm(res, g_ref[...], b_ref[...])


def _s5_out(y8, x, m, w_glu, w_out, vec, casts):
    x_spec = S5_X_SPEC
    grid = (SEQ // S5_BLOCK,)
    cast = [_cast_specs(w.shape, grid, l) for w, l in casts]
    return pl.pallas_call(
        _s5_out_kernel,
        out_shape=[jax.ShapeDtypeStruct((NB, SEQ, D), F32)] + [c[2] for c in cast],
        grid=grid,
        in_specs=[
            S5_U8_SPEC,
            x_spec,
            _mod_spec(1),
            _layer_spec(w_glu.shape, 0),
            _layer_spec(w_out.shape, 0),
            _const_spec(vec.shape),
        ] + [c[0] for c in cast],
        out_specs=[x_spec] + [c[1] for c in cast],
        compiler_params=_cparams("arbitrary"),
        name="s5_out_norm",
    )(y8, x, m, w_glu, w_out, vec, *[w for w, _ in casts])


def _rotate_half_cols(w):
    half = w.shape[-1] // 2
    return jnp.concatenate([-w[..., half:], w[..., :half]], axis=-1)


def _mla_weights(w_in, w_qb):
    k_pe = w_in[:, Q_LORA + KV_LORA:]
    w_in_ext = jnp.concatenate([w_in, _rotate_half_cols(k_pe)], axis=1)
    wq = w_qb.reshape(Q_LORA, HEADS, QK_NOPE + QK_ROPE) * Q_SCALE
    wq_ext = jnp.concatenate([wq, _rotate_half_cols(wq[..., QK_NOPE:])], axis=-1)
    return w_in_ext.astype(BF16), wq_ext.reshape(Q_LORA, HEADS * HEAD_W).astype(BF16)


def kernel(x, c, positions, mla_w_in, mla_q_norm, mla_w_qb, mla_kv_norm, mla_w_kvb, mla_w_o, ssm_w_in, ssm_log_dt, ssm_a_re, ssm_a_im, ssm_b_re, ssm_b_im, ssm_c_re, ssm_c_im, ssm_d, ssm_w_glu, ssm_b_glu, ssm_w_out, mlp_w1, mlp_b1, mlp_w2, mlp_b2, mod_mix_w, mod_mix_b, mod_ffn_w, mod_ffn_b, ln_mix_g, ln_mix_b, ln_ffn_g, ln_ffn_b):
    vec = _pack_vectors(ln_mix_g, ln_mix_b, ln_ffn_g, ln_ffn_b, mlp_b2, ssm_b_glu, mlp_b1, mod_mix_b, mod_ffn_b,
                        mla_q_norm, mla_kv_norm)
    m_mix = _modulation(c, mod_mix_w, vec, VEC_MOD_MIX_B)
    m_ffn = _modulation(c, mod_ffn_w, vec, VEC_MOD_FFN_B)

    inv_freq = ROPE_THETA ** (-jnp.arange(0, QK_ROPE, 2, dtype=F32) / QK_ROPE)
    freq = jnp.tile(inv_freq, 4).reshape(1, 128)
    w_in_ext, w_qb_ext = _mla_weights(mla_w_in[0], mla_w_qb[0])
    q, kn, kr, v, w1_0, w2_0, w_o = _mla_proj(
        x, m_mix, positions.reshape(NB, SEQ // 128, 128), freq, w_in_ext, vec, w_qb_ext, mla_w_kvb[0].astype(BF16),
        ((mlp_w1, 0), (mlp_w2, 0), (mla_w_o, 0)))
    o = _attention(q, kn, kr, v)
    x2, s_w_in, s_w_glu, s_w_out = _attn_out_ffn(
        o, x, m_mix, m_ffn, 0, w_o, w1_0, w2_0, vec, ((ssm_w_in, 0), (ssm_w_glu, 0), (ssm_w_out, 0)))

    a8r, a8i, wb, wck = _s5_discretise(ssm_log_dt[0], ssm_a_re[0], ssm_a_im[0], ssm_b_re[0], ssm_b_im[0],
                                       ssm_c_re[0], ssm_c_im[0], ssm_d[0])
    u8 = _s5_in(x2, m_mix, s_w_in)
    y8 = _s5_core(u8, wb, wck, a8r, a8i)
    x3, w1_1, w2_1 = _s5_out(y8, x2, m_mix, s_w_glu, s_w_out, vec, ((mlp_w1, 1), (mlp_w2, 1)))
    return _ffn(x3, m_ffn, 1, w1_1, w2_1, vec)
```

```python
import functools
import math

import jax
import jax.numpy as jnp
from jax import lax
from jax.experimental import pallas as pl
from jax.experimental.pallas import tpu as pltpu

F32 = jnp.float32
BF16 = jnp.bfloat16

D = 1024
NB = 8
SEQ = 2048
HEADS = 8
QK_NOPE = 128
QK_ROPE = 64
V_DIM = 128
Q_LORA = 256
KV_LORA = 128
ROPE_THETA = 10000.0
GROUP_CH = 16
N_GROUPS = 64
STATE = 64
D_FF = 4 * D
DEPTH = 2
ALPHA = (2 * DEPTH) ** 0.25
LN_EPS = 1e-5
RMS_EPS = 1e-6
Q_SCALE = math.log2(math.e) / math.sqrt(QK_NOPE + QK_ROPE)
NEG_BIG = -1e30
GELU_C = math.sqrt(2.0 / math.pi)

HEAD_W = 256
CHUNK = 8
N_PAIRS = N_GROUPS // 2
PAIR_CH = 2 * GROUP_CH
PAIR_W = CHUNK * PAIR_CH
assert PAIR_W == 4 * STATE

VMEM_LIMIT = 56 * 1024 * 1024
PROJ_ROWS = 1024
PROJ_SUB = 1
ATTN_BLK = 512
ATTN_HEADS = 4
_ATTN_SLOTS = 2 * (SEQ // ATTN_BLK) - 1
FFN_ROWS = 512
FFN_SUB = 2
FF_CHUNK = 1024
S5_BLOCK = 128
S5_IN_STEPS = 64
S5_OUT_STEPS = 32
PAIRS_PER_STEP = 4
DISC_GROUPS = 16


VEC_LN_MIX_G, VEC_LN_MIX_B, VEC_LN_FFN_G, VEC_LN_FFN_B, VEC_MLP_B2 = 0, DEPTH, 2 * DEPTH, 3 * DEPTH, 4 * DEPTH
VEC_B_GLU = 5 * DEPTH
VEC_MLP_B1 = VEC_B_GLU + 1
VEC_MOD_MIX_B = VEC_MLP_B1 + DEPTH * (D_FF // D)
VEC_MOD_FFN_B = VEC_MOD_MIX_B + 3 * DEPTH
VEC_Q_NORM = VEC_MOD_FFN_B + 3 * DEPTH
VEC_KV_NORM = VEC_Q_NORM + 1
VEC_ROWS = -(-(VEC_KV_NORM + 1) // 8) * 8
assert FF_CHUNK == D


def _pack_vectors(ln_mix_g, ln_mix_b, ln_ffn_g, ln_ffn_b, mlp_b2, b_glu, mlp_b1, mod_mix_b, mod_ffn_b, q_norm, kv_norm):
    pad = lambda v: jnp.pad(v.reshape(1, -1), ((0, 0), (0, D - v.size)))
    rows = [ln_mix_g, ln_mix_b, ln_ffn_g, ln_ffn_b, mlp_b2, b_glu.reshape(1, D), mlp_b1.reshape(-1, D),
            mod_mix_b.reshape(-1, D), mod_ffn_b.reshape(-1, D), pad(q_norm), pad(kv_norm)]
    table = jnp.concatenate(rows, axis=0)
    return jnp.pad(table, ((0, VEC_ROWS - table.shape[0]), (0, 0)))


def _cparams(*sem):
    return pltpu.CompilerParams(dimension_semantics=sem, vmem_limit_bytes=VMEM_LIMIT)


def _const_spec(shape):
    nd = len(shape)
    return pl.BlockSpec(shape, lambda *_: (0,) * nd, pipeline_mode=pl.Buffered(1))


def _cast_specs(shape, grid, layer):
    _, rows, cols = shape
    n_steps = math.prod(grid)

    def step(*ids):
        s = 0
        for i, n in zip(ids, grid):
            s = s * n + i
        return s

    block = (1, rows // n_steps, cols)
    return (pl.BlockSpec(block, lambda *ids: (layer, step(*ids), 0)),
            pl.BlockSpec(block, lambda *ids: (0, step(*ids), 0)),
            jax.ShapeDtypeStruct((1, rows, cols), BF16))


def _cast_slices(src_refs, dst_refs):
    for src, dst in zip(src_refs, dst_refs):
        dst[...] = src[...].astype(dst.dtype)


def _layer_norm(v, g, b):
    mu = jnp.mean(v, axis=-1, keepdims=True)
    vc = v - mu
    var = jnp.mean(vc * vc, axis=-1, keepdims=True)
    return vc * lax.rsqrt(var + LN_EPS) * g + b


def _bdot(a, b):
    return jnp.dot(a, b, preferred_element_type=F32)


def _mod_kernel(c_ref, w_ref, vec_ref, o_ref, *, bias_row):
    c = c_ref[...]
    cs = c * jax.nn.sigmoid(c)
    bias = vec_ref[pl.ds(bias_row + 3 * pl.program_id(0) + pl.program_id(1), 1), :]
    o_ref[...] = _bdot(cs.astype(BF16), w_ref[...].astype(BF16)) + bias


def _modulation(c, w, vec, bias_row):
    n_layers = w.shape[0]
    return pl.pallas_call(
        functools.partial(_mod_kernel, bias_row=bias_row),
        out_shape=jax.ShapeDtypeStruct((n_layers, 3, NB, D), F32),
        grid=(n_layers, 3),
        in_specs=[
            pl.BlockSpec((NB, D), lambda l, j: (0, 0)),
            pl.BlockSpec((None, D, D), lambda l, j: (l, 0, j)),
            _const_spec(vec.shape),
        ],
        out_specs=pl.BlockSpec((None, None, NB, D), lambda l, j: (l, j, 0, 0)),
        compiler_params=_cparams("arbitrary", "arbitrary"),
        name="adaln_modulation",
    )(c, w, vec)


def _mod_spec(layer):
    return pl.BlockSpec((None, 3, NB, D), lambda *_: (layer, 0, 0, 0), pipeline_mode=pl.Buffered(1))


SHIFT, SCALE, GATE = 0, 1, 2


def _mla_proj_kernel(x_ref, m_ref, pos_ref, freq_ref, w_in_ref, vec_ref, w_qb_ref, w_kvb_ref,
                     ca_ref, cb_ref, cc_ref, q_ref, kn_ref, kr_ref, v_ref, cab_ref, cbb_ref, ccb_ref):
    _cast_slices((ca_ref, cb_ref, cc_ref), (cab_ref, cbb_ref, ccb_ref))
    qn_ref = vec_ref.at[VEC_Q_NORM:VEC_Q_NORM + 1, 0:Q_LORA]
    kvn_ref = vec_ref.at[VEC_KV_NORM:VEC_KV_NORM + 1, 0:KV_LORA]
    b = pl.program_id(0)
    shift = m_ref[SHIFT, pl.ds(b, 1), :]
    scale = m_ref[SCALE, pl.ds(b, 1), :]

    def rms(v, g):
        return v * lax.rsqrt(jnp.mean(v * v, axis=-1, keepdims=True) + RMS_EPS) * g

    eye = lax.broadcasted_iota(jnp.int32, (128, 128), 0) == lax.broadcasted_iota(jnp.int32, (128, 128), 1)
    for sb in range(PROJ_SUB):
        rows = slice(sb * PROJ_ROWS, (sb + 1) * PROJ_ROWS)
        h = (x_ref[0, rows, :] * (1.0 + scale) + shift).astype(BF16)
        z = _bdot(h, w_in_ref[...])
        cq = rms(z[:, :Q_LORA], qn_ref[...]).astype(BF16)
        ckv = rms(z[:, Q_LORA:Q_LORA + KV_LORA], kvn_ref[...]).astype(BF16)

        pos = pos_ref[0, sb * (PROJ_ROWS // 128):(sb + 1) * (PROJ_ROWS // 128), :].astype(F32)
        pos_col = jnp.concatenate(
            [jnp.sum(jnp.where(eye, jnp.broadcast_to(pos[i:i + 1, :], (128, 128)), 0.0), axis=1, keepdims=True)
             for i in range(PROJ_ROWS // 128)], axis=0)
        ang = pos_col * freq_ref[...]
        lane = lax.broadcasted_iota(jnp.int32, ang.shape, 1)
        mult = jnp.where(lane < QK_ROPE, jnp.cos(ang), jnp.sin(ang))

        def rope(slab):
            s = slab * mult
            return s + pltpu.roll(s, QK_ROPE, axis=1)

        kr_ref[0, rows, :] = jnp.where(lane < QK_ROPE, rope(z[:, Q_LORA + KV_LORA:]), 0.0).astype(BF16)

        q_all = _bdot(cq, w_qb_ref[...])
        kv = _bdot(ckv, w_kvb_ref[...])
        for hd in range(HEADS):
            c0 = hd * HEAD_W
            q_ref[0, hd, rows, 0:QK_NOPE] = q_all[:, c0:c0 + QK_NOPE].astype(BF16)
            q_ref[0, hd, rows, QK_NOPE:HEAD_W] = rope(q_all[:, c0 + QK_NOPE:c0 + HEAD_W]).astype(BF16)
            kn_ref[0, hd, rows, :] = kv[:, c0:c0 + QK_NOPE].astype(BF16)
            v_ref[0, hd, rows, :] = kv[:, c0 + QK_NOPE:c0 + HEAD_W].astype(BF16)


def _mla_proj(x, m, pos, freq, w_in, vec, w_qb, w_kvb, casts):
    nt = SEQ // (PROJ_SUB * PROJ_ROWS)
    blk = PROJ_SUB * PROJ_ROWS
    grid = (NB, nt)
    cast = [_cast_specs(w.shape, grid, layer) for w, layer in casts]
    return pl.pallas_call(
        _mla_proj_kernel,
        out_shape=[
            jax.ShapeDtypeStruct((NB, HEADS, SEQ, HEAD_W), BF16),
            jax.ShapeDtypeStruct((NB, HEADS, SEQ, QK_NOPE), BF16),
            jax.ShapeDtypeStruct((NB, SEQ, HEAD_W - QK_NOPE), BF16),
            jax.ShapeDtypeStruct((NB, HEADS, SEQ, V_DIM), BF16),
        ] + [c[2] for c in cast],
        grid=grid,
        in_specs=[
            pl.BlockSpec((1, blk, D), lambda b, i: (b, i, 0)),
            _mod_spec(0),
            pl.BlockSpec((1, blk // 128, 128), lambda b, i: (b, i, 0)),
            _const_spec((1, 128)),
            _const_spec(w_in.shape),
            _const_spec(vec.shape),
            _const_spec(w_qb.shape),
            _const_spec(w_kvb.shape),
        ] + [c[0] for c in cast],
        out_specs=[
            pl.BlockSpec((1, HEADS, blk, HEAD_W), lambda b, i: (b, 0, i, 0)),
            pl.BlockSpec((1, HEADS, blk, QK_NOPE), lambda b, i: (b, 0, i, 0)),
            pl.BlockSpec((1, blk, HEAD_W - QK_NOPE), lambda b, i: (b, i, 0)),
            pl.BlockSpec((1, HEADS, blk, V_DIM), lambda b, i: (b, 0, i, 0)),
        ] + [c[1] for c in cast],
        compiler_params=_cparams("arbitrary", "arbitrary"),
        name="mla_projections",
    )(x, m, pos, freq, w_in, vec, w_qb, w_kvb, *[w for w, _ in casts])


def _lane_groups(v):
    return [v[:, c:c + 128] for c in range(0, v.shape[1], 128)]


def _attn_kernel(q_ref, kn_ref, kr_ref, v_ref, o_ref, s_ref):
    blk = ATTN_BLK
    half = blk // 2
    nq = SEQ // blk
    tri = lax.broadcasted_iota(jnp.int32, (half, half), 1) <= lax.broadcasted_iota(jnp.int32, (half, half), 0)

    def rows(n):
        return slice(n * blk, (n + 1) * blk)

    def slot(hd, n, j):
        return hd * _ATTN_SLOTS + (0 if n % 2 == 0 else nq - 1) + j

    def nt_dot(a, b):
        return lax.dot_general(a, b, (((1,), (1,)), ((), ())), preferred_element_type=F32)

    def keys(hd, j):
        return jnp.concatenate([kn_ref[hd, rows(j), :], kr_ref[rows(j), :]], axis=1)

    def lane_max(vals):
        return functools.reduce(jnp.maximum, vals)

    def pass1(hd, n):
        q = q_ref[hd, rows(n), :]
        m = []
        for j in range(n):
            s = nt_dot(q, keys(hd, j))
            s_ref[slot(hd, n, j)] = s
            m = [lane_max(_lane_groups(s) + m)]
        k = keys(hd, n)
        s_top = jnp.where(tri, nt_dot(q[:half], k[:half]), NEG_BIG)
        s_bot = nt_dot(q[half:], k)
        s_bot = jnp.concatenate([s_bot[:, :half], jnp.where(tri, s_bot[:, half:], NEG_BIG)], axis=1)
        s_ref[slot(hd, n, n), 0:half, 0:half] = s_top
        s_ref[slot(hd, n, n), half:blk, :] = s_bot
        m_top = lane_max(_lane_groups(s_top) + [v[:half] for v in m])
        m_bot = lane_max(_lane_groups(s_bot) + [v[half:] for v in m])
        m = jnp.concatenate([m_top, m_bot], axis=0)
        return jnp.broadcast_to(jnp.max(m, axis=-1, keepdims=True), (blk, 128))

    def pass2(hd, n, mb):
        diag = slot(hd, n, n)
        ps = [jnp.exp2(sg - mb) for j in range(n) for sg in _lane_groups(s_ref[slot(hd, n, j)])]
        p_top = [jnp.exp2(sg - mb[:half]) for sg in _lane_groups(s_ref[diag, 0:half, 0:half])]
        p_bot = [jnp.exp2(sg - mb[half:]) for sg in _lane_groups(s_ref[diag, half:blk, :])]
        l_top = functools.reduce(jnp.add, p_top + [p[:half] for p in ps])
        l_bot = functools.reduce(jnp.add, p_bot + [p[half:] for p in ps])
        l = jnp.sum(jnp.concatenate([l_top, l_bot], axis=0), axis=-1, keepdims=True)
        v0 = n * blk
        acc = jnp.concatenate([
            _bdot(jnp.concatenate(p_top, axis=1).astype(BF16), v_ref[hd, v0:v0 + half, :]),
            _bdot(jnp.concatenate(p_bot, axis=1).astype(BF16), v_ref[hd, v0:v0 + blk, :])], axis=0)
        if n > 0:
            acc = acc + _bdot(jnp.concatenate(ps, axis=1).astype(BF16), v_ref[hd, 0:v0, :])
        o_ref[hd, rows(n), :] = (acc / l).astype(BF16)

    mb = [pass1(hd, 0) for hd in range(ATTN_HEADS)]
    for n in range(nq):
        for hd in range(ATTN_HEADS):
            mb_next = pass1(hd, n + 1) if n + 1 < nq else None
            pass2(hd, n, mb[hd])
            mb[hd] = mb_next


def _attention(q, kn, kr, v):
    head_spec = lambda w: pl.BlockSpec((None, ATTN_HEADS, SEQ, w), lambda b, h: (b, h, 0, 0))
    kr_spec = pl.BlockSpec((None, SEQ, HEAD_W - QK_NOPE), lambda b, h: (b, 0, 0))
    return pl.pallas_call(
        _attn_kernel,
        out_shape=jax.ShapeDtypeStruct((NB, HEADS, SEQ, V_DIM), BF16),
        grid=(NB, HEADS // ATTN_HEADS),
        in_specs=[head_spec(HEAD_W), head_spec(QK_NOPE), kr_spec, head_spec(V_DIM)],
        out_specs=head_spec(V_DIM),
        scratch_shapes=[pltpu.VMEM((ATTN_HEADS * _ATTN_SLOTS, ATTN_BLK, ATTN_BLK), F32)],
        compiler_params=_cparams("arbitrary", "arbitrary"),
        name="mla_attention",
    )(q, kn, kr, v)


def _vec_row(vec_ref, row):
    return vec_ref[row:row + 1, :]


def _ffn_rows(x, shift, scale, gate, w1_ref, w2_ref, vec_ref, layer):
    h = (x * (1.0 + scale) + shift).astype(BF16)
    acc = jnp.zeros(x.shape, F32)
    for c in range(D_FF // FF_CHUNK):
        cols = slice(c * FF_CHUNK, (c + 1) * FF_CHUNK)
        b1 = _vec_row(vec_ref, VEC_MLP_B1 + layer * (D_FF // FF_CHUNK) + c)
        a = jnp.maximum(_bdot(h, w1_ref[:, cols]) + b1, 0.0)
        acc = acc + _bdot((a * a).astype(BF16), w2_ref[cols, :])
    y = acc + _vec_row(vec_ref, VEC_MLP_B2 + layer)
    return _layer_norm(ALPHA * x + (1.0 + gate) * y,
                       _vec_row(vec_ref, VEC_LN_FFN_G + layer), _vec_row(vec_ref, VEC_LN_FFN_B + layer))


def _batch_mod(m_ref):
    b = pl.program_id(0)
    return [m_ref[i, pl.ds(b, 1), :] for i in (SHIFT, SCALE, GATE)]


def _ffn_kernel(x_ref, m_ref, w1_ref, w2_ref, vec_ref, y_ref, *, layer):
    mod = _batch_mod(m_ref)
    for sb in range(FFN_SUB):
        rows = slice(sb * FFN_ROWS, (sb + 1) * FFN_ROWS)
        y_ref[rows, :] = _ffn_rows(x_ref[rows, :], *mod, w1_ref, w2_ref, vec_ref, layer)


def _attn_out_ffn_kernel(o_ref, x_ref, mm_ref, mf_ref, wo_ref, w1_ref, w2_ref, vec_ref, sa_ref, sb_ref, sc_ref,
                         y_ref, sab_ref, sbb_ref, scb_ref, *, layer):
    _cast_slices((sa_ref, sb_ref, sc_ref), (sab_ref, sbb_ref, scb_ref))
    gm_ref = vec_ref.at[VEC_LN_MIX_G + layer:VEC_LN_MIX_G + layer + 1]
    bm_ref = vec_ref.at[VEC_LN_MIX_B + layer:VEC_LN_MIX_B + layer + 1]
    gate_mix = _batch_mod(mm_ref)[GATE]
    mod = _batch_mod(mf_ref)
    for sb in range(FFN_SUB):
        rows = slice(sb * FFN_ROWS, (sb + 1) * FFN_ROWS)
        o = jnp.concatenate([o_ref[hd, rows, :] for hd in range(HEADS)], axis=1)
        y_ref[rows, :] = _layer_norm(ALPHA * x_ref[rows, :] + (1.0 + gate_mix) * _bdot(o, wo_ref[...]),
                                     gm_ref[...], bm_ref[...])
    for sb in range(FFN_SUB):
        rows = slice(sb * FFN_ROWS, (sb + 1) * FFN_ROWS)
        y_ref[rows, :] = _ffn_rows(y_ref[rows, :], *mod, w1_ref, w2_ref, vec_ref, layer)


_FFN_X_SPEC = pl.BlockSpec((None, FFN_SUB * FFN_ROWS, D), lambda b, i: (b, i, 0))
_FFN_GRID = (NB, SEQ // (FFN_SUB * FFN_ROWS))


def _layer_spec(shape, layer):
    return pl.BlockSpec((None,) + tuple(shape[1:]), lambda *_: (layer, 0, 0), pipeline_mode=pl.Buffered(1))


def _ffn(x, m, layer, w1, w2, vec):
    return pl.pallas_call(
        functools.partial(_ffn_kernel, layer=layer),
        out_shape=jax.ShapeDtypeStruct((NB, SEQ, D), F32),
        grid=_FFN_GRID,
        in_specs=[_FFN_X_SPEC, _mod_spec(layer), _layer_spec(w1.shape, 0), _layer_spec(w2.shape, 0),
                  _const_spec(vec.shape)],
        out_specs=_FFN_X_SPEC,
        compiler_params=_cparams("arbitrary", "arbitrary"),
        name="ffn_norm",
    )(x, m, w1, w2, vec)


def _attn_out_ffn(o, x, m_mix, m_ffn, layer, w_o, w1, w2, vec, casts):
    o_spec = pl.BlockSpec((None, HEADS, FFN_SUB * FFN_ROWS, V_DIM), lambda b, i: (b, 0, i, 0))
    cast = [_cast_specs(w.shape, _FFN_GRID, l) for w, l in casts]
    return pl.pallas_call(
        functools.partial(_attn_out_ffn_kernel, layer=layer),
        out_shape=[jax.ShapeDtypeStruct((NB, SEQ, D), F32)] + [c[2] for c in cast],
        grid=_FFN_GRID,
        in_specs=[o_spec, _FFN_X_SPEC, _mod_spec(layer), _mod_spec(layer), _layer_spec(w_o.shape, 0),
                  _layer_spec(w1.shape, 0), _layer_spec(w2.shape, 0), _const_spec(vec.shape)]
                 + [c[0] for c in cast],
        out_specs=[_FFN_X_SPEC] + [c[1] for c in cast],
        compiler_params=_cparams("arbitrary", "arbitrary"),
        name="attn_out_ffn_norm",
    )(o, x, m_mix, m_ffn, w_o, w1, w2, vec, *[w for w, _ in casts])


def _cmul(ar, ai, br, bi):
    return ar * br - ai * bi, ar * bi + ai * br


def _pair_slot(piece, zero, slot):
    return [piece, zero] if slot == 0 else [zero, piece]


def _s5_disc_kernel(ldt_ref, are_ref, aim_ref, bre_ref, bim_ref, cre_ref, cim_ref, d_ref,
                    a8r_ref, a8i_ref, wb_ref, wck_ref):
    lr = are_ref[...]
    li = aim_ref[...]
    dt = jnp.exp(ldt_ref[...])
    mag = jnp.exp(lr * dt)
    ab_re = mag * jnp.cos(li * dt)
    ab_im = mag * jnp.sin(li * dt)
    den = lr * lr + li * li
    nr = ab_re - 1.0
    coef_re = (nr * lr + ab_im * li) / den
    coef_im = (ab_im * lr - nr * li) / den
    bb_re, bb_im = _cmul(coef_re, coef_im, bre_ref[...], bim_ref[...])
    c_re = cre_ref[...]
    c_im = cim_ref[...]

    pw = [(jnp.ones_like(ab_re), jnp.zeros_like(ab_im))]
    for _ in range(CHUNK):
        pw.append(_cmul(pw[-1][0], pw[-1][1], ab_re, ab_im))
    a8r_ref[...] = pw[CHUNK][0]
    a8i_ref[...] = pw[CHUNK][1]

    n_pairs = wb_ref.shape[0]

    def pairs(v):
        v4 = v.reshape(n_pairs, 2, v.shape[1], v.shape[2])
        return v4[:, 0], v4[:, 1]

    z64 = jnp.zeros((n_pairs, GROUP_CH, STATE), F32)
    z16 = jnp.zeros((n_pairs, GROUP_CH, GROUP_CH), F32)
    eye = (lax.broadcasted_iota(jnp.int32, (GROUP_CH, GROUP_CH), 0)
           == lax.broadcasted_iota(jnp.int32, (GROUP_CH, GROUP_CH), 1)).astype(F32)

    ab_l = [_cmul(pw[l][0], pw[l][1], bb_re, bb_im) for l in range(CHUNK)]
    wb_rows = []
    for k in range(CHUNK):
        m_re, m_im = ab_l[CHUNK - 1 - k]
        for slot in range(2):
            wb_rows.append(jnp.concatenate(_pair_slot(pairs(m_re)[slot], z64, slot)
                                           + _pair_slot(pairs(m_im)[slot], z64, slot), axis=2))
    wb_ref[...] = jnp.concatenate(wb_rows, axis=1).astype(BF16)

    def nt(a, b):
        return lax.dot_general(a.astype(BF16), b.astype(BF16), (((2,), (2,)), ((0,), (0,))),
                               preferred_element_type=F32)

    k_l = [nt(c_re, ab_l[l][0]) - nt(c_im, ab_l[l][1]) for l in range(CHUNK)]
    k_l[0] = k_l[0] + d_ref[...] * eye[None]

    wck_rows = []
    for kp in range(CHUNK):
        ca_re, ca_im = _cmul(c_re, c_im, pw[kp + 1][0], pw[kp + 1][1])
        for slot in range(2):
            state_cols = (_pair_slot(pairs(ca_re)[slot], z64, slot)
                          + _pair_slot(-pairs(ca_im)[slot], z64, slot))
            direct_cols = []
            for k in range(CHUNK):
                blk = pairs(k_l[kp - k])[slot] if k <= kp else z16
                direct_cols += _pair_slot(blk, z16, slot)
            wck_rows.append(jnp.concatenate(state_cols + direct_cols, axis=2))
    wck_ref[...] = jnp.concatenate(wck_rows, axis=1).astype(BF16)


def _s5_discretise(log_dt, a_re, a_im, b_re, b_im, c_re, c_im, d_skip):
    g3 = (N_GROUPS, 1, STATE)
    gb = DISC_GROUPS
    spec = lambda *tail: pl.BlockSpec((gb,) + tail, lambda i: (i,) + (0,) * len(tail))
    pair_spec = lambda cols: pl.BlockSpec((gb // 2, PAIR_W, cols), lambda i: (i, 0, 0))
    big = spec(GROUP_CH, STATE)
    return pl.pallas_call(
        _s5_disc_kernel,
        out_shape=(jax.ShapeDtypeStruct(g3, F32), jax.ShapeDtypeStruct(g3, F32),
                   jax.ShapeDtypeStruct((N_PAIRS, PAIR_W, PAIR_W), BF16),
                   jax.ShapeDtypeStruct((N_PAIRS, PAIR_W, 2 * PAIR_W), BF16)),
        grid=(N_GROUPS // gb,),
        in_specs=[spec(1, 1), spec(1, STATE), spec(1, STATE), big, big, big, big, spec(GROUP_CH, 1)],
        out_specs=(spec(1, STATE), spec(1, STATE), pair_spec(PAIR_W), pair_spec(2 * PAIR_W)),
        compiler_params=_cparams("arbitrary"),
        name="s5_discretise",
    )(log_dt.reshape(N_GROUPS, 1, 1), a_re.reshape(g3), a_im.reshape(g3),
      jnp.swapaxes(b_re, 1, 2), jnp.swapaxes(b_im, 1, 2), c_re, c_im, d_skip.reshape(N_GROUPS, GROUP_CH, 1))


S5_X_SPEC = pl.BlockSpec((NB, S5_BLOCK, D), lambda i: (0, i, 0))
S5_U8_SPEC = pl.BlockSpec((S5_BLOCK // CHUNK * NB, N_PAIRS * PAIR_W), lambda i: (i, 0))


def _s5_in_kernel(x_ref, m_ref, w_in_ref, u8_ref):
    shift = m_ref[SHIFT]
    scale = m_ref[SCALE]
    steps = S5_IN_STEPS
    chunks = steps // CHUNK
    for sb in range(S5_BLOCK // steps):
        x3 = jnp.swapaxes(x_ref[:, sb * steps:(sb + 1) * steps, :], 0, 1)
        h = (x3 * (1.0 + scale)[None] + shift[None]).reshape(steps * NB, D).astype(BF16)
        u = _bdot(h, w_in_ref[...])
        u4 = u.reshape(chunks, CHUNK, NB, D)
        per_k = [u4[:, k].reshape(chunks * NB, D) for k in range(CHUNK)]
        cols = [per_k[k][:, j * PAIR_CH:(j + 1) * PAIR_CH] for j in range(N_PAIRS) for k in range(CHUNK)]
        u8_ref[sb * chunks * NB:(sb + 1) * chunks * NB, :] = jnp.concatenate(cols, axis=1).astype(BF16)


def _s5_in(x, m, w_in):
    return pl.pallas_call(
        _s5_in_kernel,
        out_shape=jax.ShapeDtypeStruct((SEQ // CHUNK * NB, N_PAIRS * PAIR_W), BF16),
        grid=(SEQ // S5_BLOCK,),
        in_specs=[
            S5_X_SPEC,
            _mod_spec(1),
            _layer_spec(w_in.shape, 0),
        ],
        out_specs=S5_U8_SPEC,
        compiler_params=_cparams("arbitrary"),
        name="s5_in_proj",
    )(x, m, w_in)


def _s5_core_kernel(u8_ref, wb_ref, wck_ref, a8r_ref, a8i_ref, y8_ref):
    half = PAIR_W // 2
    for jj in range(PAIRS_PER_STEP):
        lanes = slice(jj * PAIR_W, (jj + 1) * PAIR_W)
        u8 = u8_ref[:, lanes]
        v = _bdot(u8, wb_ref[jj])
        pair = lambda ref: jnp.concatenate([ref[2 * jj], ref[2 * jj + 1]], axis=1)
        ar = jnp.broadcast_to(pair(a8r_ref), (NB, half))
        ai = jnp.broadcast_to(pair(a8i_ref), (NB, half))
        xr = jnp.zeros((NB, half), F32)
        xi = jnp.zeros((NB, half), F32)
        prev_r, prev_i = [], []
        for s in range(SEQ // CHUNK):
            prev_r.append(xr)
            prev_i.append(xi)
            rows = slice(s * NB, (s + 1) * NB)
            xr, xi = ar * xr - ai * xi + v[rows, :half], ar * xi + ai * xr + v[rows, half:]
        x_prev = jnp.concatenate([jnp.concatenate(prev_r, axis=0), jnp.concatenate(prev_i, axis=0)], axis=1)
        lhs = jnp.concatenate([x_prev.astype(BF16), u8], axis=1)
        y8_ref[:, lanes] = lax.dot_general(lhs, wck_ref[jj], (((1,), (1,)), ((), ())), preferred_element_type=F32)


def _s5_core(u8, wb, wck, a8r, a8i):
    n_rows = u8.shape[0]
    w = PAIRS_PER_STEP * PAIR_W
    return pl.pallas_call(
        _s5_core_kernel,
        out_shape=jax.ShapeDtypeStruct(u8.shape, F32),
        grid=(N_PAIRS // PAIRS_PER_STEP,),
        in_specs=[
            pl.BlockSpec((n_rows, w), lambda i: (0, i)),
            pl.BlockSpec((PAIRS_PER_STEP, PAIR_W, PAIR_W), lambda i: (i, 0, 0)),
            pl.BlockSpec((PAIRS_PER_STEP, PAIR_W, 2 * PAIR_W), lambda i: (i, 0, 0)),
            pl.BlockSpec((2 * PAIRS_PER_STEP, 1, STATE), lambda i: (i, 0, 0)),
            pl.BlockSpec((2 * PAIRS_PER_STEP, 1, STATE), lambda i: (i, 0, 0)),
        ],
        out_specs=pl.BlockSpec((n_rows, w), lambda i: (0, i)),
        compiler_params=_cparams("arbitrary"),
        name="s5_recurrence",
    )(u8, wb, wck, a8r, a8i)


def _s5_out_kernel(y8_ref, x_ref, m_ref, w_glu_ref, w_out_ref, vec_ref, ca_ref, cb_ref, o_ref, cab_ref, cbb_ref):
    _cast_slices((ca_ref, cb_ref), (cab_ref, cbb_ref))
    b_glu_ref = vec_ref.at[VEC_B_GLU:VEC_B_GLU + 1]
    g_ref = vec_ref.at[VEC_LN_MIX_G + 1:VEC_LN_MIX_G + 2]
    b_ref = vec_ref.at[VEC_LN_MIX_B + 1:VEC_LN_MIX_B + 2]
    gate = m_ref[GATE]
    steps = S5_OUT_STEPS
    chunks = steps // CHUNK
    for sb in range(S5_BLOCK // steps):
        y8 = y8_ref[sb * chunks * NB:(sb + 1) * chunks * NB, :]
        per_k = [jnp.concatenate([y8[:, j * PAIR_W + k * PAIR_CH:j * PAIR_W + (k + 1) * PAIR_CH]
                                  for j in range(N_PAIRS)], axis=1) for k in range(CHUNK)]
        y = jnp.stack([p.reshape(chunks, NB, D) for p in per_k], axis=1).reshape(steps * NB, D)

        half_y = 0.5 * y
        g = half_y + half_y * jnp.tanh(y * (GELU_C + (GELU_C * 0.044715) * (y * y)))
        zz = g * jax.nn.sigmoid(_bdot(g.astype(BF16), w_glu_ref[...]) + b_glu_ref[...])
        out = jnp.swapaxes(_bdot(zz.astype(BF16), w_out_ref[...]).reshape(steps, NB, D), 0, 1)
        t_rows = slice(sb * steps, (sb + 1) * steps)
        res = ALPHA * x_ref[:, t_rows, :] + (1.0 + gate)[:, None, :] * out
        o_ref[:, t_rows, :] = _layer_norm(res, g_ref[...], b_ref[...])


def _s5_out(y8, x, m, w_glu, w_out, vec, casts):
    x_spec = S5_X_SPEC
    grid = (SEQ // S5_BLOCK,)
    cast = [_cast_specs(w.shape, grid, l) for w, l in casts]
    return pl.pallas_call(
        _s5_out_kernel,
        out_shape=[jax.ShapeDtypeStruct((NB, SEQ, D), F32)] + [c[2] for c in cast],
        grid=grid,
        in_specs=[
            S5_U8_SPEC,
            x_spec,
            _mod_spec(1),
            _layer_spec(w_glu.shape, 0),
            _layer_spec(w_out.shape, 0),
            _const_spec(vec.shape),
        ] + [c[0] for c in cast],
        out_specs=[x_spec] + [c[1] for c in cast],
        compiler_params=_cparams("arbitrary"),
        name="s5_out_norm",
    )(y8, x, m, w_glu, w_out, vec, *[w for w, _ in casts])


def _rotate_half_cols(w):
    half = w.shape[-1] // 2
    return jnp.concatenate([-w[..., half:], w[..., :half]], axis=-1)


def _mla_weights(w_in, w_qb):
    k_pe = w_in[:, Q_LORA + KV_LORA:]
    w_in_ext = jnp.concatenate([w_in, _rotate_half_cols(k_pe)], axis=1)
    wq = w_qb.reshape(Q_LORA, HEADS, QK_NOPE + QK_ROPE) * Q_SCALE
    wq_ext = jnp.concatenate([wq, _rotate_half_cols(wq[..., QK_NOPE:])], axis=-1)
    return w_in_ext.astype(BF16), wq_ext.reshape(Q_LORA, HEADS * HEAD_W).astype(BF16)


def kernel(x, c, positions, mla_w_in, mla_q_norm, mla_w_qb, mla_kv_norm, mla_w_kvb, mla_w_o, ssm_w_in, ssm_log_dt, ssm_a_re, ssm_a_im, ssm_b_re, ssm_b_im, ssm_c_re, ssm_c_im, ssm_d, ssm_w_glu, ssm_b_glu, ssm_w_out, mlp_w1, mlp_b1, mlp_w2, mlp_b2, mod_mix_w, mod_mix_b, mod_ffn_w, mod_ffn_b, ln_mix_g, ln_mix_b, ln_ffn_g, ln_ffn_b):
    vec = _pack_vectors(ln_mix_g, ln_mix_b, ln_ffn_g, ln_ffn_b, mlp_b2, ssm_b_glu, mlp_b1, mod_mix_b, mod_ffn_b,
                        mla_q_norm, mla_kv_norm)
    m_mix = _modulation(c, mod_mix_w, vec, VEC_MOD_MIX_B)
    m_ffn = _modulation(c, mod_ffn_w, vec, VEC_MOD_FFN_B)

    inv_freq = ROPE_THETA ** (-jnp.arange(0, QK_ROPE, 2, dtype=F32) / QK_ROPE)
    freq = jnp.tile(inv_freq, 4).reshape(1, 128)
    w_in_ext, w_qb_ext = _mla_weights(mla_w_in[0], mla_w_qb[0])
    q, kn, kr, v, w1_0, w2_0, w_o = _mla_proj(
        x, m_mix, positions.reshape(NB, SEQ // 128, 128), freq, w_in_ext, vec, w_qb_ext, mla_w_kvb[0].astype(BF16),
        ((mlp_w1, 0), (mlp_w2, 0), (mla_w_o, 0)))
    o = _attention(q, kn, kr, v)
    x2, s_w_in, s_w_glu, s_w_out = _attn_out_ffn(
        o, x, m_mix, m_ffn, 0, w_o, w1_0, w2_0, vec, ((ssm_w_in, 0), (ssm_w_glu, 0), (ssm_w_out, 0)))

    a8r, a8i, wb, wck = _s5_discretise(ssm_log_dt[0], ssm_a_re[0], ssm_a_im[0], ssm_b_re[0], ssm_b_im[0],
                                       ssm_c_re[0], ssm_c_im[0], ssm_d[0])
    u8 = _s5_in(x2, m_mix, s_w_in)
    y8 = _s5_core(u8, wb, wck, a8r, a8i)
    x3, w1_1, w2_1 = _s5_out(y8, x2, m_mix, s_w_glu, s_w_out, vec, ((mlp_w1, 1), (mlp_w2, 1)))
    return _ffn(x3, m_ffn, 1, w1_1, w2_1, vec)
```

```python
import functools
import math

import jax
import jax.numpy as jnp
from jax import lax
from jax.experimental import pallas as pl
from jax.experimental.pallas import tpu as pltpu

F32 = jnp.float32
BF16 = jnp.bfloat16

D = 1024
NB = 8
SEQ = 2048
HEADS = 8
QK_NOPE = 128
QK_ROPE = 64
V_DIM = 128
Q_LORA = 256
KV_LORA = 128
ROPE_THETA = 10000.0
GROUP_CH = 16
N_GROUPS = 64
STATE = 64
D_FF = 4 * D
DEPTH = 2
ALPHA = (2 * DEPTH) ** 0.25
LN_EPS = 1e-5
RMS_EPS = 1e-6
Q_SCALE = math.log2(math.e) / math.sqrt(QK_NOPE + QK_ROPE)
NEG_BIG = -1e30
GELU_C = math.sqrt(2.0 / math.pi)

HEAD_W = 256
CHUNK = 8
N_PAIRS = N_GROUPS // 2
PAIR_CH = 2 * GROUP_CH
PAIR_W = CHUNK * PAIR_CH
assert PAIR_W == 4 * STATE

VMEM_LIMIT = 56 * 1024 * 1024
PROJ_ROWS = 1024
PROJ_SUB = 1
ATTN_BLK = 512
ATTN_HEADS = 4
_ATTN_SLOTS = 2 * (SEQ // ATTN_BLK) - 1
FFN_ROWS = 512
FFN_SUB = 2
FF_CHUNK = 1024
S5_IN_BLOCK, S5_IN_STEPS = 256, 64
S5_OUT_BLOCK, S5_OUT_STEPS = 128, 32
PAIRS_PER_STEP = 4
DISC_GROUPS = 16


VEC_LN_MIX_G, VEC_LN_MIX_B, VEC_LN_FFN_G, VEC_LN_FFN_B, VEC_MLP_B2 = 0, DEPTH, 2 * DEPTH, 3 * DEPTH, 4 * DEPTH
VEC_B_GLU = 5 * DEPTH
VEC_MLP_B1 = VEC_B_GLU + 1
VEC_MOD_B = VEC_MLP_B1 + DEPTH * (D_FF // D)
VEC_Q_NORM = VEC_MOD_B + 2 * 3 * DEPTH
VEC_KV_NORM = VEC_Q_NORM + 1
VEC_ROWS = -(-(VEC_KV_NORM + 1) // 8) * 8
assert FF_CHUNK == D


def _pack_vectors(ln_mix_g, ln_mix_b, ln_ffn_g, ln_ffn_b, mlp_b2, b_glu, mlp_b1, mod_mix_b, mod_ffn_b, q_norm, kv_norm):
    pad = lambda v: jnp.pad(v.reshape(1, -1), ((0, 0), (0, D - v.size)))
    rows = [ln_mix_g, ln_mix_b, ln_ffn_g, ln_ffn_b, mlp_b2, b_glu.reshape(1, D), mlp_b1.reshape(-1, D),
            mod_mix_b.reshape(-1, D), mod_ffn_b.reshape(-1, D), pad(q_norm), pad(kv_norm)]
    table = jnp.concatenate(rows, axis=0)
    return jnp.pad(table, ((0, VEC_ROWS - table.shape[0]), (0, 0)))


def _cparams(*sem):
    return pltpu.CompilerParams(dimension_semantics=sem, vmem_limit_bytes=VMEM_LIMIT)


def _const_spec(shape):
    nd = len(shape)
    return pl.BlockSpec(shape, lambda *_: (0,) * nd, pipeline_mode=pl.Buffered(1))


def _cast_specs(shape, grid, layer):
    _, rows, cols = shape
    n_steps = math.prod(grid)

    def step(*ids):
        s = 0
        for i, n in zip(ids, grid):
            s = s * n + i
        return s

    block = (1, rows // n_steps, cols)
    return (pl.BlockSpec(block, lambda *ids: (layer, step(*ids), 0)),
            pl.BlockSpec(block, lambda *ids: (0, step(*ids), 0)),
            jax.ShapeDtypeStruct((1, rows, cols), BF16))


def _cast_slices(src_refs, dst_refs):
    for src, dst in zip(src_refs, dst_refs):
        dst[...] = src[...].astype(dst.dtype)


def _layer_norm(v, g, b):
    mu = jnp.mean(v, axis=-1, keepdims=True)
    vc = v - mu
    var = jnp.mean(vc * vc, axis=-1, keepdims=True)
    return vc * lax.rsqrt(var + LN_EPS) * g + b


def _bdot(a, b):
    return jnp.dot(a, b, preferred_element_type=F32)


MIX, FFN = 0, 1
SHIFT, SCALE, GATE = 0, 1, 2


def _mod_kernel(c_ref, w_mix_ref, w_ffn_ref, vec_ref, o_ref):
    kind, layer, part = pl.program_id(0), pl.program_id(1), pl.program_id(2)
    c = c_ref[...]
    cs = (c * jax.nn.sigmoid(c)).astype(BF16)
    bias = vec_ref[pl.ds(VEC_MOD_B + 3 * (DEPTH * kind + layer) + part, 1), :]
    for k, w_ref in ((MIX, w_mix_ref), (FFN, w_ffn_ref)):
        @pl.when(kind == k)
        def _():
            o_ref[...] = _bdot(cs, w_ref[...].astype(BF16)) + bias


def _modulation(c, w_mix, w_ffn, vec):
    def w_spec(k):
        idle = ((1 - k) * (DEPTH - 1), 2 * (1 - k))
        return pl.BlockSpec((None, D, D), lambda kind, l, j: (jnp.where(kind == k, l, idle[0]), 0,
                                                            jnp.where(kind == k, j, idle[1])))
    return pl.pallas_call(
        _mod_kernel,
        out_shape=jax.ShapeDtypeStruct((2, DEPTH, 3, NB, D), F32),
        grid=(2, DEPTH, 3),
        in_specs=[
            pl.BlockSpec((NB, D), lambda kind, l, j: (0, 0)),
            w_spec(MIX),
            w_spec(FFN),
            _const_spec(vec.shape),
        ],
        out_specs=pl.BlockSpec((None, None, None, NB, D), lambda kind, l, j: (kind, l, j, 0, 0)),
        compiler_params=_cparams("arbitrary", "arbitrary", "arbitrary"),
        name="adaln_modulation",
    )(c, w_mix, w_ffn, vec)


def _mod_spec(kind, layer):
    return pl.BlockSpec((None, None, 3, NB, D), lambda *_: (kind, layer, 0, 0, 0), pipeline_mode=pl.Buffered(1))


def _mla_proj_kernel(x_ref, m_ref, pos_ref, freq_ref, w_in_ref, vec_ref, w_qb_ref, w_kvb_ref,
                     ca_ref, cb_ref, cc_ref, q_ref, kn_ref, kr_ref, v_ref, cab_ref, cbb_ref, ccb_ref):
    _cast_slices((ca_ref, cb_ref, cc_ref), (cab_ref, cbb_ref, ccb_ref))
    qn_ref = vec_ref.at[VEC_Q_NORM:VEC_Q_NORM + 1, 0:Q_LORA]
    kvn_ref = vec_ref.at[VEC_KV_NORM:VEC_KV_NORM + 1, 0:KV_LORA]
    b = pl.program_id(0)
    shift = m_ref[SHIFT, pl.ds(b, 1), :]
    scale = m_ref[SCALE, pl.ds(b, 1), :]

    def rms(v, g):
        return v * lax.rsqrt(jnp.mean(v * v, axis=-1, keepdims=True) + RMS_EPS) * g

    eye = lax.broadcasted_iota(jnp.int32, (128, 128), 0) == lax.broadcasted_iota(jnp.int32, (128, 128), 1)
    for sb in range(PROJ_SUB):
        rows = slice(sb * PROJ_ROWS, (sb + 1) * PROJ_ROWS)
        h = (x_ref[0, rows, :] * (1.0 + scale) + shift).astype(BF16)
        z = _bdot(h, w_in_ref[...])
        cq = rms(z[:, :Q_LORA], qn_ref[...]).astype(BF16)
        ckv = rms(z[:, Q_LORA:Q_LORA + KV_LORA], kvn_ref[...]).astype(BF16)

        pos = pos_ref[0, sb * (PROJ_ROWS // 128):(sb + 1) * (PROJ_ROWS // 128), :].astype(F32)
        pos_col = jnp.concatenate(
            [jnp.sum(jnp.where(eye, jnp.broadcast_to(pos[i:i + 1, :], (128, 128)), 0.0), axis=1, keepdims=True)
             for i in range(PROJ_ROWS // 128)], axis=0)
        ang = pos_col * freq_ref[...]
        lane = lax.broadcasted_iota(jnp.int32, ang.shape, 1)
        mult = jnp.where(lane < QK_ROPE, jnp.cos(ang), jnp.sin(ang))

        def rope(slab):
            s = slab * mult
            return s + pltpu.roll(s, QK_ROPE, axis=1)

        kr_ref[0, rows, :] = jnp.where(lane < QK_ROPE, rope(z[:, Q_LORA + KV_LORA:]), 0.0).astype(BF16)

        q_all = _bdot(cq, w_qb_ref[...])
        kv = _bdot(ckv, w_kvb_ref[...])
        for hd in range(HEADS):
            c0 = hd * HEAD_W
            q_ref[0, hd, rows, 0:QK_NOPE] = q_all[:, c0:c0 + QK_NOPE].astype(BF16)
            q_ref[0, hd, rows, QK_NOPE:HEAD_W] = rope(q_all[:, c0 + QK_NOPE:c0 + HEAD_W]).astype(BF16)
            kn_ref[0, hd, rows, :] = kv[:, c0:c0 + QK_NOPE].astype(BF16)
            v_ref[0, hd, rows, :] = kv[:, c0 + QK_NOPE:c0 + HEAD_W].astype(BF16)


def _mla_proj(x, m, pos, freq, w_in, vec, w_qb, w_kvb, casts):
    nt = SEQ // (PROJ_SUB * PROJ_ROWS)
    blk = PROJ_SUB * PROJ_ROWS
    grid = (NB, nt)
    cast = [_cast_specs(w.shape, grid, layer) for w, layer in casts]
    return pl.pallas_call(
        _mla_proj_kernel,
        out_shape=[
            jax.ShapeDtypeStruct((NB, HEADS, SEQ, HEAD_W), BF16),
            jax.ShapeDtypeStruct((NB, HEADS, SEQ, QK_NOPE), BF16),
            jax.ShapeDtypeStruct((NB, SEQ, HEAD_W - QK_NOPE), BF16),
            jax.ShapeDtypeStruct((NB, HEADS, SEQ, V_DIM), BF16),
        ] + [c[2] for c in cast],
        grid=grid,
        in_specs=[
            pl.BlockSpec((1, blk, D), lambda b, i: (b, i, 0)),
            _mod_spec(MIX, 0),
            pl.BlockSpec((1, blk // 128, 128), lambda b, i: (b, i, 0)),
            _const_spec((1, 128)),
            _const_spec(w_in.shape),
            _const_spec(vec.shape),
            _const_spec(w_qb.shape),
            _const_spec(w_kvb.shape),
        ] + [c[0] for c in cast],
        out_specs=[
            pl.BlockSpec((1, HEADS, blk, HEAD_W), lambda b, i: (b, 0, i, 0)),
            pl.BlockSpec((1, HEADS, blk, QK_NOPE), lambda b, i: (b, 0, i, 0)),
            pl.BlockSpec((1, blk, HEAD_W - QK_NOPE), lambda b, i: (b, i, 0)),
            pl.BlockSpec((1, HEADS, blk, V_DIM), lambda b, i: (b, 0, i, 0)),
        ] + [c[1] for c in cast],
        compiler_params=_cparams("arbitrary", "arbitrary"),
        name="mla_projections",
    )(x, m, pos, freq, w_in, vec, w_qb, w_kvb, *[w for w, _ in casts])


def _lane_groups(v):
    return [v[:, c:c + 128] for c in range(0, v.shape[1], 128)]


def _attn_kernel(q_ref, kn_ref, kr_ref, v_ref, o_ref, s_ref):
    blk = ATTN_BLK
    half = blk // 2
    nq = SEQ // blk
    tri = lax.broadcasted_iota(jnp.int32, (half, half), 1) <= lax.broadcasted_iota(jnp.int32, (half, half), 0)

    def rows(n):
        return slice(n * blk, (n + 1) * blk)

    def slot(hd, n, j):
        return hd * _ATTN_SLOTS + (0 if n % 2 == 0 else nq - 1) + j

    def nt_dot(a, b):
        return lax.dot_general(a, b, (((1,), (1,)), ((), ())), preferred_element_type=F32)

    def keys(hd, j):
        return jnp.concatenate([kn_ref[hd, rows(j), :], kr_ref[rows(j), :]], axis=1)

    def lane_max(vals):
        return functools.reduce(jnp.maximum, vals)

    def pass1(hd, n):
        q = q_ref[hd, rows(n), :]
        m = []
        for j in range(n):
            s = nt_dot(q, keys(hd, j))
            s_ref[slot(hd, n, j)] = s
            m = [lane_max(_lane_groups(s) + m)]
        k = keys(hd, n)
        s_top = jnp.where(tri, nt_dot(q[:half], k[:half]), NEG_BIG)
        s_bot = nt_dot(q[half:], k)
        s_bot = jnp.concatenate([s_bot[:, :half], jnp.where(tri, s_bot[:, half:], NEG_BIG)], axis=1)
        s_ref[slot(hd, n, n), 0:half, 0:half] = s_top
        s_ref[slot(hd, n, n), half:blk, :] = s_bot
        m_top = lane_max(_lane_groups(s_top) + [v[:half] for v in m])
        m_bot = lane_max(_lane_groups(s_bot) + [v[half:] for v in m])
        m = jnp.concatenate([m_top, m_bot], axis=0)
        return jnp.broadcast_to(jnp.max(m, axis=-1, keepdims=True), (blk, 128))

    def pass2(hd, n, mb):
        diag = slot(hd, n, n)
        ps = [jnp.exp2(sg - mb) for j in range(n) for sg in _lane_groups(s_ref[slot(hd, n, j)])]
        p_top = [jnp.exp2(sg - mb[:half]) for sg in _lane_groups(s_ref[diag, 0:half, 0:half])]
        p_bot = [jnp.exp2(sg - mb[half:]) for sg in _lane_groups(s_ref[diag, half:blk, :])]
        l_top = functools.reduce(jnp.add, p_top + [p[:half] for p in ps])
        l_bot = functools.reduce(jnp.add, p_bot + [p[half:] for p in ps])
        l = jnp.sum(jnp.concatenate([l_top, l_bot], axis=0), axis=-1, keepdims=True)
        v0 = n * blk
        acc = jnp.concatenate([
            _bdot(jnp.concatenate(p_top, axis=1).astype(BF16), v_ref[hd, v0:v0 + half, :]),
            _bdot(jnp.concatenate(p_bot, axis=1).astype(BF16), v_ref[hd, v0:v0 + blk, :])], axis=0)
        if n > 0:
            acc = acc + _bdot(jnp.concatenate(ps, axis=1).astype(BF16), v_ref[hd, 0:v0, :])
        o_ref[hd, rows(n), :] = (acc / l).astype(BF16)

    mb = [pass1(hd, 0) for hd in range(ATTN_HEADS)]
    for n in range(nq):
        for hd in range(ATTN_HEADS):
            mb_next = pass1(hd, n + 1) if n + 1 < nq else None
            pass2(hd, n, mb[hd])
            mb[hd] = mb_next


def _attention(q, kn, kr, v):
    head_spec = lambda w: pl.BlockSpec((None, ATTN_HEADS, SEQ, w), lambda b, h: (b, h, 0, 0))
    kr_spec = pl.BlockSpec((None, SEQ, HEAD_W - QK_NOPE), lambda b, h: (b, 0, 0))
    return pl.pallas_call(
        _attn_kernel,
        out_shape=jax.ShapeDtypeStruct((NB, HEADS, SEQ, V_DIM), BF16),
        grid=(NB, HEADS // ATTN_HEADS),
        in_specs=[head_spec(HEAD_W), head_spec(QK_NOPE), kr_spec, head_spec(V_DIM)],
        out_specs=head_spec(V_DIM),
        scratch_shapes=[pltpu.VMEM((ATTN_HEADS * _ATTN_SLOTS, ATTN_BLK, ATTN_BLK), F32)],
        compiler_params=_cparams("arbitrary", "arbitrary"),
        name="mla_attention",
    )(q, kn, kr, v)


def _vec_row(vec_ref, row):
    return vec_ref[row:row + 1, :]


def _ffn_rows(x, shift, scale, gate, w1_ref, w2_ref, vec_ref, layer):
    h = (x * (1.0 + scale) + shift).astype(BF16)
    a2 = []
    for c in range(D_FF // FF_CHUNK):
        cols = slice(c * FF_CHUNK, (c + 1) * FF_CHUNK)
        b1 = _vec_row(vec_ref, VEC_MLP_B1 + layer * (D_FF // FF_CHUNK) + c)
        a = jnp.maximum(_bdot(h, w1_ref[:, cols]) + b1, 0.0)
        a2.append((a * a).astype(BF16))
    y = _bdot(jnp.concatenate(a2, axis=1), w2_ref[...]) + _vec_row(vec_ref, VEC_MLP_B2 + layer)
    return _layer_norm(ALPHA * x + (1.0 + gate) * y,
                       _vec_row(vec_ref, VEC_LN_FFN_G + layer), _vec_row(vec_ref, VEC_LN_FFN_B + layer))


def _batch_mod(m_ref):
    b = pl.program_id(0)
    return [m_ref[i, pl.ds(b, 1), :] for i in (SHIFT, SCALE, GATE)]


def _ffn_kernel(x_ref, m_ref, w1_ref, w2_ref, vec_ref, y_ref, *, layer):
    mod = _batch_mod(m_ref)
    for sb in range(FFN_SUB):
        rows = slice(sb * FFN_ROWS, (sb + 1) * FFN_ROWS)
        y_ref[rows, :] = _ffn_rows(x_ref[rows, :], *mod, w1_ref, w2_ref, vec_ref, layer)


def _attn_out_ffn_kernel(o_ref, x_ref, mm_ref, mf_ref, wo_ref, w1_ref, w2_ref, vec_ref, sa_ref, sb_ref, sc_ref,
                         y_ref, sab_ref, sbb_ref, scb_ref, *, layer):
    _cast_slices((sa_ref, sb_ref, sc_ref), (sab_ref, sbb_ref, scb_ref))
    gm_ref = vec_ref.at[VEC_LN_MIX_G + layer:VEC_LN_MIX_G + layer + 1]
    bm_ref = vec_ref.at[VEC_LN_MIX_B + layer:VEC_LN_MIX_B + layer + 1]
    gate_mix = _batch_mod(mm_ref)[GATE]
    mod = _batch_mod(mf_ref)
    for sb in range(FFN_SUB):
        rows = slice(sb * FFN_ROWS, (sb + 1) * FFN_ROWS)
        o = jnp.concatenate([o_ref[hd, rows, :] for hd in range(HEADS)], axis=1)
        y_ref[rows, :] = _layer_norm(ALPHA * x_ref[rows, :] + (1.0 + gate_mix) * _bdot(o, wo_ref[...]),
                                     gm_ref[...], bm_ref[...])
    for sb in range(FFN_SUB):
        rows = slice(sb * FFN_ROWS, (sb + 1) * FFN_ROWS)
        y_ref[rows, :] = _ffn_rows(y_ref[rows, :], *mod, w1_ref, w2_ref, vec_ref, layer)


_FFN_X_SPEC = pl.BlockSpec((None, FFN_SUB * FFN_ROWS, D), lambda b, i: (b, i, 0))
_FFN_GRID = (NB, SEQ // (FFN_SUB * FFN_ROWS))


def _layer_spec(shape, layer):
    return pl.BlockSpec((None,) + tuple(shape[1:]), lambda *_: (layer, 0, 0), pipeline_mode=pl.Buffered(1))


def _ffn(x, m, layer, w1, w2, vec):
    return pl.pallas_call(
        functools.partial(_ffn_kernel, layer=layer),
        out_shape=jax.ShapeDtypeStruct((NB, SEQ, D), F32),
        grid=_FFN_GRID,
        in_specs=[_FFN_X_SPEC, _mod_spec(FFN, layer), _layer_spec(w1.shape, 0), _layer_spec(w2.shape, 0),
                  _const_spec(vec.shape)],
        out_specs=_FFN_X_SPEC,
        compiler_params=_cparams("arbitrary", "arbitrary"),
        name="ffn_norm",
    )(x, m, w1, w2, vec)


def _attn_out_ffn(o, x, m, layer, w_o, w1, w2, vec, casts):
    o_spec = pl.BlockSpec((None, HEADS, FFN_SUB * FFN_ROWS, V_DIM), lambda b, i: (b, 0, i, 0))
    cast = [_cast_specs(w.shape, _FFN_GRID, l) for w, l in casts]
    return pl.pallas_call(
        functools.partial(_attn_out_ffn_kernel, layer=layer),
        out_shape=[jax.ShapeDtypeStruct((NB, SEQ, D), F32)] + [c[2] for c in cast],
        grid=_FFN_GRID,
        in_specs=[o_spec, _FFN_X_SPEC, _mod_spec(MIX, layer), _mod_spec(FFN, layer), _layer_spec(w_o.shape, 0),
                  _layer_spec(w1.shape, 0), _layer_spec(w2.shape, 0), _const_spec(vec.shape)]
                 + [c[0] for c in cast],
        out_specs=[_FFN_X_SPEC] + [c[1] for c in cast],
        compiler_params=_cparams("arbitrary", "arbitrary"),
        name="attn_out_ffn_norm",
    )(o, x, m, m, w_o, w1, w2, vec, *[w for w, _ in casts])


def _cmul(ar, ai, br, bi):
    return ar * br - ai * bi, ar * bi + ai * br


def _pair_slot(piece, zero, slot):
    return [piece, zero] if slot == 0 else [zero, piece]


def _s5_disc_kernel(ldt_ref, are_ref, aim_ref, bre_ref, bim_ref, cre_ref, cim_ref, d_ref,
                    a8r_ref, a8i_ref, wb_ref, wck_ref):
    lr = are_ref[...]
    li = aim_ref[...]
    dt = jnp.exp(ldt_ref[...])
    mag = jnp.exp(lr * dt)
    ab_re = mag * jnp.cos(li * dt)
    ab_im = mag * jnp.sin(li * dt)
    den = lr * lr + li * li
    nr = ab_re - 1.0
    coef_re = (nr * lr + ab_im * li) / den
    coef_im = (ab_im * lr - nr * li) / den
    bb_re, bb_im = _cmul(coef_re, coef_im, bre_ref[...], bim_ref[...])
    c_re = cre_ref[...]
    c_im = cim_ref[...]

    pw = [(jnp.ones_like(ab_re), jnp.zeros_like(ab_im))]
    for _ in range(CHUNK):
        pw.append(_cmul(pw[-1][0], pw[-1][1], ab_re, ab_im))
    a8r_ref[...] = pw[CHUNK][0]
    a8i_ref[...] = pw[CHUNK][1]

    n_pairs = wb_ref.shape[0]

    def pairs(v):
        v4 = v.reshape(n_pairs, 2, v.shape[1], v.shape[2])
        return v4[:, 0], v4[:, 1]

    z64 = jnp.zeros((n_pairs, GROUP_CH, STATE), F32)
    z16 = jnp.zeros((n_pairs, GROUP_CH, GROUP_CH), F32)
    eye = (lax.broadcasted_iota(jnp.int32, (GROUP_CH, GROUP_CH), 0)
           == lax.broadcasted_iota(jnp.int32, (GROUP_CH, GROUP_CH), 1)).astype(F32)

    ab_l = [_cmul(pw[l][0], pw[l][1], bb_re, bb_im) for l in range(CHUNK)]
    wb_rows = []
    for k in range(CHUNK):
        m_re, m_im = ab_l[CHUNK - 1 - k]
        for slot in range(2):
            wb_rows.append(jnp.concatenate(_pair_slot(pairs(m_re)[slot], z64, slot)
                                           + _pair_slot(pairs(m_im)[slot], z64, slot), axis=2))
    wb_ref[...] = jnp.concatenate(wb_rows, axis=1).astype(BF16)

    def nt(a, b):
        return lax.dot_general(a.astype(BF16), b.astype(BF16), (((2,), (2,)), ((0,), (0,))),
                               preferred_element_type=F32)

    k_l = [nt(c_re, ab_l[l][0]) - nt(c_im, ab_l[l][1]) for l in range(CHUNK)]
    k_l[0] = k_l[0] + d_ref[...] * eye[None]

    wck_rows = []
    for kp in range(CHUNK):
        ca_re, ca_im = _cmul(c_re, c_im, pw[kp + 1][0], pw[kp + 1][1])
        for slot in range(2):
            state_cols = (_pair_slot(pairs(ca_re)[slot], z64, slot)
                          + _pair_slot(-pairs(ca_im)[slot], z64, slot))
            direct_cols = []
            for k in range(CHUNK):
                blk = pairs(k_l[kp - k])[slot] if k <= kp else z16
                direct_cols += _pair_slot(blk, z16, slot)
            wck_rows.append(jnp.concatenate(state_cols + direct_cols, axis=2))
    wck_ref[...] = jnp.concatenate(wck_rows, axis=1).astype(BF16)


def _s5_discretise(log_dt, a_re, a_im, b_re, b_im, c_re, c_im, d_skip):
    g3 = (N_GROUPS, 1, STATE)
    gb = DISC_GROUPS
    spec = lambda *tail: pl.BlockSpec((gb,) + tail, lambda i: (i,) + (0,) * len(tail))
    pair_spec = lambda cols: pl.BlockSpec((gb // 2, PAIR_W, cols), lambda i: (i, 0, 0))
    big = spec(GROUP_CH, STATE)
    return pl.pallas_call(
        _s5_disc_kernel,
        out_shape=(jax.ShapeDtypeStruct(g3, F32), jax.ShapeDtypeStruct(g3, F32),
                   jax.ShapeDtypeStruct((N_PAIRS, PAIR_W, PAIR_W), BF16),
                   jax.ShapeDtypeStruct((N_PAIRS, PAIR_W, 2 * PAIR_W), BF16)),
        grid=(N_GROUPS // gb,),
        in_specs=[spec(1, 1), spec(1, STATE), spec(1, STATE), big, big, big, big, spec(GROUP_CH, 1)],
        out_specs=(spec(1, STATE), spec(1, STATE), pair_spec(PAIR_W), pair_spec(2 * PAIR_W)),
        compiler_params=_cparams("arbitrary"),
        name="s5_discretise",
    )(log_dt.reshape(N_GROUPS, 1, 1), a_re.reshape(g3), a_im.reshape(g3),
      jnp.swapaxes(b_re, 1, 2), jnp.swapaxes(b_im, 1, 2), c_re, c_im, d_skip.reshape(N_GROUPS, GROUP_CH, 1))


def _s5_x_spec(block):
    return pl.BlockSpec((NB, block, D), lambda i: (0, i, 0))


def _s5_u8_spec(block):
    return pl.BlockSpec((block // CHUNK * NB, N_PAIRS * PAIR_W), lambda i: (i, 0))


def _s5_in_kernel(x_ref, m_ref, w_in_ref, u8_ref):
    shift = m_ref[SHIFT]
    scale = m_ref[SCALE]
    steps = S5_IN_STEPS
    chunks = steps // CHUNK
    for sb in range(S5_IN_BLOCK // steps):
        x3 = jnp.swapaxes(x_ref[:, sb * steps:(sb + 1) * steps, :], 0, 1)
        h = (x3 * (1.0 + scale)[None] + shift[None]).reshape(steps * NB, D).astype(BF16)
        u = _bdot(h, w_in_ref[...])
        u4 = u.reshape(chunks, CHUNK, NB, D)
        per_k = [u4[:, k].reshape(chunks * NB, D) for k in range(CHUNK)]
        cols = [per_k[k][:, j * PAIR_CH:(j + 1) * PAIR_CH] for j in range(N_PAIRS) for k in range(CHUNK)]
        u8_ref[sb * chunks * NB:(sb + 1) * chunks * NB, :] = jnp.concatenate(cols, axis=1).astype(BF16)


def _s5_in(x, m, w_in):
    return pl.pallas_call(
        _s5_in_kernel,
        out_shape=jax.ShapeDtypeStruct((SEQ // CHUNK * NB, N_PAIRS * PAIR_W), BF16),
        grid=(SEQ // S5_IN_BLOCK,),
        in_specs=[
            _s5_x_spec(S5_IN_BLOCK),
            _mod_spec(MIX, 1),
            _layer_spec(w_in.shape, 0),
        ],
        out_specs=_s5_u8_spec(S5_IN_BLOCK),
        compiler_params=_cparams("arbitrary"),
        name="s5_in_proj",
    )(x, m, w_in)


def _s5_core_kernel(u8_ref, wb_ref, wck_ref, a8r_ref, a8i_ref, y8_ref):
    half = PAIR_W // 2
    for jj in range(PAIRS_PER_STEP):
        lanes = slice(jj * PAIR_W, (jj + 1) * PAIR_W)
        u8 = u8_ref[:, lanes]
        v = _bdot(u8, wb_ref[jj])
        pair = lambda ref: jnp.concatenate([ref[2 * jj], ref[2 * jj + 1]], axis=1)
        ar = jnp.broadcast_to(pair(a8r_ref), (NB, half))
        ai = jnp.broadcast_to(pair(a8i_ref), (NB, half))
        xr = jnp.zeros((NB, half), F32)
        xi = jnp.zeros((NB, half), F32)
        prev_r, prev_i = [], []
        for s in range(SEQ // CHUNK):
            prev_r.append(xr)
            prev_i.append(xi)
            rows = slice(s * NB, (s + 1) * NB)
            xr, xi = ar * xr - ai * xi + v[rows, :half], ar * xi + ai * xr + v[rows, half:]
        x_prev = jnp.concatenate([jnp.concatenate(prev_r, axis=0), jnp.concatenate(prev_i, axis=0)], axis=1)
        lhs = jnp.concatenate([x_prev.astype(BF16), u8], axis=1)
        y8_ref[:, lanes] = lax.dot_general(lhs, wck_ref[jj], (((1,), (1,)), ((), ())), preferred_element_type=F32)


def _s5_core(u8, wb, wck, a8r, a8i):
    n_rows = u8.shape[0]
    w = PAIRS_PER_STEP * PAIR_W
    return pl.pallas_call(
        _s5_core_kernel,
        out_shape=jax.ShapeDtypeStruct(u8.shape, F32),
        grid=(N_PAIRS // PAIRS_PER_STEP,),
        in_specs=[
            pl.BlockSpec((n_rows, w), lambda i: (0, i)),
            pl.BlockSpec((PAIRS_PER_STEP, PAIR_W, PAIR_W), lambda i: (i, 0, 0)),
            pl.BlockSpec((PAIRS_PER_STEP, PAIR_W, 2 * PAIR_W), lambda i: (i, 0, 0)),
            pl.BlockSpec((2 * PAIRS_PER_STEP, 1, STATE), lambda i: (i, 0, 0)),
            pl.BlockSpec((2 * PAIRS_PER_STEP, 1, STATE), lambda i: (i, 0, 0)),
        ],
        out_specs=pl.BlockSpec((n_rows, w), lambda i: (0, i)),
        compiler_params=_cparams("arbitrary"),
        name="s5_recurrence",
    )(u8, wb, wck, a8r, a8i)


def _s5_out_kernel(y8_ref, x_ref, m_ref, w_glu_ref, w_out_ref, vec_ref, ca_ref, cb_ref, o_ref, cab_ref, cbb_ref):
    _cast_slices((ca_ref, cb_ref), (cab_ref, cbb_ref))
    b_glu_ref = vec_ref.at[VEC_B_GLU:VEC_B_GLU + 1]
    g_ref = vec_ref.at[VEC_LN_MIX_G + 1:VEC_LN_MIX_G + 2]
    b_ref = vec_ref.at[VEC_LN_MIX_B + 1:VEC_LN_MIX_B + 2]
    gate = m_ref[GATE]
    steps = S5_OUT_STEPS
    chunks = steps // CHUNK
    for sb in range(S5_OUT_BLOCK // steps):
        y8 = y8_ref[sb * chunks * NB:(sb + 1) * chunks * NB, :]
        per_k = [jnp.concatenate([y8[:, j * PAIR_W + k * PAIR_CH:j * PAIR_W + (k + 1) * PAIR_CH]
                                  for j in range(N_PAIRS)], axis=1) for k in range(CHUNK)]
        y = jnp.stack([p.reshape(chunks, NB, D) for p in per_k], axis=1).reshape(steps * NB, D)

        half_y = 0.5 * y
        g = half_y + half_y * jnp.tanh(y * (GELU_C + (GELU_C * 0.044715) * (y * y)))
        zz = g * jax.nn.sigmoid(_bdot(g.astype(BF16), w_glu_ref[...]) + b_glu_ref[...])
        out = jnp.swapaxes(_bdot(zz.astype(BF16), w_out_ref[...]).reshape(steps, NB, D), 0, 1)
        t_rows = slice(sb * steps, (sb + 1) * steps)
        res = ALPHA * x_ref[:, t_rows, :] + (1.0 + gate)[:, None, :] * out
        o_ref[:, t_rows, :] = _layer_norm(res, g_ref[...], b_ref[...])


def _s5_out(y8, x, m, w_glu, w_out, vec, casts):
    x_spec = _s5_x_spec(S5_OUT_BLOCK)
    grid = (SEQ // S5_OUT_BLOCK,)
    cast = [_cast_specs(w.shape, grid, l) for w, l in casts]
    return pl.pallas_call(
        _s5_out_kernel,
        out_shape=[jax.ShapeDtypeStruct((NB, SEQ, D), F32)] + [c[2] for c in cast],
        grid=grid,
        in_specs=[
            _s5_u8_spec(S5_OUT_BLOCK),
            x_spec,
            _mod_spec(MIX, 1),
            _layer_spec(w_glu.shape, 0),
            _layer_spec(w_out.shape, 0),
            _const_spec(vec.shape),
        ] + [c[0] for c in cast],
        out_specs=[x_spec] + [c[1] for c in cast],
        compiler_params=_cparams("arbitrary"),
        name="s5_out_norm",
    )(y8, x, m, w_glu, w_out, vec, *[w for w, _ in casts])


def _rotate_half_cols(w):
    half = w.shape[-1] // 2
    return jnp.concatenate([-w[..., half:], w[..., :half]], axis=-1)


def _mla_weights(w_in, w_qb):
    k_pe = w_in[:, Q_LORA + KV_LORA:]
    w_in_ext = jnp.concatenate([w_in, _rotate_half_cols(k_pe)], axis=1)
    wq = w_qb.reshape(Q_LORA, HEADS, QK_NOPE + QK_ROPE) * Q_SCALE
    wq_ext = jnp.concatenate([wq, _rotate_half_cols(wq[..., QK_NOPE:])], axis=-1)
    return w_in_ext.astype(BF16), wq_ext.reshape(Q_LORA, HEADS * HEAD_W).astype(BF16)


def kernel(x, c, positions, mla_w_in, mla_q_norm, mla_w_qb, mla_kv_norm, mla_w_kvb, mla_w_o, ssm_w_in, ssm_log_dt, ssm_a_re, ssm_a_im, ssm_b_re, ssm_b_im, ssm_c_re, ssm_c_im, ssm_d, ssm_w_glu, ssm_b_glu, ssm_w_out, mlp_w1, mlp_b1, mlp_w2, mlp_b2, mod_mix_w, mod_mix_b, mod_ffn_w, mod_ffn_b, ln_mix_g, ln_mix_b, ln_ffn_g, ln_ffn_b):
    vec = _pack_vectors(ln_mix_g, ln_mix_b, ln_ffn_g, ln_ffn_b, mlp_b2, ssm_b_glu, mlp_b1, mod_mix_b, mod_ffn_b,
                        mla_q_norm, mla_kv_norm)
    m = _modulation(c, mod_mix_w, mod_ffn_w, vec)

    inv_freq = ROPE_THETA ** (-jnp.arange(0, QK_ROPE, 2, dtype=F32) / QK_ROPE)
    freq = jnp.tile(inv_freq, 4).reshape(1, 128)
    w_in_ext, w_qb_ext = _mla_weights(mla_w_in[0], mla_w_qb[0])
    q, kn, kr, v, w1_0, w2_0, w_o = _mla_proj(
        x, m, positions.reshape(NB, SEQ // 128, 128), freq, w_in_ext, vec, w_qb_ext, mla_w_kvb[0].astype(BF16),
        ((mlp_w1, 0), (mlp_w2, 0), (mla_w_o, 0)))
    o = _attention(q, kn, kr, v)
    x2, s_w_in, s_w_glu, s_w_out = _attn_out_ffn(
        o, x, m, 0, w_o, w1_0, w2_0, vec, ((ssm_w_in, 0), (ssm_w_glu, 0), (ssm_w_out, 0)))

    a8r, a8i, wb, wck = _s5_discretise(ssm_log_dt[0], ssm_a_re[0], ssm_a_im[0], ssm_b_re[0], ssm_b_im[0],
                                       ssm_c_re[0], ssm_c_im[0], ssm_d[0])
    u8 = _s5_in(x2, m, s_w_in)
    y8 = _s5_core(u8, wb, wck, a8r, a8i)
    x3, w1_1, w2_1 = _s5_out(y8, x2, m, s_w_glu, s_w_out, vec, ((mlp_w1, 1), (mlp_w2, 1)))
    return _ffn(x3, m, 1, w1_1, w2_1, vec)
```

```python
import functools
import math

import jax
import jax.numpy as jnp
from jax import lax
from jax.experimental import pallas as pl
from jax.experimental.pallas import tpu as pltpu

F32 = jnp.float32
BF16 = jnp.bfloat16

D = 1024
NB = 8
SEQ = 2048
HEADS = 8
QK_NOPE = 128
QK_ROPE = 64
V_DIM = 128
Q_LORA = 256
KV_LORA = 128
ROPE_THETA = 10000.0
GROUP_CH = 16
N_GROUPS = 64
STATE = 64
D_FF = 4 * D
DEPTH = 2
ALPHA = (2 * DEPTH) ** 0.25
LN_EPS = 1e-5
RMS_EPS = 1e-6
Q_SCALE = math.log2(math.e) / math.sqrt(QK_NOPE + QK_ROPE)
NEG_BIG = -1e30
GELU_C = math.sqrt(2.0 / math.pi)

HEAD_W = 256
CHUNK = 8
N_PAIRS = N_GROUPS // 2
PAIR_CH = 2 * GROUP_CH
PAIR_W = CHUNK * PAIR_CH
assert PAIR_W == 4 * STATE

VMEM_LIMIT = 56 * 1024 * 1024
PROJ_ROWS = 1024
ATTN_BLK = 512
ATTN_HEADS = 4
_ATTN_SLOTS = 2 * (SEQ // ATTN_BLK) - 1
FFN_ROWS = 512
FFN_SUB = 2
FF_CHUNK = 1024
S5_IN_BLOCK, S5_IN_STEPS = 256, 64
S5_OUT_BLOCK, S5_OUT_STEPS = 128, 32
PAIRS_PER_STEP = 4
DISC_GROUPS = 16


VEC_LN_MIX_G, VEC_LN_MIX_B, VEC_LN_FFN_G, VEC_LN_FFN_B, VEC_MLP_B2 = 0, DEPTH, 2 * DEPTH, 3 * DEPTH, 4 * DEPTH
VEC_B_GLU = 5 * DEPTH
VEC_MLP_B1 = VEC_B_GLU + 1
VEC_MOD_B = VEC_MLP_B1 + DEPTH * (D_FF // D)
VEC_Q_NORM = VEC_MOD_B + 2 * 3 * DEPTH
VEC_KV_NORM = VEC_Q_NORM + 1
VEC_ROWS = -(-(VEC_KV_NORM + 1) // 8) * 8
assert FF_CHUNK == D


def _pack_vectors(ln_mix_g, ln_mix_b, ln_ffn_g, ln_ffn_b, mlp_b2, b_glu, mlp_b1, mod_mix_b, mod_ffn_b, q_norm, kv_norm):
    pad = lambda v: jnp.pad(v.reshape(1, -1), ((0, 0), (0, D - v.size)))
    rows = [ln_mix_g, ln_mix_b, ln_ffn_g, ln_ffn_b, mlp_b2, b_glu.reshape(1, D), mlp_b1.reshape(-1, D),
            mod_mix_b.reshape(-1, D), mod_ffn_b.reshape(-1, D), pad(q_norm), pad(kv_norm)]
    table = jnp.concatenate(rows, axis=0)
    return jnp.pad(table, ((0, VEC_ROWS - table.shape[0]), (0, 0)))


def _cparams(*sem):
    return pltpu.CompilerParams(dimension_semantics=sem, vmem_limit_bytes=VMEM_LIMIT)


def _const_spec(shape):
    nd = len(shape)
    return pl.BlockSpec(shape, lambda *_: (0,) * nd, pipeline_mode=pl.Buffered(1))


def _cast_specs(shape, grid, layer):
    _, rows, cols = shape
    n_steps = math.prod(grid)

    def step(*ids):
        s = 0
        for i, n in zip(ids, grid):
            s = s * n + i
        return s

    block = (1, rows // n_steps, cols)
    return (pl.BlockSpec(block, lambda *ids: (layer, step(*ids), 0)),
            pl.BlockSpec(block, lambda *ids: (0, step(*ids), 0)),
            jax.ShapeDtypeStruct((1, rows, cols), BF16))


def _cast_slices(src_refs, dst_refs):
    for src, dst in zip(src_refs, dst_refs):
        dst[...] = src[...].astype(dst.dtype)


def _layer_norm(v, g, b):
    mu = jnp.mean(v, axis=-1, keepdims=True)
    vc = v - mu
    var = jnp.mean(vc * vc, axis=-1, keepdims=True)
    return vc * lax.rsqrt(var + LN_EPS) * g + b


def _bdot(a, b):
    return jnp.dot(a, b, preferred_element_type=F32)


MIX, FFN = 0, 1
SHIFT, SCALE, GATE = 0, 1, 2


def _mod_kernel(c_ref, w_mix_ref, w_ffn_ref, vec_ref, o_ref):
    kind, layer, part = pl.program_id(0), pl.program_id(1), pl.program_id(2)
    c = c_ref[...]
    cs = (c * jax.nn.sigmoid(c)).astype(BF16)
    bias = vec_ref[pl.ds(VEC_MOD_B + 3 * (DEPTH * kind + layer) + part, 1), :]
    for k, w_ref in ((MIX, w_mix_ref), (FFN, w_ffn_ref)):
        @pl.when(kind == k)
        def _():
            o_ref[...] = _bdot(cs, w_ref[...].astype(BF16)) + bias


def _modulation(c, w_mix, w_ffn, vec):
    def w_spec(k):
        idle = ((1 - k) * (DEPTH - 1), 2 * (1 - k))
        return pl.BlockSpec((None, D, D), lambda kind, l, j: (jnp.where(kind == k, l, idle[0]), 0,
                                                            jnp.where(kind == k, j, idle[1])))
    return pl.pallas_call(
        _mod_kernel,
        out_shape=jax.ShapeDtypeStruct((2, DEPTH, 3, NB, D), F32),
        grid=(2, DEPTH, 3),
        in_specs=[
            pl.BlockSpec((NB, D), lambda kind, l, j: (0, 0)),
            w_spec(MIX),
            w_spec(FFN),
            _const_spec(vec.shape),
        ],
        out_specs=pl.BlockSpec((None, None, None, NB, D), lambda kind, l, j: (kind, l, j, 0, 0)),
        compiler_params=_cparams("arbitrary", "arbitrary", "arbitrary"),
        name="adaln_modulation",
    )(c, w_mix, w_ffn, vec)


def _mod_spec(kind, layer):
    return pl.BlockSpec((None, None, 3, NB, D), lambda *_: (kind, layer, 0, 0, 0), pipeline_mode=pl.Buffered(1))


def _mla_proj_kernel(x_ref, m_ref, pos_ref, freq_ref, w_in_ref, vec_ref, w_qb_ref, w_kvb_ref,
                     ca_ref, cb_ref, cc_ref, q_ref, kn_ref, kr_ref, v_ref, cab_ref, cbb_ref, ccb_ref):
    _cast_slices((ca_ref, cb_ref, cc_ref), (cab_ref, cbb_ref, ccb_ref))
    qn_ref = vec_ref.at[VEC_Q_NORM:VEC_Q_NORM + 1, 0:Q_LORA]
    kvn_ref = vec_ref.at[VEC_KV_NORM:VEC_KV_NORM + 1, 0:KV_LORA]
    b = pl.program_id(0)
    shift = m_ref[SHIFT, pl.ds(b, 1), :]
    scale = m_ref[SCALE, pl.ds(b, 1), :]

    def rms(v, g):
        return v * lax.rsqrt(jnp.mean(v * v, axis=-1, keepdims=True) + RMS_EPS) * g

    eye = lax.broadcasted_iota(jnp.int32, (128, 128), 0) == lax.broadcasted_iota(jnp.int32, (128, 128), 1)
    h = (x_ref[0] * (1.0 + scale) + shift).astype(BF16)
    z = _bdot(h, w_in_ref[...])
    cq = rms(z[:, :Q_LORA], qn_ref[...]).astype(BF16)
    ckv = rms(z[:, Q_LORA:Q_LORA + KV_LORA], kvn_ref[...]).astype(BF16)

    pos = pos_ref[0].astype(F32)
    pos_col = jnp.concatenate(
        [jnp.sum(jnp.where(eye, jnp.broadcast_to(pos[i:i + 1, :], (128, 128)), 0.0), axis=1, keepdims=True)
         for i in range(PROJ_ROWS // 128)], axis=0)
    ang = pos_col * freq_ref[...]
    lane = lax.broadcasted_iota(jnp.int32, ang.shape, 1)
    mult = jnp.where(lane < QK_ROPE, jnp.cos(ang), jnp.sin(ang))

    def rope(slab):
        s = slab * mult
        return s + pltpu.roll(s, QK_ROPE, axis=1)

    kr_ref[0] = jnp.where(lane < QK_ROPE, rope(z[:, Q_LORA + KV_LORA:]), 0.0).astype(BF16)

    q_all = _bdot(cq, w_qb_ref[...])
    kv = _bdot(ckv, w_kvb_ref[...])
    for hd in range(HEADS):
        c0 = hd * HEAD_W
        q_ref[0, hd, :, 0:QK_NOPE] = q_all[:, c0:c0 + QK_NOPE].astype(BF16)
        q_ref[0, hd, :, QK_NOPE:HEAD_W] = rope(q_all[:, c0 + QK_NOPE:c0 + HEAD_W]).astype(BF16)
        kn_ref[0, hd] = kv[:, c0:c0 + QK_NOPE].astype(BF16)
        v_ref[0, hd] = kv[:, c0 + QK_NOPE:c0 + HEAD_W].astype(BF16)


def _mla_proj(x, m, pos, freq, w_in, vec, w_qb, w_kvb, casts):
    nt = SEQ // PROJ_ROWS
    blk = PROJ_ROWS
    grid = (NB, nt)
    cast = [_cast_specs(w.shape, grid, layer) for w, layer in casts]
    return pl.pallas_call(
        _mla_proj_kernel,
        out_shape=[
            jax.ShapeDtypeStruct((NB, HEADS, SEQ, HEAD_W), BF16),
            jax.ShapeDtypeStruct((NB, HEADS, SEQ, QK_NOPE), BF16),
            jax.ShapeDtypeStruct((NB, SEQ, HEAD_W - QK_NOPE), BF16),
            jax.ShapeDtypeStruct((NB, HEADS, SEQ, V_DIM), BF16),
        ] + [c[2] for c in cast],
        grid=grid,
        in_specs=[
            pl.BlockSpec((1, blk, D), lambda b, i: (b, i, 0)),
            _mod_spec(MIX, 0),
            pl.BlockSpec((1, blk // 128, 128), lambda b, i: (b, i, 0)),
            _const_spec((1, 128)),
            _const_spec(w_in.shape),
            _const_spec(vec.shape),
            _const_spec(w_qb.shape),
            _const_spec(w_kvb.shape),
        ] + [c[0] for c in cast],
        out_specs=[
            pl.BlockSpec((1, HEADS, blk, HEAD_W), lambda b, i: (b, 0, i, 0)),
            pl.BlockSpec((1, HEADS, blk, QK_NOPE), lambda b, i: (b, 0, i, 0)),
            pl.BlockSpec((1, blk, HEAD_W - QK_NOPE), lambda b, i: (b, i, 0)),
            pl.BlockSpec((1, HEADS, blk, V_DIM), lambda b, i: (b, 0, i, 0)),
        ] + [c[1] for c in cast],
        compiler_params=_cparams("arbitrary", "arbitrary"),
        name="mla_projections",
    )(x, m, pos, freq, w_in, vec, w_qb, w_kvb, *[w for w, _ in casts])


def _lane_groups(v):
    return [v[:, c:c + 128] for c in range(0, v.shape[1], 128)]


def _attn_kernel(q_ref, kn_ref, kr_ref, v_ref, o_ref, s_ref):
    blk = ATTN_BLK
    half = blk // 2
    nq = SEQ // blk
    tri = lax.broadcasted_iota(jnp.int32, (half, half), 1) <= lax.broadcasted_iota(jnp.int32, (half, half), 0)

    def rows(n):
        return slice(n * blk, (n + 1) * blk)

    def slot(hd, n, j):
        return hd * _ATTN_SLOTS + (0 if n % 2 == 0 else nq - 1) + j

    def nt_dot(a, b):
        return lax.dot_general(a, b, (((1,), (1,)), ((), ())), preferred_element_type=F32)

    def keys(hd, j):
        return jnp.concatenate([kn_ref[hd, rows(j), :], kr_ref[rows(j), :]], axis=1)

    def lane_max(vals):
        return functools.reduce(jnp.maximum, vals)

    def pass1(hd, n):
        q = q_ref[hd, rows(n), :]
        m = []
        for j in range(n):
            s = nt_dot(q, keys(hd, j))
            s_ref[slot(hd, n, j)] = s
            m = [lane_max(_lane_groups(s) + m)]
        k = keys(hd, n)
        s_top = jnp.where(tri, nt_dot(q[:half], k[:half]), NEG_BIG)
        s_bot = nt_dot(q[half:], k)
        s_bot = jnp.concatenate([s_bot[:, :half], jnp.where(tri, s_bot[:, half:], NEG_BIG)], axis=1)
        s_ref[slot(hd, n, n), 0:half, 0:half] = s_top
        s_ref[slot(hd, n, n), half:blk, :] = s_bot
        m_top = lane_max(_lane_groups(s_top) + [v[:half] for v in m])
        m_bot = lane_max(_lane_groups(s_bot) + [v[half:] for v in m])
        m = jnp.concatenate([m_top, m_bot], axis=0)
        return jnp.broadcast_to(jnp.max(m, axis=-1, keepdims=True), (blk, 128))

    def pass2(hd, n, mb):
        diag = slot(hd, n, n)
        ps = [jnp.exp2(sg - mb) for j in range(n) for sg in _lane_groups(s_ref[slot(hd, n, j)])]
        p_top = [jnp.exp2(sg - mb[:half]) for sg in _lane_groups(s_ref[diag, 0:half, 0:half])]
        p_bot = [jnp.exp2(sg - mb[half:]) for sg in _lane_groups(s_ref[diag, half:blk, :])]
        l_top = functools.reduce(jnp.add, p_top + [p[:half] for p in ps])
        l_bot = functools.reduce(jnp.add, p_bot + [p[half:] for p in ps])
        l = jnp.sum(jnp.concatenate([l_top, l_bot], axis=0), axis=-1, keepdims=True)
        v0 = n * blk
        acc = jnp.concatenate([
            _bdot(jnp.concatenate(p_top, axis=1).astype(BF16), v_ref[hd, v0:v0 + half, :]),
            _bdot(jnp.concatenate(p_bot, axis=1).astype(BF16), v_ref[hd, v0:v0 + blk, :])], axis=0)
        if n > 0:
            acc = acc + _bdot(jnp.concatenate(ps, axis=1).astype(BF16), v_ref[hd, 0:v0, :])
        o_ref[hd, rows(n), :] = (acc / l).astype(BF16)

    mb = [pass1(hd, 0) for hd in range(ATTN_HEADS)]
    for n in range(nq):
        for hd in range(ATTN_HEADS):
            mb_next = pass1(hd, n + 1) if n + 1 < nq else None
            pass2(hd, n, mb[hd])
            mb[hd] = mb_next


def _attention(q, kn, kr, v):
    head_spec = lambda w: pl.BlockSpec((None, ATTN_HEADS, SEQ, w), lambda b, h: (b, h, 0, 0))
    kr_spec = pl.BlockSpec((None, SEQ, HEAD_W - QK_NOPE), lambda b, h: (b, 0, 0))
    return pl.pallas_call(
        _attn_kernel,
        out_shape=jax.ShapeDtypeStruct((NB, HEADS, SEQ, V_DIM), BF16),
        grid=(NB, HEADS // ATTN_HEADS),
        in_specs=[head_spec(HEAD_W), head_spec(QK_NOPE), kr_spec, head_spec(V_DIM)],
        out_specs=head_spec(V_DIM),
        scratch_shapes=[pltpu.VMEM((ATTN_HEADS * _ATTN_SLOTS, ATTN_BLK, ATTN_BLK), F32)],
        compiler_params=_cparams("arbitrary", "arbitrary"),
        name="mla_attention",
    )(q, kn, kr, v)


def _vec_row(vec_ref, row):
    return vec_ref[row:row + 1, :]


def _ffn_rows(x, shift, scale, gate, w1_ref, w2_ref, vec_ref, layer):
    h = (x * (1.0 + scale) + shift).astype(BF16)
    a2 = []
    for c in range(D_FF // FF_CHUNK):
        cols = slice(c * FF_CHUNK, (c + 1) * FF_CHUNK)
        b1 = _vec_row(vec_ref, VEC_MLP_B1 + layer * (D_FF // FF_CHUNK) + c)
        a = jnp.maximum(_bdot(h, w1_ref[:, cols]) + b1, 0.0)
        a2.append((a * a).astype(BF16))
    y = _bdot(jnp.concatenate(a2, axis=1), w2_ref[...]) + _vec_row(vec_ref, VEC_MLP_B2 + layer)
    return _layer_norm(ALPHA * x + (1.0 + gate) * y,
                       _vec_row(vec_ref, VEC_LN_FFN_G + layer), _vec_row(vec_ref, VEC_LN_FFN_B + layer))


def _batch_mod(m_ref):
    b = pl.program_id(0)
    return [m_ref[i, pl.ds(b, 1), :] for i in (SHIFT, SCALE, GATE)]


def _ffn_kernel(x_ref, m_ref, w1_ref, w2_ref, vec_ref, y_ref, *, layer):
    mod = _batch_mod(m_ref)
    for sb in range(FFN_SUB):
        rows = slice(sb * FFN_ROWS, (sb + 1) * FFN_ROWS)
        y_ref[rows, :] = _ffn_rows(x_ref[rows, :], *mod, w1_ref, w2_ref, vec_ref, layer)


def _attn_out_ffn_kernel(o_ref, x_ref, mm_ref, mf_ref, wo_ref, w1_ref, w2_ref, vec_ref, sa_ref, sb_ref, sc_ref,
                         y_ref, sab_ref, sbb_ref, scb_ref, *, layer):
    _cast_slices((sa_ref, sb_ref, sc_ref), (sab_ref, sbb_ref, scb_ref))
    gm_ref = vec_ref.at[VEC_LN_MIX_G + layer:VEC_LN_MIX_G + layer + 1]
    bm_ref = vec_ref.at[VEC_LN_MIX_B + layer:VEC_LN_MIX_B + layer + 1]
    gate_mix = _batch_mod(mm_ref)[GATE]
    mod = _batch_mod(mf_ref)
    for sb in range(FFN_SUB):
        rows = slice(sb * FFN_ROWS, (sb + 1) * FFN_ROWS)
        o = jnp.concatenate([o_ref[hd, rows, :] for hd in range(HEADS)], axis=1)
        y_ref[rows, :] = _layer_norm(ALPHA * x_ref[rows, :] + (1.0 + gate_mix) * _bdot(o, wo_ref[...]),
                                     gm_ref[...], bm_ref[...])
    for sb in range(FFN_SUB):
        rows = slice(sb * FFN_ROWS, (sb + 1) * FFN_ROWS)
        y_ref[rows, :] = _ffn_rows(y_ref[rows, :], *mod, w1_ref, w2_ref, vec_ref, layer)


_FFN_X_SPEC = pl.BlockSpec((None, FFN_SUB * FFN_ROWS, D), lambda b, i: (b, i, 0))
_FFN_GRID = (NB, SEQ // (FFN_SUB * FFN_ROWS))


def _layer_spec(shape, layer):
    return pl.BlockSpec((None,) + tuple(shape[1:]), lambda *_: (layer, 0, 0), pipeline_mode=pl.Buffered(1))


def _ffn(x, m, layer, w1, w2, vec):
    return pl.pallas_call(
        functools.partial(_ffn_kernel, layer=layer),
        out_shape=jax.ShapeDtypeStruct((NB, SEQ, D), F32),
        grid=_FFN_GRID,
        in_specs=[_FFN_X_SPEC, _mod_spec(FFN, layer), _layer_spec(w1.shape, 0), _layer_spec(w2.shape, 0),
                  _const_spec(vec.shape)],
        out_specs=_FFN_X_SPEC,
        compiler_params=_cparams("arbitrary", "arbitrary"),
        name="ffn_norm",
    )(x, m, w1, w2, vec)


def _attn_out_ffn(o, x, m, layer, w_o, w1, w2, vec, casts):
    o_spec = pl.BlockSpec((None, HEADS, FFN_SUB * FFN_ROWS, V_DIM), lambda b, i: (b, 0, i, 0))
    cast = [_cast_specs(w.shape, _FFN_GRID, l) for w, l in casts]
    return pl.pallas_call(
        functools.partial(_attn_out_ffn_kernel, layer=layer),
        out_shape=[jax.ShapeDtypeStruct((NB, SEQ, D), F32)] + [c[2] for c in cast],
        grid=_FFN_GRID,
        in_specs=[o_spec, _FFN_X_SPEC, _mod_spec(MIX, layer), _mod_spec(FFN, layer), _layer_spec(w_o.shape, 0),
                  _layer_spec(w1.shape, 0), _layer_spec(w2.shape, 0), _const_spec(vec.shape)]
                 + [c[0] for c in cast],
        out_specs=[_FFN_X_SPEC] + [c[1] for c in cast],
        compiler_params=_cparams("arbitrary", "arbitrary"),
        name="attn_out_ffn_norm",
    )(o, x, m, m, w_o, w1, w2, vec, *[w for w, _ in casts])


def _cmul(ar, ai, br, bi):
    return ar * br - ai * bi, ar * bi + ai * br


def _pair_slot(piece, zero, slot):
    return [piece, zero] if slot == 0 else [zero, piece]


def _s5_disc_kernel(ldt_ref, are_ref, aim_ref, bre_ref, bim_ref, cre_ref, cim_ref, d_ref,
                    a8r_ref, a8i_ref, wb_ref, wck_ref):
    lr = are_ref[...]
    li = aim_ref[...]
    dt = jnp.exp(ldt_ref[...])
    mag = jnp.exp(lr * dt)
    ab_re = mag * jnp.cos(li * dt)
    ab_im = mag * jnp.sin(li * dt)
    den = lr * lr + li * li
    nr = ab_re - 1.0
    coef_re = (nr * lr + ab_im * li) / den
    coef_im = (ab_im * lr - nr * li) / den
    bb_re, bb_im = _cmul(coef_re, coef_im, bre_ref[...], bim_ref[...])
    c_re = cre_ref[...]
    c_im = cim_ref[...]

    pw = [(jnp.ones_like(ab_re), jnp.zeros_like(ab_im))]
    for _ in range(CHUNK):
        pw.append(_cmul(pw[-1][0], pw[-1][1], ab_re, ab_im))
    a8r_ref[...] = pw[CHUNK][0]
    a8i_ref[...] = pw[CHUNK][1]

    n_pairs = wb_ref.shape[0]

    def pairs(v):
        v4 = v.reshape(n_pairs, 2, v.shape[1], v.shape[2])
        return v4[:, 0], v4[:, 1]

    z64 = jnp.zeros((n_pairs, GROUP_CH, STATE), F32)
    z16 = jnp.zeros((n_pairs, GROUP_CH, GROUP_CH), F32)
    eye = (lax.broadcasted_iota(jnp.int32, (GROUP_CH, GROUP_CH), 0)
           == lax.broadcasted_iota(jnp.int32, (GROUP_CH, GROUP_CH), 1)).astype(F32)

    ab_l = [_cmul(pw[l][0], pw[l][1], bb_re, bb_im) for l in range(CHUNK)]
    wb_rows = []
    for k in range(CHUNK):
        m_re, m_im = ab_l[CHUNK - 1 - k]
        for slot in range(2):
            wb_rows.append(jnp.concatenate(_pair_slot(pairs(m_re)[slot], z64, slot)
                                           + _pair_slot(pairs(m_im)[slot], z64, slot), axis=2))
    wb_ref[...] = jnp.concatenate(wb_rows, axis=1).astype(BF16)

    def nt(a, b):
        return lax.dot_general(a.astype(BF16), b.astype(BF16), (((2,), (2,)), ((0,), (0,))),
                               preferred_element_type=F32)

    k_l = [nt(c_re, ab_l[l][0]) - nt(c_im, ab_l[l][1]) for l in range(CHUNK)]
    k_l[0] = k_l[0] + d_ref[...] * eye[None]

    wck_rows = []
    for kp in range(CHUNK):
        ca_re, ca_im = _cmul(c_re, c_im, pw[kp + 1][0], pw[kp + 1][1])
        for slot in range(2):
            state_cols = (_pair_slot(pairs(ca_re)[slot], z64, slot)
                          + _pair_slot(-pairs(ca_im)[slot], z64, slot))
            direct_cols = []
            for k in range(CHUNK):
                blk = pairs(k_l[kp - k])[slot] if k <= kp else z16
                direct_cols += _pair_slot(blk, z16, slot)
            wck_rows.append(jnp.concatenate(state_cols + direct_cols, axis=2))
    wck_ref[...] = jnp.concatenate(wck_rows, axis=1).astype(BF16)


def _s5_discretise(log_dt, a_re, a_im, b_re, b_im, c_re, c_im, d_skip):
    g3 = (N_GROUPS, 1, STATE)
    gb = DISC_GROUPS
    spec = lambda *tail: pl.BlockSpec((gb,) + tail, lambda i: (i,) + (0,) * len(tail))
    pair_spec = lambda cols: pl.BlockSpec((gb // 2, PAIR_W, cols), lambda i: (i, 0, 0))
    big = spec(GROUP_CH, STATE)
    return pl.pallas_call(
        _s5_disc_kernel,
        out_shape=(jax.ShapeDtypeStruct(g3, F32), jax.ShapeDtypeStruct(g3, F32),
                   jax.ShapeDtypeStruct((N_PAIRS, PAIR_W, PAIR_W), BF16),
                   jax.ShapeDtypeStruct((N_PAIRS, PAIR_W, 2 * PAIR_W), BF16)),
        grid=(N_GROUPS // gb,),
        in_specs=[spec(1, 1), spec(1, STATE), spec(1, STATE), big, big, big, big, spec(GROUP_CH, 1)],
        out_specs=(spec(1, STATE), spec(1, STATE), pair_spec(PAIR_W), pair_spec(2 * PAIR_W)),
        compiler_params=_cparams("arbitrary"),
        name="s5_discretise",
    )(log_dt.reshape(N_GROUPS, 1, 1), a_re.reshape(g3), a_im.reshape(g3),
      jnp.swapaxes(b_re, 1, 2), jnp.swapaxes(b_im, 1, 2), c_re, c_im, d_skip.reshape(N_GROUPS, GROUP_CH, 1))


def _s5_x_spec(block):
    return pl.BlockSpec((NB, block, D), lambda i: (0, i, 0))


def _s5_u8_spec(block):
    return pl.BlockSpec((block // CHUNK * NB, N_PAIRS * PAIR_W), lambda i: (i, 0))


def _s5_in_kernel(x_ref, m_ref, w_in_ref, u8_ref):
    shift = m_ref[SHIFT]
    scale = m_ref[SCALE]
    steps = S5_IN_STEPS
    chunks = steps // CHUNK
    for sb in range(S5_IN_BLOCK // steps):
        x3 = jnp.swapaxes(x_ref[:, sb * steps:(sb + 1) * steps, :], 0, 1)
        h = (x3 * (1.0 + scale)[None] + shift[None]).reshape(steps * NB, D).astype(BF16)
        u = _bdot(h, w_in_ref[...])
        u4 = u.reshape(chunks, CHUNK, NB, D)
        per_k = [u4[:, k].reshape(chunks * NB, D) for k in range(CHUNK)]
        cols = [per_k[k][:, j * PAIR_CH:(j + 1) * PAIR_CH] for j in range(N_PAIRS) for k in range(CHUNK)]
        u8_ref[sb * chunks * NB:(sb + 1) * chunks * NB, :] = jnp.concatenate(cols, axis=1).astype(BF16)


def _s5_in(x, m, w_in):
    return pl.pallas_call(
        _s5_in_kernel,
        out_shape=jax.ShapeDtypeStruct((SEQ // CHUNK * NB, N_PAIRS * PAIR_W), BF16),
        grid=(SEQ // S5_IN_BLOCK,),
        in_specs=[
            _s5_x_spec(S5_IN_BLOCK),
            _mod_spec(MIX, 1),
            _layer_spec(w_in.shape, 0),
        ],
        out_specs=_s5_u8_spec(S5_IN_BLOCK),
        compiler_params=_cparams("arbitrary"),
        name="s5_in_proj",
    )(x, m, w_in)


def _s5_core_kernel(u8_ref, wb_ref, wck_ref, a8r_ref, a8i_ref, y8_ref):
    half = PAIR_W // 2
    for jj in range(PAIRS_PER_STEP):
        lanes = slice(jj * PAIR_W, (jj + 1) * PAIR_W)
        u8 = u8_ref[:, lanes]
        v = _bdot(u8, wb_ref[jj])
        pair = lambda ref: jnp.concatenate([ref[2 * jj], ref[2 * jj + 1]], axis=1)
        ar = jnp.broadcast_to(pair(a8r_ref), (NB, half))
        ai = jnp.broadcast_to(pair(a8i_ref), (NB, half))
        xr = jnp.zeros((NB, half), F32)
        xi = jnp.zeros((NB, half), F32)
        prev_r, prev_i = [], []
        for s in range(SEQ // CHUNK):
            prev_r.append(xr)
            prev_i.append(xi)
            rows = slice(s * NB, (s + 1) * NB)
            xr, xi = ar * xr - ai * xi + v[rows, :half], ar * xi + ai * xr + v[rows, half:]
        x_prev = jnp.concatenate([jnp.concatenate(prev_r, axis=0), jnp.concatenate(prev_i, axis=0)], axis=1)
        lhs = jnp.concatenate([x_prev.astype(BF16), u8], axis=1)
        y8_ref[:, lanes] = lax.dot_general(lhs, wck_ref[jj], (((1,), (1,)), ((), ())), preferred_element_type=F32)


def _s5_core(u8, wb, wck, a8r, a8i):
    n_rows = u8.shape[0]
    w = PAIRS_PER_STEP * PAIR_W
    return pl.pallas_call(
        _s5_core_kernel,
        out_shape=jax.ShapeDtypeStruct(u8.shape, F32),
        grid=(N_PAIRS // PAIRS_PER_STEP,),
        in_specs=[
            pl.BlockSpec((n_rows, w), lambda i: (0, i)),
            pl.BlockSpec((PAIRS_PER_STEP, PAIR_W, PAIR_W), lambda i: (i, 0, 0)),
            pl.BlockSpec((PAIRS_PER_STEP, PAIR_W, 2 * PAIR_W), lambda i: (i, 0, 0)),
            pl.BlockSpec((2 * PAIRS_PER_STEP, 1, STATE), lambda i: (i, 0, 0)),
            pl.BlockSpec((2 * PAIRS_PER_STEP, 1, STATE), lambda i: (i, 0, 0)),
        ],
        out_specs=pl.BlockSpec((n_rows, w), lambda i: (0, i)),
        compiler_params=_cparams("arbitrary"),
        name="s5_recurrence",
    )(u8, wb, wck, a8r, a8i)


def _s5_out_kernel(y8_ref, x_ref, m_ref, w_glu_ref, w_out_ref, vec_ref, ca_ref, cb_ref, o_ref, cab_ref, cbb_ref):
    _cast_slices((ca_ref, cb_ref), (cab_ref, cbb_ref))
    b_glu_ref = vec_ref.at[VEC_B_GLU:VEC_B_GLU + 1]
    g_ref = vec_ref.at[VEC_LN_MIX_G + 1:VEC_LN_MIX_G + 2]
    b_ref = vec_ref.at[VEC_LN_MIX_B + 1:VEC_LN_MIX_B + 2]
    gate = m_ref[GATE]
    steps = S5_OUT_STEPS
    chunks = steps // CHUNK
    for sb in range(S5_OUT_BLOCK // steps):
        y8 = y8_ref[sb * chunks * NB:(sb + 1) * chunks * NB, :]
        per_k = [jnp.concatenate([y8[:, j * PAIR_W + k * PAIR_CH:j * PAIR_W + (k + 1) * PAIR_CH]
                                  for j in range(N_PAIRS)], axis=1) for k in range(CHUNK)]
        y = jnp.stack([p.reshape(chunks, NB, D) for p in per_k], axis=1).reshape(steps * NB, D)

        half_y = 0.5 * y
        g = half_y + half_y * jnp.tanh(y * (GELU_C + (GELU_C * 0.044715) * (y * y)))
        zz = g * jax.nn.sigmoid(_bdot(g.astype(BF16), w_glu_ref[...]) + b_glu_ref[...])
        out = jnp.swapaxes(_bdot(zz.astype(BF16), w_out_ref[...]).reshape(steps, NB, D), 0, 1)
        t_rows = slice(sb * steps, (sb + 1) * steps)
        res = ALPHA * x_ref[:, t_rows, :] + (1.0 + gate)[:, None, :] * out
        o_ref[:, t_rows, :] = _layer_norm(res, g_ref[...], b_ref[...])


def _s5_out(y8, x, m, w_glu, w_out, vec, casts):
    x_spec = _s5_x_spec(S5_OUT_BLOCK)
    grid = (SEQ // S5_OUT_BLOCK,)
    cast = [_cast_specs(w.shape, grid, l) for w, l in casts]
    return pl.pallas_call(
        _s5_out_kernel,
        out_shape=[jax.ShapeDtypeStruct((NB, SEQ, D), F32)] + [c[2] for c in cast],
        grid=grid,
        in_specs=[
            _s5_u8_spec(S5_OUT_BLOCK),
            x_spec,
            _mod_spec(MIX, 1),
            _layer_spec(w_glu.shape, 0),
            _layer_spec(w_out.shape, 0),
            _const_spec(vec.shape),
        ] + [c[0] for c in cast],
        out_specs=[x_spec] + [c[1] for c in cast],
        compiler_params=_cparams("arbitrary"),
        name="s5_out_norm",
    )(y8, x, m, w_glu, w_out, vec, *[w for w, _ in casts])


def _rotate_half_cols(w):
    half = w.shape[-1] // 2
    return jnp.concatenate([-w[..., half:], w[..., :half]], axis=-1)


def _mla_weights(w_in, w_qb):
    k_pe = w_in[:, Q_LORA + KV_LORA:]
    w_in_ext = jnp.concatenate([w_in, _rotate_half_cols(k_pe)], axis=1)
    wq = w_qb.reshape(Q_LORA, HEADS, QK_NOPE + QK_ROPE) * Q_SCALE
    wq_ext = jnp.concatenate([wq, _rotate_half_cols(wq[..., QK_NOPE:])], axis=-1)
    return w_in_ext.astype(BF16), wq_ext.reshape(Q_LORA, HEADS * HEAD_W).astype(BF16)


def kernel(x, c, positions, mla_w_in, mla_q_norm, mla_w_qb, mla_kv_norm, mla_w_kvb, mla_w_o, ssm_w_in, ssm_log_dt, ssm_a_re, ssm_a_im, ssm_b_re, ssm_b_im, ssm_c_re, ssm_c_im, ssm_d, ssm_w_glu, ssm_b_glu, ssm_w_out, mlp_w1, mlp_b1, mlp_w2, mlp_b2, mod_mix_w, mod_mix_b, mod_ffn_w, mod_ffn_b, ln_mix_g, ln_mix_b, ln_ffn_g, ln_ffn_b):
    vec = _pack_vectors(ln_mix_g, ln_mix_b, ln_ffn_g, ln_ffn_b, mlp_b2, ssm_b_glu, mlp_b1, mod_mix_b, mod_ffn_b,
                        mla_q_norm, mla_kv_norm)
    m = _modulation(c, mod_mix_w, mod_ffn_w, vec)

    inv_freq = ROPE_THETA ** (-jnp.arange(0, QK_ROPE, 2, dtype=F32) / QK_ROPE)
    freq = jnp.tile(inv_freq, 4).reshape(1, 128)
    w_in_ext, w_qb_ext = _mla_weights(mla_w_in[0], mla_w_qb[0])
    q, kn, kr, v, w1_0, w2_0, w_o = _mla_proj(
        x, m, positions.reshape(NB, SEQ // 128, 128), freq, w_in_ext, vec, w_qb_ext, mla_w_kvb[0].astype(BF16),
        ((mlp_w1, 0), (mlp_w2, 0), (mla_w_o, 0)))
    o = _attention(q, kn, kr, v)
    x2, s_w_in, s_w_glu, s_w_out = _attn_out_ffn(
        o, x, m, 0, w_o, w1_0, w2_0, vec, ((ssm_w_in, 0), (ssm_w_glu, 0), (ssm_w_out, 0)))

    a8r, a8i, wb, wck = _s5_discretise(ssm_log_dt[0], ssm_a_re[0], ssm_a_im[0], ssm_b_re[0], ssm_b_im[0],
                                       ssm_c_re[0], ssm_c_im[0], ssm_d[0])
    u8 = _s5_in(x2, m, s_w_in)
    y8 = _s5_core(u8, wb, wck, a8r, a8i)
    x3, w1_1, w2_1 = _s5_out(y8, x2, m, s_w_glu, s_w_out, vec, ((mlp_w1, 1), (mlp_w2, 1)))
    return _ffn(x3, m, 1, w1_1, w2_1, vec)
```

```python
import functools
import math

import jax
import jax.numpy as jnp
from jax import lax
from jax.experimental import pallas as pl
from jax.experimental.pallas import tpu as pltpu

F32 = jnp.float32
BF16 = jnp.bfloat16

D = 1024
NB = 8
SEQ = 2048
HEADS = 8
QK_NOPE = 128
QK_ROPE = 64
V_DIM = 128
Q_LORA = 256
KV_LORA = 128
ROPE_THETA = 10000.0
GROUP_CH = 16
N_GROUPS = 64
STATE = 64
D_FF = 4 * D
DEPTH = 2
ALPHA = (2 * DEPTH) ** 0.25
LN_EPS = 1e-5
RMS_EPS = 1e-6
Q_SCALE = math.log2(math.e) / math.sqrt(QK_NOPE + QK_ROPE)
NEG_BIG = -1e30
GELU_C = math.sqrt(2.0 / math.pi)

HEAD_W = 256
CHUNK = 8
N_PAIRS = N_GROUPS // 2
PAIR_CH = 2 * GROUP_CH
PAIR_W = CHUNK * PAIR_CH
assert PAIR_W == 4 * STATE

VMEM_LIMIT = 56 * 1024 * 1024
PROJ_ROWS = 1024
ATTN_BLK = 512
ATTN_HEADS = 4
_ATTN_SLOTS = 2 * (SEQ // ATTN_BLK) - 1
FFN_ROWS = 256
FFN_SUB = 4
FF_CHUNK = 1024
S5_IN_BLOCK, S5_IN_STEPS = 256, 64
S5_OUT_BLOCK, S5_OUT_STEPS = 128, 32
PAIRS_PER_STEP = 4
DISC_GROUPS = 16


VEC_LN_MIX_G, VEC_LN_MIX_B, VEC_LN_FFN_G, VEC_LN_FFN_B, VEC_MLP_B2 = 0, DEPTH, 2 * DEPTH, 3 * DEPTH, 4 * DEPTH
VEC_B_GLU = 5 * DEPTH
VEC_MLP_B1 = VEC_B_GLU + 1
VEC_MOD_B = VEC_MLP_B1 + DEPTH * (D_FF // D)
VEC_Q_NORM = VEC_MOD_B + 2 * 3 * DEPTH
VEC_KV_NORM = VEC_Q_NORM + 1
VEC_ROWS = -(-(VEC_KV_NORM + 1) // 8) * 8
assert FF_CHUNK == D


def _pack_vectors(ln_mix_g, ln_mix_b, ln_ffn_g, ln_ffn_b, mlp_b2, b_glu, mlp_b1, mod_mix_b, mod_ffn_b, q_norm, kv_norm):
    pad = lambda v: jnp.pad(v.reshape(1, -1), ((0, 0), (0, D - v.size)))
    rows = [ln_mix_g, ln_mix_b, ln_ffn_g, ln_ffn_b, mlp_b2, b_glu.reshape(1, D), mlp_b1.reshape(-1, D),
            mod_mix_b.reshape(-1, D), mod_ffn_b.reshape(-1, D), pad(q_norm), pad(kv_norm)]
    table = jnp.concatenate(rows, axis=0)
    return jnp.pad(table, ((0, VEC_ROWS - table.shape[0]), (0, 0)))


def _cparams(*sem):
    return pltpu.CompilerParams(dimension_semantics=sem, vmem_limit_bytes=VMEM_LIMIT)


def _const_spec(shape):
    nd = len(shape)
    return pl.BlockSpec(shape, lambda *_: (0,) * nd, pipeline_mode=pl.Buffered(1))


def _cast_specs(shape, grid, layer):
    _, rows, cols = shape
    n_steps = math.prod(grid)

    def step(*ids):
        s = 0
        for i, n in zip(ids, grid):
            s = s * n + i
        return s

    block = (1, rows // n_steps, cols)
    return (pl.BlockSpec(block, lambda *ids: (layer, step(*ids), 0)),
            pl.BlockSpec(block, lambda *ids: (0, step(*ids), 0)),
            jax.ShapeDtypeStruct((1, rows, cols), BF16))


def _cast_slices(src_refs, dst_refs):
    for src, dst in zip(src_refs, dst_refs):
        dst[...] = src[...].astype(dst.dtype)


def _layer_norm(v, g, b):
    mu = jnp.mean(v, axis=-1, keepdims=True)
    vc = v - mu
    var = jnp.mean(vc * vc, axis=-1, keepdims=True)
    return vc * lax.rsqrt(var + LN_EPS) * g + b


def _bdot(a, b):
    return jnp.dot(a, b, preferred_element_type=F32)


MIX, FFN = 0, 1
SHIFT, SCALE, GATE = 0, 1, 2


def _mod_kernel(c_ref, w_mix_ref, w_ffn_ref, vec_ref, o_ref):
    kind, layer, part = pl.program_id(0), pl.program_id(1), pl.program_id(2)
    c = c_ref[...]
    cs = (c * jax.nn.sigmoid(c)).astype(BF16)
    bias = vec_ref[pl.ds(VEC_MOD_B + 3 * (DEPTH * kind + layer) + part, 1), :]
    for k, w_ref in ((MIX, w_mix_ref), (FFN, w_ffn_ref)):
        @pl.when(kind == k)
        def _():
            o_ref[...] = _bdot(cs, w_ref[...].astype(BF16)) + bias


def _modulation(c, w_mix, w_ffn, vec):
    def w_spec(k):
        idle = ((1 - k) * (DEPTH - 1), 2 * (1 - k))
        return pl.BlockSpec((None, D, D), lambda kind, l, j: (jnp.where(kind == k, l, idle[0]), 0,
                                                            jnp.where(kind == k, j, idle[1])))
    return pl.pallas_call(
        _mod_kernel,
        out_shape=jax.ShapeDtypeStruct((2, DEPTH, 3, NB, D), F32),
        grid=(2, DEPTH, 3),
        in_specs=[
            pl.BlockSpec((NB, D), lambda kind, l, j: (0, 0)),
            w_spec(MIX),
            w_spec(FFN),
            _const_spec(vec.shape),
        ],
        out_specs=pl.BlockSpec((None, None, None, NB, D), lambda kind, l, j: (kind, l, j, 0, 0)),
        compiler_params=_cparams("arbitrary", "arbitrary", "arbitrary"),
        name="adaln_modulation",
    )(c, w_mix, w_ffn, vec)


def _mod_spec(kind, layer):
    return pl.BlockSpec((None, None, 3, NB, D), lambda *_: (kind, layer, 0, 0, 0), pipeline_mode=pl.Buffered(1))


def _mla_proj_kernel(x_ref, m_ref, pos_ref, freq_ref, w_in_ref, vec_ref, w_qb_ref, w_kvb_ref,
                     ca_ref, cb_ref, cc_ref, q_ref, kn_ref, kr_ref, v_ref, cab_ref, cbb_ref, ccb_ref):
    _cast_slices((ca_ref, cb_ref, cc_ref), (cab_ref, cbb_ref, ccb_ref))
    qn_ref = vec_ref.at[VEC_Q_NORM:VEC_Q_NORM + 1, 0:Q_LORA]
    kvn_ref = vec_ref.at[VEC_KV_NORM:VEC_KV_NORM + 1, 0:KV_LORA]
    b = pl.program_id(0)
    shift = m_ref[SHIFT, pl.ds(b, 1), :]
    scale = m_ref[SCALE, pl.ds(b, 1), :]

    def rms(v, g):
        return v * lax.rsqrt(jnp.mean(v * v, axis=-1, keepdims=True) + RMS_EPS) * g

    eye = lax.broadcasted_iota(jnp.int32, (128, 128), 0) == lax.broadcasted_iota(jnp.int32, (128, 128), 1)
    h = (x_ref[0] * (1.0 + scale) + shift).astype(BF16)
    z = _bdot(h, w_in_ref[...])
    cq = rms(z[:, :Q_LORA], qn_ref[...]).astype(BF16)
    ckv = rms(z[:, Q_LORA:Q_LORA + KV_LORA], kvn_ref[...]).astype(BF16)

    pos = pos_ref[0].astype(F32)
    pos_col = jnp.concatenate(
        [jnp.sum(jnp.where(eye, jnp.broadcast_to(pos[i:i + 1, :], (128, 128)), 0.0), axis=1, keepdims=True)
         for i in range(PROJ_ROWS // 128)], axis=0)
    ang = pos_col * freq_ref[...]
    lane = lax.broadcasted_iota(jnp.int32, ang.shape, 1)
    mult = jnp.where(lane < QK_ROPE, jnp.cos(ang), jnp.sin(ang))

    def rope(slab):
        s = slab * mult
        return s + pltpu.roll(s, QK_ROPE, axis=1)

    kr_ref[0] = jnp.where(lane < QK_ROPE, rope(z[:, Q_LORA + KV_LORA:]), 0.0).astype(BF16)

    q_all = _bdot(cq, w_qb_ref[...])
    kv = _bdot(ckv, w_kvb_ref[...])
    for hd in range(HEADS):
        c0 = hd * HEAD_W
        q_ref[0, hd, :, 0:QK_NOPE] = q_all[:, c0:c0 + QK_NOPE].astype(BF16)
        q_ref[0, hd, :, QK_NOPE:HEAD_W] = rope(q_all[:, c0 + QK_NOPE:c0 + HEAD_W]).astype(BF16)
        kn_ref[0, hd] = kv[:, c0:c0 + QK_NOPE].astype(BF16)
        v_ref[0, hd] = kv[:, c0 + QK_NOPE:c0 + HEAD_W].astype(BF16)


def _mla_proj(x, m, pos, freq, w_in, vec, w_qb, w_kvb, casts):
    nt = SEQ // PROJ_ROWS
    blk = PROJ_ROWS
    grid = (NB, nt)
    cast = [_cast_specs(w.shape, grid, layer) for w, layer in casts]
    return pl.pallas_call(
        _mla_proj_kernel,
        out_shape=[
            jax.ShapeDtypeStruct((NB, HEADS, SEQ, HEAD_W), BF16),
            jax.ShapeDtypeStruct((NB, HEADS, SEQ, QK_NOPE), BF16),
            jax.ShapeDtypeStruct((NB, SEQ, HEAD_W - QK_NOPE), BF16),
            jax.ShapeDtypeStruct((NB, HEADS, SEQ, V_DIM), BF16),
        ] + [c[2] for c in cast],
        grid=grid,
        in_specs=[
            pl.BlockSpec((1, blk, D), lambda b, i: (b, i, 0)),
            _mod_spec(MIX, 0),
            pl.BlockSpec((1, blk // 128, 128), lambda b, i: (b, i, 0)),
            _const_spec((1, 128)),
            _const_spec(w_in.shape),
            _const_spec(vec.shape),
            _const_spec(w_qb.shape),
            _const_spec(w_kvb.shape),
        ] + [c[0] for c in cast],
        out_specs=[
            pl.BlockSpec((1, HEADS, blk, HEAD_W), lambda b, i: (b, 0, i, 0)),
            pl.BlockSpec((1, HEADS, blk, QK_NOPE), lambda b, i: (b, 0, i, 0)),
            pl.BlockSpec((1, blk, HEAD_W - QK_NOPE), lambda b, i: (b, i, 0)),
            pl.BlockSpec((1, HEADS, blk, V_DIM), lambda b, i: (b, 0, i, 0)),
        ] + [c[1] for c in cast],
        compiler_params=_cparams("arbitrary", "arbitrary"),
        name="mla_projections",
    )(x, m, pos, freq, w_in, vec, w_qb, w_kvb, *[w for w, _ in casts])


def _lane_groups(v):
    return [v[:, c:c + 128] for c in range(0, v.shape[1], 128)]


def _attn_kernel(q_ref, kn_ref, kr_ref, v_ref, o_ref, s_ref):
    blk = ATTN_BLK
    half = blk // 2
    nq = SEQ // blk
    tri = lax.broadcasted_iota(jnp.int32, (half, half), 1) <= lax.broadcasted_iota(jnp.int32, (half, half), 0)

    def rows(n):
        return slice(n * blk, (n + 1) * blk)

    def slot(hd, n, j):
        return hd * _ATTN_SLOTS + (0 if n % 2 == 0 else nq - 1) + j

    def nt_dot(a, b):
        return lax.dot_general(a, b, (((1,), (1,)), ((), ())), preferred_element_type=F32)

    def keys(hd, j):
        return jnp.concatenate([kn_ref[hd, rows(j), :], kr_ref[rows(j), :]], axis=1)

    def lane_max(vals):
        return functools.reduce(jnp.maximum, vals)

    def pass1(hd, n):
        q = q_ref[hd, rows(n), :]
        m = []
        for j in range(n):
            s = nt_dot(q, keys(hd, j))
            s_ref[slot(hd, n, j)] = s
            m = [lane_max(_lane_groups(s) + m)]
        k = keys(hd, n)
        s_top = jnp.where(tri, nt_dot(q[:half], k[:half]), NEG_BIG)
        s_bot = nt_dot(q[half:], k)
        s_bot = jnp.concatenate([s_bot[:, :half], jnp.where(tri, s_bot[:, half:], NEG_BIG)], axis=1)
        s_ref[slot(hd, n, n), 0:half, 0:half] = s_top
        s_ref[slot(hd, n, n), half:blk, :] = s_bot
        m_top = lane_max(_lane_groups(s_top) + [v[:half] for v in m])
        m_bot = lane_max(_lane_groups(s_bot) + [v[half:] for v in m])
        m = jnp.concatenate([m_top, m_bot], axis=0)
        return jnp.broadcast_to(jnp.max(m, axis=-1, keepdims=True), (blk, 128))

    def pass2(hd, n, mb):
        diag = slot(hd, n, n)
        ps = [jnp.exp2(sg - mb) for j in range(n) for sg in _lane_groups(s_ref[slot(hd, n, j)])]
        p_top = [jnp.exp2(sg - mb[:half]) for sg in _lane_groups(s_ref[diag, 0:half, 0:half])]
        p_bot = [jnp.exp2(sg - mb[half:]) for sg in _lane_groups(s_ref[diag, half:blk, :])]
        l_top = functools.reduce(jnp.add, p_top + [p[:half] for p in ps])
        l_bot = functools.reduce(jnp.add, p_bot + [p[half:] for p in ps])
        l = jnp.sum(jnp.concatenate([l_top, l_bot], axis=0), axis=-1, keepdims=True)
        v0 = n * blk
        acc = jnp.concatenate([
            _bdot(jnp.concatenate(p_top, axis=1).astype(BF16), v_ref[hd, v0:v0 + half, :]),
            _bdot(jnp.concatenate(p_bot, axis=1).astype(BF16), v_ref[hd, v0:v0 + blk, :])], axis=0)
        if n > 0:
            acc = acc + _bdot(jnp.concatenate(ps, axis=1).astype(BF16), v_ref[hd, 0:v0, :])
        o_ref[hd, rows(n), :] = (acc / l).astype(BF16)

    mb = [pass1(hd, 0) for hd in range(ATTN_HEADS)]
    for n in range(nq):
        for hd in range(ATTN_HEADS):
            mb_next = pass1(hd, n + 1) if n + 1 < nq else None
            pass2(hd, n, mb[hd])
            mb[hd] = mb_next


def _attention(q, kn, kr, v):
    head_spec = lambda w: pl.BlockSpec((None, ATTN_HEADS, SEQ, w), lambda b, h: (b, h, 0, 0))
    kr_spec = pl.BlockSpec((None, SEQ, HEAD_W - QK_NOPE), lambda b, h: (b, 0, 0))
    return pl.pallas_call(
        _attn_kernel,
        out_shape=jax.ShapeDtypeStruct((NB, HEADS, SEQ, V_DIM), BF16),
        grid=(NB, HEADS // ATTN_HEADS),
        in_specs=[head_spec(HEAD_W), head_spec(QK_NOPE), kr_spec, head_spec(V_DIM)],
        out_specs=head_spec(V_DIM),
        scratch_shapes=[pltpu.VMEM((ATTN_HEADS * _ATTN_SLOTS, ATTN_BLK, ATTN_BLK), F32)],
        compiler_params=_cparams("arbitrary", "arbitrary"),
        name="mla_attention",
    )(q, kn, kr, v)


def _vec_row(vec_ref, row):
    return vec_ref[row:row + 1, :]


def _ffn_rows(x, shift, scale, gate, w1_ref, w2_ref, vec_ref, layer):
    h = (x * (1.0 + scale) + shift).astype(BF16)
    a2 = []
    for c in range(D_FF // FF_CHUNK):
        cols = slice(c * FF_CHUNK, (c + 1) * FF_CHUNK)
        b1 = _vec_row(vec_ref, VEC_MLP_B1 + layer * (D_FF // FF_CHUNK) + c)
        a = jnp.maximum(_bdot(h, w1_ref[:, cols]) + b1, 0.0)
        a2.append((a * a).astype(BF16))
    y = _bdot(jnp.concatenate(a2, axis=1), w2_ref[...]) + _vec_row(vec_ref, VEC_MLP_B2 + layer)
    return _layer_norm(ALPHA * x + (1.0 + gate) * y,
                       _vec_row(vec_ref, VEC_LN_FFN_G + layer), _vec_row(vec_ref, VEC_LN_FFN_B + layer))


def _batch_mod(m_ref):
    b = pl.program_id(0)
    return [m_ref[i, pl.ds(b, 1), :] for i in (SHIFT, SCALE, GATE)]


def _ffn_kernel(x_ref, m_ref, w1_ref, w2_ref, vec_ref, y_ref, *, layer):
    mod = _batch_mod(m_ref)
    for sb in range(FFN_SUB):
        rows = slice(sb * FFN_ROWS, (sb + 1) * FFN_ROWS)
        y_ref[rows, :] = _ffn_rows(x_ref[rows, :], *mod, w1_ref, w2_ref, vec_ref, layer)


def _attn_out_ffn_kernel(o_ref, x_ref, mm_ref, mf_ref, wo_ref, w1_ref, w2_ref, vec_ref, sa_ref, sb_ref, sc_ref,
                         y_ref, sab_ref, sbb_ref, scb_ref, *, layer):
    _cast_slices((sa_ref, sb_ref, sc_ref), (sab_ref, sbb_ref, scb_ref))
    gm_ref = vec_ref.at[VEC_LN_MIX_G + layer:VEC_LN_MIX_G + layer + 1]
    bm_ref = vec_ref.at[VEC_LN_MIX_B + layer:VEC_LN_MIX_B + layer + 1]
    gate_mix = _batch_mod(mm_ref)[GATE]
    mod = _batch_mod(mf_ref)
    for sb in range(FFN_SUB):
        rows = slice(sb * FFN_ROWS, (sb + 1) * FFN_ROWS)
        o = jnp.concatenate([o_ref[hd, rows, :] for hd in range(HEADS)], axis=1)
        y_ref[rows, :] = _layer_norm(ALPHA * x_ref[rows, :] + (1.0 + gate_mix) * _bdot(o, wo_ref[...]),
                                     gm_ref[...], bm_ref[...])
    for sb in range(FFN_SUB):
        rows = slice(sb * FFN_ROWS, (sb + 1) * FFN_ROWS)
        y_ref[rows, :] = _ffn_rows(y_ref[rows, :], *mod, w1_ref, w2_ref, vec_ref, layer)


_FFN_X_SPEC = pl.BlockSpec((None, FFN_SUB * FFN_ROWS, D), lambda b, i: (b, i, 0))
_FFN_GRID = (NB, SEQ // (FFN_SUB * FFN_ROWS))


def _layer_spec(shape, layer):
    return pl.BlockSpec((None,) + tuple(shape[1:]), lambda *_: (layer, 0, 0), pipeline_mode=pl.Buffered(1))


def _ffn(x, m, layer, w1, w2, vec):
    return pl.pallas_call(
        functools.partial(_ffn_kernel, layer=layer),
        out_shape=jax.ShapeDtypeStruct((NB, SEQ, D), F32),
        grid=_FFN_GRID,
        in_specs=[_FFN_X_SPEC, _mod_spec(FFN, layer), _layer_spec(w1.shape, 0), _layer_spec(w2.shape, 0),
                  _const_spec(vec.shape)],
        out_specs=_FFN_X_SPEC,
        compiler_params=_cparams("arbitrary", "arbitrary"),
        name="ffn_norm",
    )(x, m, w1, w2, vec)


def _attn_out_ffn(o, x, m, layer, w_o, w1, w2, vec, casts):
    o_spec = pl.BlockSpec((None, HEADS, FFN_SUB * FFN_ROWS, V_DIM), lambda b, i: (b, 0, i, 0))
    cast = [_cast_specs(w.shape, _FFN_GRID, l) for w, l in casts]
    return pl.pallas_call(
        functools.partial(_attn_out_ffn_kernel, layer=layer),
        out_shape=[jax.ShapeDtypeStruct((NB, SEQ, D), F32)] + [c[2] for c in cast],
        grid=_FFN_GRID,
        in_specs=[o_spec, _FFN_X_SPEC, _mod_spec(MIX, layer), _mod_spec(FFN, layer), _layer_spec(w_o.shape, 0),
                  _layer_spec(w1.shape, 0), _layer_spec(w2.shape, 0), _const_spec(vec.shape)]
                 + [c[0] for c in cast],
        out_specs=[_FFN_X_SPEC] + [c[1] for c in cast],
        compiler_params=_cparams("arbitrary", "arbitrary"),
        name="attn_out_ffn_norm",
    )(o, x, m, m, w_o, w1, w2, vec, *[w for w, _ in casts])


def _cmul(ar, ai, br, bi):
    return ar * br - ai * bi, ar * bi + ai * br


def _pair_slot(piece, zero, slot):
    return [piece, zero] if slot == 0 else [zero, piece]


def _s5_disc_kernel(ldt_ref, are_ref, aim_ref, bre_ref, bim_ref, cre_ref, cim_ref, d_ref,
                    a8r_ref, a8i_ref, wb_ref, wck_ref):
    lr = are_ref[...]
    li = aim_ref[...]
    dt = jnp.exp(ldt_ref[...])
    mag = jnp.exp(lr * dt)
    ab_re = mag * jnp.cos(li * dt)
    ab_im = mag * jnp.sin(li * dt)
    den = lr * lr + li * li
    nr = ab_re - 1.0
    coef_re = (nr * lr + ab_im * li) / den
    coef_im = (ab_im * lr - nr * li) / den
    bb_re, bb_im = _cmul(coef_re, coef_im, bre_ref[...], bim_ref[...])
    c_re = cre_ref[...]
    c_im = cim_ref[...]

    pw = [(jnp.ones_like(ab_re), jnp.zeros_like(ab_im))]
    for _ in range(CHUNK):
        pw.append(_cmul(pw[-1][0], pw[-1][1], ab_re, ab_im))
    a8r_ref[...] = pw[CHUNK][0]
    a8i_ref[...] = pw[CHUNK][1]

    n_pairs = wb_ref.shape[0]

    def pairs(v):
        v4 = v.reshape(n_pairs, 2, v.shape[1], v.shape[2])
        return v4[:, 0], v4[:, 1]

    z64 = jnp.zeros((n_pairs, GROUP_CH, STATE), F32)
    z16 = jnp.zeros((n_pairs, GROUP_CH, GROUP_CH), F32)
    eye = (lax.broadcasted_iota(jnp.int32, (GROUP_CH, GROUP_CH), 0)
           == lax.broadcasted_iota(jnp.int32, (GROUP_CH, GROUP_CH), 1)).astype(F32)

    ab_l = [_cmul(pw[l][0], pw[l][1], bb_re, bb_im) for l in range(CHUNK)]
    wb_rows = []
    for k in range(CHUNK):
        m_re, m_im = ab_l[CHUNK - 1 - k]
        for slot in range(2):
            wb_rows.append(jnp.concatenate(_pair_slot(pairs(m_re)[slot], z64, slot)
                                           + _pair_slot(pairs(m_im)[slot], z64, slot), axis=2))
    wb_ref[...] = jnp.concatenate(wb_rows, axis=1).astype(BF16)

    def nt(a, b):
        return lax.dot_general(a.astype(BF16), b.astype(BF16), (((2,), (2,)), ((0,), (0,))),
                               preferred_element_type=F32)

    k_l = [nt(c_re, ab_l[l][0]) - nt(c_im, ab_l[l][1]) for l in range(CHUNK)]
    k_l[0] = k_l[0] + d_ref[...] * eye[None]

    wck_rows = []
    for kp in range(CHUNK):
        ca_re, ca_im = _cmul(c_re, c_im, pw[kp + 1][0], pw[kp + 1][1])
        for slot in range(2):
            state_cols = (_pair_slot(pairs(ca_re)[slot], z64, slot)
                          + _pair_slot(-pairs(ca_im)[slot], z64, slot))
            direct_cols = []
            for k in range(CHUNK):
                blk = pairs(k_l[kp - k])[slot] if k <= kp else z16
                direct_cols += _pair_slot(blk, z16, slot)
            wck_rows.append(jnp.concatenate(state_cols + direct_cols, axis=2))
    wck_ref[...] = jnp.concatenate(wck_rows, axis=1).astype(BF16)


def _s5_discretise(log_dt, a_re, a_im, b_re, b_im, c_re, c_im, d_skip):
    g3 = (N_GROUPS, 1, STATE)
    gb = DISC_GROUPS
    spec = lambda *tail: pl.BlockSpec((gb,) + tail, lambda i: (i,) + (0,) * len(tail))
    pair_spec = lambda cols: pl.BlockSpec((gb // 2, PAIR_W, cols), lambda i: (i, 0, 0))
    big = spec(GROUP_CH, STATE)
    return pl.pallas_call(
        _s5_disc_kernel,
        out_shape=(jax.ShapeDtypeStruct(g3, F32), jax.ShapeDtypeStruct(g3, F32),
                   jax.ShapeDtypeStruct((N_PAIRS, PAIR_W, PAIR_W), BF16),
                   jax.ShapeDtypeStruct((N_PAIRS, PAIR_W, 2 * PAIR_W), BF16)),
        grid=(N_GROUPS // gb,),
        in_specs=[spec(1, 1), spec(1, STATE), spec(1, STATE), big, big, big, big, spec(GROUP_CH, 1)],
        out_specs=(spec(1, STATE), spec(1, STATE), pair_spec(PAIR_W), pair_spec(2 * PAIR_W)),
        compiler_params=_cparams("arbitrary"),
        name="s5_discretise",
    )(log_dt.reshape(N_GROUPS, 1, 1), a_re.reshape(g3), a_im.reshape(g3),
      jnp.swapaxes(b_re, 1, 2), jnp.swapaxes(b_im, 1, 2), c_re, c_im, d_skip.reshape(N_GROUPS, GROUP_CH, 1))


def _s5_x_spec(block):
    return pl.BlockSpec((NB, block, D), lambda i: (0, i, 0))


def _s5_u8_spec(block):
    return pl.BlockSpec((block // CHUNK * NB, N_PAIRS * PAIR_W), lambda i: (i, 0))


def _s5_in_kernel(x_ref, m_ref, w_in_ref, u8_ref):
    shift = m_ref[SHIFT]
    scale = m_ref[SCALE]
    steps = S5_IN_STEPS
    chunks = steps // CHUNK
    for sb in range(S5_IN_BLOCK // steps):
        x3 = jnp.swapaxes(x_ref[:, sb * steps:(sb + 1) * steps, :], 0, 1)
        h = (x3 * (1.0 + scale)[None] + shift[None]).reshape(steps * NB, D).astype(BF16)
        u = _bdot(h, w_in_ref[...])
        u4 = u.reshape(chunks, CHUNK, NB, D)
        per_k = [u4[:, k].reshape(chunks * NB, D) for k in range(CHUNK)]
        cols = [per_k[k][:, j * PAIR_CH:(j + 1) * PAIR_CH] for j in range(N_PAIRS) for k in range(CHUNK)]
        u8_ref[sb * chunks * NB:(sb + 1) * chunks * NB, :] = jnp.concatenate(cols, axis=1).astype(BF16)


def _s5_in(x, m, w_in):
    return pl.pallas_call(
        _s5_in_kernel,
        out_shape=jax.ShapeDtypeStruct((SEQ // CHUNK * NB, N_PAIRS * PAIR_W), BF16),
        grid=(SEQ // S5_IN_BLOCK,),
        in_specs=[
            _s5_x_spec(S5_IN_BLOCK),
            _mod_spec(MIX, 1),
            _layer_spec(w_in.shape, 0),
        ],
        out_specs=_s5_u8_spec(S5_IN_BLOCK),
        compiler_params=_cparams("arbitrary"),
        name="s5_in_proj",
    )(x, m, w_in)


def _s5_core_kernel(u8_ref, wb_ref, wck_ref, a8r_ref, a8i_ref, y8_ref):
    half = PAIR_W // 2
    for jj in range(PAIRS_PER_STEP):
        lanes = slice(jj * PAIR_W, (jj + 1) * PAIR_W)
        u8 = u8_ref[:, lanes]
        v = _bdot(u8, wb_ref[jj])
        pair = lambda ref: jnp.concatenate([ref[2 * jj], ref[2 * jj + 1]], axis=1)
        ar = jnp.broadcast_to(pair(a8r_ref), (NB, half))
        ai = jnp.broadcast_to(pair(a8i_ref), (NB, half))
        xr = jnp.zeros((NB, half), F32)
        xi = jnp.zeros((NB, half), F32)
        prev_r, prev_i = [], []
        for s in range(SEQ // CHUNK):
            prev_r.append(xr)
            prev_i.append(xi)
            rows = slice(s * NB, (s + 1) * NB)
            xr, xi = ar * xr - ai * xi + v[rows, :half], ar * xi + ai * xr + v[rows, half:]
        x_prev = jnp.concatenate([jnp.concatenate(prev_r, axis=0), jnp.concatenate(prev_i, axis=0)], axis=1)
        lhs = jnp.concatenate([x_prev.astype(BF16), u8], axis=1)
        y8_ref[:, lanes] = lax.dot_general(lhs, wck_ref[jj], (((1,), (1,)), ((), ())), preferred_element_type=F32)


def _s5_core(u8, wb, wck, a8r, a8i):
    n_rows = u8.shape[0]
    w = PAIRS_PER_STEP * PAIR_W
    return pl.pallas_call(
        _s5_core_kernel,
        out_shape=jax.ShapeDtypeStruct(u8.shape, F32),
        grid=(N_PAIRS // PAIRS_PER_STEP,),
        in_specs=[
            pl.BlockSpec((n_rows, w), lambda i: (0, i)),
            pl.BlockSpec((PAIRS_PER_STEP, PAIR_W, PAIR_W), lambda i: (i, 0, 0)),
            pl.BlockSpec((PAIRS_PER_STEP, PAIR_W, 2 * PAIR_W), lambda i: (i, 0, 0)),
            pl.BlockSpec((2 * PAIRS_PER_STEP, 1, STATE), lambda i: (i, 0, 0)),
            pl.BlockSpec((2 * PAIRS_PER_STEP, 1, STATE), lambda i: (i, 0, 0)),
        ],
        out_specs=pl.BlockSpec((n_rows, w), lambda i: (0, i)),
        compiler_params=_cparams("arbitrary"),
        name="s5_recurrence",
    )(u8, wb, wck, a8r, a8i)


def _s5_out_kernel(y8_ref, x_ref, m_ref, w_glu_ref, w_out_ref, vec_ref, ca_ref, cb_ref, o_ref, cab_ref, cbb_ref):
    _cast_slices((ca_ref, cb_ref), (cab_ref, cbb_ref))
    b_glu_ref = vec_ref.at[VEC_B_GLU:VEC_B_GLU + 1]
    g_ref = vec_ref.at[VEC_LN_MIX_G + 1:VEC_LN_MIX_G + 2]
    b_ref = vec_ref.at[VEC_LN_MIX_B + 1:VEC_LN_MIX_B + 2]
    gate = m_ref[GATE]
    steps = S5_OUT_STEPS
    chunks = steps // CHUNK
    for sb in range(S5_OUT_BLOCK // steps):
        y8 = y8_ref[sb * chunks * NB:(sb + 1) * chunks * NB, :]
        per_k = [jnp.concatenate([y8[:, j * PAIR_W + k * PAIR_CH:j * PAIR_W + (k + 1) * PAIR_CH]
                                  for j in range(N_PAIRS)], axis=1) for k in range(CHUNK)]
        y = jnp.stack([p.reshape(chunks, NB, D) for p in per_k], axis=1).reshape(steps * NB, D)

        half_y = 0.5 * y
        g = half_y + half_y * jnp.tanh(y * (GELU_C + (GELU_C * 0.044715) * (y * y)))
        zz = g * jax.nn.sigmoid(_bdot(g.astype(BF16), w_glu_ref[...]) + b_glu_ref[...])
        out = jnp.swapaxes(_bdot(zz.astype(BF16), w_out_ref[...]).reshape(steps, NB, D), 0, 1)
        t_rows = slice(sb * steps, (sb + 1) * steps)
        res = ALPHA * x_ref[:, t_rows, :] + (1.0 + gate)[:, None, :] * out
        o_ref[:, t_rows, :] = _layer_norm(res, g_ref[...], b_ref[...])


def _s5_out(y8, x, m, w_glu, w_out, vec, casts):
    x_spec = _s5_x_spec(S5_OUT_BLOCK)
    grid = (SEQ // S5_OUT_BLOCK,)
    cast = [_cast_specs(w.shape, grid, l) for w, l in casts]
    return pl.pallas_call(
        _s5_out_kernel,
        out_shape=[jax.ShapeDtypeStruct((NB, SEQ, D), F32)] + [c[2] for c in cast],
        grid=grid,
        in_specs=[
            _s5_u8_spec(S5_OUT_BLOCK),
            x_spec,
            _mod_spec(MIX, 1),
            _layer_spec(w_glu.shape, 0),
            _layer_spec(w_out.shape, 0),
            _const_spec(vec.shape),
        ] + [c[0] for c in cast],
        out_specs=[x_spec] + [c[1] for c in cast],
        compiler_params=_cparams("arbitrary"),
        name="s5_out_norm",
    )(y8, x, m, w_glu, w_out, vec, *[w for w, _ in casts])


def _rotate_half_cols(w):
    half = w.shape[-1] // 2
    return jnp.concatenate([-w[..., half:], w[..., :half]], axis=-1)


def _mla_weights(w_in, w_qb):
    k_pe = w_in[:, Q_LORA + KV_LORA:]
    w_in_ext = jnp.concatenate([w_in, _rotate_half_cols(k_pe)], axis=1)
    wq = w_qb.reshape(Q_LORA, HEADS, QK_NOPE + QK_ROPE) * Q_SCALE
    wq_ext = jnp.concatenate([wq, _rotate_half_cols(wq[..., QK_NOPE:])], axis=-1)
    return w_in_ext.astype(BF16), wq_ext.reshape(Q_LORA, HEADS * HEAD_W).astype(BF16)


def kernel(x, c, positions, mla_w_in, mla_q_norm, mla_w_qb, mla_kv_norm, mla_w_kvb, mla_w_o, ssm_w_in, ssm_log_dt, ssm_a_re, ssm_a_im, ssm_b_re, ssm_b_im, ssm_c_re, ssm_c_im, ssm_d, ssm_w_glu, ssm_b_glu, ssm_w_out, mlp_w1, mlp_b1, mlp_w2, mlp_b2, mod_mix_w, mod_mix_b, mod_ffn_w, mod_ffn_b, ln_mix_g, ln_mix_b, ln_ffn_g, ln_ffn_b):
    vec = _pack_vectors(ln_mix_g, ln_mix_b, ln_ffn_g, ln_ffn_b, mlp_b2, ssm_b_glu, mlp_b1, mod_mix_b, mod_ffn_b,
                        mla_q_norm, mla_kv_norm)
    m = _modulation(c, mod_mix_w, mod_ffn_w, vec)

    inv_freq = ROPE_THETA ** (-jnp.arange(0, QK_ROPE, 2, dtype=F32) / QK_ROPE)
    freq = jnp.tile(inv_freq, 4).reshape(1, 128)
    w_in_ext, w_qb_ext = _mla_weights(mla_w_in[0], mla_w_qb[0])
    q, kn, kr, v, w1_0, w2_0, w_o = _mla_proj(
        x, m, positions.reshape(NB, SEQ // 128, 128), freq, w_in_ext, vec, w_qb_ext, mla_w_kvb[0].astype(BF16),
        ((mlp_w1, 0), (mlp_w2, 0), (mla_w_o, 0)))
    o = _attention(q, kn, kr, v)
    x2, s_w_in, s_w_glu, s_w_out = _attn_out_ffn(
        o, x, m, 0, w_o, w1_0, w2_0, vec, ((ssm_w_in, 0), (ssm_w_glu, 0), (ssm_w_out, 0)))

    a8r, a8i, wb, wck = _s5_discretise(ssm_log_dt[0], ssm_a_re[0], ssm_a_im[0], ssm_b_re[0], ssm_b_im[0],
                                       ssm_c_re[0], ssm_c_im[0], ssm_d[0])
    u8 = _s5_in(x2, m, s_w_in)
    y8 = _s5_core(u8, wb, wck, a8r, a8i)
    x3, w1_1, w2_1 = _s5_out(y8, x2, m, s_w_glu, s_w_out, vec, ((mlp_w1, 1), (mlp_w2, 1)))
    return _ffn(x3, m, 1, w1_1, w2_1, vec)
```

```python
import functools
import math

import jax
import jax.numpy as jnp
from jax import lax
from jax.experimental import pallas as pl
from jax.experimental.pallas import tpu as pltpu

F32 = jnp.float32
BF16 = jnp.bfloat16

D = 1024
NB = 8
SEQ = 2048
HEADS = 8
QK_NOPE = 128
QK_ROPE = 64
V_DIM = 128
Q_LORA = 256
KV_LORA = 128
ROPE_THETA = 10000.0
GROUP_CH = 16
N_GROUPS = 64
STATE = 64
D_FF = 4 * D
DEPTH = 2
ALPHA = (2 * DEPTH) ** 0.25
LN_EPS = 1e-5
RMS_EPS = 1e-6
Q_SCALE = math.log2(math.e) / math.sqrt(QK_NOPE + QK_ROPE)
NEG_BIG = -1e30
GELU_C = math.sqrt(2.0 / math.pi)

HEAD_W = 256
CHUNK = 8
N_PAIRS = N_GROUPS // 2
PAIR_CH = 2 * GROUP_CH
PAIR_W = CHUNK * PAIR_CH
assert PAIR_W == 4 * STATE

VMEM_LIMIT = 56 * 1024 * 1024
PROJ_ROWS = 1024
ATTN_BLK = 512
ATTN_HEADS = 4
_ATTN_SLOTS = 2 * (SEQ // ATTN_BLK) - 1
FFN_ROWS = 256
FFN_SUB = 4
FF_CHUNK = 1024
S5_IN_BLOCK, S5_IN_STEPS = 256, 128
S5_OUT_BLOCK, S5_OUT_STEPS = 128, 32
PAIRS_PER_STEP = 4
DISC_GROUPS = 16


VEC_LN_MIX_G, VEC_LN_MIX_B, VEC_LN_FFN_G, VEC_LN_FFN_B, VEC_MLP_B2 = 0, DEPTH, 2 * DEPTH, 3 * DEPTH, 4 * DEPTH
VEC_B_GLU = 5 * DEPTH
VEC_MLP_B1 = VEC_B_GLU + 1
VEC_MOD_B = VEC_MLP_B1 + DEPTH * (D_FF // D)
VEC_Q_NORM = VEC_MOD_B + 2 * 3 * DEPTH
VEC_KV_NORM = VEC_Q_NORM + 1
VEC_ROWS = -(-(VEC_KV_NORM + 1) // 8) * 8
assert FF_CHUNK == D


def _pack_vectors(ln_mix_g, ln_mix_b, ln_ffn_g, ln_ffn_b, mlp_b2, b_glu, mlp_b1, mod_mix_b, mod_ffn_b, q_norm, kv_norm):
    pad = lambda v: jnp.pad(v.reshape(1, -1), ((0, 0), (0, D - v.size)))
    rows = [ln_mix_g, ln_mix_b, ln_ffn_g, ln_ffn_b, mlp_b2, b_glu.reshape(1, D), mlp_b1.reshape(-1, D),
            mod_mix_b.reshape(-1, D), mod_ffn_b.reshape(-1, D), pad(q_norm), pad(kv_norm)]
    table = jnp.concatenate(rows, axis=0)
    return jnp.pad(table, ((0, VEC_ROWS - table.shape[0]), (0, 0)))


def _cparams(*sem):
    return pltpu.CompilerParams(dimension_semantics=sem, vmem_limit_bytes=VMEM_LIMIT)


def _const_spec(shape):
    nd = len(shape)
    return pl.BlockSpec(shape, lambda *_: (0,) * nd, pipeline_mode=pl.Buffered(1))


def _cast_specs(shape, grid, layer):
    _, rows, cols = shape
    n_steps = math.prod(grid)

    def step(*ids):
        s = 0
        for i, n in zip(ids, grid):
            s = s * n + i
        return s

    block = (1, rows // n_steps, cols)
    return (pl.BlockSpec(block, lambda *ids: (layer, step(*ids), 0)),
            pl.BlockSpec(block, lambda *ids: (0, step(*ids), 0)),
            jax.ShapeDtypeStruct((1, rows, cols), BF16))


def _cast_slices(src_refs, dst_refs):
    for src, dst in zip(src_refs, dst_refs):
        dst[...] = src[...].astype(dst.dtype)


def _layer_norm(v, g, b):
    mu = jnp.mean(v, axis=-1, keepdims=True)
    vc = v - mu
    var = jnp.mean(vc * vc, axis=-1, keepdims=True)
    return vc * lax.rsqrt(var + LN_EPS) * g + b


def _bdot(a, b):
    return jnp.dot(a, b, preferred_element_type=F32)


MIX, FFN = 0, 1
SHIFT, SCALE, GATE = 0, 1, 2


def _mod_kernel(c_ref, w_mix_ref, w_ffn_ref, vec_ref, o_ref):
    kind, layer, part = pl.program_id(0), pl.program_id(1), pl.program_id(2)
    c = c_ref[...]
    cs = (c * jax.nn.sigmoid(c)).astype(BF16)
    bias = vec_ref[pl.ds(VEC_MOD_B + 3 * (DEPTH * kind + layer) + part, 1), :]
    for k, w_ref in ((MIX, w_mix_ref), (FFN, w_ffn_ref)):
        @pl.when(kind == k)
        def _():
            o_ref[...] = _bdot(cs, w_ref[...].astype(BF16)) + bias


def _modulation(c, w_mix, w_ffn, vec):
    def w_spec(k):
        idle = ((1 - k) * (DEPTH - 1), 2 * (1 - k))
        return pl.BlockSpec((None, D, D), lambda kind, l, j: (jnp.where(kind == k, l, idle[0]), 0,
                                                            jnp.where(kind == k, j, idle[1])))
    return pl.pallas_call(
        _mod_kernel,
        out_shape=jax.ShapeDtypeStruct((2, DEPTH, 3, NB, D), F32),
        grid=(2, DEPTH, 3),
        in_specs=[
            pl.BlockSpec((NB, D), lambda kind, l, j: (0, 0)),
            w_spec(MIX),
            w_spec(FFN),
            _const_spec(vec.shape),
        ],
        out_specs=pl.BlockSpec((None, None, None, NB, D), lambda kind, l, j: (kind, l, j, 0, 0)),
        compiler_params=_cparams("arbitrary", "arbitrary", "arbitrary"),
        name="adaln_modulation",
    )(c, w_mix, w_ffn, vec)


def _mod_spec(kind, layer):
    return pl.BlockSpec((None, None, 3, NB, D), lambda *_: (kind, layer, 0, 0, 0), pipeline_mode=pl.Buffered(1))


def _mla_proj_kernel(x_ref, m_ref, pos_ref, freq_ref, w_in_ref, vec_ref, w_qb_ref, w_kvb_ref,
                     ca_ref, cb_ref, cc_ref, q_ref, kn_ref, kr_ref, v_ref, cab_ref, cbb_ref, ccb_ref):
    _cast_slices((ca_ref, cb_ref, cc_ref), (cab_ref, cbb_ref, ccb_ref))
    qn_ref = vec_ref.at[VEC_Q_NORM:VEC_Q_NORM + 1, 0:Q_LORA]
    kvn_ref = vec_ref.at[VEC_KV_NORM:VEC_KV_NORM + 1, 0:KV_LORA]
    b = pl.program_id(0)
    shift = m_ref[SHIFT, pl.ds(b, 1), :]
    scale = m_ref[SCALE, pl.ds(b, 1), :]

    def rms(v, g):
        return v * lax.rsqrt(jnp.mean(v * v, axis=-1, keepdims=True) + RMS_EPS) * g

    eye = lax.broadcasted_iota(jnp.int32, (128, 128), 0) == lax.broadcasted_iota(jnp.int32, (128, 128), 1)
    h = (x_ref[0] * (1.0 + scale) + shift).astype(BF16)
    z = _bdot(h, w_in_ref[...])
    cq = rms(z[:, :Q_LORA], qn_ref[...]).astype(BF16)
    ckv = rms(z[:, Q_LORA:Q_LORA + KV_LORA], kvn_ref[...]).astype(BF16)

    pos = pos_ref[0].astype(F32)
    pos_col = jnp.concatenate(
        [jnp.sum(jnp.where(eye, jnp.broadcast_to(pos[i:i + 1, :], (128, 128)), 0.0), axis=1, keepdims=True)
         for i in range(PROJ_ROWS // 128)], axis=0)
    ang = pos_col * freq_ref[...]
    lane = lax.broadcasted_iota(jnp.int32, ang.shape, 1)
    mult = jnp.where(lane < QK_ROPE, jnp.cos(ang), jnp.sin(ang))

    def rope(slab):
        s = slab * mult
        return s + pltpu.roll(s, QK_ROPE, axis=1)

    kr_ref[0] = jnp.where(lane < QK_ROPE, rope(z[:, Q_LORA + KV_LORA:]), 0.0).astype(BF16)

    q_all = _bdot(cq, w_qb_ref[...])
    kv = _bdot(ckv, w_kvb_ref[...])
    for hd in range(HEADS):
        c0 = hd * HEAD_W
        q_ref[0, hd, :, 0:QK_NOPE] = q_all[:, c0:c0 + QK_NOPE].astype(BF16)
        q_ref[0, hd, :, QK_NOPE:HEAD_W] = rope(q_all[:, c0 + QK_NOPE:c0 + HEAD_W]).astype(BF16)
        kn_ref[0, hd] = kv[:, c0:c0 + QK_NOPE].astype(BF16)
        v_ref[0, hd] = kv[:, c0 + QK_NOPE:c0 + HEAD_W].astype(BF16)


def _mla_proj(x, m, pos, freq, w_in, vec, w_qb, w_kvb, casts):
    nt = SEQ // PROJ_ROWS
    blk = PROJ_ROWS
    grid = (NB, nt)
    cast = [_cast_specs(w.shape, grid, layer) for w, layer in casts]
    return pl.pallas_call(
        _mla_proj_kernel,
        out_shape=[
            jax.ShapeDtypeStruct((NB, HEADS, SEQ, HEAD_W), BF16),
            jax.ShapeDtypeStruct((NB, HEADS, SEQ, QK_NOPE), BF16),
            jax.ShapeDtypeStruct((NB, SEQ, HEAD_W - QK_NOPE), BF16),
            jax.ShapeDtypeStruct((NB, HEADS, SEQ, V_DIM), BF16),
        ] + [c[2] for c in cast],
        grid=grid,
        in_specs=[
            pl.BlockSpec((1, blk, D), lambda b, i: (b, i, 0)),
            _mod_spec(MIX, 0),
            pl.BlockSpec((1, blk // 128, 128), lambda b, i: (b, i, 0)),
            _const_spec((1, 128)),
            _const_spec(w_in.shape),
            _const_spec(vec.shape),
            _const_spec(w_qb.shape),
            _const_spec(w_kvb.shape),
        ] + [c[0] for c in cast],
        out_specs=[
            pl.BlockSpec((1, HEADS, blk, HEAD_W), lambda b, i: (b, 0, i, 0)),
            pl.BlockSpec((1, HEADS, blk, QK_NOPE), lambda b, i: (b, 0, i, 0)),
            pl.BlockSpec((1, blk, HEAD_W - QK_NOPE), lambda b, i: (b, i, 0)),
            pl.BlockSpec((1, HEADS, blk, V_DIM), lambda b, i: (b, 0, i, 0)),
        ] + [c[1] for c in cast],
        compiler_params=_cparams("arbitrary", "arbitrary"),
        name="mla_projections",
    )(x, m, pos, freq, w_in, vec, w_qb, w_kvb, *[w for w, _ in casts])


def _lane_groups(v):
    return [v[:, c:c + 128] for c in range(0, v.shape[1], 128)]


def _attn_kernel(q_ref, kn_ref, kr_ref, v_ref, o_ref, s_ref):
    blk = ATTN_BLK
    half = blk // 2
    nq = SEQ // blk
    tri = lax.broadcasted_iota(jnp.int32, (half, half), 1) <= lax.broadcasted_iota(jnp.int32, (half, half), 0)

    def rows(n):
        return slice(n * blk, (n + 1) * blk)

    def slot(hd, n, j):
        return hd * _ATTN_SLOTS + (0 if n % 2 == 0 else nq - 1) + j

    def nt_dot(a, b):
        return lax.dot_general(a, b, (((1,), (1,)), ((), ())), preferred_element_type=F32)

    def keys(hd, j):
        return jnp.concatenate([kn_ref[hd, rows(j), :], kr_ref[rows(j), :]], axis=1)

    def lane_max(vals):
        return functools.reduce(jnp.maximum, vals)

    def pass1(hd, n):
        q = q_ref[hd, rows(n), :]
        m = []
        for j in range(n):
            s = nt_dot(q, keys(hd, j))
            s_ref[slot(hd, n, j)] = s
            m = [lane_max(_lane_groups(s) + m)]
        k = keys(hd, n)
        s_top = jnp.where(tri, nt_dot(q[:half], k[:half]), NEG_BIG)
        s_bot = nt_dot(q[half:], k)
        s_bot = jnp.concatenate([s_bot[:, :half], jnp.where(tri, s_bot[:, half:], NEG_BIG)], axis=1)
        s_ref[slot(hd, n, n), 0:half, 0:half] = s_top
        s_ref[slot(hd, n, n), half:blk, :] = s_bot
        m_top = lane_max(_lane_groups(s_top) + [v[:half] for v in m])
        m_bot = lane_max(_lane_groups(s_bot) + [v[half:] for v in m])
        m = jnp.concatenate([m_top, m_bot], axis=0)
        return jnp.broadcast_to(jnp.max(m, axis=-1, keepdims=True), (blk, 128))

    def pass2(hd, n, mb):
        diag = slot(hd, n, n)
        ps = [jnp.exp2(sg - mb) for j in range(n) for sg in _lane_groups(s_ref[slot(hd, n, j)])]
        p_top = [jnp.exp2(sg - mb[:half]) for sg in _lane_groups(s_ref[diag, 0:half, 0:half])]
        p_bot = [jnp.exp2(sg - mb[half:]) for sg in _lane_groups(s_ref[diag, half:blk, :])]
        l_top = functools.reduce(jnp.add, p_top + [p[:half] for p in ps])
        l_bot = functools.reduce(jnp.add, p_bot + [p[half:] for p in ps])
        l = jnp.sum(jnp.concatenate([l_top, l_bot], axis=0), axis=-1, keepdims=True)
        v0 = n * blk
        acc = jnp.concatenate([
            _bdot(jnp.concatenate(p_top, axis=1).astype(BF16), v_ref[hd, v0:v0 + half, :]),
            _bdot(jnp.concatenate(p_bot, axis=1).astype(BF16), v_ref[hd, v0:v0 + blk, :])], axis=0)
        if n > 0:
            acc = acc + _bdot(jnp.concatenate(ps, axis=1).astype(BF16), v_ref[hd, 0:v0, :])
        o_ref[hd, rows(n), :] = (acc / l).astype(BF16)

    mb = [pass1(hd, 0) for hd in range(ATTN_HEADS)]
    for n in range(nq):
        for hd in range(ATTN_HEADS):
            mb_next = pass1(hd, n + 1) if n + 1 < nq else None
            pass2(hd, n, mb[hd])
            mb[hd] = mb_next


def _attention(q, kn, kr, v):
    head_spec = lambda w: pl.BlockSpec((None, ATTN_HEADS, SEQ, w), lambda b, h: (b, h, 0, 0))
    kr_spec = pl.BlockSpec((None, SEQ, HEAD_W - QK_NOPE), lambda b, h: (b, 0, 0))
    return pl.pallas_call(
        _attn_kernel,
        out_shape=jax.ShapeDtypeStruct((NB, HEADS, SEQ, V_DIM), BF16),
        grid=(NB, HEADS // ATTN_HEADS),
        in_specs=[head_spec(HEAD_W), head_spec(QK_NOPE), kr_spec, head_spec(V_DIM)],
        out_specs=head_spec(V_DIM),
        scratch_shapes=[pltpu.VMEM((ATTN_HEADS * _ATTN_SLOTS, ATTN_BLK, ATTN_BLK), F32)],
        compiler_params=_cparams("arbitrary", "arbitrary"),
        name="mla_attention",
    )(q, kn, kr, v)


def _vec_row(vec_ref, row):
    return vec_ref[row:row + 1, :]


def _ffn_rows(x, shift, scale, gate, w1_ref, w2_ref, vec_ref, layer):
    h = (x * (1.0 + scale) + shift).astype(BF16)
    a2 = []
    for c in range(D_FF // FF_CHUNK):
        cols = slice(c * FF_CHUNK, (c + 1) * FF_CHUNK)
        b1 = _vec_row(vec_ref, VEC_MLP_B1 + layer * (D_FF // FF_CHUNK) + c)
        a = jnp.maximum(_bdot(h, w1_ref[:, cols]) + b1, 0.0)
        a2.append((a * a).astype(BF16))
    y = _bdot(jnp.concatenate(a2, axis=1), w2_ref[...]) + _vec_row(vec_ref, VEC_MLP_B2 + layer)
    return _layer_norm(ALPHA * x + (1.0 + gate) * y,
                       _vec_row(vec_ref, VEC_LN_FFN_G + layer), _vec_row(vec_ref, VEC_LN_FFN_B + layer))


def _batch_mod(m_ref):
    b = pl.program_id(0)
    return [m_ref[i, pl.ds(b, 1), :] for i in (SHIFT, SCALE, GATE)]


def _ffn_kernel(x_ref, m_ref, w1_ref, w2_ref, vec_ref, y_ref, *, layer):
    mod = _batch_mod(m_ref)
    for sb in range(FFN_SUB):
        rows = slice(sb * FFN_ROWS, (sb + 1) * FFN_ROWS)
        y_ref[rows, :] = _ffn_rows(x_ref[rows, :], *mod, w1_ref, w2_ref, vec_ref, layer)


def _attn_out_ffn_kernel(o_ref, x_ref, mm_ref, mf_ref, wo_ref, w1_ref, w2_ref, vec_ref, sa_ref, sb_ref, sc_ref,
                         y_ref, sab_ref, sbb_ref, scb_ref, *, layer):
    _cast_slices((sa_ref, sb_ref, sc_ref), (sab_ref, sbb_ref, scb_ref))
    gm_ref = vec_ref.at[VEC_LN_MIX_G + layer:VEC_LN_MIX_G + layer + 1]
    bm_ref = vec_ref.at[VEC_LN_MIX_B + layer:VEC_LN_MIX_B + layer + 1]
    gate_mix = _batch_mod(mm_ref)[GATE]
    mod = _batch_mod(mf_ref)
    for sb in range(FFN_SUB):
        rows = slice(sb * FFN_ROWS, (sb + 1) * FFN_ROWS)
        o = jnp.concatenate([o_ref[hd, rows, :] for hd in range(HEADS)], axis=1)
        y_ref[rows, :] = _layer_norm(ALPHA * x_ref[rows, :] + (1.0 + gate_mix) * _bdot(o, wo_ref[...]),
                                     gm_ref[...], bm_ref[...])
    for sb in range(FFN_SUB):
        rows = slice(sb * FFN_ROWS, (sb + 1) * FFN_ROWS)
        y_ref[rows, :] = _ffn_rows(y_ref[rows, :], *mod, w1_ref, w2_ref, vec_ref, layer)


_FFN_X_SPEC = pl.BlockSpec((None, FFN_SUB * FFN_ROWS, D), lambda b, i: (b, i, 0))
_FFN_GRID = (NB, SEQ // (FFN_SUB * FFN_ROWS))


def _layer_spec(shape, layer):
    return pl.BlockSpec((None,) + tuple(shape[1:]), lambda *_: (layer, 0, 0), pipeline_mode=pl.Buffered(1))


def _ffn(x, m, layer, w1, w2, vec):
    return pl.pallas_call(
        functools.partial(_ffn_kernel, layer=layer),
        out_shape=jax.ShapeDtypeStruct((NB, SEQ, D), F32),
        grid=_FFN_GRID,
        in_specs=[_FFN_X_SPEC, _mod_spec(FFN, layer), _layer_spec(w1.shape, 0), _layer_spec(w2.shape, 0),
                  _const_spec(vec.shape)],
        out_specs=_FFN_X_SPEC,
        compiler_params=_cparams("arbitrary", "arbitrary"),
        name="ffn_norm",
    )(x, m, w1, w2, vec)


def _attn_out_ffn(o, x, m, layer, w_o, w1, w2, vec, casts):
    o_spec = pl.BlockSpec((None, HEADS, FFN_SUB * FFN_ROWS, V_DIM), lambda b, i: (b, 0, i, 0))
    cast = [_cast_specs(w.shape, _FFN_GRID, l) for w, l in casts]
    return pl.pallas_call(
        functools.partial(_attn_out_ffn_kernel, layer=layer),
        out_shape=[jax.ShapeDtypeStruct((NB, SEQ, D), F32)] + [c[2] for c in cast],
        grid=_FFN_GRID,
        in_specs=[o_spec, _FFN_X_SPEC, _mod_spec(MIX, layer), _mod_spec(FFN, layer), _layer_spec(w_o.shape, 0),
                  _layer_spec(w1.shape, 0), _layer_spec(w2.shape, 0), _const_spec(vec.shape)]
                 + [c[0] for c in cast],
        out_specs=[_FFN_X_SPEC] + [c[1] for c in cast],
        compiler_params=_cparams("arbitrary", "arbitrary"),
        name="attn_out_ffn_norm",
    )(o, x, m, m, w_o, w1, w2, vec, *[w for w, _ in casts])


def _cmul(ar, ai, br, bi):
    return ar * br - ai * bi, ar * bi + ai * br


def _pair_slot(piece, zero, slot):
    return [piece, zero] if slot == 0 else [zero, piece]


def _s5_disc_kernel(ldt_ref, are_ref, aim_ref, bre_ref, bim_ref, cre_ref, cim_ref, d_ref,
                    a8r_ref, a8i_ref, wb_ref, wck_ref):
    lr = are_ref[...]
    li = aim_ref[...]
    dt = jnp.exp(ldt_ref[...])
    mag = jnp.exp(lr * dt)
    ab_re = mag * jnp.cos(li * dt)
    ab_im = mag * jnp.sin(li * dt)
    den = lr * lr + li * li
    nr = ab_re - 1.0
    coef_re = (nr * lr + ab_im * li) / den
    coef_im = (ab_im * lr - nr * li) / den
    bb_re, bb_im = _cmul(coef_re, coef_im, bre_ref[...], bim_ref[...])
    c_re = cre_ref[...]
    c_im = cim_ref[...]

    pw = [(jnp.ones_like(ab_re), jnp.zeros_like(ab_im))]
    for _ in range(CHUNK):
        pw.append(_cmul(pw[-1][0], pw[-1][1], ab_re, ab_im))
    a8r_ref[...] = pw[CHUNK][0]
    a8i_ref[...] = pw[CHUNK][1]

    n_pairs = wb_ref.shape[0]

    def pairs(v):
        v4 = v.reshape(n_pairs, 2, v.shape[1], v.shape[2])
        return v4[:, 0], v4[:, 1]

    z64 = jnp.zeros((n_pairs, GROUP_CH, STATE), F32)
    z16 = jnp.zeros((n_pairs, GROUP_CH, GROUP_CH), F32)
    eye = (lax.broadcasted_iota(jnp.int32, (GROUP_CH, GROUP_CH), 0)
           == lax.broadcasted_iota(jnp.int32, (GROUP_CH, GROUP_CH), 1)).astype(F32)

    ab_l = [_cmul(pw[l][0], pw[l][1], bb_re, bb_im) for l in range(CHUNK)]
    wb_rows = []
    for k in range(CHUNK):
        m_re, m_im = ab_l[CHUNK - 1 - k]
        for slot in range(2):
            wb_rows.append(jnp.concatenate(_pair_slot(pairs(m_re)[slot], z64, slot)
                                           + _pair_slot(pairs(m_im)[slot], z64, slot), axis=2))
    wb_ref[...] = jnp.concatenate(wb_rows, axis=1).astype(BF16)

    def nt(a, b):
        return lax.dot_general(a.astype(BF16), b.astype(BF16), (((2,), (2,)), ((0,), (0,))),
                               preferred_element_type=F32)

    k_l = [nt(c_re, ab_l[l][0]) - nt(c_im, ab_l[l][1]) for l in range(CHUNK)]
    k_l[0] = k_l[0] + d_ref[...] * eye[None]

    wck_rows = []
    for kp in range(CHUNK):
        ca_re, ca_im = _cmul(c_re, c_im, pw[kp + 1][0], pw[kp + 1][1])
        for slot in range(2):
            state_cols = (_pair_slot(pairs(ca_re)[slot], z64, slot)
                          + _pair_slot(-pairs(ca_im)[slot], z64, slot))
            direct_cols = []
            for k in range(CHUNK):
                blk = pairs(k_l[kp - k])[slot] if k <= kp else z16
                direct_cols += _pair_slot(blk, z16, slot)
            wck_rows.append(jnp.concatenate(state_cols + direct_cols, axis=2))
    wck_ref[...] = jnp.concatenate(wck_rows, axis=1).astype(BF16)


def _s5_discretise(log_dt, a_re, a_im, b_re, b_im, c_re, c_im, d_skip):
    g3 = (N_GROUPS, 1, STATE)
    gb = DISC_GROUPS
    spec = lambda *tail: pl.BlockSpec((gb,) + tail, lambda i: (i,) + (0,) * len(tail))
    pair_spec = lambda cols: pl.BlockSpec((gb // 2, PAIR_W, cols), lambda i: (i, 0, 0))
    big = spec(GROUP_CH, STATE)
    return pl.pallas_call(
        _s5_disc_kernel,
        out_shape=(jax.ShapeDtypeStruct(g3, F32), jax.ShapeDtypeStruct(g3, F32),
                   jax.ShapeDtypeStruct((N_PAIRS, PAIR_W, PAIR_W), BF16),
                   jax.ShapeDtypeStruct((N_PAIRS, PAIR_W, 2 * PAIR_W), BF16)),
        grid=(N_GROUPS // gb,),
        in_specs=[spec(1, 1), spec(1, STATE), spec(1, STATE), big, big, big, big, spec(GROUP_CH, 1)],
        out_specs=(spec(1, STATE), spec(1, STATE), pair_spec(PAIR_W), pair_spec(2 * PAIR_W)),
        compiler_params=_cparams("arbitrary"),
        name="s5_discretise",
    )(log_dt.reshape(N_GROUPS, 1, 1), a_re.reshape(g3), a_im.reshape(g3),
      jnp.swapaxes(b_re, 1, 2), jnp.swapaxes(b_im, 1, 2), c_re, c_im, d_skip.reshape(N_GROUPS, GROUP_CH, 1))


def _s5_x_spec(block):
    return pl.BlockSpec((NB, block, D), lambda i: (0, i, 0))


def _s5_u8_spec(block):
    return pl.BlockSpec((block // CHUNK * NB, N_PAIRS * PAIR_W), lambda i: (i, 0))


def _s5_in_kernel(x_ref, m_ref, w_in_ref, u8_ref):
    shift = m_ref[SHIFT]
    scale = m_ref[SCALE]
    steps = S5_IN_STEPS
    chunks = steps // CHUNK
    for sb in range(S5_IN_BLOCK // steps):
        x3 = jnp.swapaxes(x_ref[:, sb * steps:(sb + 1) * steps, :], 0, 1)
        h = (x3 * (1.0 + scale)[None] + shift[None]).reshape(steps * NB, D).astype(BF16)
        u = _bdot(h, w_in_ref[...])
        u4 = u.reshape(chunks, CHUNK, NB, D)
        per_k = [u4[:, k].reshape(chunks * NB, D) for k in range(CHUNK)]
        cols = [per_k[k][:, j * PAIR_CH:(j + 1) * PAIR_CH] for j in range(N_PAIRS) for k in range(CHUNK)]
        u8_ref[sb * chunks * NB:(sb + 1) * chunks * NB, :] = jnp.concatenate(cols, axis=1).astype(BF16)


def _s5_in(x, m, w_in):
    return pl.pallas_call(
        _s5_in_kernel,
        out_shape=jax.ShapeDtypeStruct((SEQ // CHUNK * NB, N_PAIRS * PAIR_W), BF16),
        grid=(SEQ // S5_IN_BLOCK,),
        in_specs=[
            _s5_x_spec(S5_IN_BLOCK),
            _mod_spec(MIX, 1),
            _layer_spec(w_in.shape, 0),
        ],
        out_specs=_s5_u8_spec(S5_IN_BLOCK),
        compiler_params=_cparams("arbitrary"),
        name="s5_in_proj",
    )(x, m, w_in)


def _s5_core_kernel(u8_ref, wb_ref, wck_ref, a8r_ref, a8i_ref, y8_ref):
    half = PAIR_W // 2
    for jj in range(PAIRS_PER_STEP):
        lanes = slice(jj * PAIR_W, (jj + 1) * PAIR_W)
        u8 = u8_ref[:, lanes]
        v = _bdot(u8, wb_ref[jj])
        pair = lambda ref: jnp.concatenate([ref[2 * jj], ref[2 * jj + 1]], axis=1)
        ar = jnp.broadcast_to(pair(a8r_ref), (NB, half))
        ai = jnp.broadcast_to(pair(a8i_ref), (NB, half))
        xr = jnp.zeros((NB, half), F32)
        xi = jnp.zeros((NB, half), F32)
        prev_r, prev_i = [], []
        for s in range(SEQ // CHUNK):
            prev_r.append(xr)
            prev_i.append(xi)
            rows = slice(s * NB, (s + 1) * NB)
            xr, xi = ar * xr - ai * xi + v[rows, :half], ar * xi + ai * xr + v[rows, half:]
        x_prev = jnp.concatenate([jnp.concatenate(prev_r, axis=0), jnp.concatenate(prev_i, axis=0)], axis=1)
        lhs = jnp.concatenate([x_prev.astype(BF16), u8], axis=1)
        y8_ref[:, lanes] = lax.dot_general(lhs, wck_ref[jj], (((1,), (1,)), ((), ())), preferred_element_type=F32)


def _s5_core(u8, wb, wck, a8r, a8i):
    n_rows = u8.shape[0]
    w = PAIRS_PER_STEP * PAIR_W
    return pl.pallas_call(
        _s5_core_kernel,
        out_shape=jax.ShapeDtypeStruct(u8.shape, F32),
        grid=(N_PAIRS // PAIRS_PER_STEP,),
        in_specs=[
            pl.BlockSpec((n_rows, w), lambda i: (0, i)),
            pl.BlockSpec((PAIRS_PER_STEP, PAIR_W, PAIR_W), lambda i: (i, 0, 0)),
            pl.BlockSpec((PAIRS_PER_STEP, PAIR_W, 2 * PAIR_W), lambda i: (i, 0, 0)),
            pl.BlockSpec((2 * PAIRS_PER_STEP, 1, STATE), lambda i: (i, 0, 0)),
            pl.BlockSpec((2 * PAIRS_PER_STEP, 1, STATE), lambda i: (i, 0, 0)),
        ],
        out_specs=pl.BlockSpec((n_rows, w), lambda i: (0, i)),
        compiler_params=_cparams("arbitrary"),
        name="s5_recurrence",
    )(u8, wb, wck, a8r, a8i)


def _s5_out_kernel(y8_ref, x_ref, m_ref, w_glu_ref, w_out_ref, vec_ref, ca_ref, cb_ref, o_ref, cab_ref, cbb_ref):
    _cast_slices((ca_ref, cb_ref), (cab_ref, cbb_ref))
    b_glu_ref = vec_ref.at[VEC_B_GLU:VEC_B_GLU + 1]
    g_ref = vec_ref.at[VEC_LN_MIX_G + 1:VEC_LN_MIX_G + 2]
    b_ref = vec_ref.at[VEC_LN_MIX_B + 1:VEC_LN_MIX_B + 2]
    gate = m_ref[GATE]
    steps = S5_OUT_STEPS
    chunks = steps // CHUNK
    for sb in range(S5_OUT_BLOCK // steps):
        y8 = y8_ref[sb * chunks * NB:(sb + 1) * chunks * NB, :]
        per_k = [jnp.concatenate([y8[:, j * PAIR_W + k * PAIR_CH:j * PAIR_W + (k + 1) * PAIR_CH]
                                  for j in range(N_PAIRS)], axis=1) for k in range(CHUNK)]
        y = jnp.stack([p.reshape(chunks, NB, D) for p in per_k], axis=1).reshape(steps * NB, D)

        half_y = 0.5 * y
        g = half_y + half_y * jnp.tanh(y * (GELU_C + (GELU_C * 0.044715) * (y * y)))
        zz = g * jax.nn.sigmoid(_bdot(g.astype(BF16), w_glu_ref[...]) + b_glu_ref[...])
        out = jnp.swapaxes(_bdot(zz.astype(BF16), w_out_ref[...]).reshape(steps, NB, D), 0, 1)
        t_rows = slice(sb * steps, (sb + 1) * steps)
        res = ALPHA * x_ref[:, t_rows, :] + (1.0 + gate)[:, None, :] * out
        o_ref[:, t_rows, :] = _layer_norm(res, g_ref[...], b_ref[...])


def _s5_out(y8, x, m, w_glu, w_out, vec, casts):
    x_spec = _s5_x_spec(S5_OUT_BLOCK)
    grid = (SEQ // S5_OUT_BLOCK,)
    cast = [_cast_specs(w.shape, grid, l) for w, l in casts]
    return pl.pallas_call(
        _s5_out_kernel,
        out_shape=[jax.ShapeDtypeStruct((NB, SEQ, D), F32)] + [c[2] for c in cast],
        grid=grid,
        in_specs=[
            _s5_u8_spec(S5_OUT_BLOCK),
            x_spec,
            _mod_spec(MIX, 1),
            _layer_spec(w_glu.shape, 0),
            _layer_spec(w_out.shape, 0),
            _const_spec(vec.shape),
        ] + [c[0] for c in cast],
        out_specs=[x_spec] + [c[1] for c in cast],
        compiler_params=_cparams("arbitrary"),
        name="s5_out_norm",
    )(y8, x, m, w_glu, w_out, vec, *[w for w, _ in casts])


def _rotate_half_cols(w):
    half = w.shape[-1] // 2
    return jnp.concatenate([-w[..., half:], w[..., :half]], axis=-1)


def _mla_weights(w_in, w_qb):
    k_pe = w_in[:, Q_LORA + KV_LORA:]
    w_in_ext = jnp.concatenate([w_in, _rotate_half_cols(k_pe)], axis=1)
    wq = w_qb.reshape(Q_LORA, HEADS, QK_NOPE + QK_ROPE) * Q_SCALE
    wq_ext = jnp.concatenate([wq, _rotate_half_cols(wq[..., QK_NOPE:])], axis=-1)
    return w_in_ext.astype(BF16), wq_ext.reshape(Q_LORA, HEADS * HEAD_W).astype(BF16)


def kernel(x, c, positions, mla_w_in, mla_q_norm, mla_w_qb, mla_kv_norm, mla_w_kvb, mla_w_o, ssm_w_in, ssm_log_dt, ssm_a_re, ssm_a_im, ssm_b_re, ssm_b_im, ssm_c_re, ssm_c_im, ssm_d, ssm_w_glu, ssm_b_glu, ssm_w_out, mlp_w1, mlp_b1, mlp_w2, mlp_b2, mod_mix_w, mod_mix_b, mod_ffn_w, mod_ffn_b, ln_mix_g, ln_mix_b, ln_ffn_g, ln_ffn_b):
    vec = _pack_vectors(ln_mix_g, ln_mix_b, ln_ffn_g, ln_ffn_b, mlp_b2, ssm_b_glu, mlp_b1, mod_mix_b, mod_ffn_b,
                        mla_q_norm, mla_kv_norm)
    m = _modulation(c, mod_mix_w, mod_ffn_w, vec)

    inv_freq = ROPE_THETA ** (-jnp.arange(0, QK_ROPE, 2, dtype=F32) / QK_ROPE)
    freq = jnp.tile(inv_freq, 4).reshape(1, 128)
    w_in_ext, w_qb_ext = _mla_weights(mla_w_in[0], mla_w_qb[0])
    q, kn, kr, v, w1_0, w2_0, w_o = _mla_proj(
        x, m, positions.reshape(NB, SEQ // 128, 128), freq, w_in_ext, vec, w_qb_ext, mla_w_kvb[0].astype(BF16),
        ((mlp_w1, 0), (mlp_w2, 0), (mla_w_o, 0)))
    o = _attention(q, kn, kr, v)
    x2, s_w_in, s_w_glu, s_w_out = _attn_out_ffn(
        o, x, m, 0, w_o, w1_0, w2_0, vec, ((ssm_w_in, 0), (ssm_w_glu, 0), (ssm_w_out, 0)))

    a8r, a8i, wb, wck = _s5_discretise(ssm_log_dt[0], ssm_a_re[0], ssm_a_im[0], ssm_b_re[0], ssm_b_im[0],
                                       ssm_c_re[0], ssm_c_im[0], ssm_d[0])
    u8 = _s5_in(x2, m, s_w_in)
    y8 = _s5_core(u8, wb, wck, a8r, a8i)
    x3, w1_1, w2_1 = _s5_out(y8, x2, m, s_w_glu, s_w_out, vec, ((mlp_w1, 1), (mlp_w2, 1)))
    return _ffn(x3, m, 1, w1_1, w2_1, vec)
```

```python
import functools
import math

import jax
import jax.numpy as jnp
from jax import lax
from jax.experimental import pallas as pl
from jax.experimental.pallas import tpu as pltpu

F32 = jnp.float32
BF16 = jnp.bfloat16

D = 1024
NB = 8
SEQ = 2048
HEADS = 8
QK_NOPE = 128
QK_ROPE = 64
V_DIM = 128
Q_LORA = 256
KV_LORA = 128
ROPE_THETA = 10000.0
GROUP_CH = 16
N_GROUPS = 64
STATE = 64
D_FF = 4 * D
DEPTH = 2
ALPHA = (2 * DEPTH) ** 0.25
LN_EPS = 1e-5
RMS_EPS = 1e-6
Q_SCALE = math.log2(math.e) / math.sqrt(QK_NOPE + QK_ROPE)
NEG_BIG = -1e30
GELU_C = math.sqrt(2.0 / math.pi)

HEAD_W = 256
CHUNK = 8
N_PAIRS = N_GROUPS // 2
PAIR_CH = 2 * GROUP_CH
PAIR_W = CHUNK * PAIR_CH
assert PAIR_W == 4 * STATE

VMEM_LIMIT = 56 * 1024 * 1024
PROJ_ROWS = 1024
ATTN_BLK = 512
ATTN_HEADS = 4
_ATTN_SLOTS = 2 * (SEQ // ATTN_BLK) - 1
FFN_ROWS = 256
FFN_SUB = 4
FF_CHUNK = 1024
S5_IN_BLOCK, S5_IN_STEPS = 256, 64
S5_OUT_BLOCK, S5_OUT_STEPS = 64, 32
PAIRS_PER_STEP = 4
DISC_GROUPS = 16


VEC_LN_MIX_G, VEC_LN_MIX_B, VEC_LN_FFN_G, VEC_LN_FFN_B, VEC_MLP_B2 = 0, DEPTH, 2 * DEPTH, 3 * DEPTH, 4 * DEPTH
VEC_B_GLU = 5 * DEPTH
VEC_MLP_B1 = VEC_B_GLU + 1
VEC_MOD_B = VEC_MLP_B1 + DEPTH * (D_FF // D)
VEC_Q_NORM = VEC_MOD_B + 2 * 3 * DEPTH
VEC_KV_NORM = VEC_Q_NORM + 1
VEC_ROWS = -(-(VEC_KV_NORM + 1) // 8) * 8
assert FF_CHUNK == D


def _pack_vectors(ln_mix_g, ln_mix_b, ln_ffn_g, ln_ffn_b, mlp_b2, b_glu, mlp_b1, mod_mix_b, mod_ffn_b, q_norm, kv_norm):
    pad = lambda v: jnp.pad(v.reshape(1, -1), ((0, 0), (0, D - v.size)))
    rows = [ln_mix_g, ln_mix_b, ln_ffn_g, ln_ffn_b, mlp_b2, b_glu.reshape(1, D), mlp_b1.reshape(-1, D),
            mod_mix_b.reshape(-1, D), mod_ffn_b.reshape(-1, D), pad(q_norm), pad(kv_norm)]
    table = jnp.concatenate(rows, axis=0)
    return jnp.pad(table, ((0, VEC_ROWS - table.shape[0]), (0, 0)))


def _cparams(*sem):
    return pltpu.CompilerParams(dimension_semantics=sem, vmem_limit_bytes=VMEM_LIMIT)


def _const_spec(shape):
    nd = len(shape)
    return pl.BlockSpec(shape, lambda *_: (0,) * nd, pipeline_mode=pl.Buffered(1))


def _cast_specs(shape, grid, layer):
    _, rows, cols = shape
    n_steps = math.prod(grid)

    def step(*ids):
        s = 0
        for i, n in zip(ids, grid):
            s = s * n + i
        return s

    block = (1, rows // n_steps, cols)
    return (pl.BlockSpec(block, lambda *ids: (layer, step(*ids), 0)),
            pl.BlockSpec(block, lambda *ids: (0, step(*ids), 0)),
            jax.ShapeDtypeStruct((1, rows, cols), BF16))


def _cast_slices(src_refs, dst_refs):
    for src, dst in zip(src_refs, dst_refs):
        dst[...] = src[...].astype(dst.dtype)


def _layer_norm(v, g, b):
    mu = jnp.mean(v, axis=-1, keepdims=True)
    vc = v - mu
    var = jnp.mean(vc * vc, axis=-1, keepdims=True)
    return vc * lax.rsqrt(var + LN_EPS) * g + b


def _bdot(a, b):
    return jnp.dot(a, b, preferred_element_type=F32)


MIX, FFN = 0, 1
SHIFT, SCALE, GATE = 0, 1, 2


def _mod_kernel(c_ref, w_mix_ref, w_ffn_ref, vec_ref, o_ref):
    kind, layer, part = pl.program_id(0), pl.program_id(1), pl.program_id(2)
    c = c_ref[...]
    cs = (c * jax.nn.sigmoid(c)).astype(BF16)
    bias = vec_ref[pl.ds(VEC_MOD_B + 3 * (DEPTH * kind + layer) + part, 1), :]
    for k, w_ref in ((MIX, w_mix_ref), (FFN, w_ffn_ref)):
        @pl.when(kind == k)
        def _():
            o_ref[...] = _bdot(cs, w_ref[...].astype(BF16)) + bias


def _modulation(c, w_mix, w_ffn, vec):
    def w_spec(k):
        idle = ((1 - k) * (DEPTH - 1), 2 * (1 - k))
        return pl.BlockSpec((None, D, D), lambda kind, l, j: (jnp.where(kind == k, l, idle[0]), 0,
                                                            jnp.where(kind == k, j, idle[1])))
    return pl.pallas_call(
        _mod_kernel,
        out_shape=jax.ShapeDtypeStruct((2, DEPTH, 3, NB, D), F32),
        grid=(2, DEPTH, 3),
        in_specs=[
            pl.BlockSpec((NB, D), lambda kind, l, j: (0, 0)),
            w_spec(MIX),
            w_spec(FFN),
            _const_spec(vec.shape),
        ],
        out_specs=pl.BlockSpec((None, None, None, NB, D), lambda kind, l, j: (kind, l, j, 0, 0)),
        compiler_params=_cparams("arbitrary", "arbitrary", "arbitrary"),
        name="adaln_modulation",
    )(c, w_mix, w_ffn, vec)


def _mod_spec(kind, layer):
    return pl.BlockSpec((None, None, 3, NB, D), lambda *_: (kind, layer, 0, 0, 0), pipeline_mode=pl.Buffered(1))


def _mla_proj_kernel(x_ref, m_ref, pos_ref, freq_ref, w_in_ref, vec_ref, w_qb_ref, w_kvb_ref,
                     ca_ref, cb_ref, cc_ref, q_ref, kn_ref, kr_ref, v_ref, cab_ref, cbb_ref, ccb_ref):
    _cast_slices((ca_ref, cb_ref, cc_ref), (cab_ref, cbb_ref, ccb_ref))
    qn_ref = vec_ref.at[VEC_Q_NORM:VEC_Q_NORM + 1, 0:Q_LORA]
    kvn_ref = vec_ref.at[VEC_KV_NORM:VEC_KV_NORM + 1, 0:KV_LORA]
    b = pl.program_id(0)
    shift = m_ref[SHIFT, pl.ds(b, 1), :]
    scale = m_ref[SCALE, pl.ds(b, 1), :]

    def rms(v, g):
        return v * lax.rsqrt(jnp.mean(v * v, axis=-1, keepdims=True) + RMS_EPS) * g

    eye = lax.broadcasted_iota(jnp.int32, (128, 128), 0) == lax.broadcasted_iota(jnp.int32, (128, 128), 1)
    h = (x_ref[0] * (1.0 + scale) + shift).astype(BF16)
    z = _bdot(h, w_in_ref[...])
    cq = rms(z[:, :Q_LORA], qn_ref[...]).astype(BF16)
    ckv = rms(z[:, Q_LORA:Q_LORA + KV_LORA], kvn_ref[...]).astype(BF16)

    pos = pos_ref[0].astype(F32)
    pos_col = jnp.concatenate(
        [jnp.sum(jnp.where(eye, jnp.broadcast_to(pos[i:i + 1, :], (128, 128)), 0.0), axis=1, keepdims=True)
         for i in range(PROJ_ROWS // 128)], axis=0)
    ang = pos_col * freq_ref[...]
    lane = lax.broadcasted_iota(jnp.int32, ang.shape, 1)
    mult = jnp.where(lane < QK_ROPE, jnp.cos(ang), jnp.sin(ang))

    def rope(slab):
        s = slab * mult
        return s + pltpu.roll(s, QK_ROPE, axis=1)

    kr_ref[0] = jnp.where(lane < QK_ROPE, rope(z[:, Q_LORA + KV_LORA:]), 0.0).astype(BF16)

    q_all = _bdot(cq, w_qb_ref[...])
    kv = _bdot(ckv, w_kvb_ref[...])
    for hd in range(HEADS):
        c0 = hd * HEAD_W
        q_ref[0, hd, :, 0:QK_NOPE] = q_all[:, c0:c0 + QK_NOPE].astype(BF16)
        q_ref[0, hd, :, QK_NOPE:HEAD_W] = rope(q_all[:, c0 + QK_NOPE:c0 + HEAD_W]).astype(BF16)
        kn_ref[0, hd] = kv[:, c0:c0 + QK_NOPE].astype(BF16)
        v_ref[0, hd] = kv[:, c0 + QK_NOPE:c0 + HEAD_W].astype(BF16)


def _mla_proj(x, m, pos, freq, w_in, vec, w_qb, w_kvb, casts):
    nt = SEQ // PROJ_ROWS
    blk = PROJ_ROWS
    grid = (NB, nt)
    cast = [_cast_specs(w.shape, grid, layer) for w, layer in casts]
    return pl.pallas_call(
        _mla_proj_kernel,
        out_shape=[
            jax.ShapeDtypeStruct((NB, HEADS, SEQ, HEAD_W), BF16),
            jax.ShapeDtypeStruct((NB, HEADS, SEQ, QK_NOPE), BF16),
            jax.ShapeDtypeStruct((NB, SEQ, HEAD_W - QK_NOPE), BF16),
            jax.ShapeDtypeStruct((NB, HEADS, SEQ, V_DIM), BF16),
        ] + [c[2] for c in cast],
        grid=grid,
        in_specs=[
            pl.BlockSpec((1, blk, D), lambda b, i: (b, i, 0)),
            _mod_spec(MIX, 0),
            pl.BlockSpec((1, blk // 128, 128), lambda b, i: (b, i, 0)),
            _const_spec((1, 128)),
            _const_spec(w_in.shape),
            _const_spec(vec.shape),
            _const_spec(w_qb.shape),
            _const_spec(w_kvb.shape),
        ] + [c[0] for c in cast],
        out_specs=[
            pl.BlockSpec((1, HEADS, blk, HEAD_W), lambda b, i: (b, 0, i, 0)),
            pl.BlockSpec((1, HEADS, blk, QK_NOPE), lambda b, i: (b, 0, i, 0)),
            pl.BlockSpec((1, blk, HEAD_W - QK_NOPE), lambda b, i: (b, i, 0)),
            pl.BlockSpec((1, HEADS, blk, V_DIM), lambda b, i: (b, 0, i, 0)),
        ] + [c[1] for c in cast],
        compiler_params=_cparams("arbitrary", "arbitrary"),
        name="mla_projections",
    )(x, m, pos, freq, w_in, vec, w_qb, w_kvb, *[w for w, _ in casts])


def _lane_groups(v):
    return [v[:, c:c + 128] for c in range(0, v.shape[1], 128)]


def _attn_kernel(q_ref, kn_ref, kr_ref, v_ref, o_ref, s_ref):
    blk = ATTN_BLK
    half = blk // 2
    nq = SEQ // blk
    tri = lax.broadcasted_iota(jnp.int32, (half, half), 1) <= lax.broadcasted_iota(jnp.int32, (half, half), 0)

    def rows(n):
        return slice(n * blk, (n + 1) * blk)

    def slot(hd, n, j):
        return hd * _ATTN_SLOTS + (0 if n % 2 == 0 else nq - 1) + j

    def nt_dot(a, b):
        return lax.dot_general(a, b, (((1,), (1,)), ((), ())), preferred_element_type=F32)

    def keys(hd, j):
        return jnp.concatenate([kn_ref[hd, rows(j), :], kr_ref[rows(j), :]], axis=1)

    def lane_max(vals):
        return functools.reduce(jnp.maximum, vals)

    def pass1(hd, n):
        q = q_ref[hd, rows(n), :]
        m = []
        for j in range(n):
            s = nt_dot(q, keys(hd, j))
            s_ref[slot(hd, n, j)] = s
            m = [lane_max(_lane_groups(s) + m)]
        k = keys(hd, n)
        s_top = jnp.where(tri, nt_dot(q[:half], k[:half]), NEG_BIG)
        s_bot = nt_dot(q[half:], k)
        s_bot = jnp.concatenate([s_bot[:, :half], jnp.where(tri, s_bot[:, half:], NEG_BIG)], axis=1)
        s_ref[slot(hd, n, n), 0:half, 0:half] = s_top
        s_ref[slot(hd, n, n), half:blk, :] = s_bot
        m_top = lane_max(_lane_groups(s_top) + [v[:half] for v in m])
        m_bot = lane_max(_lane_groups(s_bot) + [v[half:] for v in m])
        m = jnp.concatenate([m_top, m_bot], axis=0)
        return jnp.broadcast_to(jnp.max(m, axis=-1, keepdims=True), (blk, 128))

    def pass2(hd, n, mb):
        diag = slot(hd, n, n)
        ps = [jnp.exp2(sg - mb) for j in range(n) for sg in _lane_groups(s_ref[slot(hd, n, j)])]
        p_top = [jnp.exp2(sg - mb[:half]) for sg in _lane_groups(s_ref[diag, 0:half, 0:half])]
        p_bot = [jnp.exp2(sg - mb[half:]) for sg in _lane_groups(s_ref[diag, half:blk, :])]
        l_top = functools.reduce(jnp.add, p_top + [p[:half] for p in ps])
        l_bot = functools.reduce(jnp.add, p_bot + [p[half:] for p in ps])
        l = jnp.sum(jnp.concatenate([l_top, l_bot], axis=0), axis=-1, keepdims=True)
        v0 = n * blk
        acc = jnp.concatenate([
            _bdot(jnp.concatenate(p_top, axis=1).astype(BF16), v_ref[hd, v0:v0 + half, :]),
            _bdot(jnp.concatenate(p_bot, axis=1).astype(BF16), v_ref[hd, v0:v0 + blk, :])], axis=0)
        if n > 0:
            acc = acc + _bdot(jnp.concatenate(ps, axis=1).astype(BF16), v_ref[hd, 0:v0, :])
        o_ref[hd, rows(n), :] = (acc / l).astype(BF16)

    mb = [pass1(hd, 0) for hd in range(ATTN_HEADS)]
    for n in range(nq):
        for hd in range(ATTN_HEADS):
            mb_next = pass1(hd, n + 1) if n + 1 < nq else None
            pass2(hd, n, mb[hd])
            mb[hd] = mb_next


def _attention(q, kn, kr, v):
    head_spec = lambda w: pl.BlockSpec((None, ATTN_HEADS, SEQ, w), lambda b, h: (b, h, 0, 0))
    kr_spec = pl.BlockSpec((None, SEQ, HEAD_W - QK_NOPE), lambda b, h: (b, 0, 0))
    return pl.pallas_call(
        _attn_kernel,
        out_shape=jax.ShapeDtypeStruct((NB, HEADS, SEQ, V_DIM), BF16),
        grid=(NB, HEADS // ATTN_HEADS),
        in_specs=[head_spec(HEAD_W), head_spec(QK_NOPE), kr_spec, head_spec(V_DIM)],
        out_specs=head_spec(V_DIM),
        scratch_shapes=[pltpu.VMEM((ATTN_HEADS * _ATTN_SLOTS, ATTN_BLK, ATTN_BLK), F32)],
        compiler_params=_cparams("arbitrary", "arbitrary"),
        name="mla_attention",
    )(q, kn, kr, v)


def _vec_row(vec_ref, row):
    return vec_ref[row:row + 1, :]


def _ffn_rows(x, shift, scale, gate, w1_ref, w2_ref, vec_ref, layer):
    h = (x * (1.0 + scale) + shift).reshape(-1, D).astype(BF16)
    a2 = []
    for c in range(D_FF // FF_CHUNK):
        cols = slice(c * FF_CHUNK, (c + 1) * FF_CHUNK)
        b1 = _vec_row(vec_ref, VEC_MLP_B1 + layer * (D_FF // FF_CHUNK) + c)
        a = jnp.maximum(_bdot(h, w1_ref[:, cols]) + b1, 0.0)
        a2.append((a * a).astype(BF16))
    y = (_bdot(jnp.concatenate(a2, axis=1), w2_ref[...]) + _vec_row(vec_ref, VEC_MLP_B2 + layer)).reshape(x.shape)
    return _layer_norm(ALPHA * x + (1.0 + gate) * y,
                       _vec_row(vec_ref, VEC_LN_FFN_G + layer), _vec_row(vec_ref, VEC_LN_FFN_B + layer))


def _batch_mod(m_ref):
    b = pl.program_id(0)
    return [m_ref[i, pl.ds(b, 1), :] for i in (SHIFT, SCALE, GATE)]


def _attn_out_ffn_kernel(o_ref, x_ref, mm_ref, mf_ref, wo_ref, w1_ref, w2_ref, vec_ref, sa_ref, sb_ref, sc_ref,
                         y_ref, sab_ref, sbb_ref, scb_ref, *, layer):
    _cast_slices((sa_ref, sb_ref, sc_ref), (sab_ref, sbb_ref, scb_ref))
    gm_ref = vec_ref.at[VEC_LN_MIX_G + layer:VEC_LN_MIX_G + layer + 1]
    bm_ref = vec_ref.at[VEC_LN_MIX_B + layer:VEC_LN_MIX_B + layer + 1]
    gate_mix = _batch_mod(mm_ref)[GATE]
    mod = _batch_mod(mf_ref)
    for sb in range(FFN_SUB):
        rows = slice(sb * FFN_ROWS, (sb + 1) * FFN_ROWS)
        o = jnp.concatenate([o_ref[hd, rows, :] for hd in range(HEADS)], axis=1)
        y_ref[rows, :] = _layer_norm(ALPHA * x_ref[rows, :] + (1.0 + gate_mix) * _bdot(o, wo_ref[...]),
                                     gm_ref[...], bm_ref[...])
    for sb in range(FFN_SUB):
        rows = slice(sb * FFN_ROWS, (sb + 1) * FFN_ROWS)
        y_ref[rows, :] = _ffn_rows(y_ref[rows, :], *mod, w1_ref, w2_ref, vec_ref, layer)


_FFN_X_SPEC = pl.BlockSpec((None, FFN_SUB * FFN_ROWS, D), lambda b, i: (b, i, 0))
_FFN_GRID = (NB, SEQ // (FFN_SUB * FFN_ROWS))


def _layer_spec(shape, layer):
    return pl.BlockSpec((None,) + tuple(shape[1:]), lambda *_: (layer, 0, 0), pipeline_mode=pl.Buffered(1))


def _attn_out_ffn(o, x, m, layer, w_o, w1, w2, vec, casts):
    o_spec = pl.BlockSpec((None, HEADS, FFN_SUB * FFN_ROWS, V_DIM), lambda b, i: (b, 0, i, 0))
    cast = [_cast_specs(w.shape, _FFN_GRID, l) for w, l in casts]
    return pl.pallas_call(
        functools.partial(_attn_out_ffn_kernel, layer=layer),
        out_shape=[jax.ShapeDtypeStruct((NB, SEQ, D), F32)] + [c[2] for c in cast],
        grid=_FFN_GRID,
        in_specs=[o_spec, _FFN_X_SPEC, _mod_spec(MIX, layer), _mod_spec(FFN, layer), _layer_spec(w_o.shape, 0),
                  _layer_spec(w1.shape, 0), _layer_spec(w2.shape, 0), _const_spec(vec.shape)]
                 + [c[0] for c in cast],
        out_specs=[_FFN_X_SPEC] + [c[1] for c in cast],
        compiler_params=_cparams("arbitrary", "arbitrary"),
        name="attn_out_ffn_norm",
    )(o, x, m, m, w_o, w1, w2, vec, *[w for w, _ in casts])


def _cmul(ar, ai, br, bi):
    return ar * br - ai * bi, ar * bi + ai * br


def _pair_slot(piece, zero, slot):
    return [piece, zero] if slot == 0 else [zero, piece]


def _s5_disc_kernel(ldt_ref, are_ref, aim_ref, bre_ref, bim_ref, cre_ref, cim_ref, d_ref,
                    a8r_ref, a8i_ref, wb_ref, wck_ref):
    lr = are_ref[...]
    li = aim_ref[...]
    dt = jnp.exp(ldt_ref[...])
    mag = jnp.exp(lr * dt)
    ab_re = mag * jnp.cos(li * dt)
    ab_im = mag * jnp.sin(li * dt)
    den = lr * lr + li * li
    nr = ab_re - 1.0
    coef_re = (nr * lr + ab_im * li) / den
    coef_im = (ab_im * lr - nr * li) / den
    bb_re, bb_im = _cmul(coef_re, coef_im, bre_ref[...], bim_ref[...])
    c_re = cre_ref[...]
    c_im = cim_ref[...]

    pw = [(jnp.ones_like(ab_re), jnp.zeros_like(ab_im))]
    for _ in range(CHUNK):
        pw.append(_cmul(pw[-1][0], pw[-1][1], ab_re, ab_im))
    a8r_ref[...] = pw[CHUNK][0]
    a8i_ref[...] = pw[CHUNK][1]

    n_pairs = wb_ref.shape[0]

    def pairs(v):
        v4 = v.reshape(n_pairs, 2, v.shape[1], v.shape[2])
        return v4[:, 0], v4[:, 1]

    z64 = jnp.zeros((n_pairs, GROUP_CH, STATE), F32)
    z16 = jnp.zeros((n_pairs, GROUP_CH, GROUP_CH), F32)
    eye = (lax.broadcasted_iota(jnp.int32, (GROUP_CH, GROUP_CH), 0)
           == lax.broadcasted_iota(jnp.int32, (GROUP_CH, GROUP_CH), 1)).astype(F32)

    ab_l = [_cmul(pw[l][0], pw[l][1], bb_re, bb_im) for l in range(CHUNK)]
    wb_rows = []
    for k in range(CHUNK):
        m_re, m_im = ab_l[CHUNK - 1 - k]
        for slot in range(2):
            wb_rows.append(jnp.concatenate(_pair_slot(pairs(m_re)[slot], z64, slot)
                                           + _pair_slot(pairs(m_im)[slot], z64, slot), axis=2))
    wb_ref[...] = jnp.concatenate(wb_rows, axis=1).astype(BF16)

    def nt(a, b):
        return lax.dot_general(a.astype(BF16), b.astype(BF16), (((2,), (2,)), ((0,), (0,))),
                               preferred_element_type=F32)

    k_l = [nt(c_re, ab_l[l][0]) - nt(c_im, ab_l[l][1]) for l in range(CHUNK)]
    k_l[0] = k_l[0] + d_ref[...] * eye[None]

    wck_rows = []
    for kp in range(CHUNK):
        ca_re, ca_im = _cmul(c_re, c_im, pw[kp + 1][0], pw[kp + 1][1])
        for slot in range(2):
            state_cols = (_pair_slot(pairs(ca_re)[slot], z64, slot)
                          + _pair_slot(-pairs(ca_im)[slot], z64, slot))
            direct_cols = []
            for k in range(CHUNK):
                blk = pairs(k_l[kp - k])[slot] if k <= kp else z16
                direct_cols += _pair_slot(blk, z16, slot)
            wck_rows.append(jnp.concatenate(state_cols + direct_cols, axis=2))
    wck_ref[...] = jnp.concatenate(wck_rows, axis=1).astype(BF16)


def _s5_discretise(log_dt, a_re, a_im, b_re, b_im, c_re, c_im, d_skip):
    g3 = (N_GROUPS, 1, STATE)
    gb = DISC_GROUPS
    spec = lambda *tail: pl.BlockSpec((gb,) + tail, lambda i: (i,) + (0,) * len(tail))
    pair_spec = lambda cols: pl.BlockSpec((gb // 2, PAIR_W, cols), lambda i: (i, 0, 0))
    big = spec(GROUP_CH, STATE)
    return pl.pallas_call(
        _s5_disc_kernel,
        out_shape=(jax.ShapeDtypeStruct(g3, F32), jax.ShapeDtypeStruct(g3, F32),
                   jax.ShapeDtypeStruct((N_PAIRS, PAIR_W, PAIR_W), BF16),
                   jax.ShapeDtypeStruct((N_PAIRS, PAIR_W, 2 * PAIR_W), BF16)),
        grid=(N_GROUPS // gb,),
        in_specs=[spec(1, 1), spec(1, STATE), spec(1, STATE), big, big, big, big, spec(GROUP_CH, 1)],
        out_specs=(spec(1, STATE), spec(1, STATE), pair_spec(PAIR_W), pair_spec(2 * PAIR_W)),
        compiler_params=_cparams("arbitrary"),
        name="s5_discretise",
    )(log_dt.reshape(N_GROUPS, 1, 1), a_re.reshape(g3), a_im.reshape(g3),
      jnp.swapaxes(b_re, 1, 2), jnp.swapaxes(b_im, 1, 2), c_re, c_im, d_skip.reshape(N_GROUPS, GROUP_CH, 1))


def _s5_x_spec(block):
    return pl.BlockSpec((NB, block, D), lambda i: (0, i, 0))


def _s5_u8_spec(block):
    return pl.BlockSpec((block // CHUNK * NB, N_PAIRS * PAIR_W), lambda i: (i, 0))


def _s5_in_kernel(x_ref, m_ref, w_in_ref, ca_ref, cb_ref, u8_ref, cab_ref, cbb_ref):
    _cast_slices((ca_ref, cb_ref), (cab_ref, cbb_ref))
    shift = m_ref[SHIFT]
    scale = m_ref[SCALE]
    steps = S5_IN_STEPS
    chunks = steps // CHUNK
    for sb in range(S5_IN_BLOCK // steps):
        x3 = jnp.swapaxes(x_ref[:, sb * steps:(sb + 1) * steps, :], 0, 1)
        h = (x3 * (1.0 + scale)[None] + shift[None]).reshape(steps * NB, D).astype(BF16)
        u = _bdot(h, w_in_ref[...])
        u4 = u.reshape(chunks, CHUNK, NB, D)
        per_k = [u4[:, k].reshape(chunks * NB, D) for k in range(CHUNK)]
        cols = [per_k[k][:, j * PAIR_CH:(j + 1) * PAIR_CH] for j in range(N_PAIRS) for k in range(CHUNK)]
        u8_ref[sb * chunks * NB:(sb + 1) * chunks * NB, :] = jnp.concatenate(cols, axis=1).astype(BF16)


def _s5_in(x, m, w_in, casts):
    grid = (SEQ // S5_IN_BLOCK,)
    cast = [_cast_specs(w.shape, grid, l) for w, l in casts]
    return pl.pallas_call(
        _s5_in_kernel,
        out_shape=[jax.ShapeDtypeStruct((SEQ // CHUNK * NB, N_PAIRS * PAIR_W), BF16)] + [c[2] for c in cast],
        grid=grid,
        in_specs=[
            _s5_x_spec(S5_IN_BLOCK),
            _mod_spec(MIX, 1),
            _layer_spec(w_in.shape, 0),
        ] + [c[0] for c in cast],
        out_specs=[_s5_u8_spec(S5_IN_BLOCK)] + [c[1] for c in cast],
        compiler_params=_cparams("arbitrary"),
        name="s5_in_proj",
    )(x, m, w_in, *[w for w, _ in casts])


def _s5_core_kernel(u8_ref, wb_ref, wck_ref, a8r_ref, a8i_ref, y8_ref):
    half = PAIR_W // 2
    for jj in range(PAIRS_PER_STEP):
        lanes = slice(jj * PAIR_W, (jj + 1) * PAIR_W)
        u8 = u8_ref[:, lanes]
        v = _bdot(u8, wb_ref[jj])
        pair = lambda ref: jnp.concatenate([ref[2 * jj], ref[2 * jj + 1]], axis=1)
        ar = jnp.broadcast_to(pair(a8r_ref), (NB, half))
        ai = jnp.broadcast_to(pair(a8i_ref), (NB, half))
        xr = jnp.zeros((NB, half), F32)
        xi = jnp.zeros((NB, half), F32)
        prev_r, prev_i = [], []
        for s in range(SEQ // CHUNK):
            prev_r.append(xr)
            prev_i.append(xi)
            rows = slice(s * NB, (s + 1) * NB)
            xr, xi = ar * xr - ai * xi + v[rows, :half], ar * xi + ai * xr + v[rows, half:]
        x_prev = jnp.concatenate([jnp.concatenate(prev_r, axis=0), jnp.concatenate(prev_i, axis=0)], axis=1)
        lhs = jnp.concatenate([x_prev.astype(BF16), u8], axis=1)
        y8_ref[:, lanes] = lax.dot_general(lhs, wck_ref[jj], (((1,), (1,)), ((), ())), preferred_element_type=F32)


def _s5_core(u8, wb, wck, a8r, a8i):
    n_rows = u8.shape[0]
    w = PAIRS_PER_STEP * PAIR_W
    return pl.pallas_call(
        _s5_core_kernel,
        out_shape=jax.ShapeDtypeStruct(u8.shape, F32),
        grid=(N_PAIRS // PAIRS_PER_STEP,),
        in_specs=[
            pl.BlockSpec((n_rows, w), lambda i: (0, i)),
            pl.BlockSpec((PAIRS_PER_STEP, PAIR_W, PAIR_W), lambda i: (i, 0, 0)),
            pl.BlockSpec((PAIRS_PER_STEP, PAIR_W, 2 * PAIR_W), lambda i: (i, 0, 0)),
            pl.BlockSpec((2 * PAIRS_PER_STEP, 1, STATE), lambda i: (i, 0, 0)),
            pl.BlockSpec((2 * PAIRS_PER_STEP, 1, STATE), lambda i: (i, 0, 0)),
        ],
        out_specs=pl.BlockSpec((n_rows, w), lambda i: (0, i)),
        compiler_params=_cparams("arbitrary"),
        name="s5_recurrence",
    )(u8, wb, wck, a8r, a8i)


def _s5_out_ffn_kernel(y8_ref, x_ref, mm_ref, mf_ref, w_glu_ref, w_out_ref, w1_ref, w2_ref, vec_ref, o_ref):
    b_glu_ref = vec_ref.at[VEC_B_GLU:VEC_B_GLU + 1]
    g_ref = vec_ref.at[VEC_LN_MIX_G + 1:VEC_LN_MIX_G + 2]
    b_ref = vec_ref.at[VEC_LN_MIX_B + 1:VEC_LN_MIX_B + 2]
    gate = mm_ref[GATE]
    steps = S5_OUT_STEPS
    chunks = steps // CHUNK
    n_sub = S5_OUT_BLOCK // steps
    for sb in range(n_sub):
        y8 = y8_ref[sb * chunks * NB:(sb + 1) * chunks * NB, :]
        per_k = [jnp.concatenate([y8[:, j * PAIR_W + k * PAIR_CH:j * PAIR_W + (k + 1) * PAIR_CH]
                                  for j in range(N_PAIRS)], axis=1) for k in range(CHUNK)]
        y = jnp.stack([p.reshape(chunks, NB, D) for p in per_k], axis=1).reshape(steps * NB, D)

        half_y = 0.5 * y
        g = half_y + half_y * jnp.tanh(y * (GELU_C + (GELU_C * 0.044715) * (y * y)))
        zz = g * jax.nn.sigmoid(_bdot(g.astype(BF16), w_glu_ref[...]) + b_glu_ref[...])
        out = jnp.swapaxes(_bdot(zz.astype(BF16), w_out_ref[...]).reshape(steps, NB, D), 0, 1)
        t_rows = slice(sb * steps, (sb + 1) * steps)
        res = ALPHA * x_ref[:, t_rows, :] + (1.0 + gate)[:, None, :] * out
        o_ref[:, t_rows, :] = _layer_norm(res, g_ref[...], b_ref[...])
    mod = [mf_ref[i][:, None, :] for i in (SHIFT, SCALE, GATE)]
    for sb in range(n_sub):
        t_rows = slice(sb * steps, (sb + 1) * steps)
        o_ref[:, t_rows, :] = _ffn_rows(o_ref[:, t_rows, :], *mod, w1_ref, w2_ref, vec_ref, 1)


def _s5_out_ffn(y8, x, m, w_glu, w_out, w1, w2, vec):
    x_spec = _s5_x_spec(S5_OUT_BLOCK)
    return pl.pallas_call(
        _s5_out_ffn_kernel,
        out_shape=jax.ShapeDtypeStruct((NB, SEQ, D), F32),
        grid=(SEQ // S5_OUT_BLOCK,),
        in_specs=[
            _s5_u8_spec(S5_OUT_BLOCK),
            x_spec,
            _mod_spec(MIX, 1),
            _mod_spec(FFN, 1),
            _layer_spec(w_glu.shape, 0),
            _layer_spec(w_out.shape, 0),
            _layer_spec(w1.shape, 0),
            _layer_spec(w2.shape, 0),
            _const_spec(vec.shape),
        ],
        out_specs=x_spec,
        compiler_params=_cparams("arbitrary"),
        name="s5_out_ffn_norm",
    )(y8, x, m, m, w_glu, w_out, w1, w2, vec)


def _rotate_half_cols(w):
    half = w.shape[-1] // 2
    return jnp.concatenate([-w[..., half:], w[..., :half]], axis=-1)


def _mla_weights(w_in, w_qb):
    k_pe = w_in[:, Q_LORA + KV_LORA:]
    w_in_ext = jnp.concatenate([w_in, _rotate_half_cols(k_pe)], axis=1)
    wq = w_qb.reshape(Q_LORA, HEADS, QK_NOPE + QK_ROPE) * Q_SCALE
    wq_ext = jnp.concatenate([wq, _rotate_half_cols(wq[..., QK_NOPE:])], axis=-1)
    return w_in_ext.astype(BF16), wq_ext.reshape(Q_LORA, HEADS * HEAD_W).astype(BF16)


def kernel(x, c, positions, mla_w_in, mla_q_norm, mla_w_qb, mla_kv_norm, mla_w_kvb, mla_w_o, ssm_w_in, ssm_log_dt, ssm_a_re, ssm_a_im, ssm_b_re, ssm_b_im, ssm_c_re, ssm_c_im, ssm_d, ssm_w_glu, ssm_b_glu, ssm_w_out, mlp_w1, mlp_b1, mlp_w2, mlp_b2, mod_mix_w, mod_mix_b, mod_ffn_w, mod_ffn_b, ln_mix_g, ln_mix_b, ln_ffn_g, ln_ffn_b):
    vec = _pack_vectors(ln_mix_g, ln_mix_b, ln_ffn_g, ln_ffn_b, mlp_b2, ssm_b_glu, mlp_b1, mod_mix_b, mod_ffn_b,
                        mla_q_norm, mla_kv_norm)
    m = _modulation(c, mod_mix_w, mod_ffn_w, vec)

    inv_freq = ROPE_THETA ** (-jnp.arange(0, QK_ROPE, 2, dtype=F32) / QK_ROPE)
    freq = jnp.tile(inv_freq, 4).reshape(1, 128)
    w_in_ext, w_qb_ext = _mla_weights(mla_w_in[0], mla_w_qb[0])
    q, kn, kr, v, w1_0, w2_0, w_o = _mla_proj(
        x, m, positions.reshape(NB, SEQ // 128, 128), freq, w_in_ext, vec, w_qb_ext, mla_w_kvb[0].astype(BF16),
        ((mlp_w1, 0), (mlp_w2, 0), (mla_w_o, 0)))
    o = _attention(q, kn, kr, v)
    x2, s_w_in, s_w_glu, s_w_out = _attn_out_ffn(
        o, x, m, 0, w_o, w1_0, w2_0, vec, ((ssm_w_in, 0), (ssm_w_glu, 0), (ssm_w_out, 0)))

    a8r, a8i, wb, wck = _s5_discretise(ssm_log_dt[0], ssm_a_re[0], ssm_a_im[0], ssm_b_re[0], ssm_b_im[0],
                                       ssm_c_re[0], ssm_c_im[0], ssm_d[0])
    u8, w1_1, w2_1 = _s5_in(x2, m, s_w_in, ((mlp_w1, 1), (mlp_w2, 1)))
    y8 = _s5_core(u8, wb, wck, a8r, a8i)
    return _s5_out_ffn(y8, x2, m, s_w_glu, s_w_out, w1_1, w2_1, vec)
```

```python
import functools
import math

import jax
import jax.numpy as jnp
from jax import lax
from jax.experimental import pallas as pl
from jax.experimental.pallas import tpu as pltpu

F32 = jnp.float32
BF16 = jnp.bfloat16

D = 1024
NB = 8
SEQ = 2048
HEADS = 8
QK_NOPE = 128
QK_ROPE = 64
V_DIM = 128
Q_LORA = 256
KV_LORA = 128
ROPE_THETA = 10000.0
GROUP_CH = 16
N_GROUPS = 64
STATE = 64
D_FF = 4 * D
DEPTH = 2
ALPHA = (2 * DEPTH) ** 0.25
LN_EPS = 1e-5
RMS_EPS = 1e-6
Q_SCALE = math.log2(math.e) / math.sqrt(QK_NOPE + QK_ROPE)
NEG_BIG = -1e30
GELU_C = math.sqrt(2.0 / math.pi)

HEAD_W = 256
CHUNK = 8
N_PAIRS = N_GROUPS // 2
PAIR_CH = 2 * GROUP_CH
PAIR_W = CHUNK * PAIR_CH
assert PAIR_W == 4 * STATE

VMEM_LIMIT = 56 * 1024 * 1024
PROJ_ROWS = 1024
ATTN_BLK = 512
ATTN_HEADS = 4
_ATTN_SLOTS = 2 * (SEQ // ATTN_BLK) - 1
FFN_ROWS = 256
FFN_SUB = 4
FF_CHUNK = 1024
S5_IN_BLOCK, S5_IN_STEPS = 256, 64
S5_OUT_BLOCK, S5_OUT_STEPS = 64, 32
PAIRS_PER_STEP = 4
DISC_GROUPS = 16


VEC_LN_MIX_G, VEC_LN_MIX_B, VEC_LN_FFN_G, VEC_LN_FFN_B, VEC_MLP_B2 = 0, DEPTH, 2 * DEPTH, 3 * DEPTH, 4 * DEPTH
VEC_B_GLU = 5 * DEPTH
VEC_MLP_B1 = VEC_B_GLU + 1
VEC_MOD_B = VEC_MLP_B1 + DEPTH * (D_FF // D)
VEC_Q_NORM = VEC_MOD_B + 2 * 3 * DEPTH
VEC_KV_NORM = VEC_Q_NORM + 1
VEC_ROWS = -(-(VEC_KV_NORM + 1) // 8) * 8
assert FF_CHUNK == D


def _pack_vectors(ln_mix_g, ln_mix_b, ln_ffn_g, ln_ffn_b, mlp_b2, b_glu, mlp_b1, mod_mix_b, mod_ffn_b, q_norm, kv_norm):
    pad = lambda v: jnp.pad(v.reshape(1, -1), ((0, 0), (0, D - v.size)))
    rows = [ln_mix_g, ln_mix_b, ln_ffn_g, ln_ffn_b, mlp_b2, b_glu.reshape(1, D), mlp_b1.reshape(-1, D),
            mod_mix_b.reshape(-1, D), mod_ffn_b.reshape(-1, D), pad(q_norm), pad(kv_norm)]
    table = jnp.concatenate(rows, axis=0)
    return jnp.pad(table, ((0, VEC_ROWS - table.shape[0]), (0, 0)))


def _cparams(*sem):
    return pltpu.CompilerParams(dimension_semantics=sem, vmem_limit_bytes=VMEM_LIMIT)


def _const_spec(shape):
    nd = len(shape)
    return pl.BlockSpec(shape, lambda *_: (0,) * nd, pipeline_mode=pl.Buffered(1))


def _cast_specs(shape, grid, layer):
    _, rows, cols = shape
    n_steps = math.prod(grid)

    def step(*ids):
        s = 0
        for i, n in zip(ids, grid):
            s = s * n + i
        return s

    block = (1, rows // n_steps, cols)
    return (pl.BlockSpec(block, lambda *ids: (layer, step(*ids), 0)),
            pl.BlockSpec(block, lambda *ids: (0, step(*ids), 0)),
            jax.ShapeDtypeStruct((1, rows, cols), BF16))


def _cast_slices(src_refs, dst_refs):
    for src, dst in zip(src_refs, dst_refs):
        dst[...] = src[...].astype(dst.dtype)


def _layer_norm(v, g, b):
    mu = jnp.mean(v, axis=-1, keepdims=True)
    vc = v - mu
    var = jnp.mean(vc * vc, axis=-1, keepdims=True)
    return vc * lax.rsqrt(var + LN_EPS) * g + b


def _bdot(a, b):
    return jnp.dot(a, b, preferred_element_type=F32)


MIX, FFN = 0, 1
SHIFT, SCALE, GATE = 0, 1, 2


def _mod_kernel(c_ref, w_mix_ref, w_ffn_ref, vec_ref, o_ref):
    kind, layer, part = pl.program_id(0), pl.program_id(1), pl.program_id(2)
    c = c_ref[...]
    cs = (c * jax.nn.sigmoid(c)).astype(BF16)
    bias = vec_ref[pl.ds(VEC_MOD_B + 3 * (DEPTH * kind + layer) + part, 1), :]
    for k, w_ref in ((MIX, w_mix_ref), (FFN, w_ffn_ref)):
        @pl.when(kind == k)
        def _():
            o_ref[...] = _bdot(cs, w_ref[...].astype(BF16)) + bias


def _modulation(c, w_mix, w_ffn, vec):
    def w_spec(k):
        idle = ((1 - k) * (DEPTH - 1), 2 * (1 - k))
        return pl.BlockSpec((None, D, D), lambda kind, l, j: (jnp.where(kind == k, l, idle[0]), 0,
                                                            jnp.where(kind == k, j, idle[1])))
    return pl.pallas_call(
        _mod_kernel,
        out_shape=jax.ShapeDtypeStruct((2, DEPTH, 3, NB, D), F32),
        grid=(2, DEPTH, 3),
        in_specs=[
            pl.BlockSpec((NB, D), lambda kind, l, j: (0, 0)),
            w_spec(MIX),
            w_spec(FFN),
            _const_spec(vec.shape),
        ],
        out_specs=pl.BlockSpec((None, None, None, NB, D), lambda kind, l, j: (kind, l, j, 0, 0)),
        compiler_params=_cparams("arbitrary", "arbitrary", "arbitrary"),
        name="adaln_modulation",
    )(c, w_mix, w_ffn, vec)


def _mod_spec(kind, layer):
    return pl.BlockSpec((None, None, 3, NB, D), lambda *_: (kind, layer, 0, 0, 0), pipeline_mode=pl.Buffered(1))


def _mla_proj_kernel(x_ref, m_ref, pos_ref, freq_ref, w_in_ref, vec_ref, w_qb_ref, w_kvb_ref,
                     ca_ref, cb_ref, cc_ref, q_ref, kn_ref, kr_ref, v_ref, cab_ref, cbb_ref, ccb_ref):
    _cast_slices((ca_ref, cb_ref, cc_ref), (cab_ref, cbb_ref, ccb_ref))
    qn_ref = vec_ref.at[VEC_Q_NORM:VEC_Q_NORM + 1, 0:Q_LORA]
    kvn_ref = vec_ref.at[VEC_KV_NORM:VEC_KV_NORM + 1, 0:KV_LORA]
    b = pl.program_id(0)
    shift = m_ref[SHIFT, pl.ds(b, 1), :]
    scale = m_ref[SCALE, pl.ds(b, 1), :]

    def rms(v, g):
        return v * lax.rsqrt(jnp.mean(v * v, axis=-1, keepdims=True) + RMS_EPS) * g

    eye = lax.broadcasted_iota(jnp.int32, (128, 128), 0) == lax.broadcasted_iota(jnp.int32, (128, 128), 1)
    h = (x_ref[0] * (1.0 + scale) + shift).astype(BF16)
    z = _bdot(h, w_in_ref[...])
    cq = rms(z[:, :Q_LORA], qn_ref[...]).astype(BF16)
    ckv = rms(z[:, Q_LORA:Q_LORA + KV_LORA], kvn_ref[...]).astype(BF16)

    pos = pos_ref[0].astype(F32)
    pos_col = jnp.concatenate(
        [jnp.sum(jnp.where(eye, jnp.broadcast_to(pos[i:i + 1, :], (128, 128)), 0.0), axis=1, keepdims=True)
         for i in range(PROJ_ROWS // 128)], axis=0)
    ang = pos_col * freq_ref[...]
    lane = lax.broadcasted_iota(jnp.int32, ang.shape, 1)
    mult = jnp.where(lane < QK_ROPE, jnp.cos(ang), jnp.sin(ang))

    def rope(slab):
        s = slab * mult
        return s + pltpu.roll(s, QK_ROPE, axis=1)

    kr_ref[0] = jnp.where(lane < QK_ROPE, rope(z[:, Q_LORA + KV_LORA:]), 0.0).astype(BF16)

    q_all = _bdot(cq, w_qb_ref[...])
    kv = _bdot(ckv, w_kvb_ref[...])
    for hd in range(HEADS):
        c0 = hd * HEAD_W
        q_ref[0, hd, :, 0:QK_NOPE] = q_all[:, c0:c0 + QK_NOPE].astype(BF16)
        q_ref[0, hd, :, QK_NOPE:HEAD_W] = rope(q_all[:, c0 + QK_NOPE:c0 + HEAD_W]).astype(BF16)
        kn_ref[0, hd] = kv[:, c0:c0 + QK_NOPE].astype(BF16)
        v_ref[0, hd] = kv[:, c0 + QK_NOPE:c0 + HEAD_W].astype(BF16)


def _mla_proj(x, m, pos, freq, w_in, vec, w_qb, w_kvb, casts):
    nt = SEQ // PROJ_ROWS
    blk = PROJ_ROWS
    grid = (NB, nt)
    cast = [_cast_specs(w.shape, grid, layer) for w, layer in casts]
    return pl.pallas_call(
        _mla_proj_kernel,
        out_shape=[
            jax.ShapeDtypeStruct((NB, HEADS, SEQ, HEAD_W), BF16),
            jax.ShapeDtypeStruct((NB, HEADS, SEQ, QK_NOPE), BF16),
            jax.ShapeDtypeStruct((NB, SEQ, HEAD_W - QK_NOPE), BF16),
            jax.ShapeDtypeStruct((NB, HEADS, SEQ, V_DIM), BF16),
        ] + [c[2] for c in cast],
        grid=grid,
        in_specs=[
            pl.BlockSpec((1, blk, D), lambda b, i: (b, i, 0)),
            _mod_spec(MIX, 0),
            pl.BlockSpec((1, blk // 128, 128), lambda b, i: (b, i, 0)),
            _const_spec((1, 128)),
            _const_spec(w_in.shape),
            _const_spec(vec.shape),
            _const_spec(w_qb.shape),
            _const_spec(w_kvb.shape),
        ] + [c[0] for c in cast],
        out_specs=[
            pl.BlockSpec((1, HEADS, blk, HEAD_W), lambda b, i: (b, 0, i, 0)),
            pl.BlockSpec((1, HEADS, blk, QK_NOPE), lambda b, i: (b, 0, i, 0)),
            pl.BlockSpec((1, blk, HEAD_W - QK_NOPE), lambda b, i: (b, i, 0)),
            pl.BlockSpec((1, HEADS, blk, V_DIM), lambda b, i: (b, 0, i, 0)),
        ] + [c[1] for c in cast],
        compiler_params=_cparams("arbitrary", "arbitrary"),
        name="mla_projections",
    )(x, m, pos, freq, w_in, vec, w_qb, w_kvb, *[w for w, _ in casts])


def _lane_groups(v):
    return [v[:, c:c + 128] for c in range(0, v.shape[1], 128)]


def _attn_kernel(q_ref, kn_ref, kr_ref, v_ref, o_ref, s_ref):
    blk = ATTN_BLK
    half = blk // 2
    nq = SEQ // blk
    tri = lax.broadcasted_iota(jnp.int32, (half, half), 1) <= lax.broadcasted_iota(jnp.int32, (half, half), 0)

    def rows(n):
        return slice(n * blk, (n + 1) * blk)

    def slot(hd, n, j):
        return hd * _ATTN_SLOTS + (0 if n % 2 == 0 else nq - 1) + j

    def nt_dot(a, b):
        return lax.dot_general(a, b, (((1,), (1,)), ((), ())), preferred_element_type=F32)

    def keys(hd, j):
        return jnp.concatenate([kn_ref[hd, rows(j), :], kr_ref[rows(j), :]], axis=1)

    def lane_max(vals):
        return functools.reduce(jnp.maximum, vals)

    def pass1(hd, n):
        q = q_ref[hd, rows(n), :]
        m = []
        for j in range(n):
            s = nt_dot(q, keys(hd, j))
            s_ref[slot(hd, n, j)] = s
            m = [lane_max(_lane_groups(s) + m)]
        k = keys(hd, n)
        s_top = jnp.where(tri, nt_dot(q[:half], k[:half]), NEG_BIG)
        s_bot = nt_dot(q[half:], k)
        s_bot = jnp.concatenate([s_bot[:, :half], jnp.where(tri, s_bot[:, half:], NEG_BIG)], axis=1)
        s_ref[slot(hd, n, n), 0:half, 0:half] = s_top
        s_ref[slot(hd, n, n), half:blk, :] = s_bot
        m_top = lane_max(_lane_groups(s_top) + [v[:half] for v in m])
        m_bot = lane_max(_lane_groups(s_bot) + [v[half:] for v in m])
        m = jnp.concatenate([m_top, m_bot], axis=0)
        return jnp.broadcast_to(jnp.max(m, axis=-1, keepdims=True), (blk, 128))

    def pass2(hd, n, mb):
        diag = slot(hd, n, n)
        ps = [jnp.exp2(sg - mb) for j in range(n) for sg in _lane_groups(s_ref[slot(hd, n, j)])]
        p_top = [jnp.exp2(sg - mb[:half]) for sg in _lane_groups(s_ref[diag, 0:half, 0:half])]
        p_bot = [jnp.exp2(sg - mb[half:]) for sg in _lane_groups(s_ref[diag, half:blk, :])]
        l_top = functools.reduce(jnp.add, p_top + [p[:half] for p in ps])
        l_bot = functools.reduce(jnp.add, p_bot + [p[half:] for p in ps])
        l = jnp.sum(jnp.concatenate([l_top, l_bot], axis=0), axis=-1, keepdims=True)
        v0 = n * blk
        acc = jnp.concatenate([
            _bdot(jnp.concatenate(p_top, axis=1).astype(BF16), v_ref[hd, v0:v0 + half, :]),
            _bdot(jnp.concatenate(p_bot, axis=1).astype(BF16), v_ref[hd, v0:v0 + blk, :])], axis=0)
        if n > 0:
            acc = acc + _bdot(jnp.concatenate(ps, axis=1).astype(BF16), v_ref[hd, 0:v0, :])
        o_ref[hd, rows(n), :] = (acc / l).astype(BF16)

    mb = [pass1(hd, 0) for hd in range(ATTN_HEADS)]
    for n in range(nq):
        for hd in range(ATTN_HEADS):
            mb_next = pass1(hd, n + 1) if n + 1 < nq else None
            pass2(hd, n, mb[hd])
            mb[hd] = mb_next


def _attention(q, kn, kr, v):
    head_spec = lambda w: pl.BlockSpec((None, ATTN_HEADS, SEQ, w), lambda b, h: (b, h, 0, 0))
    kr_spec = pl.BlockSpec((None, SEQ, HEAD_W - QK_NOPE), lambda b, h: (b, 0, 0))
    return pl.pallas_call(
        _attn_kernel,
        out_shape=jax.ShapeDtypeStruct((NB, HEADS, SEQ, V_DIM), BF16),
        grid=(NB, HEADS // ATTN_HEADS),
        in_specs=[head_spec(HEAD_W), head_spec(QK_NOPE), kr_spec, head_spec(V_DIM)],
        out_specs=head_spec(V_DIM),
        scratch_shapes=[pltpu.VMEM((ATTN_HEADS * _ATTN_SLOTS, ATTN_BLK, ATTN_BLK), F32)],
        compiler_params=_cparams("arbitrary", "arbitrary"),
        name="mla_attention",
    )(q, kn, kr, v)


def _vec_row(vec_ref, row):
    return vec_ref[row:row + 1, :]


def _ffn_rows(x, shift, scale, gate, w1_ref, w2_ref, vec_ref, layer):
    h = (x * (1.0 + scale) + shift).reshape(-1, D).astype(BF16)
    a2 = []
    for c in range(D_FF // FF_CHUNK):
        cols = slice(c * FF_CHUNK, (c + 1) * FF_CHUNK)
        b1 = _vec_row(vec_ref, VEC_MLP_B1 + layer * (D_FF // FF_CHUNK) + c)
        a = jnp.maximum(_bdot(h, w1_ref[:, cols]) + b1, 0.0)
        a2.append((a * a).astype(BF16))
    y = (_bdot(jnp.concatenate(a2, axis=1), w2_ref[...]) + _vec_row(vec_ref, VEC_MLP_B2 + layer)).reshape(x.shape)
    return _layer_norm(ALPHA * x + (1.0 + gate) * y,
                       _vec_row(vec_ref, VEC_LN_FFN_G + layer), _vec_row(vec_ref, VEC_LN_FFN_B + layer))


def _batch_mod(m_ref):
    b = pl.program_id(0)
    return [m_ref[i, pl.ds(b, 1), :] for i in (SHIFT, SCALE, GATE)]


def _attn_out_ffn_kernel(o_ref, x_ref, mm_ref, mf_ref, wo_ref, w1_ref, w2_ref, vec_ref, sa_ref, sb_ref, sc_ref,
                         y_ref, sab_ref, sbb_ref, scb_ref, *, layer):
    _cast_slices((sa_ref, sb_ref, sc_ref), (sab_ref, sbb_ref, scb_ref))
    gm_ref = vec_ref.at[VEC_LN_MIX_G + layer:VEC_LN_MIX_G + layer + 1]
    bm_ref = vec_ref.at[VEC_LN_MIX_B + layer:VEC_LN_MIX_B + layer + 1]
    gate_mix = _batch_mod(mm_ref)[GATE]
    mod = _batch_mod(mf_ref)
    for sb in range(FFN_SUB):
        rows = slice(sb * FFN_ROWS, (sb + 1) * FFN_ROWS)
        o = jnp.concatenate([o_ref[hd, rows, :] for hd in range(HEADS)], axis=1)
        y_ref[rows, :] = _layer_norm(ALPHA * x_ref[rows, :] + (1.0 + gate_mix) * _bdot(o, wo_ref[...]),
                                     gm_ref[...], bm_ref[...])
    for sb in range(FFN_SUB):
        rows = slice(sb * FFN_ROWS, (sb + 1) * FFN_ROWS)
        y_ref[rows, :] = _ffn_rows(y_ref[rows, :], *mod, w1_ref, w2_ref, vec_ref, layer)


_FFN_X_SPEC = pl.BlockSpec((None, FFN_SUB * FFN_ROWS, D), lambda b, i: (b, i, 0))
_FFN_GRID = (NB, SEQ // (FFN_SUB * FFN_ROWS))


def _layer_spec(shape, layer):
    return pl.BlockSpec((None,) + tuple(shape[1:]), lambda *_: (layer, 0, 0), pipeline_mode=pl.Buffered(1))


def _attn_out_ffn(o, x, m, layer, w_o, w1, w2, vec, casts):
    o_spec = pl.BlockSpec((None, HEADS, FFN_SUB * FFN_ROWS, V_DIM), lambda b, i: (b, 0, i, 0))
    cast = [_cast_specs(w.shape, _FFN_GRID, l) for w, l in casts]
    return pl.pallas_call(
        functools.partial(_attn_out_ffn_kernel, layer=layer),
        out_shape=[jax.ShapeDtypeStruct((NB, SEQ, D), F32)] + [c[2] for c in cast],
        grid=_FFN_GRID,
        in_specs=[o_spec, _FFN_X_SPEC, _mod_spec(MIX, layer), _mod_spec(FFN, layer), _layer_spec(w_o.shape, 0),
                  _layer_spec(w1.shape, 0), _layer_spec(w2.shape, 0), _const_spec(vec.shape)]
                 + [c[0] for c in cast],
        out_specs=[_FFN_X_SPEC] + [c[1] for c in cast],
        compiler_params=_cparams("arbitrary", "arbitrary"),
        name="attn_out_ffn_norm",
    )(o, x, m, m, w_o, w1, w2, vec, *[w for w, _ in casts])


def _cmul(ar, ai, br, bi):
    return ar * br - ai * bi, ar * bi + ai * br


def _pair_slot(piece, zero, slot):
    return [piece, zero] if slot == 0 else [zero, piece]


def _s5_disc_kernel(ldt_ref, are_ref, aim_ref, bre_ref, bim_ref, cre_ref, cim_ref, d_ref,
                    a8r_ref, a8i_ref, wb_ref, wck_ref):
    lr = are_ref[...]
    li = aim_ref[...]
    dt = jnp.exp(ldt_ref[...])
    mag = jnp.exp(lr * dt)
    ab_re = mag * jnp.cos(li * dt)
    ab_im = mag * jnp.sin(li * dt)
    den = lr * lr + li * li
    nr = ab_re - 1.0
    coef_re = (nr * lr + ab_im * li) / den
    coef_im = (ab_im * lr - nr * li) / den
    bb_re, bb_im = _cmul(coef_re, coef_im, bre_ref[...], bim_ref[...])
    c_re = cre_ref[...]
    c_im = cim_ref[...]

    pw = [(jnp.ones_like(ab_re), jnp.zeros_like(ab_im))]
    for _ in range(CHUNK):
        pw.append(_cmul(pw[-1][0], pw[-1][1], ab_re, ab_im))
    a8r_ref[...] = pw[CHUNK][0]
    a8i_ref[...] = pw[CHUNK][1]

    n_pairs = wb_ref.shape[0]

    def pairs(v):
        v4 = v.reshape(n_pairs, 2, v.shape[1], v.shape[2])
        return v4[:, 0], v4[:, 1]

    z64 = jnp.zeros((n_pairs, GROUP_CH, STATE), F32)
    z16 = jnp.zeros((n_pairs, GROUP_CH, GROUP_CH), F32)
    eye = (lax.broadcasted_iota(jnp.int32, (GROUP_CH, GROUP_CH), 0)
           == lax.broadcasted_iota(jnp.int32, (GROUP_CH, GROUP_CH), 1)).astype(F32)

    ab_l = [_cmul(pw[l][0], pw[l][1], bb_re, bb_im) for l in range(CHUNK)]
    wb_rows = []
    for k in range(CHUNK):
        m_re, m_im = ab_l[CHUNK - 1 - k]
        for slot in range(2):
            wb_rows.append(jnp.concatenate(_pair_slot(pairs(m_re)[slot], z64, slot)
                                           + _pair_slot(pairs(m_im)[slot], z64, slot), axis=2))
    wb_ref[...] = jnp.concatenate(wb_rows, axis=1).astype(BF16)

    def nt(a, b):
        return lax.dot_general(a.astype(BF16), b.astype(BF16), (((2,), (2,)), ((0,), (0,))),
                               preferred_element_type=F32)

    k_l = [nt(c_re, ab_l[l][0]) - nt(c_im, ab_l[l][1]) for l in range(CHUNK)]
    k_l[0] = k_l[0] + d_ref[...] * eye[None]

    wck_rows = []
    for kp in range(CHUNK):
        ca_re, ca_im = _cmul(c_re, c_im, pw[kp + 1][0], pw[kp + 1][1])
        for slot in range(2):
            state_cols = (_pair_slot(pairs(ca_re)[slot], z64, slot)
                          + _pair_slot(-pairs(ca_im)[slot], z64, slot))
            direct_cols = []
            for k in range(CHUNK):
                blk = pairs(k_l[kp - k])[slot] if k <= kp else z16
                direct_cols += _pair_slot(blk, z16, slot)
            wck_rows.append(jnp.concatenate(state_cols + direct_cols, axis=2))
    wck_ref[...] = jnp.concatenate(wck_rows, axis=1).astype(BF16)


def _s5_discretise(log_dt, a_re, a_im, b_re, b_im, c_re, c_im, d_skip):
    g3 = (N_GROUPS, 1, STATE)
    gb = DISC_GROUPS
    spec = lambda *tail: pl.BlockSpec((gb,) + tail, lambda i: (i,) + (0,) * len(tail))
    pair_spec = lambda cols: pl.BlockSpec((gb // 2, PAIR_W, cols), lambda i: (i, 0, 0))
    big = spec(GROUP_CH, STATE)
    return pl.pallas_call(
        _s5_disc_kernel,
        out_shape=(jax.ShapeDtypeStruct(g3, F32), jax.ShapeDtypeStruct(g3, F32),
                   jax.ShapeDtypeStruct((N_PAIRS, PAIR_W, PAIR_W), BF16),
                   jax.ShapeDtypeStruct((N_PAIRS, PAIR_W, 2 * PAIR_W), BF16)),
        grid=(N_GROUPS // gb,),
        in_specs=[spec(1, 1), spec(1, STATE), spec(1, STATE), big, big, big, big, spec(GROUP_CH, 1)],
        out_specs=(spec(1, STATE), spec(1, STATE), pair_spec(PAIR_W), pair_spec(2 * PAIR_W)),
        compiler_params=_cparams("arbitrary"),
        name="s5_discretise",
    )(log_dt.reshape(N_GROUPS, 1, 1), a_re.reshape(g3), a_im.reshape(g3),
      jnp.swapaxes(b_re, 1, 2), jnp.swapaxes(b_im, 1, 2), c_re, c_im, d_skip.reshape(N_GROUPS, GROUP_CH, 1))


def _s5_x_spec(block):
    return pl.BlockSpec((NB, block, D), lambda i: (0, i, 0))


def _s5_u8_spec(block):
    return pl.BlockSpec((block // CHUNK * NB, N_PAIRS * PAIR_W), lambda i: (i, 0))


def _s5_in_kernel(x_ref, m_ref, w_in_ref, ca_ref, cb_ref, u8_ref, cab_ref, cbb_ref):
    _cast_slices((ca_ref, cb_ref), (cab_ref, cbb_ref))
    shift = m_ref[SHIFT]
    scale = m_ref[SCALE]
    steps = S5_IN_STEPS
    chunks = steps // CHUNK
    for sb in range(S5_IN_BLOCK // steps):
        x3 = jnp.swapaxes(x_ref[:, sb * steps:(sb + 1) * steps, :], 0, 1)
        h = (x3 * (1.0 + scale)[None] + shift[None]).reshape(steps * NB, D).astype(BF16)
        u = _bdot(h, w_in_ref[...])
        u4 = u.reshape(chunks, CHUNK, NB, D)
        per_k = [u4[:, k].reshape(chunks * NB, D) for k in range(CHUNK)]
        cols = [per_k[k][:, j * PAIR_CH:(j + 1) * PAIR_CH] for j in range(N_PAIRS) for k in range(CHUNK)]
        u8_ref[sb * chunks * NB:(sb + 1) * chunks * NB, :] = jnp.concatenate(cols, axis=1).astype(BF16)


def _s5_in(x, m, w_in, casts):
    grid = (SEQ // S5_IN_BLOCK,)
    cast = [_cast_specs(w.shape, grid, l) for w, l in casts]
    return pl.pallas_call(
        _s5_in_kernel,
        out_shape=[jax.ShapeDtypeStruct((SEQ // CHUNK * NB, N_PAIRS * PAIR_W), BF16)] + [c[2] for c in cast],
        grid=grid,
        in_specs=[
            _s5_x_spec(S5_IN_BLOCK),
            _mod_spec(MIX, 1),
            _layer_spec(w_in.shape, 0),
        ] + [c[0] for c in cast],
        out_specs=[_s5_u8_spec(S5_IN_BLOCK)] + [c[1] for c in cast],
        compiler_params=_cparams("arbitrary"),
        name="s5_in_proj",
    )(x, m, w_in, *[w for w, _ in casts])


def _s5_core_kernel(u8_ref, wb_ref, wck_ref, a8r_ref, a8i_ref, y8_ref):
    half = PAIR_W // 2
    for jj in range(PAIRS_PER_STEP):
        lanes = slice(jj * PAIR_W, (jj + 1) * PAIR_W)
        u8 = u8_ref[:, lanes]
        v = _bdot(u8, wb_ref[jj])
        pair = lambda ref: jnp.concatenate([ref[2 * jj], ref[2 * jj + 1]], axis=1)
        ar = jnp.broadcast_to(pair(a8r_ref), (NB, half))
        ai = jnp.broadcast_to(pair(a8i_ref), (NB, half))
        xr = jnp.zeros((NB, half), F32)
        xi = jnp.zeros((NB, half), F32)
        prev_r, prev_i = [], []
        for s in range(SEQ // CHUNK):
            prev_r.append(xr)
            prev_i.append(xi)
            rows = slice(s * NB, (s + 1) * NB)
            xr, xi = ar * xr - ai * xi + v[rows, :half], ar * xi + ai * xr + v[rows, half:]
        x_prev = jnp.concatenate([jnp.concatenate(prev_r, axis=0), jnp.concatenate(prev_i, axis=0)], axis=1)
        lhs = jnp.concatenate([x_prev.astype(BF16), u8], axis=1)
        y8_ref[:, lanes] = lax.dot_general(lhs, wck_ref[jj], (((1,), (1,)), ((), ())),
                                           preferred_element_type=F32).astype(y8_ref.dtype)


def _s5_core(u8, wb, wck, a8r, a8i):
    n_rows = u8.shape[0]
    w = PAIRS_PER_STEP * PAIR_W
    return pl.pallas_call(
        _s5_core_kernel,
        out_shape=jax.ShapeDtypeStruct(u8.shape, BF16),
        grid=(N_PAIRS // PAIRS_PER_STEP,),
        in_specs=[
            pl.BlockSpec((n_rows, w), lambda i: (0, i)),
            pl.BlockSpec((PAIRS_PER_STEP, PAIR_W, PAIR_W), lambda i: (i, 0, 0)),
            pl.BlockSpec((PAIRS_PER_STEP, PAIR_W, 2 * PAIR_W), lambda i: (i, 0, 0)),
            pl.BlockSpec((2 * PAIRS_PER_STEP, 1, STATE), lambda i: (i, 0, 0)),
            pl.BlockSpec((2 * PAIRS_PER_STEP, 1, STATE), lambda i: (i, 0, 0)),
        ],
        out_specs=pl.BlockSpec((n_rows, w), lambda i: (0, i)),
        compiler_params=_cparams("arbitrary"),
        name="s5_recurrence",
    )(u8, wb, wck, a8r, a8i)


def _s5_out_ffn_kernel(y8_ref, x_ref, mm_ref, mf_ref, w_glu_ref, w_out_ref, w1_ref, w2_ref, vec_ref, o_ref):
    b_glu_ref = vec_ref.at[VEC_B_GLU:VEC_B_GLU + 1]
    g_ref = vec_ref.at[VEC_LN_MIX_G + 1:VEC_LN_MIX_G + 2]
    b_ref = vec_ref.at[VEC_LN_MIX_B + 1:VEC_LN_MIX_B + 2]
    gate = mm_ref[GATE]
    steps = S5_OUT_STEPS
    chunks = steps // CHUNK
    n_sub = S5_OUT_BLOCK // steps
    for sb in range(n_sub):
        y8 = y8_ref[sb * chunks * NB:(sb + 1) * chunks * NB, :].astype(F32)
        per_k = [jnp.concatenate([y8[:, j * PAIR_W + k * PAIR_CH:j * PAIR_W + (k + 1) * PAIR_CH]
                                  for j in range(N_PAIRS)], axis=1) for k in range(CHUNK)]
        y = jnp.stack([p.reshape(chunks, NB, D) for p in per_k], axis=1).reshape(steps * NB, D)

        half_y = 0.5 * y
        g = half_y + half_y * jnp.tanh(y * (GELU_C + (GELU_C * 0.044715) * (y * y)))
        zz = g * jax.nn.sigmoid(_bdot(g.astype(BF16), w_glu_ref[...]) + b_glu_ref[...])
        out = jnp.swapaxes(_bdot(zz.astype(BF16), w_out_ref[...]).reshape(steps, NB, D), 0, 1)
        t_rows = slice(sb * steps, (sb + 1) * steps)
        res = ALPHA * x_ref[:, t_rows, :] + (1.0 + gate)[:, None, :] * out
        o_ref[:, t_rows, :] = _layer_norm(res, g_ref[...], b_ref[...])
    mod = [mf_ref[i][:, None, :] for i in (SHIFT, SCALE, GATE)]
    for sb in range(n_sub):
        t_rows = slice(sb * steps, (sb + 1) * steps)
        o_ref[:, t_rows, :] = _ffn_rows(o_ref[:, t_rows, :], *mod, w1_ref, w2_ref, vec_ref, 1)


def _s5_out_ffn(y8, x, m, w_glu, w_out, w1, w2, vec):
    x_spec = _s5_x_spec(S5_OUT_BLOCK)
    return pl.pallas_call(
        _s5_out_ffn_kernel,
        out_shape=jax.ShapeDtypeStruct((NB, SEQ, D), F32),
        grid=(SEQ // S5_OUT_BLOCK,),
        in_specs=[
            _s5_u8_spec(S5_OUT_BLOCK),
            x_spec,
            _mod_spec(MIX, 1),
            _mod_spec(FFN, 1),
            _layer_spec(w_glu.shape, 0),
            _layer_spec(w_out.shape, 0),
            _layer_spec(w1.shape, 0),
            _layer_spec(w2.shape, 0),
            _const_spec(vec.shape),
        ],
        out_specs=x_spec,
        compiler_params=_cparams("arbitrary"),
        name="s5_out_ffn_norm",
    )(y8, x, m, m, w_glu, w_out, w1, w2, vec)


def _rotate_half_cols(w):
    half = w.shape[-1] // 2
    return jnp.concatenate([-w[..., half:], w[..., :half]], axis=-1)


def _mla_weights(w_in, w_qb):
    k_pe = w_in[:, Q_LORA + KV_LORA:]
    w_in_ext = jnp.concatenate([w_in, _rotate_half_cols(k_pe)], axis=1)
    wq = w_qb.reshape(Q_LORA, HEADS, QK_NOPE + QK_ROPE) * Q_SCALE
    wq_ext = jnp.concatenate([wq, _rotate_half_cols(wq[..., QK_NOPE:])], axis=-1)
    return w_in_ext.astype(BF16), wq_ext.reshape(Q_LORA, HEADS * HEAD_W).astype(BF16)


def kernel(x, c, positions, mla_w_in, mla_q_norm, mla_w_qb, mla_kv_norm, mla_w_kvb, mla_w_o, ssm_w_in, ssm_log_dt, ssm_a_re, ssm_a_im, ssm_b_re, ssm_b_im, ssm_c_re, ssm_c_im, ssm_d, ssm_w_glu, ssm_b_glu, ssm_w_out, mlp_w1, mlp_b1, mlp_w2, mlp_b2, mod_mix_w, mod_mix_b, mod_ffn_w, mod_ffn_b, ln_mix_g, ln_mix_b, ln_ffn_g, ln_ffn_b):
    vec = _pack_vectors(ln_mix_g, ln_mix_b, ln_ffn_g, ln_ffn_b, mlp_b2, ssm_b_glu, mlp_b1, mod_mix_b, mod_ffn_b,
                        mla_q_norm, mla_kv_norm)
    m = _modulation(c, mod_mix_w, mod_ffn_w, vec)

    inv_freq = ROPE_THETA ** (-jnp.arange(0, QK_ROPE, 2, dtype=F32) / QK_ROPE)
    freq = jnp.tile(inv_freq, 4).reshape(1, 128)
    w_in_ext, w_qb_ext = _mla_weights(mla_w_in[0], mla_w_qb[0])
    q, kn, kr, v, w1_0, w2_0, w_o = _mla_proj(
        x, m, positions.reshape(NB, SEQ // 128, 128), freq, w_in_ext, vec, w_qb_ext, mla_w_kvb[0].astype(BF16),
        ((mlp_w1, 0), (mlp_w2, 0), (mla_w_o, 0)))
    o = _attention(q, kn, kr, v)
    x2, s_w_in, s_w_glu, s_w_out = _attn_out_ffn(
        o, x, m, 0, w_o, w1_0, w2_0, vec, ((ssm_w_in, 0), (ssm_w_glu, 0), (ssm_w_out, 0)))

    a8r, a8i, wb, wck = _s5_discretise(ssm_log_dt[0], ssm_a_re[0], ssm_a_im[0], ssm_b_re[0], ssm_b_im[0],
                                       ssm_c_re[0], ssm_c_im[0], ssm_d[0])
    u8, w1_1, w2_1 = _s5_in(x2, m, s_w_in, ((mlp_w1, 1), (mlp_w2, 1)))
    y8 = _s5_core(u8, wb, wck, a8r, a8i)
    return _s5_out_ffn(y8, x2, m, s_w_glu, s_w_out, w1_1, w2_1, vec)
```
